```python
import math
import jax, jax.numpy as jnp
from jax import lax
import numpy as np

D_MODEL = 2048
BATCH = 2
SEQ = 8192
DEPTH = 2
DEC_BATCH = 16
DEC_SEQ = 16
PAST_LEN = 4096

CHUNK = 64
Q_BLOCK = 128
HEAD_DIM = 128
N_DIFF_HEADS = 4
DIFF_V_DIM = 2 * HEAD_DIM
N_FOX_HEADS = 8
N_MLA_HEADS = 16
Q_LORA = 512
KV_LORA = 512
NOPE_DIM = 128
ROPE_DIM = 64
MLA_V_DIM = 128
D_FF = 5632
ROPE_THETA = 10000.0
EPS = 1e-6
NEG_INF = -1e30
N_MOD = 9
N_EVEN = (DEPTH + 1) // 2
N_ODD = DEPTH // 2
FORGET_BIAS_MEAN = 3.0

DIFF_Q = N_DIFF_HEADS * 2 * HEAD_DIM
DIFF_V = N_DIFF_HEADS * DIFF_V_DIM
FOX_W = N_FOX_HEADS * HEAD_DIM
EVEN_IN = 2 * DIFF_Q + DIFF_V + 3 * FOX_W + N_FOX_HEADS
EVEN_MIX = DIFF_V + FOX_W
ODD_IN = Q_LORA + KV_LORA + ROPE_DIM
MLA_QK_DIM = NOPE_DIM + ROPE_DIM

kernel_name = "hybrid_streaming_encoder_step"


def rmsnorm(x, g):
    xf = x.astype(jnp.float32)
    y = xf * lax.rsqrt(jnp.mean(xf * xf, axis=-1, keepdims=True) + EPS)
    return (y * g.astype(jnp.float32)).astype(x.dtype)


def rope(x, pos):
    d = x.shape[-1]
    half = d // 2
    inv = ROPE_THETA ** (-jnp.arange(half, dtype=jnp.float32) * 2.0 / d)
    ang = pos.astype(jnp.float32)[:, None] * inv[None, :]
    ang = ang.reshape(ang.shape[0], *([1] * (x.ndim - 3)), half)
    cos, sin = jnp.cos(ang), jnp.sin(ang)
    xf = x.astype(jnp.float32)
    x1, x2 = xf[..., :half], xf[..., half:]
    return jnp.concatenate([x1 * cos - x2 * sin, x2 * cos + x1 * sin], axis=-1).astype(x.dtype)


def chunk_mask(q_pos, k_pos):
    return (k_pos[None, :] // CHUNK) <= (q_pos[:, None] // CHUNK)


def causal_mask(q_pos, k_pos):
    return k_pos[None, :] <= q_pos[:, None]


def sweep_queries(fn, q_args, q_pos, kv_args):
    T = q_pos.shape[0]
    if T <= Q_BLOCK:
        return fn(*q_args, q_pos, *kv_args)
    nb = T // Q_BLOCK

    def split(a):
        return jnp.moveaxis(a.reshape(a.shape[0], nb, Q_BLOCK, *a.shape[2:]), 1, 0)

    blocks = tuple(split(a) for a in q_args) + (q_pos.reshape(nb, Q_BLOCK),)
    out = lax.map(lambda blk: fn(*blk[:-1], blk[-1], *kv_args), blocks)
    out = jnp.moveaxis(out, 0, 1)
    return out.reshape(out.shape[0], T, *out.shape[3:])


def diff_block(q, q_pos, k, v, k_pos, lam):
    s = jnp.einsum('bqhcd,bkhcd->bchqk', q, k, preferred_element_type=jnp.float32) * (HEAD_DIM ** -0.5)
    s = jnp.where(chunk_mask(q_pos, k_pos), s, NEG_INF)
    p = jax.nn.softmax(s, axis=-1)
    a = p[:, 0] - lam * p[:, 1]
    return jnp.einsum('bhqk,bkhe->bqhe', a.astype(v.dtype), v)


def fox_block(q, fq, q_pos, k, v, fk, k_pos):
    s = jnp.einsum('bqhd,bkhd->bhqk', q, k, preferred_element_type=jnp.float32) * (HEAD_DIM ** -0.5)
    bias = jnp.transpose(fq, (0, 2, 1))[:, :, :, None] - jnp.transpose(fk, (0, 2, 1))[:, :, None, :]
    s = jnp.where(causal_mask(q_pos, k_pos), s + bias, NEG_INF)
    p = jax.nn.softmax(s, axis=-1)
    return jnp.einsum('bhqk,bkhd->bqhd', p.astype(v.dtype), v)


def mla_block(qn, qp, q_pos, kn, kp, v, k_pos):
    s = (jnp.einsum('bqhd,bkhd->bhqk', qn, kn, preferred_element_type=jnp.float32)
         + jnp.einsum('bqhr,bkr->bhqk', qp, kp, preferred_element_type=jnp.float32)) * (MLA_QK_DIM ** -0.5)
    s = jnp.where(chunk_mask(q_pos, k_pos), s, NEG_INF)
    p = jax.nn.softmax(s, axis=-1)
    return jnp.einsum('bhqk,bkhd->bqhd', p.astype(v.dtype), v)


def mix_even(h, q_pos, k_pos, past, w_in, b_f, qk_g, lam_p, subln_g, w_out, lam_init):
    B, T, _ = h.shape
    z = h @ w_in
    idx = [DIFF_Q, 2 * DIFF_Q, 2 * DIFF_Q + DIFF_V, 2 * DIFF_Q + DIFF_V + FOX_W,
           2 * DIFF_Q + DIFF_V + 2 * FOX_W, 2 * DIFF_Q + DIFF_V + 3 * FOX_W]
    qa, ka, va, qb, kb, vb, fg = jnp.split(z, idx, axis=-1)
    qa = rope(rmsnorm(qa.reshape(B, T, N_DIFF_HEADS, 2, HEAD_DIM), qk_g[0]), q_pos)
    ka = rope(rmsnorm(ka.reshape(B, T, N_DIFF_HEADS, 2, HEAD_DIM), qk_g[1]), q_pos)
    va = va.reshape(B, T, N_DIFF_HEADS, DIFF_V_DIM)
    qb = rmsnorm(qb.reshape(B, T, N_FOX_HEADS, HEAD_DIM), qk_g[2])
    kb = rmsnorm(kb.reshape(B, T, N_FOX_HEADS, HEAD_DIM), qk_g[3])
    vb = vb.reshape(B, T, N_FOX_HEADS, HEAD_DIM)
    logf = jax.nn.log_sigmoid(fg.astype(jnp.float32) + b_f.astype(jnp.float32))
    if past is None:
        ka_all, va_all, kb_all, vb_all, logf_all = ka, va, kb, vb, logf
    else:
        pk_a, pv_a, pk_b, pv_b, p_logf = past
        ka_all = jnp.concatenate([pk_a.astype(ka.dtype), ka], axis=1)
        va_all = jnp.concatenate([pv_a.astype(va.dtype), va], axis=1)
        kb_all = jnp.concatenate([pk_b.astype(kb.dtype), kb], axis=1)
        vb_all = jnp.concatenate([pv_b.astype(vb.dtype), vb], axis=1)
        logf_all = jnp.concatenate([p_logf.astype(jnp.float32), logf], axis=1)
    F_all = jnp.cumsum(logf_all, axis=1)
    Fq = F_all[:, -T:]
    lp = lam_p.astype(jnp.float32)
    lam = jnp.exp(jnp.sum(lp[0] * lp[1])) - jnp.exp(jnp.sum(lp[2] * lp[3])) + lam_init
    oa = sweep_queries(diff_block, (qa,), q_pos, (ka_all, va_all, k_pos, lam))
    oa = rmsnorm(oa, subln_g) * (1.0 - lam_init)
    ob = sweep_queries(fox_block, (qb, Fq), q_pos, (kb_all, vb_all, F_all, k_pos))
    y = jnp.concatenate([oa.reshape(B, T, DIFF_V), ob.reshape(B, T, FOX_W)], axis=-1) @ w_out
    return y, (ka, va, kb, vb, logf)


def mix_odd(h, q_pos, k_pos, past, w_in, g_cq, g_ckv, w_uq, w_ukv, g_nope, g_rope, w_out):
    B, T, _ = h.shape
    z = h @ w_in
    cq, ckv, kpe = jnp.split(z, [Q_LORA, Q_LORA + KV_LORA], axis=-1)
    cq = rmsnorm(cq, g_cq)
    ckv = rmsnorm(ckv, g_ckv)
    kpe = rope(rmsnorm(kpe, g_rope[1]), q_pos)
    q = (cq @ w_uq).reshape(B, T, N_MLA_HEADS, MLA_QK_DIM)
    qn = rmsnorm(q[..., :NOPE_DIM], g_nope[0])
    qp = rope(rmsnorm(q[..., NOPE_DIM:], g_rope[0]), q_pos)
    if past is None:
        ckv_all, kpe_all = ckv, kpe
    else:
        p_ckv, p_kpe = past
        ckv_all = jnp.concatenate([p_ckv.astype(ckv.dtype), ckv], axis=1)
        kpe_all = jnp.concatenate([p_kpe.astype(kpe.dtype), kpe], axis=1)
    Tk = ckv_all.shape[1]
    kv = (ckv_all @ w_ukv).reshape(B, Tk, N_MLA_HEADS, NOPE_DIM + MLA_V_DIM)
    kn = rmsnorm(kv[..., :NOPE_DIM], g_nope[1])
    v = kv[..., NOPE_DIM:]
    o = sweep_queries(mla_block, (qn, qp), q_pos, (kn, kpe_all, v, k_pos))
    y = o.reshape(B, T, N_MLA_HEADS * MLA_V_DIM) @ w_out
    return y, (ckv, kpe)


def swiglu(h, w_in, w_out):
    g, u = jnp.split(h @ w_in, 2, axis=-1)
    return (jax.nn.silu(g) * u) @ w_out


def modulate(x, g, shift, scale):
    return rmsnorm(x, g) * (1.0 + scale[:, None, :]) + shift[:, None, :]


def layer_stack(x, c, q_pos, k_pos, caches, p):
    lists = tuple([] for _ in range(7))
    for l in range(DEPTH):
        i = l // 2
        mod = (jax.nn.silu(c) @ p['w_ada'][l] + p['b_ada'][l]).reshape(c.shape[0], N_MOD, D_MODEL)
        sh, sc, gt = mod[:, 0::3], mod[:, 1::3], mod[:, 2::3]
        g = p['norm_gains'][l]
        h = modulate(x, g[0], sh[:, 0], sc[:, 0])
        x = x + 0.5 * gt[:, 0, None] * swiglu(h, p['w_ffn_in'][l, 0], p['w_ffn_out'][l, 0])
        h = modulate(x, g[1], sh[:, 1], sc[:, 1])
        if l % 2 == 0:
            past = None if caches is None else tuple(a[i] for a in caches[:5])
            y, new = mix_even(h, q_pos, k_pos, past, p['w_in_even'][i], p['b_forget'][i],
                              p['qk_norm_even'][i], p['diff_lambda'][i], p['diff_subln'][i],
                              p['w_out_even'][i], 0.8 - 0.6 * math.exp(-0.3 * l))
            for lst, a in zip(lists[:5], new):
                lst.append(a)
        else:
            past = None if caches is None else tuple(a[i] for a in caches[5:])
            y, new = mix_odd(h, q_pos, k_pos, past, p['w_in_odd'][i], p['mla_cq_norm'][i],
                             p['mla_ckv_norm'][i], p['w_uq'][i], p['w_ukv'][i],
                             p['mla_qk_norm_nope'][i], p['mla_qk_norm_rope'][i], p['w_out_odd'][i])
            for lst, a in zip(lists[5:], new):
                lst.append(a)
        x = x + gt[:, 1, None] * y
        h = modulate(x, g[2], sh[:, 2], sc[:, 2])
        x = x + 0.5 * gt[:, 2, None] * swiglu(h, p['w_ffn_in'][l, 1], p['w_ffn_out'][l, 1])
        x = rmsnorm(x, g[3])
    return x, tuple(jnp.stack(lst) for lst in lists)


def setup_inputs(seed: int = 0) -> dict:
    key = jax.random.key(seed)
    ks = iter(jax.random.split(key, 48))

    def nrm(shape, scale=1.0):
        return jax.random.normal(next(ks), shape, jnp.float32) * scale

    def gain(shape):
        return 1.0 + nrm(shape, 0.02)

    D = D_MODEL
    return {
        'x_prompt': nrm((BATCH, SEQ, D)),
        'x_sample': nrm((DEC_BATCH, DEC_SEQ, D)),
        'c_prompt': nrm((BATCH, D)),
        'c_sample': nrm((DEC_BATCH, D)),
        'cache_diff_k': nrm((N_EVEN, DEC_BATCH, PAST_LEN, N_DIFF_HEADS, 2, HEAD_DIM)),
        'cache_diff_v': nrm((N_EVEN, DEC_BATCH, PAST_LEN, N_DIFF_HEADS, DIFF_V_DIM)),
        'cache_fox_k': nrm((N_EVEN, DEC_BATCH, PAST_LEN, N_FOX_HEADS, HEAD_DIM)),
        'cache_fox_v': nrm((N_EVEN, DEC_BATCH, PAST_LEN, N_FOX_HEADS, HEAD_DIM)),
        'cache_fox_logf': jax.nn.log_sigmoid(FORGET_BIAS_MEAN + nrm((N_EVEN, DEC_BATCH, PAST_LEN, N_FOX_HEADS))),
        'cache_mla_ckv': nrm((N_ODD, DEC_BATCH, PAST_LEN, KV_LORA)),
        'cache_mla_kpe': nrm((N_ODD, DEC_BATCH, PAST_LEN, ROPE_DIM)),
        'w_ada': nrm((DEPTH, D, N_MOD * D), 0.5 * D ** -0.5),
        'b_ada': nrm((DEPTH, N_MOD * D), 0.02),
        'norm_gains': gain((DEPTH, 4, D)),
        'w_ffn_in': nrm((DEPTH, 2, D, 2 * D_FF), D ** -0.5),
        'w_ffn_out': nrm((DEPTH, 2, D_FF, D), D_FF ** -0.5),
        'w_in_even': nrm((N_EVEN, D, EVEN_IN), D ** -0.5),
        'b_forget': FORGET_BIAS_MEAN + nrm((N_EVEN, N_FOX_HEADS), 0.5),
        'qk_norm_even': gain((N_EVEN, 4, HEAD_DIM)),
        'diff_lambda': nrm((N_EVEN, 4, HEAD_DIM), 0.1),
        'diff_subln': gain((N_EVEN, DIFF_V_DIM)),
        'w_out_even': nrm((N_EVEN, EVEN_MIX, D), EVEN_MIX ** -0.5),
        'w_in_odd': nrm((N_ODD, D, ODD_IN), D ** -0.5),
        'mla_cq_norm': gain((N_ODD, Q_LORA)),
        'mla_ckv_norm': gain((N_ODD, KV_LORA)),
        'w_uq': nrm((N_ODD, Q_LORA, N_MLA_HEADS * MLA_QK_DIM), Q_LORA ** -0.5),
        'w_ukv': nrm((N_ODD, KV_LORA, N_MLA_HEADS * (NOPE_DIM + MLA_V_DIM)), KV_LORA ** -0.5),
        'mla_qk_norm_nope': gain((N_ODD, 2, NOPE_DIM)),
        'mla_qk_norm_rope': gain((N_ODD, 2, ROPE_DIM)),
        'w_out_odd': nrm((N_ODD, N_MLA_HEADS * MLA_V_DIM, D), (N_MLA_HEADS * MLA_V_DIM) ** -0.5),
    }


def reference(x_prompt, x_sample, c_prompt, c_sample,
              cache_diff_k, cache_diff_v, cache_fox_k, cache_fox_v, cache_fox_logf,
              cache_mla_ckv, cache_mla_kpe,
              w_ada, b_ada, norm_gains, w_ffn_in, w_ffn_out,
              w_in_even, b_forget, qk_norm_even, diff_lambda, diff_subln, w_out_even,
              w_in_odd, mla_cq_norm, mla_ckv_norm, w_uq, w_ukv, mla_qk_norm_nope, mla_qk_norm_rope,
              w_out_odd):
    p = {
        'w_ada': w_ada, 'b_ada': b_ada, 'norm_gains': norm_gains,
        'w_ffn_in': w_ffn_in, 'w_ffn_out': w_ffn_out,
        'w_in_even': w_in_even, 'b_forget': b_forget, 'qk_norm_even': qk_norm_even,
        'diff_lambda': diff_lambda, 'diff_subln': diff_subln, 'w_out_even': w_out_even,
        'w_in_odd': w_in_odd, 'mla_cq_norm': mla_cq_norm, 'mla_ckv_norm': mla_ckv_norm,
        'w_uq': w_uq, 'w_ukv': w_ukv, 'mla_qk_norm_nope': mla_qk_norm_nope,
        'mla_qk_norm_rope': mla_qk_norm_rope, 'w_out_odd': w_out_odd,
    }
    pos_p = jnp.arange(SEQ, dtype=jnp.int32)
    y_prompt, st_p = layer_stack(x_prompt, c_prompt, pos_p, pos_p, None, p)
    T_s = x_sample.shape[1]
    q_pos_s = PAST_LEN + jnp.arange(T_s, dtype=jnp.int32)
    k_pos_s = jnp.arange(PAST_LEN + T_s, dtype=jnp.int32)
    caches = (cache_diff_k, cache_diff_v, cache_fox_k, cache_fox_v, cache_fox_logf,
              cache_mla_ckv, cache_mla_kpe)
    y_sample, st_s = layer_stack(x_sample, c_sample, q_pos_s, k_pos_s, caches, p)
    dk_p, dv_p, fk_p, fv_p, ff_p, mc_p, mr_p = st_p
    dk_s, dv_s, fk_s, fv_s, ff_s, mc_s, mr_s = st_s
    return (y_prompt, y_sample,
            dk_p, dv_p, fk_p, fv_p, ff_p, mc_p, mr_p,
            dk_s, dv_s, fk_s, fv_s, ff_s, mc_s, mr_s)
```

```python
import functools
import math

import numpy as np
import jax
import jax.numpy as jnp
from jax import lax
from jax.experimental import pallas as pl
from jax.experimental.pallas import tpu as pltpu

F32 = jnp.float32
BF16 = jnp.bfloat16

CHUNK = 64
ROPE_THETA = 10000.0
EPS = 1e-6
NEG_INF = -1e30
N_MOD = 9

LANES = 128
SUBLANES = 8
VMEM_CAP_BYTES = 56 * 1024 * 1024

CHUNK_SHIFT = CHUNK.bit_length() - 1
assert (1 << CHUNK_SHIFT) == CHUNK


def _round_up(n, m):
    return (n + m - 1) // m * m


def _pick_tile(n, target, quantum):
    if n <= target:
        return n
    best = None
    t = quantum
    while t <= target:
        if n % t == 0:
            best = t
        t += quantum
    assert best is not None, (n, target, quantum)
    return best


def _params(semantics, vmem_bytes):
    limit = int(min(max(vmem_bytes, 16 * 1024 * 1024), VMEM_CAP_BYTES))
    return pltpu.CompilerParams(dimension_semantics=semantics, vmem_limit_bytes=limit)


def _rms(x, gain):
    return x * lax.rsqrt(jnp.mean(x * x, axis=-1, keepdims=True) + EPS) * gain


def _silu(g):
    return g / (1.0 + jnp.exp(-g))


def _dot(a, b):
    return jnp.dot(a, b, preferred_element_type=F32)


def _dot_nt(a, b):
    return lax.dot_general(a, b, (((1,), (1,)), ((), ())), preferred_element_type=F32)


def _ada_kernel(c_ref, w_ref, b_ref, o_ref):
    a = _silu(c_ref[...]).astype(BF16)
    o_ref[...] = _dot(a, w_ref[...].astype(BF16)) + b_ref[...]


def _ada(c_all, w_ada, b_ada):
    depth, d, n = w_ada.shape
    r = c_all.shape[0]
    tn = _pick_tile(n, 1024, LANES)
    return pl.pallas_call(
        _ada_kernel,
        out_shape=jax.ShapeDtypeStruct((depth, r, n), F32),
        grid=(depth, n // tn),
        in_specs=[
            pl.BlockSpec((r, d), lambda l, j: (0, 0)),
            pl.BlockSpec((None, d, tn), lambda l, j: (l, 0, j)),
            pl.BlockSpec((None, 1, tn), lambda l, j: (l, 0, j)),
        ],
        out_specs=pl.BlockSpec((None, r, tn), lambda l, j: (l, 0, j)),
        compiler_params=_params(("arbitrary", "arbitrary"), 2 * d * tn * 4 + 3 * d * tn * 2 + (4 << 20)),
        name="ada_mod",
    )(c_all, w_ada, b_ada.reshape(depth, 1, n))


def _mod_spec(mod, tm):
    d = mod.shape[-1]
    if mod.shape[1] == 1:
        return pl.BlockSpec((None, 1, d), lambda b, i, j: (b, 0, 0))
    return pl.BlockSpec((None, tm, d), lambda b, i, j: (b, i, 0))


def _ffn_kernel(x_ref, sh_ref, sc_ref, gt_ref, g_ref, wg_ref, wu_ref, wo_ref, *rest,
                gate_mul, final_norm):
    if final_norm:
        gf_ref, o_ref, h_ref, acc_ref = rest
    else:
        o_ref, h_ref, acc_ref = rest
    f = pl.program_id(2)

    @pl.when(f == 0)
    def _():
        h = _rms(x_ref[...], g_ref[...]) * (1.0 + sc_ref[...]) + sh_ref[...]
        h_ref[...] = h.astype(BF16)
        acc_ref[...] = jnp.zeros_like(acc_ref)

    h = h_ref[...]
    g = _dot(h, wg_ref[...])
    u = _dot(h, wu_ref[...])
    a = (_silu(g) * u).astype(BF16)
    acc_ref[...] += _dot(a, wo_ref[...])

    @pl.when(f == pl.num_programs(2) - 1)
    def _():
        xn = x_ref[...] + (gate_mul * gt_ref[...]) * acc_ref[...]
        if final_norm:
            xn = _rms(xn, gf_ref[...])
        o_ref[...] = xn


def _ffn(x, sh, sc, gt, gain, w_in, w_out, layer, sub, gate_mul, final_gain=None):
    b, t, d = x.shape
    ff = w_out.shape[2]
    tm = _pick_tile(t, 512, 16)
    tf = _pick_tile(ff, 512, LANES)
    nf = ff // tf
    in_specs = [
        pl.BlockSpec((None, tm, d), lambda bi, i, f: (bi, i, 0)),
        _mod_spec(sh, tm), _mod_spec(sc, tm), _mod_spec(gt, tm),
        pl.BlockSpec((1, d), lambda bi, i, f: (0, 0)),
        pl.BlockSpec((None, None, d, tf), lambda bi, i, f: (layer, sub, 0, f)),
        pl.BlockSpec((None, None, d, tf), lambda bi, i, f: (layer, sub, 0, nf + f)),
        pl.BlockSpec((None, None, tf, d), lambda bi, i, f: (layer, sub, f, 0)),
    ]
    args = [x, sh, sc, gt, gain.reshape(1, d), w_in, w_in, w_out]
    if final_gain is not None:
        in_specs.append(pl.BlockSpec((1, d), lambda bi, i, f: (0, 0)))
        args.append(final_gain.reshape(1, d))
    vmem = (4 * tm * d * 4 + tm * d * 2 + tm * d * 4 + 6 * d * tf * 2 + 4 * tm * tf * 4
            + 6 * tm * d * 4 * (sh.shape[1] != 1) + (4 << 20))
    return pl.pallas_call(
        functools.partial(_ffn_kernel, gate_mul=gate_mul, final_norm=final_gain is not None),
        out_shape=jax.ShapeDtypeStruct((b, t, d), F32),
        grid=(b, t // tm, nf),
        in_specs=in_specs,
        out_specs=pl.BlockSpec((None, tm, d), lambda bi, i, f: (bi, i, 0)),
        scratch_shapes=[pltpu.VMEM((tm, d), BF16), pltpu.VMEM((tm, d), F32)],
        compiler_params=_params(("parallel", "parallel", "arbitrary"), vmem),
        name="ffn",
    )(*args)


def _modproj_kernel(x_ref, sh_ref, sc_ref, g_ref, w_ref, o_ref, h_ref):
    @pl.when(pl.program_id(2) == 0)
    def _():
        h = _rms(x_ref[...], g_ref[...]) * (1.0 + sc_ref[...]) + sh_ref[...]
        h_ref[...] = h.astype(BF16)

    o_ref[...] = _dot(h_ref[...], w_ref[...])


def _modproj(x, sh, sc, gain, w):
    b, t, d = x.shape
    n = w.shape[1]
    tm = _pick_tile(t, 512, 16)
    tn = _pick_tile(n, 1280, LANES)
    vmem = (2 * tm * d * 4 + tm * d * 2 + 2 * d * tn * 2 + 3 * tm * tn * 4
            + 4 * tm * d * 4 * (sh.shape[1] != 1) + 3 * tm * d * 4 + (4 << 20))
    return pl.pallas_call(
        _modproj_kernel,
        out_shape=jax.ShapeDtypeStruct((b, t, n), F32),
        grid=(b, t // tm, n // tn),
        in_specs=[
            pl.BlockSpec((None, tm, d), lambda bi, i, j: (bi, i, 0)),
            _mod_spec(sh, tm), _mod_spec(sc, tm),
            pl.BlockSpec((1, d), lambda bi, i, j: (0, 0)),
            pl.BlockSpec((d, tn), lambda bi, i, j: (0, j)),
        ],
        out_specs=pl.BlockSpec((None, tm, tn), lambda bi, i, j: (bi, i, j)),
        scratch_shapes=[pltpu.VMEM((tm, d), BF16)],
        compiler_params=_params(("parallel", "parallel", "arbitrary"), vmem),
        name="mod_proj",
    )(x, sh, sc, gain.reshape(1, d), w)


def _outproj_kernel(a1_ref, a2_ref, w1_ref, w2_ref, x_ref, gt_ref, o_ref):
    y = _dot(a1_ref[...], w1_ref[...]) + _dot(a2_ref[...], w2_ref[...])
    o_ref[...] = x_ref[...] + gt_ref[...] * y


def _outproj(a1, a2, blk1, blk2, w, x, gt):
    b, t, d = x.shape
    kh = w.shape[0] // 2
    tm = _pick_tile(t, 512, 16)
    vmem = 4 * tm * kh * 2 + 4 * kh * d * 2 + 5 * tm * d * 4 + 2 * tm * d * 4 * (gt.shape[1] != 1) + (4 << 20)
    return pl.pallas_call(
        _outproj_kernel,
        out_shape=jax.ShapeDtypeStruct((b, t, d), F32),
        grid=(b, t // tm, 1),
        in_specs=[
            pl.BlockSpec((None, tm, kh), lambda bi, i, j: (bi, i, blk1)),
            pl.BlockSpec((None, tm, kh), lambda bi, i, j: (bi, i, blk2)),
            pl.BlockSpec((kh, d), lambda bi, i, j: (0, 0)),
            pl.BlockSpec((kh, d), lambda bi, i, j: (1, 0)),
            pl.BlockSpec((None, tm, d), lambda bi, i, j: (bi, i, 0)),
            _mod_spec(gt, tm),
        ],
        out_specs=pl.BlockSpec((None, tm, d), lambda bi, i, j: (bi, i, 0)),
        compiler_params=_params(("parallel", "parallel", "arbitrary"), vmem),
        name="out_proj",
    )(a1, a2, w, w, x, gt)


def _rope128(y, cos, sin):
    return y * cos + pltpu.roll(y, LANES // 2, 1) * sin


def _log_sigmoid(x):
    return jnp.minimum(x, 0.0) - jnp.log(1.0 + jnp.exp(-jnp.abs(x)))


def _even_post_kernel(z_ref, cos_ref, sin_ref, g_ref, bf_ref,
                      qa_ref, kaf_ref, kab_ref, vaf_ref, vab_ref,
                      qb_ref, kbf_ref, kbb_ref, vbf_ref, vbb_ref, lf_ref,
                      *, n_qa, n_fox, n_forget):
    cos = cos_ref[...]
    sin = sin_ref[...]
    hd = LANES
    wa = n_qa * hd
    wb = n_fox * hd
    for j in range(n_qa):
        q = _rope128(_rms(z_ref[:, j * hd:(j + 1) * hd], g_ref[0:1, :]), cos, sin)
        qa_ref[:, j * hd:(j + 1) * hd] = q.astype(BF16)
        k = _rope128(_rms(z_ref[:, wa + j * hd:wa + (j + 1) * hd], g_ref[1:2, :]), cos, sin)
        kaf_ref[:, j * hd:(j + 1) * hd] = k
        kab_ref[:, j * hd:(j + 1) * hd] = k.astype(BF16)
    va = z_ref[:, 2 * wa:3 * wa]
    vaf_ref[...] = va
    vab_ref[...] = va.astype(BF16)
    o = 3 * wa
    for j in range(n_fox):
        q = _rms(z_ref[:, o + j * hd:o + (j + 1) * hd], g_ref[2:3, :])
        qb_ref[:, j * hd:(j + 1) * hd] = q.astype(BF16)
        k = _rms(z_ref[:, o + wb + j * hd:o + wb + (j + 1) * hd], g_ref[3:4, :])
        kbf_ref[:, j * hd:(j + 1) * hd] = k
        kbb_ref[:, j * hd:(j + 1) * hd] = k.astype(BF16)
    vb = z_ref[:, o + 2 * wb:o + 3 * wb]
    vbf_ref[...] = vb
    vbb_ref[...] = vb.astype(BF16)
    fg = z_ref[:, o + 3 * wb:o + 3 * wb + LANES] + bf_ref[...]
    lane = lax.broadcasted_iota(jnp.int32, fg.shape, 1)
    lf_ref[...] = jnp.where(lane < n_forget, _log_sigmoid(fg), 0.0)


def _even_post(z, cos, sin, qk_gain, b_forget_pad, n_qa, n_fox):
    b, t, n = z.shape
    wa, wb = n_qa * LANES, n_fox * LANES
    tm = _pick_tile(t, 256, 16)
    row = lambda w: pl.BlockSpec((None, tm, w), lambda bi, i: (bi, i, 0))
    tab = pl.BlockSpec((tm, LANES), lambda bi, i: (i, 0))
    outs = [(wa, BF16), (wa, F32), (wa, BF16), (wa, F32), (wa, BF16),
            (wb, BF16), (wb, F32), (wb, BF16), (wb, F32), (wb, BF16), (LANES, F32)]
    vmem = 2 * tm * n * 4 + 2 * sum(tm * w * jnp.dtype(dt).itemsize for w, dt in outs) + (8 << 20)
    return pl.pallas_call(
        functools.partial(_even_post_kernel, n_qa=n_qa, n_fox=n_fox, n_forget=n_fox),
        out_shape=[jax.ShapeDtypeStruct((b, t, w), dt) for w, dt in outs],
        grid=(b, t // tm),
        in_specs=[row(n), tab, tab,
                  pl.BlockSpec((4, LANES), lambda bi, i: (0, 0)),
                  pl.BlockSpec((1, LANES), lambda bi, i: (0, 0))],
        out_specs=[row(w) for w, _ in outs],
        compiler_params=_params(("parallel", "parallel"), vmem),
        name="even_post",
    )(z, cos, sin, qk_gain, b_forget_pad)


def _split3(x):
    hi = x.astype(BF16)
    r = x - hi.astype(F32)
    mid = r.astype(BF16)
    lo = (r - mid.astype(F32)).astype(BF16)
    return hi, mid, lo


def _fox_prep_kernel(lf_ref, aq_ref, ak_ref, carry_ref, *, n_heads):
    @pl.when(pl.program_id(1) == 0)
    def _():
        carry_ref[...] = jnp.zeros_like(carry_ref)

    x = lf_ref[...]
    tb = x.shape[0]
    r = lax.broadcasted_iota(jnp.int32, (tb, tb), 0)
    c = lax.broadcasted_iota(jnp.int32, (tb, tb), 1)
    tri = jnp.where(r >= c, 1.0, 0.0).astype(BF16)
    hi, mid, lo = _split3(x)
    cum = _dot(tri, hi) + _dot(tri, mid) + _dot(tri, lo) + carry_ref[...]
    carry_ref[...] = cum[tb - 1:tb, :]
    fh, fm, fl = (p.astype(F32) for p in _split3(cum))
    lane = lax.broadcasted_iota(jnp.int32, (tb, LANES), 1)
    ones = jnp.where((lane >= 3) & (lane < 6), 1.0, 0.0)
    ones_k = jnp.where(lane < 3, 1.0, 0.0)
    for h in range(n_heads):
        a, m, l = fh[:, h:h + 1], fm[:, h:h + 1], fl[:, h:h + 1]
        aq = jnp.where(lane == 0, a, jnp.where(lane == 1, m, jnp.where(lane == 2, l, ones)))
        ak = jnp.where(lane == 3, -a, jnp.where(lane == 4, -m, jnp.where(lane == 5, -l, ones_k)))
        aq_ref[:, h * LANES:(h + 1) * LANES] = aq.astype(BF16)
        ak_ref[:, h * LANES:(h + 1) * LANES] = ak.astype(BF16)


def _fox_prep(logf_pad, n_heads):
    b, t, _ = logf_pad.shape
    tb = _pick_tile(t, 512, LANES)
    w = n_heads * LANES
    return pl.pallas_call(
        functools.partial(_fox_prep_kernel, n_heads=n_heads),
        out_shape=[jax.ShapeDtypeStruct((b, t, w), BF16)] * 2,
        grid=(b, t // tb),
        in_specs=[pl.BlockSpec((None, tb, LANES), lambda bi, i: (bi, i, 0))],
        out_specs=[pl.BlockSpec((None, tb, w), lambda bi, i: (bi, i, 0))] * 2,
        scratch_shapes=[pltpu.VMEM((1, LANES), F32)],
        compiler_params=_params(("parallel", "arbitrary"), 32 << 20),
        name="fox_prep",
    )(logf_pad)


FLAG_FIRST, FLAG_LAST, FLAG_MASK = 1, 2, 4


def _pair_table(n_q, n_k, tq, tk, q_off, causal):
    tk_pad = _round_up(n_k, tk)
    qi, kj, fl = [], [], []
    for i in range(n_q // tq):
        qmin, qmax = q_off + i * tq, q_off + (i + 1) * tq - 1
        row = []
        for j in range(tk_pad // tk):
            kmin, kmax = j * tk, min((j + 1) * tk, n_k) - 1
            if kmin >= n_k:
                continue
            if causal:
                any_vis, all_vis = kmin <= qmax, kmax <= qmin
            else:
                any_vis, all_vis = kmin // CHUNK <= qmax // CHUNK, kmax // CHUNK <= qmin // CHUNK
            all_vis = all_vis and (j + 1) * tk <= n_k
            if any_vis:
                row.append((j, 0 if all_vis else FLAG_MASK))
        assert row and row[0][0] == 0
        for idx, (j, f) in enumerate(row):
            qi.append(i)
            kj.append(j)
            fl.append(f | (FLAG_FIRST if idx == 0 else 0) | (FLAG_LAST if idx == len(row) - 1 else 0))
    return (jnp.asarray(np.array(qi, np.int32)), jnp.asarray(np.array(kj, np.int32)),
            jnp.asarray(np.array(fl, np.int32)))


def _visible(shape, qpos0, kpos0, n_k, causal):
    rows = qpos0 + lax.broadcasted_iota(jnp.int32, shape, 0)
    cols = kpos0 + lax.broadcasted_iota(jnp.int32, shape, 1)
    if causal:
        ok = cols <= rows
    else:
        ok = (cols >> CHUNK_SHIFT) <= (rows >> CHUNK_SHIFT)
    return ok & (cols < n_k)


def _softmax_step(s, v, m_ref, l_ref, acc_ref):
    m_prev = m_ref[...]
    m_new = jnp.maximum(m_prev, jnp.max(s, axis=-1, keepdims=True))
    alpha = jnp.exp(m_prev - m_new)
    p = jnp.exp(s - m_new)
    l_ref[...] = alpha * l_ref[...] + jnp.sum(p, axis=-1, keepdims=True)
    acc_ref[...] = alpha * acc_ref[...] + _dot(p.astype(v.dtype), v)
    m_ref[...] = m_new


def _flash_frame(qi_ref, kj_ref, fl_ref, m_ref, l_ref, acc_ref, step, finish, *, tq, tk, q_off):
    n = pl.program_id(2)
    flags = fl_ref[n]

    @pl.when((flags & FLAG_FIRST) != 0)
    def _():
        m_ref[...] = jnp.full_like(m_ref, NEG_INF)
        l_ref[...] = jnp.zeros_like(l_ref)
        acc_ref[...] = jnp.zeros_like(acc_ref)

    qpos0 = q_off + qi_ref[n] * tq
    kpos0 = kj_ref[n] * tk

    @pl.when((flags & FLAG_MASK) != 0)
    def _():
        step(True, qpos0, kpos0)

    @pl.when((flags & FLAG_MASK) == 0)
    def _():
        step(False, qpos0, kpos0)

    @pl.when((flags & FLAG_LAST) != 0)
    def _():
        finish()


def _diff_kernel(qi_ref, kj_ref, fl_ref, q_ref, k_ref, v_ref, lam_ref, sub_ref, o_ref,
                 m_ref, l_ref, acc_ref, *, tq, tk, q_off, n_k, scale, lam_init):
    hd = LANES

    def step(masked, qpos0, kpos0):
        v = v_ref[...]
        for c in range(2):
            s = _dot_nt(q_ref[:, c * hd:(c + 1) * hd], k_ref[:, c * hd:(c + 1) * hd]) * scale
            if masked:
                s = jnp.where(_visible(s.shape, qpos0, kpos0, n_k, False), s, NEG_INF)
            _softmax_step(s, v, m_ref.at[c], l_ref.at[c], acc_ref.at[c])

    def finish():
        lp = lam_ref[...]
        lam = (jnp.exp(jnp.sum(lp[0:1] * lp[1:2], axis=-1, keepdims=True))
               - jnp.exp(jnp.sum(lp[2:3] * lp[3:4], axis=-1, keepdims=True)) + lam_init)
        o = acc_ref[0] / l_ref[0] - lam * (acc_ref[1] / l_ref[1])
        o_ref[...] = (_rms(o, sub_ref[...]) * (1.0 - lam_init)).astype(o_ref.dtype)

    _flash_frame(qi_ref, kj_ref, fl_ref, m_ref, l_ref, acc_ref, step, finish, tq=tq, tk=tk, q_off=q_off)


def _fox_kernel(qi_ref, kj_ref, fl_ref, q_ref, aq_ref, k_ref, ak_ref, v_ref, o_ref,
                m_ref, l_ref, acc_ref, *, tq, tk, q_off, n_k, scale):
    def step(masked, qpos0, kpos0):
        s = _dot_nt(q_ref[...], k_ref[...]) * scale + _dot_nt(aq_ref[...], ak_ref[...])
        if masked:
            s = jnp.where(_visible(s.shape, qpos0, kpos0, n_k, True), s, NEG_INF)
        _softmax_step(s, v_ref[...], m_ref, l_ref, acc_ref)

    def finish():
        o_ref[...] = (acc_ref[...] / l_ref[...]).astype(o_ref.dtype)

    _flash_frame(qi_ref, kj_ref, fl_ref, m_ref, l_ref, acc_ref, step, finish, tq=tq, tk=tk, q_off=q_off)


def _mla_kernel(qi_ref, kj_ref, fl_ref, qn_ref, qp_ref, kn_ref, kp_ref, v_ref, o_ref,
                m_ref, l_ref, acc_ref, *, tq, tk, q_off, n_k, scale):
    def step(masked, qpos0, kpos0):
        s = (_dot_nt(qn_ref[...], kn_ref[...]) + _dot_nt(qp_ref[...], kp_ref[...])) * scale
        if masked:
            s = jnp.where(_visible(s.shape, qpos0, kpos0, n_k, False), s, NEG_INF)
        _softmax_step(s, v_ref[...], m_ref, l_ref, acc_ref)

    def finish():
        o_ref[...] = (acc_ref[...] / l_ref[...]).astype(o_ref.dtype)

    _flash_frame(qi_ref, kj_ref, fl_ref, m_ref, l_ref, acc_ref, step, finish, tq=tq, tk=tk, q_off=q_off)


def _flash_tiles(n_q, n_k_pad):
    return _pick_tile(n_q, 512, 16), _pick_tile(n_k_pad, 1536, LANES) if n_k_pad % 512 else 512


def _flash_call(kernel, n_heads, q_like, operands, out_width, scratch, n_k, q_off, causal, name, **kw):
    b, n_q = q_like.shape[0], q_like.shape[1]
    n_k_pad = next(a.shape[1] for a, _, is_q, _ in operands if not is_q)
    tq, tk = _flash_tiles(n_q, n_k_pad)
    qi, kj, fl = _pair_table(n_q, n_k, tq, tk, q_off, causal)

    def spec(width, is_q, per_head):
        t = tq if is_q else tk
        if is_q:
            imap = (lambda bi, h, n, qi, kj, fl: (bi, qi[n], h)) if per_head else (lambda bi, h, n, qi, kj, fl: (bi, qi[n], 0))
        else:
            imap = (lambda bi, h, n, qi, kj, fl: (bi, kj[n], h)) if per_head else (lambda bi, h, n, qi, kj, fl: (bi, kj[n], 0))
        return pl.BlockSpec((None, t, width), imap)

    in_specs, args = [], []
    for a, width, is_q, per_head in operands:
        if a.ndim == 2:
            in_specs.append(pl.BlockSpec(a.shape, lambda bi, h, n, qi, kj, fl: (0, 0)))
        else:
            in_specs.append(spec(width, is_q, per_head))
        args.append(a)
    vmem = 8 * tq * tk * 4 + 6 * max(tq, tk) * 2 * LANES * 2 * len(operands) + 8 * tq * 2 * LANES * 4 + (8 << 20)
    grid_spec = pltpu.PrefetchScalarGridSpec(
        num_scalar_prefetch=3,
        grid=(b, n_heads, int(qi.shape[0])),
        in_specs=in_specs,
        out_specs=pl.BlockSpec((None, tq, out_width), lambda bi, h, n, qi, kj, fl: (bi, qi[n], h)),
        scratch_shapes=scratch(tq),
    )
    return pl.pallas_call(
        functools.partial(kernel, tq=tq, tk=tk, q_off=q_off, n_k=n_k, **kw),
        out_shape=jax.ShapeDtypeStruct((b, n_q, n_heads * out_width), BF16),
        grid_spec=grid_spec,
        compiler_params=_params(("parallel", "parallel", "arbitrary"), vmem),
        name=name,
    )(qi, kj, fl, *args)


def _flash_diff(q, k, v, lam_p, subln, lam_init, n_heads, n_k, q_off):
    w = 2 * LANES
    scratch = lambda tq: [pltpu.VMEM((2, tq, 1), F32), pltpu.VMEM((2, tq, 1), F32), pltpu.VMEM((2, tq, w), F32)]
    ops = [(q, w, True, True), (k, w, False, True), (v, w, False, True),
           (lam_p, None, None, None), (subln.reshape(1, w), None, None, None)]
    return _flash_call(_diff_kernel, n_heads, q, ops, w, scratch, n_k, q_off, False, "diff_attn",
                       scale=LANES ** -0.5, lam_init=lam_init)


def _flash_fox(q, aq, k, ak, v, n_heads, n_k, q_off):
    w = LANES
    scratch = lambda tq: [pltpu.VMEM((tq, 1), F32), pltpu.VMEM((tq, 1), F32), pltpu.VMEM((tq, w), F32)]
    ops = [(q, w, True, True), (aq, w, True, True), (k, w, False, True), (ak, w, False, True), (v, w, False, True)]
    return _flash_call(_fox_kernel, n_heads, q, ops, w, scratch, n_k, q_off, True, "fox_attn", scale=LANES ** -0.5)


def _flash_mla(qn, qp, kn, kp, v, n_heads, n_k, q_off, qk_dim):
    w = LANES
    scratch = lambda tq: [pltpu.VMEM((tq, 1), F32), pltpu.VMEM((tq, 1), F32), pltpu.VMEM((tq, w), F32)]
    ops = [(qn, w, True, True), (qp, w, True, True), (kn, w, False, True), (kp, w, False, False), (v, w, False, True)]
    return _flash_call(_mla_kernel, n_heads, qn, ops, w, scratch, n_k, q_off, False, "mla_attn", scale=qk_dim ** -0.5)


def _rope_half(y, cos, sin, rope_dim):
    half = rope_dim // 2
    lane = lax.broadcasted_iota(jnp.int32, y.shape, 1)
    rot = jnp.where(lane < half, pltpu.roll(y, LANES - half, 1), pltpu.roll(y, half, 1))
    return y * cos + rot * sin


def _rms_low(x, gain, n):
    return x * lax.rsqrt(jnp.sum(x * x, axis=-1, keepdims=True) * (1.0 / n) + EPS) * gain


def _odd_post_kernel(z_ref, cos_ref, sin_ref, gq_ref, gkv_ref, gr_ref,
                     cq_ref, ckvf_ref, ckvb_ref, kpf_ref, kpb_ref, *, q_lora, kv_lora, rope_dim):
    cq_ref[...] = _rms(z_ref[:, 0:q_lora], gq_ref[...]).astype(BF16)
    ckv = _rms(z_ref[:, q_lora:q_lora + kv_lora], gkv_ref[...])
    ckvf_ref[...] = ckv
    ckvb_ref[...] = ckv.astype(BF16)
    kp = _rms_low(z_ref[:, q_lora + kv_lora:q_lora + kv_lora + LANES], gr_ref[...], rope_dim)
    kp = _rope_half(kp, cos_ref[...], sin_ref[...], rope_dim)
    kpf_ref[...] = kp
    kpb_ref[...] = kp.astype(BF16)


def _odd_post(z, cos, sin, g_cq, g_ckv, g_rope_k_pad, q_lora, kv_lora, rope_dim):
    b, t, n = z.shape
    tm = _pick_tile(t, 512, 16)
    row = lambda w: pl.BlockSpec((None, tm, w), lambda bi, i: (bi, i, 0))
    tab = pl.BlockSpec((tm, LANES), lambda bi, i: (i, 0))
    vec = lambda w: pl.BlockSpec((1, w), lambda bi, i: (0, 0))
    outs = [(q_lora, BF16), (kv_lora, F32), (kv_lora, BF16), (LANES, F32), (LANES, BF16)]
    return pl.pallas_call(
        functools.partial(_odd_post_kernel, q_lora=q_lora, kv_lora=kv_lora, rope_dim=rope_dim),
        out_shape=[jax.ShapeDtypeStruct((b, t, w), dt) for w, dt in outs],
        grid=(b, t // tm),
        in_specs=[row(n), tab, tab, vec(q_lora), vec(kv_lora), vec(LANES)],
        out_specs=[row(w) for w, _ in outs],
        compiler_params=_params(("parallel", "parallel"), 32 << 20),
        name="odd_post",
    )(z, cos, sin, g_cq.reshape(1, -1), g_ckv.reshape(1, -1), g_rope_k_pad)


def _qup_kernel(cq_ref, w_ref, cos_ref, sin_ref, gn_ref, gr_ref, qn_ref, qp_ref, *, heads, rope_dim):
    q = _dot(cq_ref[...], w_ref[...])
    cos, sin = cos_ref[...], sin_ref[...]
    for h in range(heads):
        qn = _rms(q[:, 2 * h * LANES:(2 * h + 1) * LANES], gn_ref[...])
        qn_ref[:, h * LANES:(h + 1) * LANES] = qn.astype(BF16)
        qp = _rms_low(q[:, (2 * h + 1) * LANES:(2 * h + 2) * LANES], gr_ref[...], rope_dim)
        qp_ref[:, h * LANES:(h + 1) * LANES] = _rope_half(qp, cos, sin, rope_dim).astype(BF16)


def _qup(cq, w_pad, cos, sin, g_nope_q, g_rope_q_pad, n_heads, rope_dim):
    b, t, kq = cq.shape
    tm = _pick_tile(t, 512, 16)
    hg = 4 if n_heads % 4 == 0 else 1
    tn = hg * 2 * LANES
    out = pl.BlockSpec((None, tm, hg * LANES), lambda bi, i, j: (bi, i, j))
    return pl.pallas_call(
        functools.partial(_qup_kernel, heads=hg, rope_dim=rope_dim),
        out_shape=[jax.ShapeDtypeStruct((b, t, n_heads * LANES), BF16)] * 2,
        grid=(b, t // tm, n_heads // hg),
        in_specs=[
            pl.BlockSpec((None, tm, kq), lambda bi, i, j: (bi, i, 0)),
            pl.BlockSpec((kq, tn), lambda bi, i, j: (0, j)),
            pl.BlockSpec((tm, LANES), lambda bi, i, j: (i, 0)),
            pl.BlockSpec((tm, LANES), lambda bi, i, j: (i, 0)),
            pl.BlockSpec((1, LANES), lambda bi, i, j: (0, 0)),
            pl.BlockSpec((1, LANES), lambda bi, i, j: (0, 0)),
        ],
        out_specs=[out, out],
        compiler_params=_params(("parallel", "parallel", "arbitrary"), 32 << 20),
        name="mla_q_up",
    )(cq, w_pad, cos, sin, g_nope_q.reshape(1, LANES), g_rope_q_pad)


def _kvup_kernel(ckv_ref, w_ref, gn_ref, kn_ref, v_ref, *, heads):
    kv = _dot(ckv_ref[...], w_ref[...])
    for h in range(heads):
        kn = _rms(kv[:, 2 * h * LANES:(2 * h + 1) * LANES], gn_ref[...])
        kn_ref[:, h * LANES:(h + 1) * LANES] = kn.astype(BF16)
        v_ref[:, h * LANES:(h + 1) * LANES] = kv[:, (2 * h + 1) * LANES:(2 * h + 2) * LANES].astype(BF16)


def _kvup(ckv, w, g_nope_k, n_heads):
    b, t, kk = ckv.shape
    tm = _pick_tile(t, 512, LANES)
    hg = 4 if n_heads % 4 == 0 else 1
    tn = hg * 2 * LANES
    out = pl.BlockSpec((None, tm, hg * LANES), lambda bi, i, j: (bi, i, j))
    return pl.pallas_call(
        functools.partial(_kvup_kernel, heads=hg),
        out_shape=[jax.ShapeDtypeStruct((b, t, n_heads * LANES), BF16)] * 2,
        grid=(b, t // tm, n_heads // hg),
        in_specs=[
            pl.BlockSpec((None, tm, kk), lambda bi, i, j: (bi, i, 0)),
            pl.BlockSpec((kk, tn), lambda bi, i, j: (0, j)),
            pl.BlockSpec((1, LANES), lambda bi, i, j: (0, 0)),
        ],
        out_specs=[out, out],
        compiler_params=_params(("parallel", "parallel", "arbitrary"), 32 << 20),
        name="mla_kv_up",
    )(ckv, w, g_nope_k.reshape(1, LANES))


def _rope_tables(pos, dim):
    half = dim // 2
    inv = ROPE_THETA ** (-jnp.arange(half, dtype=F32) * 2.0 / dim)
    ang = pos.astype(F32)[:, None] * inv[None, :]
    cos, sin = jnp.cos(ang), jnp.sin(ang)
    pad = ((0, 0), (0, LANES - dim))
    return (jnp.pad(jnp.concatenate([cos, cos], axis=-1), pad),
            jnp.pad(jnp.concatenate([-sin, sin], axis=-1), pad))


def _pad_lanes(a, width):
    return jnp.pad(a, [(0, 0)] * (a.ndim - 1) + [(0, width - a.shape[-1])])


def _pad_rows(a, rows):
    return jnp.pad(a, [(0, 0), (0, rows - a.shape[1])] + [(0, 0)] * (a.ndim - 2))


def _with_past(past, new_bf, n_k_pad):
    if past is None:
        return new_bf
    b, p = past.shape[0], past.shape[1]
    allk = jnp.concatenate([past.reshape(b, p, -1).astype(BF16), new_bf], axis=1)
    return _pad_rows(allk, n_k_pad)


def _layer_stack(x, c_mod, tok_pos, seq_shape, caches, p):
    bx, tx, d = x.shape
    ba, ta = seq_shape
    depth = p['w_ffn_in'].shape[0]
    n_diff = p['n_diff']
    n_fox = p['n_fox']
    n_mla = p['n_mla']
    past_len = 0 if caches is None else caches[0].shape[2]
    n_k = past_len + ta
    n_k_pad = n_k if caches is None else _round_up(n_k, LANES)
    cos128, sin128 = _rope_tables(tok_pos, LANES)
    rope_dim = p['rope_dim']
    cos_r, sin_r = _rope_tables(tok_pos, rope_dim)
    new = [[] for _ in range(7)]

    def mods(l, s):
        m = c_mod[l]
        sh, sc, gt = m[:, 3 * s], m[:, 3 * s + 1], m[:, 3 * s + 2]
        if bx == m.shape[0]:
            return tuple(a[:, None, :] for a in (sh, sc, gt))
        rep = lambda a: jnp.repeat(a, ta, axis=0).reshape(bx, tx, d)
        return rep(sh), rep(sc), rep(gt)

    seq = lambda a: a.reshape(ba, ta, a.shape[-1])
    for l in range(depth):
        i = l // 2
        g = p['norm_gains'][l]
        sh, sc, gt = mods(l, 0)
        x = _ffn(x, sh, sc, gt, g[0], p['w_ffn_in'], p['w_ffn_out'], l, 0, 0.5)
        sh, sc, gt = mods(l, 1)
        if l % 2 == 0:
            z = _modproj(x, sh, sc, g[1], p['w_in_even'][i])
            (qa, kaf, kab, vaf, vab, qb, kbf, kbb, vbf, vbb, lf) = _even_post(
                z, cos128, sin128, p['qk_norm_even'][i], p['b_forget_pad'][i], 2 * n_diff, n_fox)
            new[0].append(kaf.reshape(ba, ta, n_diff, 2, LANES))
            new[1].append(vaf.reshape(ba, ta, n_diff, 2 * LANES))
            new[2].append(kbf.reshape(ba, ta, n_fox, LANES))
            new[3].append(vbf.reshape(ba, ta, n_fox, LANES))
            new[4].append(seq(lf)[:, :, :n_fox])
            past = (None,) * 5 if caches is None else tuple(a[i] for a in caches[:5])
            ka_all = _with_past(past[0], seq(kab), n_k_pad)
            va_all = _with_past(past[1], seq(vab), n_k_pad)
            kb_all = _with_past(past[2], seq(kbb), n_k_pad)
            vb_all = _with_past(past[3], seq(vbb), n_k_pad)
            lf_all = seq(lf)
            if caches is not None:
                lf_all = _pad_rows(jnp.concatenate([_pad_lanes(past[4].astype(F32), LANES), lf_all], axis=1), n_k_pad)
            aq, ak = _fox_prep(lf_all, n_fox)
            aq = aq[:, past_len:past_len + ta]
            lam_init = 0.8 - 0.6 * math.exp(-0.3 * l)
            oa = _flash_diff(seq(qa), ka_all, va_all, p['diff_lambda'][i], p['diff_subln'][i], lam_init,
                             n_diff, n_k, past_len)
            ob = _flash_fox(seq(qb), aq, kb_all, ak, vb_all, n_fox, n_k, past_len)
            x = _outproj(oa.reshape(bx, tx, -1), ob.reshape(bx, tx, -1), 0, 0, p['w_out_even'][i], x, gt)
        else:
            q_lora, kv_lora = p['q_lora'], p['kv_lora']
            z = _modproj(x, sh, sc, g[1], p['w_in_odd'][i])
            cq, ckvf, ckvb, kpf, kpb = _odd_post(z, cos_r, sin_r, p['mla_cq_norm'][i], p['mla_ckv_norm'][i],
                                                 p['g_rope_pad'][i, 1:2], q_lora, kv_lora, rope_dim)
            new[5].append(seq(ckvf))
            new[6].append(seq(kpf)[:, :, :rope_dim])
            qn, qp = _qup(cq, p['w_uq_pad'][i], cos_r, sin_r, p['mla_qk_norm_nope'][i, 0],
                          p['g_rope_pad'][i, 0:1], n_mla, rope_dim)
            past = (None,) * 2 if caches is None else tuple(a[i] for a in caches[5:])
            ckv_all = _with_past(past[0], seq(ckvb), n_k_pad)
            kp_all = seq(kpb)
            if caches is not None:
                kp_all = _pad_rows(jnp.concatenate([_pad_lanes(past[1], LANES).astype(BF16), kp_all], axis=1), n_k_pad)
            kn, v = _kvup(ckv_all, p['w_ukv'][i], p['mla_qk_norm_nope'][i, 1], n_mla)
            o = _flash_mla(seq(qn), seq(qp), kn, kp_all, v, n_mla, n_k, past_len, p['mla_qk_dim'])
            o = o.reshape(bx, tx, -1)
            x = _outproj(o, o, 0, 1, p['w_out_odd'][i], x, gt)
        sh, sc, gt = mods(l, 2)
        x = _ffn(x, sh, sc, gt, g[2], p['w_ffn_in'], p['w_ffn_out'], l, 1, 0.5, final_gain=g[3])
    return x, tuple(jnp.stack(lst) for lst in new)


def kernel(x_prompt, x_sample, c_prompt, c_sample, cache_diff_k, cache_diff_v, cache_fox_k, cache_fox_v, cache_fox_logf, cache_mla_ckv, cache_mla_kpe, w_ada, b_ada, norm_gains, w_ffn_in, w_ffn_out, w_in_even, b_forget, qk_norm_even, diff_lambda, diff_subln, w_out_even, w_in_odd, mla_cq_norm, mla_ckv_norm, w_uq, w_ukv, mla_qk_norm_nope, mla_qk_norm_rope, w_out_odd):
    d = x_prompt.shape[-1]
    n_diff, n_fox = cache_diff_k.shape[3], cache_fox_k.shape[3]
    assert cache_diff_k.shape[-1] == LANES and cache_fox_k.shape[-1] == LANES
    q_lora, kv_lora = mla_cq_norm.shape[-1], mla_ckv_norm.shape[-1]
    rope_dim, nope = cache_mla_kpe.shape[-1], mla_qk_norm_nope.shape[-1]
    n_mla = w_uq.shape[-1] // (nope + rope_dim)
    assert nope == LANES and rope_dim <= LANES and w_ukv.shape[-1] == n_mla * 2 * LANES
    n_odd = w_uq.shape[0]

    w_uq_pad = _pad_lanes(w_uq.reshape(n_odd, q_lora, n_mla, nope + rope_dim), 2 * LANES)
    p = {
        'n_diff': n_diff, 'n_fox': n_fox, 'n_mla': n_mla, 'rope_dim': rope_dim,
        'q_lora': q_lora, 'kv_lora': kv_lora, 'mla_qk_dim': nope + rope_dim,
        'norm_gains': norm_gains,
        'w_ffn_in': w_ffn_in.astype(BF16), 'w_ffn_out': w_ffn_out.astype(BF16),
        'w_in_even': _pad_lanes(w_in_even, _round_up(w_in_even.shape[-1], LANES)).astype(BF16),
        'b_forget_pad': _pad_lanes(b_forget, LANES)[:, None, :],
        'qk_norm_even': qk_norm_even, 'diff_lambda': diff_lambda, 'diff_subln': diff_subln,
        'w_out_even': w_out_even.astype(BF16),
        'w_in_odd': _pad_lanes(w_in_odd, q_lora + kv_lora + LANES).astype(BF16),
        'mla_cq_norm': mla_cq_norm, 'mla_ckv_norm': mla_ckv_norm,
        'w_uq_pad': w_uq_pad.reshape(n_odd, q_lora, n_mla * 2 * LANES).astype(BF16),
        'w_ukv': w_ukv.astype(BF16),
        'mla_qk_norm_nope': mla_qk_norm_nope,
        'g_rope_pad': _pad_lanes(mla_qk_norm_rope, LANES),
        'w_out_odd': w_out_odd.astype(BF16),
    }

    bp, tp = x_prompt.shape[:2]
    bs, ts = x_sample.shape[:2]
    past_len = cache_diff_k.shape[2]
    mod = _ada(jnp.concatenate([c_prompt, c_sample], axis=0), w_ada, b_ada)
    mod = mod.reshape(mod.shape[0], bp + bs, N_MOD, d)

    pos_p = jnp.arange(tp, dtype=jnp.int32)
    y_prompt, st_p = _layer_stack(x_prompt, mod[:, :bp], pos_p, (bp, tp), None, p)

    pos_s = jnp.tile(past_len + jnp.arange(ts, dtype=jnp.int32), bs)
    caches = (cache_diff_k, cache_diff_v, cache_fox_k, cache_fox_v, cache_fox_logf, cache_mla_ckv, cache_mla_kpe)
    y_sample, st_s = _layer_stack(x_sample.reshape(1, bs * ts, d), mod[:, bp:], pos_s, (bs, ts), caches, p)
    return (y_prompt, y_sample.reshape(bs, ts, d)) + st_p + st_s
```

```python
import functools
import math

import numpy as np
import jax
import jax.numpy as jnp
from jax import lax
from jax.experimental import pallas as pl
from jax.experimental.pallas import tpu as pltpu

F32 = jnp.float32
BF16 = jnp.bfloat16

CHUNK = 64
ROPE_THETA = 10000.0
EPS = 1e-6
NEG_INF = -1e30
N_MOD = 9

LANES = 128
SUBLANES = 8
VMEM_CAP_BYTES = 56 * 1024 * 1024

LOG2E = math.log2(math.e)
CHUNK_SHIFT = CHUNK.bit_length() - 1
assert (1 << CHUNK_SHIFT) == CHUNK


def _round_up(n, m):
    return (n + m - 1) // m * m


def _pick_tile(n, target, quantum):
    if n <= target:
        return n
    best = None
    t = quantum
    while t <= target:
        if n % t == 0:
            best = t
        t += quantum
    assert best is not None, (n, target, quantum)
    return best


def _params(semantics, vmem_bytes):
    limit = int(min(max(vmem_bytes, 16 * 1024 * 1024), VMEM_CAP_BYTES))
    return pltpu.CompilerParams(dimension_semantics=semantics, vmem_limit_bytes=limit)


def _rms(x, gain):
    return x * lax.rsqrt(jnp.mean(x * x, axis=-1, keepdims=True) + EPS) * gain


def _silu(g):
    return g / (1.0 + jnp.exp(-g))


def _dot(a, b):
    return jnp.dot(a, b, preferred_element_type=F32)


def _dot_nt(a, b):
    return lax.dot_general(a, b, (((1,), (1,)), ((), ())), preferred_element_type=F32)


def _ada_kernel(c_ref, w_ref, b_ref, o_ref):
    a = _silu(c_ref[...]).astype(BF16)
    o_ref[...] = _dot(a, w_ref[...].astype(BF16)) + b_ref[...]


def _ada(c_all, w_ada, b_ada):
    depth, d, n = w_ada.shape
    r = c_all.shape[0]
    tn = _pick_tile(n, 1024, LANES)
    return pl.pallas_call(
        _ada_kernel,
        out_shape=jax.ShapeDtypeStruct((depth, r, n), F32),
        grid=(depth, n // tn),
        in_specs=[
            pl.BlockSpec((r, d), lambda l, j: (0, 0)),
            pl.BlockSpec((None, d, tn), lambda l, j: (l, 0, j)),
            pl.BlockSpec((None, 1, tn), lambda l, j: (l, 0, j)),
        ],
        out_specs=pl.BlockSpec((None, r, tn), lambda l, j: (l, 0, j)),
        compiler_params=_params(("arbitrary", "arbitrary"), 2 * d * tn * 4 + 3 * d * tn * 2 + (4 << 20)),
        name="ada_mod",
    )(c_all, w_ada, b_ada.reshape(depth, 1, n))


def _mod_spec(mod, tm):
    d = mod.shape[-1]
    if mod.shape[1] == 1:
        return pl.BlockSpec((None, 1, d), lambda b, i, j: (b, 0, 0))
    return pl.BlockSpec((None, tm, d), lambda b, i, j: (b, i, 0))


def _ffn_kernel(x_ref, sh_ref, sc_ref, gt_ref, g_ref, wg_ref, wu_ref, wo_ref, *rest,
                gate_mul, final_norm):
    if final_norm:
        gf_ref, o_ref, h_ref, acc_ref = rest
    else:
        o_ref, h_ref, acc_ref = rest
    f = pl.program_id(2)

    @pl.when(f == 0)
    def _():
        h = _rms(x_ref[...], g_ref[...]) * (1.0 + sc_ref[...]) + sh_ref[...]
        h_ref[...] = h.astype(BF16)
        acc_ref[...] = jnp.zeros_like(acc_ref)

    h = h_ref[...]
    g = _dot(h, wg_ref[...])
    u = _dot(h, wu_ref[...])
    a = (_silu(g) * u).astype(BF16)
    acc_ref[...] += _dot(a, wo_ref[...])

    @pl.when(f == pl.num_programs(2) - 1)
    def _():
        xn = x_ref[...] + (gate_mul * gt_ref[...]) * acc_ref[...]
        if final_norm:
            xn = _rms(xn, gf_ref[...])
        o_ref[...] = xn


def _ffn(x, sh, sc, gt, gain, w_in, w_out, layer, sub, gate_mul, final_gain=None):
    b, t, d = x.shape
    ff = w_out.shape[2]
    tm = _pick_tile(t, 512, 16)
    tf = _pick_tile(ff, 512, LANES)
    nf = ff // tf
    in_specs = [
        pl.BlockSpec((None, tm, d), lambda bi, i, f: (bi, i, 0)),
        _mod_spec(sh, tm), _mod_spec(sc, tm), _mod_spec(gt, tm),
        pl.BlockSpec((1, d), lambda bi, i, f: (0, 0)),
        pl.BlockSpec((None, None, d, tf), lambda bi, i, f: (layer, sub, 0, f)),
        pl.BlockSpec((None, None, d, tf), lambda bi, i, f: (layer, sub, 0, nf + f)),
        pl.BlockSpec((None, None, tf, d), lambda bi, i, f: (layer, sub, f, 0)),
    ]
    args = [x, sh, sc, gt, gain.reshape(1, d), w_in, w_in, w_out]
    if final_gain is not None:
        in_specs.append(pl.BlockSpec((1, d), lambda bi, i, f: (0, 0)))
        args.append(final_gain.reshape(1, d))
    vmem = (4 * tm * d * 4 + tm * d * 2 + tm * d * 4 + 6 * d * tf * 2 + 4 * tm * tf * 4
            + 6 * tm * d * 4 * (sh.shape[1] != 1) + (4 << 20))
    return pl.pallas_call(
        functools.partial(_ffn_kernel, gate_mul=gate_mul, final_norm=final_gain is not None),
        out_shape=jax.ShapeDtypeStruct((b, t, d), F32),
        grid=(b, t // tm, nf),
        in_specs=in_specs,
        out_specs=pl.BlockSpec((None, tm, d), lambda bi, i, f: (bi, i, 0)),
        scratch_shapes=[pltpu.VMEM((tm, d), BF16), pltpu.VMEM((tm, d), F32)],
        compiler_params=_params(("parallel", "parallel", "arbitrary"), vmem),
        name="ffn",
    )(*args)


def _modproj_kernel(x_ref, sh_ref, sc_ref, g_ref, w_ref, o_ref, h_ref):
    @pl.when(pl.program_id(2) == 0)
    def _():
        h = _rms(x_ref[...], g_ref[...]) * (1.0 + sc_ref[...]) + sh_ref[...]
        h_ref[...] = h.astype(BF16)

    o_ref[...] = _dot(h_ref[...], w_ref[...])


def _modproj(x, sh, sc, gain, w):
    b, t, d = x.shape
    n = w.shape[1]
    tm = _pick_tile(t, 512, 16)
    tn = _pick_tile(n, 1280, LANES)
    vmem = (2 * tm * d * 4 + tm * d * 2 + 2 * d * tn * 2 + 3 * tm * tn * 4
            + 4 * tm * d * 4 * (sh.shape[1] != 1) + 3 * tm * d * 4 + (4 << 20))
    return pl.pallas_call(
        _modproj_kernel,
        out_shape=jax.ShapeDtypeStruct((b, t, n), F32),
        grid=(b, t // tm, n // tn),
        in_specs=[
            pl.BlockSpec((None, tm, d), lambda bi, i, j: (bi, i, 0)),
            _mod_spec(sh, tm), _mod_spec(sc, tm),
            pl.BlockSpec((1, d), lambda bi, i, j: (0, 0)),
            pl.BlockSpec((d, tn), lambda bi, i, j: (0, j)),
        ],
        out_specs=pl.BlockSpec((None, tm, tn), lambda bi, i, j: (bi, i, j)),
        scratch_shapes=[pltpu.VMEM((tm, d), BF16)],
        compiler_params=_params(("parallel", "parallel", "arbitrary"), vmem),
        name="mod_proj",
    )(x, sh, sc, gain.reshape(1, d), w)


def _outproj_kernel(a1_ref, a2_ref, w1_ref, w2_ref, x_ref, gt_ref, o_ref):
    y = _dot(a1_ref[...], w1_ref[...]) + _dot(a2_ref[...], w2_ref[...])
    o_ref[...] = x_ref[...] + gt_ref[...] * y


def _outproj(a1, a2, blk1, blk2, w, x, gt):
    b, t, d = x.shape
    kh = w.shape[0] // 2
    tm = _pick_tile(t, 512, 16)
    vmem = 4 * tm * kh * 2 + 4 * kh * d * 2 + 5 * tm * d * 4 + 2 * tm * d * 4 * (gt.shape[1] != 1) + (4 << 20)
    return pl.pallas_call(
        _outproj_kernel,
        out_shape=jax.ShapeDtypeStruct((b, t, d), F32),
        grid=(b, t // tm, 1),
        in_specs=[
            pl.BlockSpec((None, tm, kh), lambda bi, i, j: (bi, i, blk1)),
            pl.BlockSpec((None, tm, kh), lambda bi, i, j: (bi, i, blk2)),
            pl.BlockSpec((kh, d), lambda bi, i, j: (0, 0)),
            pl.BlockSpec((kh, d), lambda bi, i, j: (1, 0)),
            pl.BlockSpec((None, tm, d), lambda bi, i, j: (bi, i, 0)),
            _mod_spec(gt, tm),
        ],
        out_specs=pl.BlockSpec((None, tm, d), lambda bi, i, j: (bi, i, 0)),
        compiler_params=_params(("parallel", "parallel", "arbitrary"), vmem),
        name="out_proj",
    )(a1, a2, w, w, x, gt)


def _rope128(y, cos, sin):
    return y * cos + pltpu.roll(y, LANES // 2, 1) * sin


def _log_sigmoid(x):
    return jnp.minimum(x, 0.0) - jnp.log(1.0 + jnp.exp(-jnp.abs(x)))


def _even_post_kernel(z_ref, cos_ref, sin_ref, g_ref, bf_ref,
                      qa_ref, kaf_ref, kab_ref, vaf_ref, vab_ref,
                      qb_ref, kbf_ref, kbb_ref, vbf_ref, vbb_ref, lf_ref,
                      *, n_qa, n_fox, n_forget, q_scale):
    cos = cos_ref[...]
    sin = sin_ref[...]
    hd = LANES
    wa = n_qa * hd
    wb = n_fox * hd
    for j in range(n_qa):
        q = _rope128(_rms(z_ref[:, j * hd:(j + 1) * hd], g_ref[0:1, :]), cos, sin)
        qa_ref[:, j * hd:(j + 1) * hd] = (q * q_scale).astype(BF16)
        k = _rope128(_rms(z_ref[:, wa + j * hd:wa + (j + 1) * hd], g_ref[1:2, :]), cos, sin)
        kaf_ref[:, j * hd:(j + 1) * hd] = k
        kab_ref[:, j * hd:(j + 1) * hd] = k.astype(BF16)
    va = z_ref[:, 2 * wa:3 * wa]
    vaf_ref[...] = va
    vab_ref[...] = va.astype(BF16)
    o = 3 * wa
    for j in range(n_fox):
        q = _rms(z_ref[:, o + j * hd:o + (j + 1) * hd], g_ref[2:3, :])
        qb_ref[:, j * hd:(j + 1) * hd] = (q * q_scale).astype(BF16)
        k = _rms(z_ref[:, o + wb + j * hd:o + wb + (j + 1) * hd], g_ref[3:4, :])
        kbf_ref[:, j * hd:(j + 1) * hd] = k
        kbb_ref[:, j * hd:(j + 1) * hd] = k.astype(BF16)
    vb = z_ref[:, o + 2 * wb:o + 3 * wb]
    vbf_ref[...] = vb
    vbb_ref[...] = vb.astype(BF16)
    fg = z_ref[:, o + 3 * wb:o + 3 * wb + LANES] + bf_ref[...]
    lane = lax.broadcasted_iota(jnp.int32, fg.shape, 1)
    lf_ref[...] = jnp.where(lane < n_forget, _log_sigmoid(fg), 0.0)


def _even_post(z, cos, sin, qk_gain, b_forget_pad, n_qa, n_fox):
    b, t, n = z.shape
    wa, wb = n_qa * LANES, n_fox * LANES
    tm = _pick_tile(t, 256, 16)
    row = lambda w: pl.BlockSpec((None, tm, w), lambda bi, i: (bi, i, 0))
    tab = pl.BlockSpec((tm, LANES), lambda bi, i: (i, 0))
    outs = [(wa, BF16), (wa, F32), (wa, BF16), (wa, F32), (wa, BF16),
            (wb, BF16), (wb, F32), (wb, BF16), (wb, F32), (wb, BF16), (LANES, F32)]
    vmem = 2 * tm * n * 4 + 2 * sum(tm * w * jnp.dtype(dt).itemsize for w, dt in outs) + (8 << 20)
    return pl.pallas_call(
        functools.partial(_even_post_kernel, n_qa=n_qa, n_fox=n_fox, n_forget=n_fox,
                          q_scale=LANES ** -0.5 * LOG2E),
        out_shape=[jax.ShapeDtypeStruct((b, t, w), dt) for w, dt in outs],
        grid=(b, t // tm),
        in_specs=[row(n), tab, tab,
                  pl.BlockSpec((4, LANES), lambda bi, i: (0, 0)),
                  pl.BlockSpec((1, LANES), lambda bi, i: (0, 0))],
        out_specs=[row(w) for w, _ in outs],
        compiler_params=_params(("parallel", "parallel"), vmem),
        name="even_post",
    )(z, cos, sin, qk_gain, b_forget_pad)


def _split3(x):
    hi = x.astype(BF16)
    r = x - hi.astype(F32)
    mid = r.astype(BF16)
    lo = (r - mid.astype(F32)).astype(BF16)
    return hi, mid, lo


def _fox_prep_kernel(lf_ref, k_ref, aq_ref, kk_ref, carry_ref, *, n_heads):
    @pl.when(pl.program_id(1) == 0)
    def _():
        carry_ref[...] = jnp.zeros_like(carry_ref)

    x = lf_ref[...]
    tb = x.shape[0]
    r = lax.broadcasted_iota(jnp.int32, (tb, tb), 0)
    c = lax.broadcasted_iota(jnp.int32, (tb, tb), 1)
    tri = jnp.where(r >= c, 1.0, 0.0).astype(BF16)
    hi, mid, lo = _split3(x)
    cum = _dot(tri, hi) + _dot(tri, mid) + _dot(tri, lo) + carry_ref[...]
    carry_ref[...] = cum[tb - 1:tb, :]
    fh, fm, fl = (p.astype(F32) for p in _split3(cum * LOG2E))
    lane = lax.broadcasted_iota(jnp.int32, (tb, LANES), 1)
    ones_q = jnp.where((lane >= 3) & (lane < 6), 1.0, 0.0)
    ones_k = jnp.where(lane < 3, 1.0, 0.0)
    for h in range(n_heads):
        a, m, l = fh[:, h:h + 1], fm[:, h:h + 1], fl[:, h:h + 1]
        aq = jnp.where(lane == 0, a, jnp.where(lane == 1, m, jnp.where(lane == 2, l, ones_q)))
        ak = jnp.where(lane == 3, -a, jnp.where(lane == 4, -m, jnp.where(lane == 5, -l, ones_k)))
        aq_ref[:, h * LANES:(h + 1) * LANES] = aq.astype(BF16)
        kk_ref[:, 2 * h * LANES:(2 * h + 1) * LANES] = k_ref[:, h * LANES:(h + 1) * LANES]
        kk_ref[:, (2 * h + 1) * LANES:(2 * h + 2) * LANES] = ak.astype(BF16)


def _fox_prep(logf_pad, k_all, n_heads):
    b, t, _ = logf_pad.shape
    tb = _pick_tile(t, 512, LANES)
    w = n_heads * LANES
    return pl.pallas_call(
        functools.partial(_fox_prep_kernel, n_heads=n_heads),
        out_shape=[jax.ShapeDtypeStruct((b, t, w), BF16), jax.ShapeDtypeStruct((b, t, 2 * w), BF16)],
        grid=(b, t // tb),
        in_specs=[pl.BlockSpec((None, tb, LANES), lambda bi, i: (bi, i, 0)),
                  pl.BlockSpec((None, tb, w), lambda bi, i: (bi, i, 0))],
        out_specs=[pl.BlockSpec((None, tb, w), lambda bi, i: (bi, i, 0)),
                   pl.BlockSpec((None, tb, 2 * w), lambda bi, i: (bi, i, 0))],
        scratch_shapes=[pltpu.VMEM((1, LANES), F32)],
        compiler_params=_params(("parallel", "arbitrary"), 32 << 20),
        name="fox_prep",
    )(logf_pad, k_all)


FLAG_FIRST, FLAG_LAST = 1, 2
KIND_SHIFT = 2
KIND_FULL, KIND_MASK, KIND_DIAG = 0, 1, 2


def _pair_table(n_q, n_k, tq, tk, q_off, causal):
    tk_pad = _round_up(n_k, tk)
    aligned = tq == tk and q_off % tq == 0
    qi, kj, fl = [], [], []
    for i in range(n_q // tq):
        qmin, qmax = q_off + i * tq, q_off + (i + 1) * tq - 1
        row = []
        for j in range(tk_pad // tk):
            kmin, kmax = j * tk, min((j + 1) * tk, n_k) - 1
            if kmin >= n_k:
                continue
            if causal:
                any_vis, all_vis = kmin <= qmax, kmax <= qmin
            else:
                any_vis, all_vis = kmin // CHUNK <= qmax // CHUNK, kmax // CHUNK <= qmin // CHUNK
            all_vis = all_vis and (j + 1) * tk <= n_k
            if any_vis:
                diag = aligned and kmin == qmin and (j + 1) * tk <= n_k
                row.append((j, KIND_FULL if all_vis else KIND_DIAG if diag else KIND_MASK))
        assert row and row[0][0] == 0
        for idx, (j, kind) in enumerate(row):
            qi.append(i)
            kj.append(j)
            fl.append((kind << KIND_SHIFT) | (FLAG_FIRST if idx == 0 else 0) | (FLAG_LAST if idx == len(row) - 1 else 0))
    kinds = sorted({f >> KIND_SHIFT for f in fl})
    as_arr = lambda v: jnp.asarray(np.array(v, np.int32))
    return as_arr(qi), as_arr(kj), as_arr(fl), kinds


def _visible(shape, qpos0, kpos0, n_k, causal):
    rows = qpos0 + lax.broadcasted_iota(jnp.int32, shape, 0)
    cols = kpos0 + lax.broadcasted_iota(jnp.int32, shape, 1)
    if causal:
        ok = cols <= rows
    else:
        ok = (cols >> CHUNK_SHIFT) <= (rows >> CHUNK_SHIFT)
    return ok & (cols < n_k)


def _lane_tile(x, n):
    return x if n == LANES else jnp.concatenate([x] * (n // LANES), axis=1)


def _lane_fold(p):
    acc = p[:, 0:LANES]
    for c in range(1, p.shape[1] // LANES):
        acc = acc + p[:, c * LANES:(c + 1) * LANES]
    return acc


def _attend(s, v, m_ref, l_ref, acc_ref, idx):
    m_prev = m_ref[idx]
    m_new = jnp.maximum(m_prev, jnp.max(s, axis=1, keepdims=True))
    alpha = jnp.exp2(m_prev - m_new)
    p = jnp.exp2(s - _lane_tile(m_new, s.shape[1]))
    l_ref[idx] = alpha * l_ref[idx] + _lane_fold(p)
    acc_ref[idx] = _lane_tile(alpha, v.shape[1]) * acc_ref[idx] + _dot(p.astype(v.dtype), v)
    m_ref[idx] = m_new


def _flash_frame(qi_ref, kj_ref, fl_ref, m_ref, l_ref, acc_ref, step, finish, first=None, *,
                 tq, tk, rs, q_off, n_k, causal, kinds):
    n = pl.program_id(2)
    flags = fl_ref[n]
    kind = flags >> KIND_SHIFT

    @pl.when((flags & FLAG_FIRST) != 0)
    def _():
        m_ref[...] = jnp.full_like(m_ref, NEG_INF)
        l_ref[...] = jnp.zeros_like(l_ref)
        acc_ref[...] = jnp.zeros_like(acc_ref)
        if first is not None:
            first()

    qpos0 = q_off + qi_ref[n] * tq
    kpos0 = kj_ref[n] * tk

    def run(k):
        for r in range(tq // rs):
            n_keys = (r + 1) * rs if k == KIND_DIAG else tk
            if k == KIND_FULL:
                mask_fn = None
            else:
                mask_fn = lambda s, r=r: jnp.where(_visible(s.shape, qpos0 + r * rs, kpos0, n_k, causal), s, NEG_INF)
            step(r, n_keys, mask_fn)

    for k in kinds:
        pl.when(kind == k)(functools.partial(run, k))

    @pl.when((flags & FLAG_LAST) != 0)
    def _():
        finish()


def _diff_kernel(qi_ref, kj_ref, fl_ref, q_ref, k_ref, v_ref, lam_ref, sub_ref, o_ref,
                 m_ref, l_ref, acc_ref, *, rs, lam_init, **frame):
    hd = LANES

    def step(r, n_keys, mask_fn):
        rows = slice(r * rs, (r + 1) * rs)
        v = v_ref[0:n_keys, :]
        for c in range(2):
            s = _dot_nt(q_ref[rows, c * hd:(c + 1) * hd], k_ref[0:n_keys, c * hd:(c + 1) * hd])
            if mask_fn is not None:
                s = mask_fn(s)
            _attend(s, v, m_ref, l_ref, acc_ref, (c, rows))

    def finish():
        lp = lam_ref[...]
        lam = (jnp.exp(jnp.sum(lp[0:1] * lp[1:2], axis=-1, keepdims=True))
               - jnp.exp(jnp.sum(lp[2:3] * lp[3:4], axis=-1, keepdims=True)) + lam_init)
        l0 = jnp.sum(l_ref[0], axis=1, keepdims=True)
        l1 = jnp.sum(l_ref[1], axis=1, keepdims=True)
        o = acc_ref[0] / l0 - lam * (acc_ref[1] / l1)
        o_ref[...] = (_rms(o, sub_ref[...]) * (1.0 - lam_init)).astype(o_ref.dtype)

    _flash_frame(qi_ref, kj_ref, fl_ref, m_ref, l_ref, acc_ref, step, finish, rs=rs, **frame)


def _fox_kernel(qi_ref, kj_ref, fl_ref, q_ref, aq_ref, k_ref, v_ref, o_ref,
                m_ref, l_ref, acc_ref, qq_ref, *, rs, **frame):
    def first():
        qq_ref[:, 0:LANES] = q_ref[...]
        qq_ref[:, LANES:2 * LANES] = aq_ref[...]

    def step(r, n_keys, mask_fn):
        rows = slice(r * rs, (r + 1) * rs)
        s = _dot_nt(qq_ref[rows, :], k_ref[0:n_keys, :])
        if mask_fn is not None:
            s = mask_fn(s)
        _attend(s, v_ref[0:n_keys, :], m_ref, l_ref, acc_ref, rows)

    def finish():
        o_ref[...] = (acc_ref[...] / jnp.sum(l_ref[...], axis=1, keepdims=True)).astype(o_ref.dtype)

    _flash_frame(qi_ref, kj_ref, fl_ref, m_ref, l_ref, acc_ref, step, finish, first, rs=rs, **frame)


def _mla_kernel(qi_ref, kj_ref, fl_ref, q_ref, k_ref, v_ref, o_ref,
                m_ref, l_ref, acc_ref, *, rs, **frame):
    def step(r, n_keys, mask_fn):
        rows = slice(r * rs, (r + 1) * rs)
        s = _dot_nt(q_ref[rows, :], k_ref[0:n_keys, :])
        if mask_fn is not None:
            s = mask_fn(s)
        _attend(s, v_ref[0:n_keys, :], m_ref, l_ref, acc_ref, rows)

    def finish():
        o_ref[...] = (acc_ref[...] / jnp.sum(l_ref[...], axis=1, keepdims=True)).astype(o_ref.dtype)

    _flash_frame(qi_ref, kj_ref, fl_ref, m_ref, l_ref, acc_ref, step, finish, rs=rs, **frame)


def _flash_tiles(n_q, n_k_pad):
    tq = _pick_tile(n_q, 1024, 16)
    tk = next((t for t in (1024, 512) if n_k_pad % t == 0), None) or _pick_tile(n_k_pad, 1536, LANES)
    rs = _pick_tile(tq, 256, 16)
    return tq, tk, rs


def _flash_call(kernel, n_heads, operands, n_q, n_k_pad, out_width, scratch, n_k, q_off, causal, name, **kw):
    b = operands[0][0].shape[0]
    tq, tk, rs = _flash_tiles(n_q, n_k_pad)
    qi, kj, fl, kinds = _pair_table(n_q, n_k, tq, tk, q_off, causal)
    if KIND_DIAG in kinds:
        assert rs % CHUNK == 0 and rs % LANES == 0

    in_specs, args = [], []
    for op in operands:
        a = op[0]
        if len(op) == 1:
            in_specs.append(pl.BlockSpec(a.shape, lambda bi, h, n, qi, kj, fl: (0, 0)))
        elif op[2]:
            in_specs.append(pl.BlockSpec((None, tq, op[1]), lambda bi, h, n, qi, kj, fl: (bi, qi[n], h)))
        else:
            in_specs.append(pl.BlockSpec((None, tk, op[1]), lambda bi, h, n, qi, kj, fl: (bi, kj[n], h)))
        args.append(a)
    vmem = 6 * tq * tk * 4 + 8 * max(tq, tk) * 2 * LANES * 2 * len(operands) + 10 * tq * 2 * LANES * 4 + (8 << 20)
    grid_spec = pltpu.PrefetchScalarGridSpec(
        num_scalar_prefetch=3,
        grid=(b, n_heads, int(qi.shape[0])),
        in_specs=in_specs,
        out_specs=pl.BlockSpec((None, tq, out_width), lambda bi, h, n, qi, kj, fl: (bi, qi[n], h)),
        scratch_shapes=scratch(tq),
    )
    return pl.pallas_call(
        functools.partial(kernel, tq=tq, tk=tk, rs=rs, q_off=q_off, n_k=n_k, causal=causal, kinds=kinds, **kw),
        out_shape=jax.ShapeDtypeStruct((b, n_q, n_heads * out_width), BF16),
        grid_spec=grid_spec,
        compiler_params=_params(("parallel", "parallel", "arbitrary"), vmem),
        name=name,
    )(qi, kj, fl, *args)


def _flash_diff(q, k, v, lam_p, subln, lam_init, n_heads, n_k, q_off):
    w = 2 * LANES
    scratch = lambda tq: [pltpu.VMEM((2, tq, LANES), F32), pltpu.VMEM((2, tq, LANES), F32), pltpu.VMEM((2, tq, w), F32)]
    ops = [(q, w, True), (k, w, False), (v, w, False), (lam_p,), (subln.reshape(1, w),)]
    return _flash_call(_diff_kernel, n_heads, ops, q.shape[1], k.shape[1], w, scratch, n_k, q_off, False,
                       "diff_attn", lam_init=lam_init)


def _flash_fox(q, aq, kk, v, n_heads, n_k, q_off):
    w = LANES
    scratch = lambda tq: [pltpu.VMEM((tq, LANES), F32), pltpu.VMEM((tq, LANES), F32), pltpu.VMEM((tq, w), F32),
                          pltpu.VMEM((tq, 2 * LANES), BF16)]
    ops = [(q, w, True), (aq, w, True), (kk, 2 * w, False), (v, w, False)]
    return _flash_call(_fox_kernel, n_heads, ops, q.shape[1], kk.shape[1], w, scratch, n_k, q_off, True, "fox_attn")


def _flash_mla(qq, kk, v, n_heads, n_k, q_off):
    w = LANES
    scratch = lambda tq: [pltpu.VMEM((tq, LANES), F32), pltpu.VMEM((tq, LANES), F32), pltpu.VMEM((tq, w), F32)]
    ops = [(qq, 2 * w, True), (kk, 2 * w, False), (v, w, False)]
    return _flash_call(_mla_kernel, n_heads, ops, qq.shape[1], kk.shape[1], w, scratch, n_k, q_off, False, "mla_attn")


def _rope_half(y, cos, sin, rope_dim):
    half = rope_dim // 2
    lane = lax.broadcasted_iota(jnp.int32, y.shape, 1)
    rot = jnp.where(lane < half, pltpu.roll(y, LANES - half, 1), pltpu.roll(y, half, 1))
    return y * cos + rot * sin


def _rms_low(x, gain, n):
    return x * lax.rsqrt(jnp.sum(x * x, axis=-1, keepdims=True) * (1.0 / n) + EPS) * gain


def _odd_post_kernel(z_ref, cos_ref, sin_ref, gq_ref, gkv_ref, gr_ref,
                     cq_ref, ckvf_ref, ckvb_ref, kpf_ref, kpb_ref, *, q_lora, kv_lora, rope_dim):
    cq_ref[...] = _rms(z_ref[:, 0:q_lora], gq_ref[...]).astype(BF16)
    ckv = _rms(z_ref[:, q_lora:q_lora + kv_lora], gkv_ref[...])
    ckvf_ref[...] = ckv
    ckvb_ref[...] = ckv.astype(BF16)
    kp = _rms_low(z_ref[:, q_lora + kv_lora:q_lora + kv_lora + LANES], gr_ref[...], rope_dim)
    kp = _rope_half(kp, cos_ref[...], sin_ref[...], rope_dim)
    kpf_ref[...] = kp
    kpb_ref[...] = kp.astype(BF16)


def _odd_post(z, cos, sin, g_cq, g_ckv, g_rope_k_pad, q_lora, kv_lora, rope_dim):
    b, t, n = z.shape
    tm = _pick_tile(t, 512, 16)
    row = lambda w: pl.BlockSpec((None, tm, w), lambda bi, i: (bi, i, 0))
    tab = pl.BlockSpec((tm, LANES), lambda bi, i: (i, 0))
    vec = lambda w: pl.BlockSpec((1, w), lambda bi, i: (0, 0))
    outs = [(q_lora, BF16), (kv_lora, F32), (kv_lora, BF16), (LANES, F32), (LANES, BF16)]
    return pl.pallas_call(
        functools.partial(_odd_post_kernel, q_lora=q_lora, kv_lora=kv_lora, rope_dim=rope_dim),
        out_shape=[jax.ShapeDtypeStruct((b, t, w), dt) for w, dt in outs],
        grid=(b, t // tm),
        in_specs=[row(n), tab, tab, vec(q_lora), vec(kv_lora), vec(LANES)],
        out_specs=[row(w) for w, _ in outs],
        compiler_params=_params(("parallel", "parallel"), 32 << 20),
        name="odd_post",
    )(z, cos, sin, g_cq.reshape(1, -1), g_ckv.reshape(1, -1), g_rope_k_pad)


def _qup_kernel(cq_ref, w_ref, cos_ref, sin_ref, gn_ref, gr_ref, qq_ref, *, heads, rope_dim, q_scale):
    q = _dot(cq_ref[...], w_ref[...])
    cos, sin = cos_ref[...], sin_ref[...]
    for h in range(heads):
        qn = _rms(q[:, 2 * h * LANES:(2 * h + 1) * LANES], gn_ref[...])
        qq_ref[:, 2 * h * LANES:(2 * h + 1) * LANES] = (qn * q_scale).astype(BF16)
        qp = _rms_low(q[:, (2 * h + 1) * LANES:(2 * h + 2) * LANES], gr_ref[...], rope_dim)
        qp = _rope_half(qp, cos, sin, rope_dim)
        qq_ref[:, (2 * h + 1) * LANES:(2 * h + 2) * LANES] = (qp * q_scale).astype(BF16)


def _qup(cq, w_pad, cos, sin, g_nope_q, g_rope_q_pad, n_heads, rope_dim, qk_dim):
    b, t, kq = cq.shape
    tm = _pick_tile(t, 512, 16)
    hg = 4 if n_heads % 4 == 0 else 1
    tn = hg * 2 * LANES
    return pl.pallas_call(
        functools.partial(_qup_kernel, heads=hg, rope_dim=rope_dim, q_scale=qk_dim ** -0.5 * LOG2E),
        out_shape=jax.ShapeDtypeStruct((b, t, n_heads * 2 * LANES), BF16),
        grid=(b, t // tm, n_heads // hg),
        in_specs=[
            pl.BlockSpec((None, tm, kq), lambda bi, i, j: (bi, i, 0)),
            pl.BlockSpec((kq, tn), lambda bi, i, j: (0, j)),
            pl.BlockSpec((tm, LANES), lambda bi, i, j: (i, 0)),
            pl.BlockSpec((tm, LANES), lambda bi, i, j: (i, 0)),
            pl.BlockSpec((1, LANES), lambda bi, i, j: (0, 0)),
            pl.BlockSpec((1, LANES), lambda bi, i, j: (0, 0)),
        ],
        out_specs=pl.BlockSpec((None, tm, tn), lambda bi, i, j: (bi, i, j)),
        compiler_params=_params(("parallel", "parallel", "arbitrary"), 32 << 20),
        name="mla_q_up",
    )(cq, w_pad, cos, sin, g_nope_q.reshape(1, LANES), g_rope_q_pad)


def _kvup_kernel(ckv_ref, kp_ref, w_ref, gn_ref, kk_ref, v_ref, *, heads):
    kv = _dot(ckv_ref[...], w_ref[...])
    kp = kp_ref[...]
    for h in range(heads):
        kn = _rms(kv[:, 2 * h * LANES:(2 * h + 1) * LANES], gn_ref[...])
        kk_ref[:, 2 * h * LANES:(2 * h + 1) * LANES] = kn.astype(BF16)
        kk_ref[:, (2 * h + 1) * LANES:(2 * h + 2) * LANES] = kp
        v_ref[:, h * LANES:(h + 1) * LANES] = kv[:, (2 * h + 1) * LANES:(2 * h + 2) * LANES].astype(BF16)


def _kvup(ckv, kp, w, g_nope_k, n_heads):
    b, t, kk = ckv.shape
    tm = _pick_tile(t, 512, LANES)
    hg = 4 if n_heads % 4 == 0 else 1
    tn = hg * 2 * LANES
    return pl.pallas_call(
        functools.partial(_kvup_kernel, heads=hg),
        out_shape=[jax.ShapeDtypeStruct((b, t, n_heads * 2 * LANES), BF16),
                   jax.ShapeDtypeStruct((b, t, n_heads * LANES), BF16)],
        grid=(b, t // tm, n_heads // hg),
        in_specs=[
            pl.BlockSpec((None, tm, kk), lambda bi, i, j: (bi, i, 0)),
            pl.BlockSpec((None, tm, LANES), lambda bi, i, j: (bi, i, 0)),
            pl.BlockSpec((kk, tn), lambda bi, i, j: (0, j)),
            pl.BlockSpec((1, LANES), lambda bi, i, j: (0, 0)),
        ],
        out_specs=[pl.BlockSpec((None, tm, tn), lambda bi, i, j: (bi, i, j)),
                   pl.BlockSpec((None, tm, hg * LANES), lambda bi, i, j: (bi, i, j))],
        compiler_params=_params(("parallel", "parallel", "arbitrary"), 32 << 20),
        name="mla_kv_up",
    )(ckv, kp, w, g_nope_k.reshape(1, LANES))


def _rope_tables(pos, dim):
    half = dim // 2
    inv = ROPE_THETA ** (-jnp.arange(half, dtype=F32) * 2.0 / dim)
    ang = pos.astype(F32)[:, None] * inv[None, :]
    cos, sin = jnp.cos(ang), jnp.sin(ang)
    pad = ((0, 0), (0, LANES - dim))
    return (jnp.pad(jnp.concatenate([cos, cos], axis=-1), pad),
            jnp.pad(jnp.concatenate([-sin, sin], axis=-1), pad))


def _pad_lanes(a, width):
    return jnp.pad(a, [(0, 0)] * (a.ndim - 1) + [(0, width - a.shape[-1])])


def _pad_rows(a, rows):
    return jnp.pad(a, [(0, 0), (0, rows - a.shape[1])] + [(0, 0)] * (a.ndim - 2))


def _with_past(past, new_bf, n_k_pad):
    if past is None:
        return new_bf
    b, p = past.shape[0], past.shape[1]
    allk = jnp.concatenate([past.reshape(b, p, -1).astype(BF16), new_bf], axis=1)
    return _pad_rows(allk, n_k_pad)


def _layer_stack(x, c_mod, tok_pos, seq_shape, caches, p):
    bx, tx, d = x.shape
    ba, ta = seq_shape
    depth = p['w_ffn_in'].shape[0]
    n_diff = p['n_diff']
    n_fox = p['n_fox']
    n_mla = p['n_mla']
    past_len = 0 if caches is None else caches[0].shape[2]
    n_k = past_len + ta
    n_k_pad = n_k if caches is None else _round_up(n_k, LANES)
    cos128, sin128 = _rope_tables(tok_pos, LANES)
    rope_dim = p['rope_dim']
    cos_r, sin_r = _rope_tables(tok_pos, rope_dim)
    new = [[] for _ in range(7)]

    def mods(l, s):
        m = c_mod[l]
        sh, sc, gt = m[:, 3 * s], m[:, 3 * s + 1], m[:, 3 * s + 2]
        if bx == m.shape[0]:
            return tuple(a[:, None, :] for a in (sh, sc, gt))
        rep = lambda a: jnp.repeat(a, ta, axis=0).reshape(bx, tx, d)
        return rep(sh), rep(sc), rep(gt)

    seq = lambda a: a.reshape(ba, ta, a.shape[-1])
    for l in range(depth):
        i = l // 2
        g = p['norm_gains'][l]
        sh, sc, gt = mods(l, 0)
        x = _ffn(x, sh, sc, gt, g[0], p['w_ffn_in'], p['w_ffn_out'], l, 0, 0.5)
        sh, sc, gt = mods(l, 1)
        if l % 2 == 0:
            z = _modproj(x, sh, sc, g[1], p['w_in_even'][i])
            (qa, kaf, kab, vaf, vab, qb, kbf, kbb, vbf, vbb, lf) = _even_post(
                z, cos128, sin128, p['qk_norm_even'][i], p['b_forget_pad'][i], 2 * n_diff, n_fox)
            new[0].append(kaf.reshape(ba, ta, n_diff, 2, LANES))
            new[1].append(vaf.reshape(ba, ta, n_diff, 2 * LANES))
            new[2].append(kbf.reshape(ba, ta, n_fox, LANES))
            new[3].append(vbf.reshape(ba, ta, n_fox, LANES))
            new[4].append(seq(lf)[:, :, :n_fox])
            past = (None,) * 5 if caches is None else tuple(a[i] for a in caches[:5])
            ka_all = _with_past(past[0], seq(kab), n_k_pad)
            va_all = _with_past(past[1], seq(vab), n_k_pad)
            kb_all = _with_past(past[2], seq(kbb), n_k_pad)
            vb_all = _with_past(past[3], seq(vbb), n_k_pad)
            lf_all = seq(lf)
            if caches is not None:
                lf_all = _pad_rows(jnp.concatenate([_pad_lanes(past[4].astype(F32), LANES), lf_all], axis=1), n_k_pad)
            aq, kk = _fox_prep(lf_all, kb_all, n_fox)
            aq = aq[:, past_len:past_len + ta]
            lam_init = 0.8 - 0.6 * math.exp(-0.3 * l)
            oa = _flash_diff(seq(qa), ka_all, va_all, p['diff_lambda'][i], p['diff_subln'][i], lam_init,
                             n_diff, n_k, past_len)
            ob = _flash_fox(seq(qb), aq, kk, vb_all, n_fox, n_k, past_len)
            x = _outproj(oa.reshape(bx, tx, -1), ob.reshape(bx, tx, -1), 0, 0, p['w_out_even'][i], x, gt)
        else:
            q_lora, kv_lora = p['q_lora'], p['kv_lora']
            z = _modproj(x, sh, sc, g[1], p['w_in_odd'][i])
            cq, ckvf, ckvb, kpf, kpb = _odd_post(z, cos_r, sin_r, p['mla_cq_norm'][i], p['mla_ckv_norm'][i],
                                                 p['g_rope_pad'][i, 1:2], q_lora, kv_lora, rope_dim)
            new[5].append(seq(ckvf))
            new[6].append(seq(kpf)[:, :, :rope_dim])
            qq = _qup(cq, p['w_uq_pad'][i], cos_r, sin_r, p['mla_qk_norm_nope'][i, 0],
                      p['g_rope_pad'][i, 0:1], n_mla, rope_dim, p['mla_qk_dim'])
            past = (None,) * 2 if caches is None else tuple(a[i] for a in caches[5:])
            ckv_all = _with_past(past[0], seq(ckvb), n_k_pad)
            kp_all = seq(kpb)
            if caches is not None:
                kp_all = _pad_rows(jnp.concatenate([_pad_lanes(past[1], LANES).astype(BF16), kp_all], axis=1), n_k_pad)
            kk, v = _kvup(ckv_all, kp_all, p['w_ukv'][i], p['mla_qk_norm_nope'][i, 1], n_mla)
            o = _flash_mla(seq(qq), kk, v, n_mla, n_k, past_len)
            o = o.reshape(bx, tx, -1)
            x = _outproj(o, o, 0, 1, p['w_out_odd'][i], x, gt)
        sh, sc, gt = mods(l, 2)
        x = _ffn(x, sh, sc, gt, g[2], p['w_ffn_in'], p['w_ffn_out'], l, 1, 0.5, final_gain=g[3])
    return x, tuple(jnp.stack(lst) for lst in new)


def kernel(x_prompt, x_sample, c_prompt, c_sample, cache_diff_k, cache_diff_v, cache_fox_k, cache_fox_v, cache_fox_logf, cache_mla_ckv, cache_mla_kpe, w_ada, b_ada, norm_gains, w_ffn_in, w_ffn_out, w_in_even, b_forget, qk_norm_even, diff_lambda, diff_subln, w_out_even, w_in_odd, mla_cq_norm, mla_ckv_norm, w_uq, w_ukv, mla_qk_norm_nope, mla_qk_norm_rope, w_out_odd):
    d = x_prompt.shape[-1]
    n_diff, n_fox = cache_diff_k.shape[3], cache_fox_k.shape[3]
    assert cache_diff_k.shape[-1] == LANES and cache_fox_k.shape[-1] == LANES
    q_lora, kv_lora = mla_cq_norm.shape[-1], mla_ckv_norm.shape[-1]
    rope_dim, nope = cache_mla_kpe.shape[-1], mla_qk_norm_nope.shape[-1]
    n_mla = w_uq.shape[-1] // (nope + rope_dim)
    assert nope == LANES and rope_dim <= LANES and w_ukv.shape[-1] == n_mla * 2 * LANES
    n_odd = w_uq.shape[0]

    w_uq_pad = _pad_lanes(w_uq.reshape(n_odd, q_lora, n_mla, nope + rope_dim), 2 * LANES)
    p = {
        'n_diff': n_diff, 'n_fox': n_fox, 'n_mla': n_mla, 'rope_dim': rope_dim,
        'q_lora': q_lora, 'kv_lora': kv_lora, 'mla_qk_dim': nope + rope_dim,
        'norm_gains': norm_gains,
        'w_ffn_in': w_ffn_in.astype(BF16), 'w_ffn_out': w_ffn_out.astype(BF16),
        'w_in_even': _pad_lanes(w_in_even, _round_up(w_in_even.shape[-1], LANES)).astype(BF16),
        'b_forget_pad': _pad_lanes(b_forget, LANES)[:, None, :],
        'qk_norm_even': qk_norm_even, 'diff_lambda': diff_lambda, 'diff_subln': diff_subln,
        'w_out_even': w_out_even.astype(BF16),
        'w_in_odd': _pad_lanes(w_in_odd, q_lora + kv_lora + LANES).astype(BF16),
        'mla_cq_norm': mla_cq_norm, 'mla_ckv_norm': mla_ckv_norm,
        'w_uq_pad': w_uq_pad.reshape(n_odd, q_lora, n_mla * 2 * LANES).astype(BF16),
        'w_ukv': w_ukv.astype(BF16),
        'mla_qk_norm_nope': mla_qk_norm_nope,
        'g_rope_pad': _pad_lanes(mla_qk_norm_rope, LANES),
        'w_out_odd': w_out_odd.astype(BF16),
    }

    bp, tp = x_prompt.shape[:2]
    bs, ts = x_sample.shape[:2]
    past_len = cache_diff_k.shape[2]
    mod = _ada(jnp.concatenate([c_prompt, c_sample], axis=0), w_ada, b_ada)
    mod = mod.reshape(mod.shape[0], bp + bs, N_MOD, d)

    pos_p = jnp.arange(tp, dtype=jnp.int32)
    y_prompt, st_p = _layer_stack(x_prompt, mod[:, :bp], pos_p, (bp, tp), None, p)

    pos_s = jnp.tile(past_len + jnp.arange(ts, dtype=jnp.int32), bs)
    caches = (cache_diff_k, cache_diff_v, cache_fox_k, cache_fox_v, cache_fox_logf, cache_mla_ckv, cache_mla_kpe)
    y_sample, st_s = _layer_stack(x_sample.reshape(1, bs * ts, d), mod[:, bp:], pos_s, (bs, ts), caches, p)
    return (y_prompt, y_sample.reshape(bs, ts, d)) + st_p + st_s
```

```python
import functools
import math

import numpy as np
import jax
import jax.numpy as jnp
from jax import lax
from jax.experimental import pallas as pl
from jax.experimental.pallas import tpu as pltpu

F32 = jnp.float32
BF16 = jnp.bfloat16

CHUNK = 64
ROPE_THETA = 10000.0
EPS = 1e-6
NEG_INF = -1e30
N_MOD = 9

LANES = 128
SUBLANES = 8
VMEM_CAP_BYTES = 56 * 1024 * 1024

LOG2E = math.log2(math.e)
CHUNK_SHIFT = CHUNK.bit_length() - 1
assert (1 << CHUNK_SHIFT) == CHUNK


def _round_up(n, m):
    return (n + m - 1) // m * m


def _pick_tile(n, target, quantum):
    if n <= target:
        return n
    best = None
    t = quantum
    while t <= target:
        if n % t == 0:
            best = t
        t += quantum
    assert best is not None, (n, target, quantum)
    return best


def _params(semantics, vmem_bytes):
    limit = int(min(max(vmem_bytes, 16 * 1024 * 1024), VMEM_CAP_BYTES))
    return pltpu.CompilerParams(dimension_semantics=semantics, vmem_limit_bytes=limit)


def _rms(x, gain):
    return x * lax.rsqrt(jnp.mean(x * x, axis=-1, keepdims=True) + EPS) * gain


def _silu(g):
    return g / (1.0 + jnp.exp(-g))


def _dot(a, b):
    return jnp.dot(a, b, preferred_element_type=F32)


def _dot_nt(a, b):
    return lax.dot_general(a, b, (((1,), (1,)), ((), ())), preferred_element_type=F32)


def _ada_kernel(c_ref, w_ref, b_ref, o_ref):
    a = _silu(c_ref[...]).astype(BF16)
    o_ref[...] = _dot(a, w_ref[...].astype(BF16)) + b_ref[...]


def _ada(c_all, w_ada, b_ada):
    depth, d, n = w_ada.shape
    r = c_all.shape[0]
    tn = _pick_tile(n, 1024, LANES)
    return pl.pallas_call(
        _ada_kernel,
        out_shape=jax.ShapeDtypeStruct((depth, r, n), F32),
        grid=(depth, n // tn),
        in_specs=[
            pl.BlockSpec((r, d), lambda l, j: (0, 0)),
            pl.BlockSpec((None, d, tn), lambda l, j: (l, 0, j)),
            pl.BlockSpec((None, 1, tn), lambda l, j: (l, 0, j)),
        ],
        out_specs=pl.BlockSpec((None, r, tn), lambda l, j: (l, 0, j)),
        compiler_params=_params(("arbitrary", "arbitrary"), 2 * d * tn * 4 + 3 * d * tn * 2 + (4 << 20)),
        name="ada_mod",
    )(c_all, w_ada, b_ada.reshape(depth, 1, n))


def _mod_spec(mod, tm):
    d = mod.shape[-1]
    if mod.shape[1] == 1:
        return pl.BlockSpec((None, 1, d), lambda b, i, j: (b, 0, 0))
    return pl.BlockSpec((None, tm, d), lambda b, i, j: (b, i, 0))


def _ffn_kernel(x_ref, sh_ref, sc_ref, gt_ref, g_ref, wg_ref, wu_ref, wo_ref, *rest,
                gate_mul, final_norm):
    if final_norm:
        gf_ref, o_ref, h_ref, acc_ref = rest
    else:
        o_ref, h_ref, acc_ref = rest
    f = pl.program_id(2)

    @pl.when(f == 0)
    def _():
        h = _rms(x_ref[...], g_ref[...]) * (1.0 + sc_ref[...]) + sh_ref[...]
        h_ref[...] = h.astype(BF16)
        acc_ref[...] = jnp.zeros_like(acc_ref)

    h = h_ref[...]
    g = _dot(h, wg_ref[...])
    u = _dot(h, wu_ref[...])
    a = (_silu(g) * u).astype(BF16)
    acc_ref[...] += _dot(a, wo_ref[...])

    @pl.when(f == pl.num_programs(2) - 1)
    def _():
        xn = x_ref[...] + (gate_mul * gt_ref[...]) * acc_ref[...]
        if final_norm:
            xn = _rms(xn, gf_ref[...])
        o_ref[...] = xn


def _ffn(x, sh, sc, gt, gain, w_in, w_out, layer, sub, gate_mul, final_gain=None):
    b, t, d = x.shape
    ff = w_out.shape[2]
    tm = _pick_tile(t, 512, 16)
    tf = _pick_tile(ff, 512, LANES)
    nf = ff // tf
    in_specs = [
        pl.BlockSpec((None, tm, d), lambda bi, i, f: (bi, i, 0)),
        _mod_spec(sh, tm), _mod_spec(sc, tm), _mod_spec(gt, tm),
        pl.BlockSpec((1, d), lambda bi, i, f: (0, 0)),
        pl.BlockSpec((None, None, d, tf), lambda bi, i, f: (layer, sub, 0, f)),
        pl.BlockSpec((None, None, d, tf), lambda bi, i, f: (layer, sub, 0, nf + f)),
        pl.BlockSpec((None, None, tf, d), lambda bi, i, f: (layer, sub, f, 0)),
    ]
    args = [x, sh, sc, gt, gain.reshape(1, d), w_in, w_in, w_out]
    if final_gain is not None:
        in_specs.append(pl.BlockSpec((1, d), lambda bi, i, f: (0, 0)))
        args.append(final_gain.reshape(1, d))
    vmem = (4 * tm * d * 4 + tm * d * 2 + tm * d * 4 + 6 * d * tf * 2 + 4 * tm * tf * 4
            + 6 * tm * d * 4 * (sh.shape[1] != 1) + (4 << 20))
    return pl.pallas_call(
        functools.partial(_ffn_kernel, gate_mul=gate_mul, final_norm=final_gain is not None),
        out_shape=jax.ShapeDtypeStruct((b, t, d), F32),
        grid=(b, t // tm, nf),
        in_specs=in_specs,
        out_specs=pl.BlockSpec((None, tm, d), lambda bi, i, f: (bi, i, 0)),
        scratch_shapes=[pltpu.VMEM((tm, d), BF16), pltpu.VMEM((tm, d), F32)],
        compiler_params=_params(("parallel", "parallel", "arbitrary"), vmem),
        name="ffn",
    )(*args)


def _modproj_kernel(x_ref, sh_ref, sc_ref, g_ref, w_ref, o_ref, h_ref):
    @pl.when(pl.program_id(2) == 0)
    def _():
        h = _rms(x_ref[...], g_ref[...]) * (1.0 + sc_ref[...]) + sh_ref[...]
        h_ref[...] = h.astype(BF16)

    o_ref[...] = _dot(h_ref[...], w_ref[...])


def _modproj(x, sh, sc, gain, w):
    b, t, d = x.shape
    n = w.shape[1]
    tm = _pick_tile(t, 512, 16)
    tn = _pick_tile(n, 1280, LANES)
    vmem = (2 * tm * d * 4 + tm * d * 2 + 2 * d * tn * 2 + 3 * tm * tn * 4
            + 4 * tm * d * 4 * (sh.shape[1] != 1) + 3 * tm * d * 4 + (4 << 20))
    return pl.pallas_call(
        _modproj_kernel,
        out_shape=jax.ShapeDtypeStruct((b, t, n), F32),
        grid=(b, t // tm, n // tn),
        in_specs=[
            pl.BlockSpec((None, tm, d), lambda bi, i, j: (bi, i, 0)),
            _mod_spec(sh, tm), _mod_spec(sc, tm),
            pl.BlockSpec((1, d), lambda bi, i, j: (0, 0)),
            pl.BlockSpec((d, tn), lambda bi, i, j: (0, j)),
        ],
        out_specs=pl.BlockSpec((None, tm, tn), lambda bi, i, j: (bi, i, j)),
        scratch_shapes=[pltpu.VMEM((tm, d), BF16)],
        compiler_params=_params(("parallel", "parallel", "arbitrary"), vmem),
        name="mod_proj",
    )(x, sh, sc, gain.reshape(1, d), w)


def _outproj_kernel(a1_ref, a2_ref, w1_ref, w2_ref, x_ref, gt_ref, o_ref):
    y = _dot(a1_ref[...], w1_ref[...]) + _dot(a2_ref[...], w2_ref[...])
    o_ref[...] = x_ref[...] + gt_ref[...] * y


def _outproj(a1, a2, blk1, blk2, w, x, gt):
    b, t, d = x.shape
    kh = w.shape[0] // 2
    tm = _pick_tile(t, 512, 16)
    vmem = 4 * tm * kh * 2 + 4 * kh * d * 2 + 5 * tm * d * 4 + 2 * tm * d * 4 * (gt.shape[1] != 1) + (4 << 20)
    return pl.pallas_call(
        _outproj_kernel,
        out_shape=jax.ShapeDtypeStruct((b, t, d), F32),
        grid=(b, t // tm, 1),
        in_specs=[
            pl.BlockSpec((None, tm, kh), lambda bi, i, j: (bi, i, blk1)),
            pl.BlockSpec((None, tm, kh), lambda bi, i, j: (bi, i, blk2)),
            pl.BlockSpec((kh, d), lambda bi, i, j: (0, 0)),
            pl.BlockSpec((kh, d), lambda bi, i, j: (1, 0)),
            pl.BlockSpec((None, tm, d), lambda bi, i, j: (bi, i, 0)),
            _mod_spec(gt, tm),
        ],
        out_specs=pl.BlockSpec((None, tm, d), lambda bi, i, j: (bi, i, 0)),
        compiler_params=_params(("parallel", "parallel", "arbitrary"), vmem),
        name="out_proj",
    )(a1, a2, w, w, x, gt)


def _rope128(y, cos, sin):
    return y * cos + pltpu.roll(y, LANES // 2, 1) * sin


def _log_sigmoid(x):
    return jnp.minimum(x, 0.0) - jnp.log(1.0 + jnp.exp(-jnp.abs(x)))


def _even_post_kernel(z_ref, cos_ref, sin_ref, g_ref, bf_ref,
                      qa_ref, kaf_ref, kab_ref, vaf_ref, vab_ref,
                      qb_ref, kbf_ref, kbb_ref, vbf_ref, vbb_ref, lf_ref,
                      *, n_qa, n_fox, n_forget, q_scale):
    cos = cos_ref[...]
    sin = sin_ref[...]
    hd = LANES
    wa = n_qa * hd
    wb = n_fox * hd
    for j in range(n_qa):
        q = _rope128(_rms(z_ref[:, j * hd:(j + 1) * hd], g_ref[0:1, :]), cos, sin)
        qa_ref[:, j * hd:(j + 1) * hd] = (q * q_scale).astype(BF16)
        k = _rope128(_rms(z_ref[:, wa + j * hd:wa + (j + 1) * hd], g_ref[1:2, :]), cos, sin)
        kaf_ref[:, j * hd:(j + 1) * hd] = k
        kab_ref[:, j * hd:(j + 1) * hd] = k.astype(BF16)
    va = z_ref[:, 2 * wa:3 * wa]
    vaf_ref[...] = va
    vab_ref[...] = va.astype(BF16)
    o = 3 * wa
    for j in range(n_fox):
        q = _rms(z_ref[:, o + j * hd:o + (j + 1) * hd], g_ref[2:3, :])
        qb_ref[:, j * hd:(j + 1) * hd] = (q * q_scale).astype(BF16)
        k = _rms(z_ref[:, o + wb + j * hd:o + wb + (j + 1) * hd], g_ref[3:4, :])
        kbf_ref[:, j * hd:(j + 1) * hd] = k
        kbb_ref[:, j * hd:(j + 1) * hd] = k.astype(BF16)
    vb = z_ref[:, o + 2 * wb:o + 3 * wb]
    vbf_ref[...] = vb
    vbb_ref[...] = vb.astype(BF16)
    fg = z_ref[:, o + 3 * wb:o + 3 * wb + LANES] + bf_ref[...]
    lane = lax.broadcasted_iota(jnp.int32, fg.shape, 1)
    lf_ref[...] = jnp.where(lane < n_forget, _log_sigmoid(fg), 0.0)


def _even_post(z, cos, sin, qk_gain, b_forget_pad, n_qa, n_fox):
    b, t, n = z.shape
    wa, wb = n_qa * LANES, n_fox * LANES
    tm = _pick_tile(t, 256, 16)
    row = lambda w: pl.BlockSpec((None, tm, w), lambda bi, i: (bi, i, 0))
    tab = pl.BlockSpec((tm, LANES), lambda bi, i: (i, 0))
    outs = [(wa, BF16), (wa, F32), (wa, BF16), (wa, F32), (wa, BF16),
            (wb, BF16), (wb, F32), (wb, BF16), (wb, F32), (wb, BF16), (LANES, F32)]
    vmem = 2 * tm * n * 4 + 2 * sum(tm * w * jnp.dtype(dt).itemsize for w, dt in outs) + (8 << 20)
    return pl.pallas_call(
        functools.partial(_even_post_kernel, n_qa=n_qa, n_fox=n_fox, n_forget=n_fox,
                          q_scale=LANES ** -0.5 * LOG2E),
        out_shape=[jax.ShapeDtypeStruct((b, t, w), dt) for w, dt in outs],
        grid=(b, t // tm),
        in_specs=[row(n), tab, tab,
                  pl.BlockSpec((4, LANES), lambda bi, i: (0, 0)),
                  pl.BlockSpec((1, LANES), lambda bi, i: (0, 0))],
        out_specs=[row(w) for w, _ in outs],
        compiler_params=_params(("parallel", "parallel"), vmem),
        name="even_post",
    )(z, cos, sin, qk_gain, b_forget_pad)


def _split3(x):
    hi = x.astype(BF16)
    r = x - hi.astype(F32)
    mid = r.astype(BF16)
    lo = (r - mid.astype(F32)).astype(BF16)
    return hi, mid, lo


def _fox_prep_kernel(lf_ref, k_ref, aq_ref, kk_ref, carry_ref, *, n_heads):
    @pl.when(pl.program_id(1) == 0)
    def _():
        carry_ref[...] = jnp.zeros_like(carry_ref)

    x = lf_ref[...]
    tb = x.shape[0]
    r = lax.broadcasted_iota(jnp.int32, (tb, tb), 0)
    c = lax.broadcasted_iota(jnp.int32, (tb, tb), 1)
    tri = jnp.where(r >= c, 1.0, 0.0).astype(BF16)
    hi, mid, lo = _split3(x)
    cum = _dot(tri, hi) + _dot(tri, mid) + _dot(tri, lo) + carry_ref[...]
    carry_ref[...] = cum[tb - 1:tb, :]
    fh, fm, fl = (p.astype(F32) for p in _split3(cum * LOG2E))
    lane = lax.broadcasted_iota(jnp.int32, (tb, LANES), 1)
    ones_q = jnp.where((lane >= 3) & (lane < 6), 1.0, 0.0)
    ones_k = jnp.where(lane < 3, 1.0, 0.0)
    for h in range(n_heads):
        a, m, l = fh[:, h:h + 1], fm[:, h:h + 1], fl[:, h:h + 1]
        aq = jnp.where(lane == 0, a, jnp.where(lane == 1, m, jnp.where(lane == 2, l, ones_q)))
        ak = jnp.where(lane == 3, -a, jnp.where(lane == 4, -m, jnp.where(lane == 5, -l, ones_k)))
        aq_ref[:, h * LANES:(h + 1) * LANES] = aq.astype(BF16)
        kk_ref[:, 2 * h * LANES:(2 * h + 1) * LANES] = k_ref[:, h * LANES:(h + 1) * LANES]
        kk_ref[:, (2 * h + 1) * LANES:(2 * h + 2) * LANES] = ak.astype(BF16)


def _fox_prep(logf_pad, k_all, n_heads):
    b, t, _ = logf_pad.shape
    tb = _pick_tile(t, 512, LANES)
    w = n_heads * LANES
    return pl.pallas_call(
        functools.partial(_fox_prep_kernel, n_heads=n_heads),
        out_shape=[jax.ShapeDtypeStruct((b, t, w), BF16), jax.ShapeDtypeStruct((b, t, 2 * w), BF16)],
        grid=(b, t // tb),
        in_specs=[pl.BlockSpec((None, tb, LANES), lambda bi, i: (bi, i, 0)),
                  pl.BlockSpec((None, tb, w), lambda bi, i: (bi, i, 0))],
        out_specs=[pl.BlockSpec((None, tb, w), lambda bi, i: (bi, i, 0)),
                   pl.BlockSpec((None, tb, 2 * w), lambda bi, i: (bi, i, 0))],
        scratch_shapes=[pltpu.VMEM((1, LANES), F32)],
        compiler_params=_params(("parallel", "arbitrary"), 32 << 20),
        name="fox_prep",
    )(logf_pad, k_all)


FLAG_FIRST, FLAG_LAST = 1, 2
KIND_SHIFT = 2
KIND_FULL, KIND_MASK, KIND_DIAG = 0, 1, 2


def _pair_table(n_q, n_k, tq, tk, q_off, causal):
    tk_pad = _round_up(n_k, tk)
    aligned = tq == tk and q_off % tq == 0
    qi, kj, fl = [], [], []
    for i in range(n_q // tq):
        qmin, qmax = q_off + i * tq, q_off + (i + 1) * tq - 1
        row = []
        for j in range(tk_pad // tk):
            kmin, kmax = j * tk, min((j + 1) * tk, n_k) - 1
            if kmin >= n_k:
                continue
            if causal:
                any_vis, all_vis = kmin <= qmax, kmax <= qmin
            else:
                any_vis, all_vis = kmin // CHUNK <= qmax // CHUNK, kmax // CHUNK <= qmin // CHUNK
            all_vis = all_vis and (j + 1) * tk <= n_k
            if any_vis:
                diag = aligned and kmin == qmin and (j + 1) * tk <= n_k
                row.append((j, KIND_FULL if all_vis else KIND_DIAG if diag else KIND_MASK))
        assert row and row[0][0] == 0
        for idx, (j, kind) in enumerate(row):
            qi.append(i)
            kj.append(j)
            fl.append((kind << KIND_SHIFT) | (FLAG_FIRST if idx == 0 else 0) | (FLAG_LAST if idx == len(row) - 1 else 0))
    kinds = sorted({f >> KIND_SHIFT for f in fl})
    as_arr = lambda v: jnp.asarray(np.array(v, np.int32))
    return as_arr(qi), as_arr(kj), as_arr(fl), kinds


def _visible(shape, qpos0, kpos0, n_k, causal, row_period=None):
    if row_period is None:
        rows = qpos0 + lax.broadcasted_iota(jnp.int32, shape, 0)
    else:
        one = lax.broadcasted_iota(jnp.int32, (row_period, shape[1]), 0)
        rows = qpos0 + jnp.concatenate([one] * (shape[0] // row_period), axis=0)
    cols = kpos0 + lax.broadcasted_iota(jnp.int32, shape, 1)
    if causal:
        ok = cols <= rows
    else:
        ok = (cols >> CHUNK_SHIFT) <= (rows >> CHUNK_SHIFT)
    return ok & (cols < n_k)


def _lane_tile(x, n):
    return x if n == LANES else jnp.concatenate([x] * (n // LANES), axis=1)


def _lane_fold(p):
    acc = p[:, 0:LANES]
    for c in range(1, p.shape[1] // LANES):
        acc = acc + p[:, c * LANES:(c + 1) * LANES]
    return acc


def _attend(s, v, m_ref, l_ref, acc_ref, idx):
    m_prev = m_ref[idx]
    m_new = jnp.maximum(m_prev, jnp.max(s, axis=1, keepdims=True))
    alpha = jnp.exp2(m_prev - m_new)
    p = jnp.exp2(s - _lane_tile(m_new, s.shape[1]))
    l_ref[idx] = alpha * l_ref[idx] + _lane_fold(p)
    acc_ref[idx] = _lane_tile(alpha, v.shape[1]) * acc_ref[idx] + _dot(p.astype(v.dtype), v)
    m_ref[idx] = m_new


def _flash_frame(qi_ref, kj_ref, fl_ref, m_ref, l_ref, acc_ref, step, finish, first=None, *,
                 tq, tk, rs, q_off, n_k, causal, kinds):
    n = pl.program_id(2)
    flags = fl_ref[n]
    kind = flags >> KIND_SHIFT

    @pl.when((flags & FLAG_FIRST) != 0)
    def _():
        m_ref[...] = jnp.full_like(m_ref, NEG_INF)
        l_ref[...] = jnp.zeros_like(l_ref)
        acc_ref[...] = jnp.zeros_like(acc_ref)
        if first is not None:
            first()

    qpos0 = q_off + qi_ref[n] * tq
    kpos0 = kj_ref[n] * tk

    def run(k):
        for r in range(tq // rs):
            n_keys = (r + 1) * rs if k == KIND_DIAG else tk
            if k == KIND_FULL:
                mask_fn = None
            else:
                mask_fn = lambda s, r=r: jnp.where(_visible(s.shape, qpos0 + r * rs, kpos0, n_k, causal), s, NEG_INF)
            step(r, n_keys, mask_fn)

    for k in kinds:
        pl.when(kind == k)(functools.partial(run, k))

    @pl.when((flags & FLAG_LAST) != 0)
    def _():
        finish()


def _diff_kernel(qi_ref, kj_ref, fl_ref, q_ref, k_ref, v_ref, lam_ref, sub_ref, o_ref,
                 m_ref, l_ref, acc_ref, *, rs, lam_init, **frame):
    hd = LANES

    def step(r, n_keys, mask_fn):
        rows = slice(r * rs, (r + 1) * rs)
        v = v_ref[0:n_keys, :]
        for c in range(2):
            s = _dot_nt(q_ref[rows, c * hd:(c + 1) * hd], k_ref[0:n_keys, c * hd:(c + 1) * hd])
            if mask_fn is not None:
                s = mask_fn(s)
            _attend(s, v, m_ref, l_ref, acc_ref, (c, rows))

    def finish():
        lp = lam_ref[...]
        lam = (jnp.exp(jnp.sum(lp[0:1] * lp[1:2], axis=-1, keepdims=True))
               - jnp.exp(jnp.sum(lp[2:3] * lp[3:4], axis=-1, keepdims=True)) + lam_init)
        l0 = jnp.sum(l_ref[0], axis=1, keepdims=True)
        l1 = jnp.sum(l_ref[1], axis=1, keepdims=True)
        o = acc_ref[0] / l0 - lam * (acc_ref[1] / l1)
        o_ref[...] = (_rms(o, sub_ref[...]) * (1.0 - lam_init)).astype(o_ref.dtype)

    _flash_frame(qi_ref, kj_ref, fl_ref, m_ref, l_ref, acc_ref, step, finish, rs=rs, **frame)


def _fox_kernel(qi_ref, kj_ref, fl_ref, q_ref, aq_ref, k_ref, v_ref, o_ref,
                m_ref, l_ref, acc_ref, qq_ref, *, rs, **frame):
    def first():
        qq_ref[:, 0:LANES] = q_ref[...]
        qq_ref[:, LANES:2 * LANES] = aq_ref[...]

    def step(r, n_keys, mask_fn):
        rows = slice(r * rs, (r + 1) * rs)
        s = _dot_nt(qq_ref[rows, :], k_ref[0:n_keys, :])
        if mask_fn is not None:
            s = mask_fn(s)
        _attend(s, v_ref[0:n_keys, :], m_ref, l_ref, acc_ref, rows)

    def finish():
        o_ref[...] = (acc_ref[...] / jnp.sum(l_ref[...], axis=1, keepdims=True)).astype(o_ref.dtype)

    _flash_frame(qi_ref, kj_ref, fl_ref, m_ref, l_ref, acc_ref, step, finish, first, rs=rs, **frame)


def _mla_kernel(qi_ref, kj_ref, fl_ref, q_ref, k_ref, v_ref, o_ref,
                m_ref, l_ref, acc_ref, *, rs, **frame):
    def step(r, n_keys, mask_fn):
        rows = slice(r * rs, (r + 1) * rs)
        s = _dot_nt(q_ref[rows, :], k_ref[0:n_keys, :])
        if mask_fn is not None:
            s = mask_fn(s)
        _attend(s, v_ref[0:n_keys, :], m_ref, l_ref, acc_ref, rows)

    def finish():
        o_ref[...] = (acc_ref[...] / jnp.sum(l_ref[...], axis=1, keepdims=True)).astype(o_ref.dtype)

    _flash_frame(qi_ref, kj_ref, fl_ref, m_ref, l_ref, acc_ref, step, finish, rs=rs, **frame)


def _flash_tiles(n_q, n_k_pad):
    tq = _pick_tile(n_q, 1024, 16)
    tk = next((t for t in (1024, 512) if n_k_pad % t == 0), None) or _pick_tile(n_k_pad, 1536, LANES)
    rs = _pick_tile(tq, 256, 16)
    return tq, tk, rs


def _flash_call(kernel, n_heads, operands, n_q, n_k_pad, out_width, scratch, n_k, q_off, causal, name, **kw):
    b = operands[0][0].shape[0]
    tq, tk, rs = _flash_tiles(n_q, n_k_pad)
    qi, kj, fl, kinds = _pair_table(n_q, n_k, tq, tk, q_off, causal)
    if KIND_DIAG in kinds:
        assert rs % CHUNK == 0 and rs % LANES == 0

    in_specs, args = [], []
    for op in operands:
        a = op[0]
        if len(op) == 1:
            in_specs.append(pl.BlockSpec(a.shape, lambda bi, h, n, qi, kj, fl: (0, 0)))
        elif op[2]:
            in_specs.append(pl.BlockSpec((None, tq, op[1]), lambda bi, h, n, qi, kj, fl: (bi, qi[n], h)))
        else:
            in_specs.append(pl.BlockSpec((None, tk, op[1]), lambda bi, h, n, qi, kj, fl: (bi, kj[n], h)))
        args.append(a)
    vmem = 6 * tq * tk * 4 + 8 * max(tq, tk) * 2 * LANES * 2 * len(operands) + 10 * tq * 2 * LANES * 4 + (8 << 20)
    grid_spec = pltpu.PrefetchScalarGridSpec(
        num_scalar_prefetch=3,
        grid=(b, n_heads, int(qi.shape[0])),
        in_specs=in_specs,
        out_specs=pl.BlockSpec((None, tq, out_width), lambda bi, h, n, qi, kj, fl: (bi, qi[n], h)),
        scratch_shapes=scratch(tq),
    )
    return pl.pallas_call(
        functools.partial(kernel, tq=tq, tk=tk, rs=rs, q_off=q_off, n_k=n_k, causal=causal, kinds=kinds, **kw),
        out_shape=jax.ShapeDtypeStruct((b, n_q, n_heads * out_width), BF16),
        grid_spec=grid_spec,
        compiler_params=_params(("parallel", "parallel", "arbitrary"), vmem),
        name=name,
    )(qi, kj, fl, *args)


def _flash_diff(q, k, v, lam_p, subln, lam_init, n_heads, n_k, q_off):
    w = 2 * LANES
    scratch = lambda tq: [pltpu.VMEM((2, tq, LANES), F32), pltpu.VMEM((2, tq, LANES), F32), pltpu.VMEM((2, tq, w), F32)]
    ops = [(q, w, True), (k, w, False), (v, w, False), (lam_p,), (subln.reshape(1, w),)]
    return _flash_call(_diff_kernel, n_heads, ops, q.shape[1], k.shape[1], w, scratch, n_k, q_off, False,
                       "diff_attn", lam_init=lam_init)


def _flash_fox(q, aq, kk, v, n_heads, n_k, q_off):
    w = LANES
    scratch = lambda tq: [pltpu.VMEM((tq, LANES), F32), pltpu.VMEM((tq, LANES), F32), pltpu.VMEM((tq, w), F32),
                          pltpu.VMEM((tq, 2 * LANES), BF16)]
    ops = [(q, w, True), (aq, w, True), (kk, 2 * w, False), (v, w, False)]
    return _flash_call(_fox_kernel, n_heads, ops, q.shape[1], kk.shape[1], w, scratch, n_k, q_off, True, "fox_attn")


def _flash_mla(qq, kk, v, n_heads, n_k, q_off):
    w = LANES
    scratch = lambda tq: [pltpu.VMEM((tq, LANES), F32), pltpu.VMEM((tq, LANES), F32), pltpu.VMEM((tq, w), F32)]
    ops = [(qq, 2 * w, True), (kk, 2 * w, False), (v, w, False)]
    return _flash_call(_mla_kernel, n_heads, ops, qq.shape[1], kk.shape[1], w, scratch, n_k, q_off, False, "mla_attn")


NEW_ROWS = LANES


def _cached_frame(m_ref, l_ref, acc_ref, chains, finish, *, nkb, tk, ta, q_off, n_k, causal):
    n = pl.program_id(1)

    @pl.when(n == 0)
    def _():
        m_ref[...] = jnp.full_like(m_ref, NEG_INF)
        l_ref[...] = jnp.zeros_like(l_ref)
        acc_ref[...] = jnp.zeros_like(acc_ref)

    @pl.when(n < nkb)
    def _():
        chains(True, tk, None)

    @pl.when(n == nkb)
    def _():
        mask_fn = lambda s: jnp.where(_visible(s.shape, q_off, q_off, n_k, causal, ta), s, NEG_INF)
        chains(False, NEW_ROWS, mask_fn)
        finish()


def _dec_diff_kernel(q_ref, kc_ref, vc_ref, kn_ref, vn_ref, lam_ref, sub_ref, o_ref,
                     m_ref, l_ref, acc_ref, *, n_heads, lam_init, **frame):
    hd, g, tk = LANES, 2 * n_heads, frame['tk']

    def chains(cached, n_keys, mask_fn):
        for h in range(n_heads):
            if cached:
                v = jnp.concatenate([vc_ref[pl.ds(2 * h, tk, stride=g), :],
                                     vc_ref[pl.ds(2 * h + 1, tk, stride=g), :]], axis=1).astype(BF16)
            else:
                v = vn_ref[:, 2 * h * hd:(2 * h + 2) * hd]
            for c in range(2):
                j = 2 * h + c
                k = kc_ref[pl.ds(j, tk, stride=g), :].astype(BF16) if cached else kn_ref[:, j * hd:(j + 1) * hd]
                s = _dot_nt(q_ref[:, j * hd:(j + 1) * hd], k)
                if mask_fn is not None:
                    s = mask_fn(s)
                _attend(s, v, m_ref, l_ref, acc_ref, j)

    def finish():
        lp = lam_ref[...]
        lam = (jnp.exp(jnp.sum(lp[0:1] * lp[1:2], axis=-1, keepdims=True))
               - jnp.exp(jnp.sum(lp[2:3] * lp[3:4], axis=-1, keepdims=True)) + lam_init)
        for h in range(n_heads):
            l0 = jnp.sum(l_ref[2 * h], axis=1, keepdims=True)
            l1 = jnp.sum(l_ref[2 * h + 1], axis=1, keepdims=True)
            o = acc_ref[2 * h] / l0 - lam * (acc_ref[2 * h + 1] / l1)
            o_ref[:, 2 * h * hd:(2 * h + 2) * hd] = (_rms(o, sub_ref[...]) * (1.0 - lam_init)).astype(o_ref.dtype)

    _cached_frame(m_ref, l_ref, acc_ref, chains, finish, **frame)


def _dec_fox_kernel(q_ref, kc_ref, vc_ref, kn_ref, vn_ref, fq_ref, fk_ref, o_ref,
                    m_ref, l_ref, acc_ref, *, n_heads, **frame):
    hd, g, tk = LANES, n_heads, frame['tk']

    def chains(cached, n_keys, mask_fn):
        for h in range(n_heads):
            if cached:
                k = kc_ref[pl.ds(h, tk, stride=g), :].astype(BF16)
                v = vc_ref[pl.ds(h, tk, stride=g), :].astype(BF16)
            else:
                k = kn_ref[:, h * hd:(h + 1) * hd]
                v = vn_ref[:, h * hd:(h + 1) * hd]
            bias = (fq_ref[:, h:h + 1] - fk_ref[h:h + 1, 0:n_keys]) * LOG2E
            s = _dot_nt(q_ref[:, h * hd:(h + 1) * hd], k) + bias
            if mask_fn is not None:
                s = mask_fn(s)
            _attend(s, v, m_ref, l_ref, acc_ref, h)

    def finish():
        for h in range(n_heads):
            o = acc_ref[h] / jnp.sum(l_ref[h], axis=1, keepdims=True)
            o_ref[:, h * hd:(h + 1) * hd] = o.astype(o_ref.dtype)

    _cached_frame(m_ref, l_ref, acc_ref, chains, finish, **frame)


def _cached_attn(kernel, q, cache_k, cache_v, new_k, new_v, extra, extra_specs, groups, v_width, n_k, causal,
                 name, **kw):
    b, ta, _ = q.shape
    past = cache_k.shape[1]
    tk = _pick_tile(past, 1024, LANES)
    nkb = past // tk
    kc = cache_k.reshape(b, past * groups, LANES)
    vc = cache_v.reshape(b, past * groups, LANES)
    kn, vn = _pad_rows(new_k, NEW_ROWS), _pad_rows(new_v, NEW_ROWS)
    whole = lambda a: pl.BlockSpec((None,) + a.shape[1:], lambda bi, n: (bi,) + (0,) * (a.ndim - 1))
    cache = pl.BlockSpec((None, tk * groups, LANES), lambda bi, n: (bi, jnp.minimum(n, nkb - 1), 0))
    out_w = q.shape[2]
    return pl.pallas_call(
        functools.partial(kernel, nkb=nkb, tk=tk, ta=ta, q_off=past, n_k=n_k, causal=causal, **kw),
        out_shape=jax.ShapeDtypeStruct((b, ta, out_w), BF16),
        grid=(b, nkb + 1),
        in_specs=[whole(q), cache, cache, whole(kn), whole(vn)] + extra_specs(tk),
        out_specs=pl.BlockSpec((None, ta, out_w), lambda bi, n: (bi, 0, 0)),
        scratch_shapes=[pltpu.VMEM((groups, ta, LANES), F32), pltpu.VMEM((groups, ta, LANES), F32),
                        pltpu.VMEM((groups, ta, v_width), F32)],
        compiler_params=_params(("parallel", "arbitrary"), 4 * tk * groups * LANES * 4 + (16 << 20)),
        name=name,
    )(q, kc, vc, kn, vn, *extra)


def _dec_mla_kernel(q_ref, ckv_ref, kp_ref, ckvn_ref, kpn_ref, wk_ref, wv_ref, gk_ref, o_ref,
                    m_ref, l_ref, acc_ref, s_ref, qp_ref, *, n_heads, **frame):
    hd, ta = LANES, frame['ta']

    @pl.when(pl.program_id(1) == 0)
    def _():
        for h in range(n_heads):
            qp_ref[h * ta:(h + 1) * ta, :] = q_ref[:, (2 * h + 1) * hd:(2 * h + 2) * hd]

    def chains(cached, n_keys, mask_fn):
        ckv = ckv_ref[...].astype(BF16) if cached else ckvn_ref[...]
        kp = kp_ref[...] if cached else kpn_ref[...]
        kvn = _dot(ckv, wk_ref[...])
        for h in range(n_heads):
            kn = _rms(kvn[:, h * hd:(h + 1) * hd], gk_ref[...]).astype(BF16)
            s_ref[h * ta:(h + 1) * ta, 0:n_keys] = _dot_nt(q_ref[:, 2 * h * hd:(2 * h + 1) * hd], kn)
        s = s_ref[:, 0:n_keys] + _dot_nt(qp_ref[...], kp)
        if mask_fn is not None:
            s = mask_fn(s)
        _attend(s, ckv, m_ref, l_ref, acc_ref, slice(None))

    def finish():
        lat = acc_ref[...] / jnp.sum(l_ref[...], axis=1, keepdims=True)
        for h in range(n_heads):
            o = _dot(lat[h * ta:(h + 1) * ta, :].astype(BF16), wv_ref[:, h * hd:(h + 1) * hd])
            o_ref[:, h * hd:(h + 1) * hd] = o.astype(o_ref.dtype)

    _cached_frame(m_ref, l_ref, acc_ref, chains, finish, **frame)


def _dec_mla(qq, cache_ckv, kp_cache, new_ckv, new_kp, w_kn, w_v, g_nope_k, n_heads, n_k):
    b, ta, _ = qq.shape
    past, c = cache_ckv.shape[1:]
    tk = _pick_tile(past, 512, LANES)
    nkb = past // tk
    rows = n_heads * ta
    ckvn, kpn = _pad_rows(new_ckv, NEW_ROWS), _pad_rows(new_kp, NEW_ROWS)
    whole = lambda a: pl.BlockSpec((None,) + a.shape[1:], lambda bi, n: (bi,) + (0,) * (a.ndim - 1))
    const = lambda a: pl.BlockSpec(a.shape, lambda bi, n: (0,) * a.ndim)
    blk = lambda w: pl.BlockSpec((None, tk, w), lambda bi, n: (bi, jnp.minimum(n, nkb - 1), 0))

    kernel = functools.partial(_dec_mla_kernel, n_heads=n_heads, nkb=nkb, tk=tk, ta=ta, q_off=past, n_k=n_k,
                               causal=False)
    gk = g_nope_k.reshape(1, LANES)
    return pl.pallas_call(
        kernel,
        out_shape=jax.ShapeDtypeStruct((b, ta, n_heads * LANES), BF16),
        grid=(b, nkb + 1),
        in_specs=[whole(qq), blk(c), blk(LANES), whole(ckvn), whole(kpn), const(w_kn), const(w_v), const(gk)],
        out_specs=pl.BlockSpec((None, ta, n_heads * LANES), lambda bi, n: (bi, 0, 0)),
        scratch_shapes=[pltpu.VMEM((rows, LANES), F32), pltpu.VMEM((rows, LANES), F32), pltpu.VMEM((rows, c), F32),
                        pltpu.VMEM((rows, tk), F32), pltpu.VMEM((rows, LANES), BF16)],
        compiler_params=_params(("parallel", "arbitrary"), 40 << 20),
        name="mla_attn_cached",
    )(qq, cache_ckv, kp_cache, ckvn, kpn, w_kn, w_v, gk)


def _cumsum_rows_kernel(x_ref, o_ref, carry_ref):
    @pl.when(pl.program_id(1) == 0)
    def _():
        carry_ref[...] = jnp.zeros_like(carry_ref)

    x = x_ref[...]
    tb = x.shape[1]
    r = lax.broadcasted_iota(jnp.int32, (tb, tb), 0)
    c = lax.broadcasted_iota(jnp.int32, (tb, tb), 1)
    tri = jnp.where(r <= c, 1.0, 0.0).astype(BF16)
    hi, mid, lo = _split3(x)
    cum = _dot(hi, tri) + _dot(mid, tri) + _dot(lo, tri) + carry_ref[:, 0:1]
    o_ref[...] = cum
    carry_ref[...] = jnp.broadcast_to(cum[:, tb - 1:tb], carry_ref.shape)


def _cumsum_rows(x, tb):
    b, g, t = x.shape
    return pl.pallas_call(
        _cumsum_rows_kernel,
        out_shape=jax.ShapeDtypeStruct((b, g, t), F32),
        grid=(b, t // tb),
        in_specs=[pl.BlockSpec((None, g, tb), lambda bi, i: (bi, 0, i))],
        out_specs=pl.BlockSpec((None, g, tb), lambda bi, i: (bi, 0, i)),
        scratch_shapes=[pltpu.VMEM((g, LANES), F32)],
        compiler_params=_params(("parallel", "arbitrary"), 32 << 20),
        name="forget_cumsum",
    )(x)


def _dec_diff(q, cache_k, cache_v, new_k, new_v, lam_p, subln, lam_init, n_heads, n_k):
    b, past = cache_k.shape[:2]
    specs = lambda tk: [pl.BlockSpec(lam_p.shape, lambda bi, n: (0, 0)),
                        pl.BlockSpec((1, 2 * LANES), lambda bi, n: (0, 0))]
    return _cached_attn(_dec_diff_kernel, q, cache_k, cache_v, new_k, new_v, [lam_p, subln.reshape(1, 2 * LANES)],
                        specs, 2 * n_heads, 2 * LANES, n_k, False, "diff_attn_cached",
                        n_heads=n_heads, lam_init=lam_init)


def _dec_fox(q, cache_k, cache_v, new_k, new_v, past_logf, new_logf, n_heads, n_k):
    b, past = cache_k.shape[:2]
    ta = q.shape[1]
    tk = _pick_tile(past, 1024, LANES)
    lf = jnp.concatenate([past_logf.astype(F32), new_logf[:, :, :n_heads]], axis=1)
    lf_rows = _pad_lanes(jnp.swapaxes(lf, 1, 2), past + tk)
    f_rows = _cumsum_rows(lf_rows, tk)
    f_q = _pad_lanes(jnp.swapaxes(f_rows[:, :, past:past + ta], 1, 2), LANES)
    specs = lambda tk: [pl.BlockSpec((None, ta, LANES), lambda bi, n: (bi, 0, 0)),
                        pl.BlockSpec((None, n_heads, tk), lambda bi, n: (bi, 0, n))]
    return _cached_attn(_dec_fox_kernel, q, cache_k, cache_v, new_k, new_v, [f_q, f_rows], specs,
                        n_heads, LANES, n_k, True, "fox_attn_cached", n_heads=n_heads)


def _rope_half(y, cos, sin, rope_dim):
    half = rope_dim // 2
    lane = lax.broadcasted_iota(jnp.int32, y.shape, 1)
    rot = jnp.where(lane < half, pltpu.roll(y, LANES - half, 1), pltpu.roll(y, half, 1))
    return y * cos + rot * sin


def _rms_low(x, gain, n):
    return x * lax.rsqrt(jnp.sum(x * x, axis=-1, keepdims=True) * (1.0 / n) + EPS) * gain


def _odd_post_kernel(z_ref, cos_ref, sin_ref, gq_ref, gkv_ref, gr_ref,
                     cq_ref, ckvf_ref, ckvb_ref, kpf_ref, kpb_ref, *, q_lora, kv_lora, rope_dim):
    cq_ref[...] = _rms(z_ref[:, 0:q_lora], gq_ref[...]).astype(BF16)
    ckv = _rms(z_ref[:, q_lora:q_lora + kv_lora], gkv_ref[...])
    ckvf_ref[...] = ckv
    ckvb_ref[...] = ckv.astype(BF16)
    kp = _rms_low(z_ref[:, q_lora + kv_lora:q_lora + kv_lora + LANES], gr_ref[...], rope_dim)
    kp = _rope_half(kp, cos_ref[...], sin_ref[...], rope_dim)
    kpf_ref[...] = kp
    kpb_ref[...] = kp.astype(BF16)


def _odd_post(z, cos, sin, g_cq, g_ckv, g_rope_k_pad, q_lora, kv_lora, rope_dim):
    b, t, n = z.shape
    tm = _pick_tile(t, 512, 16)
    row = lambda w: pl.BlockSpec((None, tm, w), lambda bi, i: (bi, i, 0))
    tab = pl.BlockSpec((tm, LANES), lambda bi, i: (i, 0))
    vec = lambda w: pl.BlockSpec((1, w), lambda bi, i: (0, 0))
    outs = [(q_lora, BF16), (kv_lora, F32), (kv_lora, BF16), (LANES, F32), (LANES, BF16)]
    return pl.pallas_call(
        functools.partial(_odd_post_kernel, q_lora=q_lora, kv_lora=kv_lora, rope_dim=rope_dim),
        out_shape=[jax.ShapeDtypeStruct((b, t, w), dt) for w, dt in outs],
        grid=(b, t // tm),
        in_specs=[row(n), tab, tab, vec(q_lora), vec(kv_lora), vec(LANES)],
        out_specs=[row(w) for w, _ in outs],
        compiler_params=_params(("parallel", "parallel"), 32 << 20),
        name="odd_post",
    )(z, cos, sin, g_cq.reshape(1, -1), g_ckv.reshape(1, -1), g_rope_k_pad)


def _qup_kernel(cq_ref, w_ref, cos_ref, sin_ref, gn_ref, gr_ref, qq_ref, *, heads, rope_dim, q_scale):
    q = _dot(cq_ref[...], w_ref[...])
    cos, sin = cos_ref[...], sin_ref[...]
    for h in range(heads):
        qn = _rms(q[:, 2 * h * LANES:(2 * h + 1) * LANES], gn_ref[...])
        qq_ref[:, 2 * h * LANES:(2 * h + 1) * LANES] = (qn * q_scale).astype(BF16)
        qp = _rms_low(q[:, (2 * h + 1) * LANES:(2 * h + 2) * LANES], gr_ref[...], rope_dim)
        qp = _rope_half(qp, cos, sin, rope_dim)
        qq_ref[:, (2 * h + 1) * LANES:(2 * h + 2) * LANES] = (qp * q_scale).astype(BF16)


def _qup(cq, w_pad, cos, sin, g_nope_q, g_rope_q_pad, n_heads, rope_dim, qk_dim):
    b, t, kq = cq.shape
    tm = _pick_tile(t, 512, 16)
    hg = 4 if n_heads % 4 == 0 else 1
    tn = hg * 2 * LANES
    return pl.pallas_call(
        functools.partial(_qup_kernel, heads=hg, rope_dim=rope_dim, q_scale=qk_dim ** -0.5 * LOG2E),
        out_shape=jax.ShapeDtypeStruct((b, t, n_heads * 2 * LANES), BF16),
        grid=(b, t // tm, n_heads // hg),
        in_specs=[
            pl.BlockSpec((None, tm, kq), lambda bi, i, j: (bi, i, 0)),
            pl.BlockSpec((kq, tn), lambda bi, i, j: (0, j)),
            pl.BlockSpec((tm, LANES), lambda bi, i, j: (i, 0)),
            pl.BlockSpec((tm, LANES), lambda bi, i, j: (i, 0)),
            pl.BlockSpec((1, LANES), lambda bi, i, j: (0, 0)),
            pl.BlockSpec((1, LANES), lambda bi, i, j: (0, 0)),
        ],
        out_specs=pl.BlockSpec((None, tm, tn), lambda bi, i, j: (bi, i, j)),
        compiler_params=_params(("parallel", "parallel", "arbitrary"), 32 << 20),
        name="mla_q_up",
    )(cq, w_pad, cos, sin, g_nope_q.reshape(1, LANES), g_rope_q_pad)


def _kvup_kernel(ckv_ref, kp_ref, w_ref, gn_ref, kk_ref, v_ref, *, heads):
    kv = _dot(ckv_ref[...], w_ref[...])
    kp = kp_ref[...]
    for h in range(heads):
        kn = _rms(kv[:, 2 * h * LANES:(2 * h + 1) * LANES], gn_ref[...])
        kk_ref[:, 2 * h * LANES:(2 * h + 1) * LANES] = kn.astype(BF16)
        kk_ref[:, (2 * h + 1) * LANES:(2 * h + 2) * LANES] = kp
        v_ref[:, h * LANES:(h + 1) * LANES] = kv[:, (2 * h + 1) * LANES:(2 * h + 2) * LANES].astype(BF16)


def _kvup(ckv, kp, w, g_nope_k, n_heads):
    b, t, kk = ckv.shape
    tm = _pick_tile(t, 512, LANES)
    hg = 4 if n_heads % 4 == 0 else 1
    tn = hg * 2 * LANES
    return pl.pallas_call(
        functools.partial(_kvup_kernel, heads=hg),
        out_shape=[jax.ShapeDtypeStruct((b, t, n_heads * 2 * LANES), BF16),
                   jax.ShapeDtypeStruct((b, t, n_heads * LANES), BF16)],
        grid=(b, t // tm, n_heads // hg),
        in_specs=[
            pl.BlockSpec((None, tm, kk), lambda bi, i, j: (bi, i, 0)),
            pl.BlockSpec((None, tm, LANES), lambda bi, i, j: (bi, i, 0)),
            pl.BlockSpec((kk, tn), lambda bi, i, j: (0, j)),
            pl.BlockSpec((1, LANES), lambda bi, i, j: (0, 0)),
        ],
        out_specs=[pl.BlockSpec((None, tm, tn), lambda bi, i, j: (bi, i, j)),
                   pl.BlockSpec((None, tm, hg * LANES), lambda bi, i, j: (bi, i, j))],
        compiler_params=_params(("parallel", "parallel", "arbitrary"), 32 << 20),
        name="mla_kv_up",
    )(ckv, kp, w, g_nope_k.reshape(1, LANES))


def _rope_tables(pos, dim):
    half = dim // 2
    inv = ROPE_THETA ** (-jnp.arange(half, dtype=F32) * 2.0 / dim)
    ang = pos.astype(F32)[:, None] * inv[None, :]
    cos, sin = jnp.cos(ang), jnp.sin(ang)
    pad = ((0, 0), (0, LANES - dim))
    return (jnp.pad(jnp.concatenate([cos, cos], axis=-1), pad),
            jnp.pad(jnp.concatenate([-sin, sin], axis=-1), pad))


def _pad_lanes(a, width):
    return jnp.pad(a, [(0, 0)] * (a.ndim - 1) + [(0, width - a.shape[-1])])


def _pad_rows(a, rows):
    return jnp.pad(a, [(0, 0), (0, rows - a.shape[1])] + [(0, 0)] * (a.ndim - 2))


def _layer_stack(x, c_mod, tok_pos, seq_shape, caches, p):
    bx, tx, d = x.shape
    ba, ta = seq_shape
    depth = p['w_ffn_in'].shape[0]
    n_diff = p['n_diff']
    n_fox = p['n_fox']
    n_mla = p['n_mla']
    past_len = 0 if caches is None else caches[0].shape[2]
    n_k = past_len + ta
    cos128, sin128 = _rope_tables(tok_pos, LANES)
    rope_dim = p['rope_dim']
    cos_r, sin_r = _rope_tables(tok_pos, rope_dim)
    new = [[] for _ in range(7)]

    def mods(l, s):
        m = c_mod[l]
        sh, sc, gt = m[:, 3 * s], m[:, 3 * s + 1], m[:, 3 * s + 2]
        if bx == m.shape[0]:
            return tuple(a[:, None, :] for a in (sh, sc, gt))
        rep = lambda a: jnp.repeat(a, ta, axis=0).reshape(bx, tx, d)
        return rep(sh), rep(sc), rep(gt)

    seq = lambda a: a.reshape(ba, ta, a.shape[-1])
    for l in range(depth):
        i = l // 2
        g = p['norm_gains'][l]
        sh, sc, gt = mods(l, 0)
        x = _ffn(x, sh, sc, gt, g[0], p['w_ffn_in'], p['w_ffn_out'], l, 0, 0.5)
        sh, sc, gt = mods(l, 1)
        if l % 2 == 0:
            z = _modproj(x, sh, sc, g[1], p['w_in_even'][i])
            (qa, kaf, kab, vaf, vab, qb, kbf, kbb, vbf, vbb, lf) = _even_post(
                z, cos128, sin128, p['qk_norm_even'][i], p['b_forget_pad'][i], 2 * n_diff, n_fox)
            new[0].append(kaf.reshape(ba, ta, n_diff, 2, LANES))
            new[1].append(vaf.reshape(ba, ta, n_diff, 2 * LANES))
            new[2].append(kbf.reshape(ba, ta, n_fox, LANES))
            new[3].append(vbf.reshape(ba, ta, n_fox, LANES))
            new[4].append(seq(lf)[:, :, :n_fox])
            lam_init = 0.8 - 0.6 * math.exp(-0.3 * l)
            if caches is None:
                aq, kk = _fox_prep(lf, kbb, n_fox)
                oa = _flash_diff(qa, kab, vab, p['diff_lambda'][i], p['diff_subln'][i], lam_init, n_diff, n_k, 0)
                ob = _flash_fox(qb, aq, kk, vbb, n_fox, n_k, 0)
            else:
                past = tuple(a[i] for a in caches[:5])
                oa = _dec_diff(seq(qa), past[0], past[1], seq(kab), seq(vab), p['diff_lambda'][i],
                               p['diff_subln'][i], lam_init, n_diff, n_k)
                ob = _dec_fox(seq(qb), past[2], past[3], seq(kbb), seq(vbb), past[4], seq(lf), n_fox, n_k)
            x = _outproj(oa.reshape(bx, tx, -1), ob.reshape(bx, tx, -1), 0, 0, p['w_out_even'][i], x, gt)
        else:
            q_lora, kv_lora = p['q_lora'], p['kv_lora']
            z = _modproj(x, sh, sc, g[1], p['w_in_odd'][i])
            cq, ckvf, ckvb, kpf, kpb = _odd_post(z, cos_r, sin_r, p['mla_cq_norm'][i], p['mla_ckv_norm'][i],
                                                 p['g_rope_pad'][i, 1:2], q_lora, kv_lora, rope_dim)
            new[5].append(seq(ckvf))
            new[6].append(seq(kpf)[:, :, :rope_dim])
            qq = _qup(cq, p['w_uq_pad'][i], cos_r, sin_r, p['mla_qk_norm_nope'][i, 0],
                      p['g_rope_pad'][i, 0:1], n_mla, rope_dim, p['mla_qk_dim'])
            if caches is None:
                kk, v = _kvup(ckvb, kpb, p['w_ukv'][i], p['mla_qk_norm_nope'][i, 1], n_mla)
                o = _flash_mla(qq, kk, v, n_mla, n_k, 0)
            else:
                kp_cache = _pad_lanes(caches[6][i], LANES).astype(BF16)
                o = _dec_mla(seq(qq), caches[5][i], kp_cache, seq(ckvb), seq(kpb), p['w_kn'][i], p['w_v'][i],
                             p['mla_qk_norm_nope'][i, 1], n_mla, n_k)
            o = o.reshape(bx, tx, -1)
            x = _outproj(o, o, 0, 1, p['w_out_odd'][i], x, gt)
        sh, sc, gt = mods(l, 2)
        x = _ffn(x, sh, sc, gt, g[2], p['w_ffn_in'], p['w_ffn_out'], l, 1, 0.5, final_gain=g[3])
    return x, tuple(jnp.stack(lst) for lst in new)


def kernel(x_prompt, x_sample, c_prompt, c_sample, cache_diff_k, cache_diff_v, cache_fox_k, cache_fox_v, cache_fox_logf, cache_mla_ckv, cache_mla_kpe, w_ada, b_ada, norm_gains, w_ffn_in, w_ffn_out, w_in_even, b_forget, qk_norm_even, diff_lambda, diff_subln, w_out_even, w_in_odd, mla_cq_norm, mla_ckv_norm, w_uq, w_ukv, mla_qk_norm_nope, mla_qk_norm_rope, w_out_odd):
    d = x_prompt.shape[-1]
    n_diff, n_fox = cache_diff_k.shape[3], cache_fox_k.shape[3]
    assert cache_diff_k.shape[-1] == LANES and cache_fox_k.shape[-1] == LANES
    q_lora, kv_lora = mla_cq_norm.shape[-1], mla_ckv_norm.shape[-1]
    rope_dim, nope = cache_mla_kpe.shape[-1], mla_qk_norm_nope.shape[-1]
    n_mla = w_uq.shape[-1] // (nope + rope_dim)
    assert nope == LANES and rope_dim <= LANES and w_ukv.shape[-1] == n_mla * 2 * LANES
    n_odd = w_uq.shape[0]

    w_uq_pad = _pad_lanes(w_uq.reshape(n_odd, q_lora, n_mla, nope + rope_dim), 2 * LANES)
    p = {
        'n_diff': n_diff, 'n_fox': n_fox, 'n_mla': n_mla, 'rope_dim': rope_dim,
        'q_lora': q_lora, 'kv_lora': kv_lora, 'mla_qk_dim': nope + rope_dim,
        'norm_gains': norm_gains,
        'w_ffn_in': w_ffn_in.astype(BF16), 'w_ffn_out': w_ffn_out.astype(BF16),
        'w_in_even': _pad_lanes(w_in_even, _round_up(w_in_even.shape[-1], LANES)).astype(BF16),
        'b_forget_pad': _pad_lanes(b_forget, LANES)[:, None, :],
        'qk_norm_even': qk_norm_even, 'diff_lambda': diff_lambda, 'diff_subln': diff_subln,
        'w_out_even': w_out_even.astype(BF16),
        'w_in_odd': _pad_lanes(w_in_odd, q_lora + kv_lora + LANES).astype(BF16),
        'mla_cq_norm': mla_cq_norm, 'mla_ckv_norm': mla_ckv_norm,
        'w_uq_pad': w_uq_pad.reshape(n_odd, q_lora, n_mla * 2 * LANES).astype(BF16),
        'w_ukv': w_ukv.astype(BF16),
        'w_kn': w_ukv.reshape(n_odd, kv_lora, n_mla, 2 * LANES)[..., :LANES].reshape(n_odd, kv_lora, -1).astype(BF16),
        'w_v': w_ukv.reshape(n_odd, kv_lora, n_mla, 2 * LANES)[..., LANES:].reshape(n_odd, kv_lora, -1).astype(BF16),
        'mla_qk_norm_nope': mla_qk_norm_nope,
        'g_rope_pad': _pad_lanes(mla_qk_norm_rope, LANES),
        'w_out_odd': w_out_odd.astype(BF16),
    }

    bp, tp = x_prompt.shape[:2]
    bs, ts = x_sample.shape[:2]
    past_len = cache_diff_k.shape[2]
    mod = _ada(jnp.concatenate([c_prompt, c_sample], axis=0), w_ada, b_ada)
    mod = mod.reshape(mod.shape[0], bp + bs, N_MOD, d)

    pos_p = jnp.arange(tp, dtype=jnp.int32)
    y_prompt, st_p = _layer_stack(x_prompt, mod[:, :bp], pos_p, (bp, tp), None, p)

    pos_s = jnp.tile(past_len + jnp.arange(ts, dtype=jnp.int32), bs)
    caches = (cache_diff_k, cache_diff_v, cache_fox_k, cache_fox_v, cache_fox_logf, cache_mla_ckv, cache_mla_kpe)
    y_sample, st_s = _layer_stack(x_sample.reshape(1, bs * ts, d), mod[:, bp:], pos_s, (bs, ts), caches, p)
    return (y_prompt, y_sample.reshape(bs, ts, d)) + st_p + st_s
```

```python
import functools
import math

import numpy as np
import jax
import jax.numpy as jnp
from jax import lax
from jax.experimental import pallas as pl
from jax.experimental.pallas import tpu as pltpu

F32 = jnp.float32
BF16 = jnp.bfloat16

CHUNK = 64
ROPE_THETA = 10000.0
EPS = 1e-6
NEG_INF = -1e30
N_MOD = 9

LANES = 128
SUBLANES = 8
VMEM_CAP_BYTES = 56 * 1024 * 1024

LOG2E = math.log2(math.e)
CHUNK_SHIFT = CHUNK.bit_length() - 1
assert (1 << CHUNK_SHIFT) == CHUNK


def _round_up(n, m):
    return (n + m - 1) // m * m


def _pick_tile(n, target, quantum):
    if n <= target:
        return n
    best = None
    t = quantum
    while t <= target:
        if n % t == 0:
            best = t
        t += quantum
    assert best is not None, (n, target, quantum)
    return best


def _params(semantics, vmem_bytes):
    limit = int(min(max(vmem_bytes, 16 * 1024 * 1024), VMEM_CAP_BYTES))
    return pltpu.CompilerParams(dimension_semantics=semantics, vmem_limit_bytes=limit)


def _rms(x, gain):
    return x * lax.rsqrt(jnp.mean(x * x, axis=-1, keepdims=True) + EPS) * gain


def _silu(g):
    return g / (1.0 + jnp.exp(-g))


def _dot(a, b):
    return jnp.dot(a, b, preferred_element_type=F32)


def _dot_nt(a, b):
    return lax.dot_general(a, b, (((1,), (1,)), ((), ())), preferred_element_type=F32)


def _ada_kernel(c_ref, w_ref, b_ref, o_ref):
    a = _silu(c_ref[...]).astype(BF16)
    o_ref[...] = _dot(a, w_ref[...].astype(BF16)) + b_ref[...]


def _ada(c_all, w_ada, b_ada):
    depth, d, n = w_ada.shape
    r = c_all.shape[0]
    tn = _pick_tile(n, 1024, LANES)
    return pl.pallas_call(
        _ada_kernel,
        out_shape=jax.ShapeDtypeStruct((depth, r, n), F32),
        grid=(depth, n // tn),
        in_specs=[
            pl.BlockSpec((r, d), lambda l, j: (0, 0)),
            pl.BlockSpec((None, d, tn), lambda l, j: (l, 0, j)),
            pl.BlockSpec((None, 1, tn), lambda l, j: (l, 0, j)),
        ],
        out_specs=pl.BlockSpec((None, r, tn), lambda l, j: (l, 0, j)),
        compiler_params=_params(("arbitrary", "arbitrary"), 2 * d * tn * 4 + 3 * d * tn * 2 + (4 << 20)),
        name="ada_mod",
    )(c_all, w_ada, b_ada.reshape(depth, 1, n))


def _mod_spec(mod, tm):
    d = mod.shape[-1]
    if mod.shape[1] == 1:
        return pl.BlockSpec((None, 1, d), lambda b, i, j: (b, 0, 0))
    return pl.BlockSpec((None, tm, d), lambda b, i, j: (b, i, 0))


def _ffn_kernel(x_ref, sh_ref, sc_ref, gt_ref, g_ref, wg_ref, wu_ref, wo_ref, *rest,
                gate_mul, final_norm):
    if final_norm:
        gf_ref, o_ref, h_ref, acc_ref = rest
    else:
        o_ref, h_ref, acc_ref = rest
    f = pl.program_id(2)

    @pl.when(f == 0)
    def _():
        h = _rms(x_ref[...], g_ref[...]) * (1.0 + sc_ref[...]) + sh_ref[...]
        h_ref[...] = h.astype(BF16)
        acc_ref[...] = jnp.zeros_like(acc_ref)

    h = h_ref[...]
    g = _dot(h, wg_ref[...])
    u = _dot(h, wu_ref[...])
    a = (_silu(g) * u).astype(BF16)
    acc_ref[...] += _dot(a, wo_ref[...])

    @pl.when(f == pl.num_programs(2) - 1)
    def _():
        xn = x_ref[...] + (gate_mul * gt_ref[...]) * acc_ref[...]
        if final_norm:
            xn = _rms(xn, gf_ref[...])
        o_ref[...] = xn


def _ffn(x, sh, sc, gt, gain, w_in, w_out, layer, sub, gate_mul, final_gain=None):
    b, t, d = x.shape
    ff = w_out.shape[2]
    tm = _pick_tile(t, 512, 16)
    tf = _pick_tile(ff, 512, LANES)
    nf = ff // tf
    in_specs = [
        pl.BlockSpec((None, tm, d), lambda bi, i, f: (bi, i, 0)),
        _mod_spec(sh, tm), _mod_spec(sc, tm), _mod_spec(gt, tm),
        pl.BlockSpec((1, d), lambda bi, i, f: (0, 0)),
        pl.BlockSpec((None, None, d, tf), lambda bi, i, f: (layer, sub, 0, f)),
        pl.BlockSpec((None, None, d, tf), lambda bi, i, f: (layer, sub, 0, nf + f)),
        pl.BlockSpec((None, None, tf, d), lambda bi, i, f: (layer, sub, f, 0)),
    ]
    args = [x, sh, sc, gt, gain.reshape(1, d), w_in, w_in, w_out]
    if final_gain is not None:
        in_specs.append(pl.BlockSpec((1, d), lambda bi, i, f: (0, 0)))
        args.append(final_gain.reshape(1, d))
    vmem = (4 * tm * d * 4 + tm * d * 2 + tm * d * 4 + 6 * d * tf * 2 + 4 * tm * tf * 4
            + 6 * tm * d * 4 * (sh.shape[1] != 1) + (4 << 20))
    return pl.pallas_call(
        functools.partial(_ffn_kernel, gate_mul=gate_mul, final_norm=final_gain is not None),
        out_shape=jax.ShapeDtypeStruct((b, t, d), F32),
        grid=(b, t // tm, nf),
        in_specs=in_specs,
        out_specs=pl.BlockSpec((None, tm, d), lambda bi, i, f: (bi, i, 0)),
        scratch_shapes=[pltpu.VMEM((tm, d), BF16), pltpu.VMEM((tm, d), F32)],
        compiler_params=_params(("parallel", "parallel", "arbitrary"), vmem),
        name="ffn",
    )(*args)


def _modproj_kernel(x_ref, sh_ref, sc_ref, g_ref, w_ref, o_ref, h_ref):
    @pl.when(pl.program_id(2) == 0)
    def _():
        h = _rms(x_ref[...], g_ref[...]) * (1.0 + sc_ref[...]) + sh_ref[...]
        h_ref[...] = h.astype(BF16)

    o_ref[...] = _dot(h_ref[...], w_ref[...])


def _modproj(x, sh, sc, gain, w):
    b, t, d = x.shape
    n = w.shape[1]
    tm = _pick_tile(t, 512, 16)
    tn = _pick_tile(n, 1280, LANES)
    vmem = (2 * tm * d * 4 + tm * d * 2 + 2 * d * tn * 2 + 3 * tm * tn * 4
            + 4 * tm * d * 4 * (sh.shape[1] != 1) + 3 * tm * d * 4 + (4 << 20))
    return pl.pallas_call(
        _modproj_kernel,
        out_shape=jax.ShapeDtypeStruct((b, t, n), F32),
        grid=(b, t // tm, n // tn),
        in_specs=[
            pl.BlockSpec((None, tm, d), lambda bi, i, j: (bi, i, 0)),
            _mod_spec(sh, tm), _mod_spec(sc, tm),
            pl.BlockSpec((1, d), lambda bi, i, j: (0, 0)),
            pl.BlockSpec((d, tn), lambda bi, i, j: (0, j)),
        ],
        out_specs=pl.BlockSpec((None, tm, tn), lambda bi, i, j: (bi, i, j)),
        scratch_shapes=[pltpu.VMEM((tm, d), BF16)],
        compiler_params=_params(("parallel", "parallel", "arbitrary"), vmem),
        name="mod_proj",
    )(x, sh, sc, gain.reshape(1, d), w)


def _outproj_kernel(a1_ref, a2_ref, w1_ref, w2_ref, x_ref, gt_ref, o_ref):
    y = _dot(a1_ref[...], w1_ref[...]) + _dot(a2_ref[...], w2_ref[...])
    o_ref[...] = x_ref[...] + gt_ref[...] * y


def _outproj(a1, a2, blk1, blk2, w, x, gt):
    b, t, d = x.shape
    kh = w.shape[0] // 2
    tm = _pick_tile(t, 512, 16)
    vmem = 4 * tm * kh * 2 + 4 * kh * d * 2 + 5 * tm * d * 4 + 2 * tm * d * 4 * (gt.shape[1] != 1) + (4 << 20)
    return pl.pallas_call(
        _outproj_kernel,
        out_shape=jax.ShapeDtypeStruct((b, t, d), F32),
        grid=(b, t // tm, 1),
        in_specs=[
            pl.BlockSpec((None, tm, kh), lambda bi, i, j: (bi, i, blk1)),
            pl.BlockSpec((None, tm, kh), lambda bi, i, j: (bi, i, blk2)),
            pl.BlockSpec((kh, d), lambda bi, i, j: (0, 0)),
            pl.BlockSpec((kh, d), lambda bi, i, j: (1, 0)),
            pl.BlockSpec((None, tm, d), lambda bi, i, j: (bi, i, 0)),
            _mod_spec(gt, tm),
        ],
        out_specs=pl.BlockSpec((None, tm, d), lambda bi, i, j: (bi, i, 0)),
        compiler_params=_params(("parallel", "parallel", "arbitrary"), vmem),
        name="out_proj",
    )(a1, a2, w, w, x, gt)


def _rope128(y, cos, sin):
    return y * cos + pltpu.roll(y, LANES // 2, 1) * sin


def _log_sigmoid(x):
    return jnp.minimum(x, 0.0) - jnp.log(1.0 + jnp.exp(-jnp.abs(x)))


def _even_post_kernel(z_ref, cos_ref, sin_ref, g_ref, bf_ref,
                      qa_ref, kaf_ref, kab_ref, vaf_ref, vab_ref,
                      qb_ref, kbf_ref, kbb_ref, vbf_ref, vbb_ref, lf_ref,
                      *, n_qa, n_fox, n_forget, q_scale):
    cos = cos_ref[...]
    sin = sin_ref[...]
    hd = LANES
    wa = n_qa * hd
    wb = n_fox * hd
    for j in range(n_qa):
        q = _rope128(_rms(z_ref[:, j * hd:(j + 1) * hd], g_ref[0:1, :]), cos, sin)
        qa_ref[:, j * hd:(j + 1) * hd] = (q * q_scale).astype(BF16)
        k = _rope128(_rms(z_ref[:, wa + j * hd:wa + (j + 1) * hd], g_ref[1:2, :]), cos, sin)
        kaf_ref[:, j * hd:(j + 1) * hd] = k
        kab_ref[:, j * hd:(j + 1) * hd] = k.astype(BF16)
    va = z_ref[:, 2 * wa:3 * wa]
    vaf_ref[...] = va
    vab_ref[...] = va.astype(BF16)
    o = 3 * wa
    for j in range(n_fox):
        q = _rms(z_ref[:, o + j * hd:o + (j + 1) * hd], g_ref[2:3, :])
        qb_ref[:, j * hd:(j + 1) * hd] = (q * q_scale).astype(BF16)
        k = _rms(z_ref[:, o + wb + j * hd:o + wb + (j + 1) * hd], g_ref[3:4, :])
        kbf_ref[:, j * hd:(j + 1) * hd] = k
        kbb_ref[:, j * hd:(j + 1) * hd] = k.astype(BF16)
    vb = z_ref[:, o + 2 * wb:o + 3 * wb]
    vbf_ref[...] = vb
    vbb_ref[...] = vb.astype(BF16)
    fg = z_ref[:, o + 3 * wb:o + 3 * wb + LANES] + bf_ref[...]
    lane = lax.broadcasted_iota(jnp.int32, fg.shape, 1)
    lf_ref[...] = jnp.where(lane < n_forget, _log_sigmoid(fg), 0.0)


def _even_post(z, cos, sin, qk_gain, b_forget_pad, n_qa, n_fox):
    b, t, n = z.shape
    wa, wb = n_qa * LANES, n_fox * LANES
    tm = _pick_tile(t, 256, 16)
    row = lambda w: pl.BlockSpec((None, tm, w), lambda bi, i: (bi, i, 0))
    tab = pl.BlockSpec((tm, LANES), lambda bi, i: (i, 0))
    outs = [(wa, BF16), (wa, F32), (wa, BF16), (wa, F32), (wa, BF16),
            (wb, BF16), (wb, F32), (wb, BF16), (wb, F32), (wb, BF16), (LANES, F32)]
    vmem = 2 * tm * n * 4 + 2 * sum(tm * w * jnp.dtype(dt).itemsize for w, dt in outs) + (8 << 20)
    return pl.pallas_call(
        functools.partial(_even_post_kernel, n_qa=n_qa, n_fox=n_fox, n_forget=n_fox,
                          q_scale=LANES ** -0.5 * LOG2E),
        out_shape=[jax.ShapeDtypeStruct((b, t, w), dt) for w, dt in outs],
        grid=(b, t // tm),
        in_specs=[row(n), tab, tab,
                  pl.BlockSpec((4, LANES), lambda bi, i: (0, 0)),
                  pl.BlockSpec((1, LANES), lambda bi, i: (0, 0))],
        out_specs=[row(w) for w, _ in outs],
        compiler_params=_params(("parallel", "parallel"), vmem),
        name="even_post",
    )(z, cos, sin, qk_gain, b_forget_pad)


def _split3(x):
    hi = x.astype(BF16)
    r = x - hi.astype(F32)
    mid = r.astype(BF16)
    lo = (r - mid.astype(F32)).astype(BF16)
    return hi, mid, lo


def _fox_prep_kernel(lf_ref, k_ref, v_ref, aq_ref, kk_ref, vv_ref, carry_ref, *, n_heads):
    @pl.when(pl.program_id(1) == 0)
    def _():
        carry_ref[...] = jnp.zeros_like(carry_ref)

    x = lf_ref[...]
    tb = x.shape[0]
    r = lax.broadcasted_iota(jnp.int32, (tb, tb), 0)
    c = lax.broadcasted_iota(jnp.int32, (tb, tb), 1)
    tri = jnp.where(r >= c, 1.0, 0.0).astype(BF16)
    hi, mid, lo = _split3(x)
    cum = _dot(tri, hi) + _dot(tri, mid) + _dot(tri, lo) + carry_ref[...]
    carry_ref[...] = cum[tb - 1:tb, :]
    fh, fm, fl = (p.astype(F32) for p in _split3(cum * LOG2E))
    lane = lax.broadcasted_iota(jnp.int32, (tb, LANES), 1)
    ones_q = jnp.where((lane >= 3) & (lane < 6), 1.0, 0.0)
    ones_k = jnp.where(lane < 3, 1.0, 0.0)
    for h in range(n_heads):
        a, m, l = fh[:, h:h + 1], fm[:, h:h + 1], fl[:, h:h + 1]
        aq = jnp.where(lane == 0, a, jnp.where(lane == 1, m, jnp.where(lane == 2, l, ones_q)))
        ak = jnp.where(lane == 3, -a, jnp.where(lane == 4, -m, jnp.where(lane == 5, -l, ones_k)))
        aq_ref[:, h * LANES:(h + 1) * LANES] = aq.astype(BF16)
        kk_ref[:, 2 * h * LANES:(2 * h + 1) * LANES] = k_ref[:, h * LANES:(h + 1) * LANES]
        kk_ref[:, (2 * h + 1) * LANES:(2 * h + 2) * LANES] = ak.astype(BF16)
        vv_ref[:, 2 * h * LANES:(2 * h + 1) * LANES] = v_ref[:, h * LANES:(h + 1) * LANES]
        vv_ref[:, (2 * h + 1) * LANES:(2 * h + 2) * LANES] = jnp.ones((tb, LANES), BF16)


def _fox_prep(logf_pad, k_all, v_all, n_heads):
    b, t, _ = logf_pad.shape
    tb = _pick_tile(t, 512, LANES)
    w = n_heads * LANES
    narrow = pl.BlockSpec((None, tb, w), lambda bi, i: (bi, i, 0))
    wide = pl.BlockSpec((None, tb, 2 * w), lambda bi, i: (bi, i, 0))
    return pl.pallas_call(
        functools.partial(_fox_prep_kernel, n_heads=n_heads),
        out_shape=[jax.ShapeDtypeStruct((b, t, w), BF16), jax.ShapeDtypeStruct((b, t, 2 * w), BF16),
                   jax.ShapeDtypeStruct((b, t, 2 * w), BF16)],
        grid=(b, t // tb),
        in_specs=[pl.BlockSpec((None, tb, LANES), lambda bi, i: (bi, i, 0)), narrow, narrow],
        out_specs=[narrow, wide, wide],
        scratch_shapes=[pltpu.VMEM((1, LANES), F32)],
        compiler_params=_params(("parallel", "arbitrary"), 32 << 20),
        name="fox_prep",
    )(logf_pad, k_all, v_all)


FLAG_FIRST, FLAG_LAST = 1, 2
KIND_SHIFT = 2
KIND_FULL, KIND_MASK, KIND_DIAG = 0, 1, 2


def _pair_table(n_q, n_k, tq, tk, q_off, causal):
    tk_pad = _round_up(n_k, tk)
    aligned = tq == tk and q_off % tq == 0
    qi, kj, fl = [], [], []
    for i in range(n_q // tq):
        qmin, qmax = q_off + i * tq, q_off + (i + 1) * tq - 1
        row = []
        for j in range(tk_pad // tk):
            kmin, kmax = j * tk, min((j + 1) * tk, n_k) - 1
            if kmin >= n_k:
                continue
            if causal:
                any_vis, all_vis = kmin <= qmax, kmax <= qmin
            else:
                any_vis, all_vis = kmin // CHUNK <= qmax // CHUNK, kmax // CHUNK <= qmin // CHUNK
            all_vis = all_vis and (j + 1) * tk <= n_k
            if any_vis:
                diag = aligned and kmin == qmin and (j + 1) * tk <= n_k
                row.append((j, KIND_FULL if all_vis else KIND_DIAG if diag else KIND_MASK))
        assert row and row[0][0] == 0
        for idx, (j, kind) in enumerate(row):
            qi.append(i)
            kj.append(j)
            fl.append((kind << KIND_SHIFT) | (FLAG_FIRST if idx == 0 else 0) | (FLAG_LAST if idx == len(row) - 1 else 0))
    kinds = sorted({f >> KIND_SHIFT for f in fl})
    as_arr = lambda v: jnp.asarray(np.array(v, np.int32))
    return as_arr(qi), as_arr(kj), as_arr(fl), kinds


def _visible(shape, qpos0, kpos0, n_k, causal, row_period=None):
    if row_period is None:
        rows = qpos0 + lax.broadcasted_iota(jnp.int32, shape, 0)
    else:
        one = lax.broadcasted_iota(jnp.int32, (row_period, shape[1]), 0)
        rows = qpos0 + jnp.concatenate([one] * (shape[0] // row_period), axis=0)
    cols = kpos0 + lax.broadcasted_iota(jnp.int32, shape, 1)
    if causal:
        ok = cols <= rows
    else:
        ok = (cols >> CHUNK_SHIFT) <= (rows >> CHUNK_SHIFT)
    return ok if n_k is None else ok & (cols < n_k)


def _lane_tile(x, n):
    return x if n == LANES else jnp.concatenate([x] * (n // LANES), axis=1)


def _lane_fold(p):
    acc = p[:, 0:LANES]
    for c in range(1, p.shape[1] // LANES):
        acc = acc + p[:, c * LANES:(c + 1) * LANES]
    return acc


def _attend(s, v, m_ref, l_ref, acc_ref, idx):
    m_prev = m_ref[idx]
    m_new = jnp.maximum(m_prev, jnp.max(s, axis=1, keepdims=True))
    alpha = jnp.exp2(m_prev - m_new)
    p = jnp.exp2(s - _lane_tile(m_new, s.shape[1]))
    if l_ref is not None:
        l_ref[idx] = alpha * l_ref[idx] + _lane_fold(p)
    acc_ref[idx] = _lane_tile(alpha, v.shape[1]) * acc_ref[idx] + _dot(p.astype(v.dtype), v)
    m_ref[idx] = m_new


def _flash_frame(qi_ref, kj_ref, fl_ref, m_ref, l_ref, acc_ref, step, finish, first=None, *,
                 tq, tk, rs, q_off, n_k, causal, kinds):
    n = pl.program_id(2)
    flags = fl_ref[n]
    kind = flags >> KIND_SHIFT

    @pl.when((flags & FLAG_FIRST) != 0)
    def _():
        m_ref[...] = jnp.full_like(m_ref, NEG_INF)
        if l_ref is not None:
            l_ref[...] = jnp.zeros_like(l_ref)
        acc_ref[...] = jnp.zeros_like(acc_ref)
        if first is not None:
            first()

    qpos0 = q_off + qi_ref[n] * tq
    kpos0 = kj_ref[n] * tk

    def run(k):
        for r in range(tq // rs):
            n_keys = (r + 1) * rs if k == KIND_DIAG else tk
            if k == KIND_FULL:
                mask_fn = None
            elif k == KIND_DIAG:
                def mask_fn(s, lo=r * rs):
                    vis = _visible((rs, rs), qpos0 + lo, kpos0 + lo, None, causal)
                    blk = jnp.where(vis, s[:, lo:], NEG_INF)
                    return blk if lo == 0 else jnp.concatenate([s[:, :lo], blk], axis=1)
            else:
                mask_fn = lambda s, r=r: jnp.where(_visible(s.shape, qpos0 + r * rs, kpos0, n_k, causal), s, NEG_INF)
            step(r, n_keys, mask_fn)

    for k in kinds:
        pl.when(kind == k)(functools.partial(run, k))

    @pl.when((flags & FLAG_LAST) != 0)
    def _():
        finish()


def _diff_kernel(qi_ref, kj_ref, fl_ref, q_ref, k_ref, v_ref, lam_ref, sub_ref, o_ref,
                 m_ref, l_ref, acc_ref, *, rs, lam_init, **frame):
    hd = LANES

    def step(r, n_keys, mask_fn):
        rows = slice(r * rs, (r + 1) * rs)
        v = v_ref[0:n_keys, :]
        for c in range(2):
            s = _dot_nt(q_ref[rows, c * hd:(c + 1) * hd], k_ref[0:n_keys, c * hd:(c + 1) * hd])
            if mask_fn is not None:
                s = mask_fn(s)
            _attend(s, v, m_ref, l_ref, acc_ref, (c, rows))

    def finish():
        lp = lam_ref[...]
        lam = (jnp.exp(jnp.sum(lp[0:1] * lp[1:2], axis=-1, keepdims=True))
               - jnp.exp(jnp.sum(lp[2:3] * lp[3:4], axis=-1, keepdims=True)) + lam_init)
        l0 = jnp.sum(l_ref[0], axis=1, keepdims=True)
        l1 = jnp.sum(l_ref[1], axis=1, keepdims=True)
        o = acc_ref[0] / l0 - lam * (acc_ref[1] / l1)
        o_ref[...] = (_rms(o, sub_ref[...]) * (1.0 - lam_init)).astype(o_ref.dtype)

    _flash_frame(qi_ref, kj_ref, fl_ref, m_ref, l_ref, acc_ref, step, finish, rs=rs, **frame)


def _ones_finish(acc_ref, o_ref):
    o_ref[...] = (acc_ref[:, 0:LANES] / acc_ref[:, LANES:2 * LANES]).astype(o_ref.dtype)


def _fox_kernel(qi_ref, kj_ref, fl_ref, q_ref, aq_ref, k_ref, v_ref, o_ref,
                m_ref, acc_ref, qq_ref, *, rs, **frame):
    def first():
        qq_ref[:, 0:LANES] = q_ref[...]
        qq_ref[:, LANES:2 * LANES] = aq_ref[...]

    def step(r, n_keys, mask_fn):
        rows = slice(r * rs, (r + 1) * rs)
        s = _dot_nt(qq_ref[rows, :], k_ref[0:n_keys, :])
        if mask_fn is not None:
            s = mask_fn(s)
        _attend(s, v_ref[0:n_keys, :], m_ref, None, acc_ref, rows)

    _flash_frame(qi_ref, kj_ref, fl_ref, m_ref, None, acc_ref, step,
                 functools.partial(_ones_finish, acc_ref, o_ref), first, rs=rs, **frame)


def _mla_kernel(qi_ref, kj_ref, fl_ref, q_ref, k_ref, v_ref, o_ref,
                m_ref, acc_ref, *, rs, **frame):
    def step(r, n_keys, mask_fn):
        rows = slice(r * rs, (r + 1) * rs)
        s = _dot_nt(q_ref[rows, :], k_ref[0:n_keys, :])
        if mask_fn is not None:
            s = mask_fn(s)
        _attend(s, v_ref[0:n_keys, :], m_ref, None, acc_ref, rows)

    _flash_frame(qi_ref, kj_ref, fl_ref, m_ref, None, acc_ref, step,
                 functools.partial(_ones_finish, acc_ref, o_ref), rs=rs, **frame)


def _flash_tiles(n_q, n_k_pad, tile):
    tq = _pick_tile(n_q, tile, 16)
    tk = next((t for t in (tile, tile // 2, tile // 4) if n_k_pad % t == 0), None) or _pick_tile(n_k_pad, tile, LANES)
    rs = _pick_tile(tq, 256, 16)
    return tq, tk, rs


def _flash_call(kernel, n_heads, operands, n_q, n_k_pad, out_width, scratch, n_k, q_off, causal, name,
                tile=2048, **kw):
    b = operands[0][0].shape[0]
    tq, tk, rs = _flash_tiles(n_q, n_k_pad, tile)
    qi, kj, fl, kinds = _pair_table(n_q, n_k, tq, tk, q_off, causal)
    if KIND_DIAG in kinds:
        assert rs % CHUNK == 0 and rs % LANES == 0

    in_specs, args = [], []
    for op in operands:
        a = op[0]
        if len(op) == 1:
            in_specs.append(pl.BlockSpec(a.shape, lambda bi, h, n, qi, kj, fl: (0, 0)))
        elif op[2]:
            in_specs.append(pl.BlockSpec((None, tq, op[1]), lambda bi, h, n, qi, kj, fl: (bi, qi[n], h)))
        else:
            in_specs.append(pl.BlockSpec((None, tk, op[1]), lambda bi, h, n, qi, kj, fl: (bi, kj[n], h)))
        args.append(a)
    vmem = 2 * tq * tk * 4 + 8 * max(tq, tk) * 2 * LANES * 2 * len(operands) + 10 * tq * 2 * LANES * 4 + (8 << 20)
    grid_spec = pltpu.PrefetchScalarGridSpec(
        num_scalar_prefetch=3,
        grid=(b, n_heads, int(qi.shape[0])),
        in_specs=in_specs,
        out_specs=pl.BlockSpec((None, tq, out_width), lambda bi, h, n, qi, kj, fl: (bi, qi[n], h)),
        scratch_shapes=scratch(tq),
    )
    return pl.pallas_call(
        functools.partial(kernel, tq=tq, tk=tk, rs=rs, q_off=q_off, n_k=n_k, causal=causal, kinds=kinds, **kw),
        out_shape=jax.ShapeDtypeStruct((b, n_q, n_heads * out_width), BF16),
        grid_spec=grid_spec,
        compiler_params=_params(("parallel", "parallel", "arbitrary"), vmem),
        name=name,
    )(qi, kj, fl, *args)


def _flash_diff(q, k, v, lam_p, subln, lam_init, n_heads, n_k, q_off):
    w = 2 * LANES
    scratch = lambda tq: [pltpu.VMEM((2, tq, LANES), F32), pltpu.VMEM((2, tq, LANES), F32), pltpu.VMEM((2, tq, w), F32)]
    ops = [(q, w, True), (k, w, False), (v, w, False), (lam_p,), (subln.reshape(1, w),)]
    return _flash_call(_diff_kernel, n_heads, ops, q.shape[1], k.shape[1], w, scratch, n_k, q_off, False,
                       "diff_attn", tile=1024, lam_init=lam_init)


def _flash_fox(q, aq, kk, vv, n_heads, n_k, q_off):
    w = LANES
    scratch = lambda tq: [pltpu.VMEM((tq, LANES), F32), pltpu.VMEM((tq, 2 * w), F32), pltpu.VMEM((tq, 2 * w), BF16)]
    ops = [(q, w, True), (aq, w, True), (kk, 2 * w, False), (vv, 2 * w, False)]
    return _flash_call(_fox_kernel, n_heads, ops, q.shape[1], kk.shape[1], w, scratch, n_k, q_off, True, "fox_attn")


def _flash_mla(qq, kk, vv, n_heads, n_k, q_off):
    w = LANES
    scratch = lambda tq: [pltpu.VMEM((tq, LANES), F32), pltpu.VMEM((tq, 2 * w), F32)]
    ops = [(qq, 2 * w, True), (kk, 2 * w, False), (vv, 2 * w, False)]
    return _flash_call(_mla_kernel, n_heads, ops, qq.shape[1], kk.shape[1], w, scratch, n_k, q_off, False, "mla_attn")


NEW_ROWS = LANES


def _cached_frame(m_ref, l_ref, acc_ref, chains, finish, *, nkb, tk, ta, q_off, n_k, causal):
    n = pl.program_id(1)

    @pl.when(n == 0)
    def _():
        m_ref[...] = jnp.full_like(m_ref, NEG_INF)
        l_ref[...] = jnp.zeros_like(l_ref)
        acc_ref[...] = jnp.zeros_like(acc_ref)

    @pl.when(n < nkb)
    def _():
        chains(True, tk, None)

    @pl.when(n == nkb)
    def _():
        mask_fn = lambda s: jnp.where(_visible(s.shape, q_off, q_off, n_k, causal, ta), s, NEG_INF)
        chains(False, NEW_ROWS, mask_fn)
        finish()


def _dec_diff_kernel(q_ref, kc_ref, vc_ref, kn_ref, vn_ref, lam_ref, sub_ref, o_ref,
                     m_ref, l_ref, acc_ref, *, n_heads, lam_init, **frame):
    hd, g, tk = LANES, 2 * n_heads, frame['tk']

    def chains(cached, n_keys, mask_fn):
        for h in range(n_heads):
            if cached:
                v = jnp.concatenate([vc_ref[pl.ds(2 * h, tk, stride=g), :],
                                     vc_ref[pl.ds(2 * h + 1, tk, stride=g), :]], axis=1).astype(BF16)
            else:
                v = vn_ref[:, 2 * h * hd:(2 * h + 2) * hd]
            for c in range(2):
                j = 2 * h + c
                k = kc_ref[pl.ds(j, tk, stride=g), :].astype(BF16) if cached else kn_ref[:, j * hd:(j + 1) * hd]
                s = _dot_nt(q_ref[:, j * hd:(j + 1) * hd], k)
                if mask_fn is not None:
                    s = mask_fn(s)
                _attend(s, v, m_ref, l_ref, acc_ref, j)

    def finish():
        lp = lam_ref[...]
        lam = (jnp.exp(jnp.sum(lp[0:1] * lp[1:2], axis=-1, keepdims=True))
               - jnp.exp(jnp.sum(lp[2:3] * lp[3:4], axis=-1, keepdims=True)) + lam_init)
        for h in range(n_heads):
            l0 = jnp.sum(l_ref[2 * h], axis=1, keepdims=True)
            l1 = jnp.sum(l_ref[2 * h + 1], axis=1, keepdims=True)
            o = acc_ref[2 * h] / l0 - lam * (acc_ref[2 * h + 1] / l1)
            o_ref[:, 2 * h * hd:(2 * h + 2) * hd] = (_rms(o, sub_ref[...]) * (1.0 - lam_init)).astype(o_ref.dtype)

    _cached_frame(m_ref, l_ref, acc_ref, chains, finish, **frame)


def _dec_fox_kernel(q_ref, kc_ref, vc_ref, kn_ref, vn_ref, fq_ref, fk_ref, o_ref,
                    m_ref, l_ref, acc_ref, *, n_heads, **frame):
    hd, g, tk = LANES, n_heads, frame['tk']

    def chains(cached, n_keys, mask_fn):
        for h in range(n_heads):
            if cached:
                k = kc_ref[pl.ds(h, tk, stride=g), :].astype(BF16)
                v = vc_ref[pl.ds(h, tk, stride=g), :].astype(BF16)
            else:
                k = kn_ref[:, h * hd:(h + 1) * hd]
                v = vn_ref[:, h * hd:(h + 1) * hd]
            bias = (fq_ref[:, h:h + 1] - fk_ref[h:h + 1, 0:n_keys]) * LOG2E
            s = _dot_nt(q_ref[:, h * hd:(h + 1) * hd], k) + bias
            if mask_fn is not None:
                s = mask_fn(s)
            _attend(s, v, m_ref, l_ref, acc_ref, h)

    def finish():
        for h in range(n_heads):
            o = acc_ref[h] / jnp.sum(l_ref[h], axis=1, keepdims=True)
            o_ref[:, h * hd:(h + 1) * hd] = o.astype(o_ref.dtype)

    _cached_frame(m_ref, l_ref, acc_ref, chains, finish, **frame)


def _cached_attn(kernel, q, cache_k, cache_v, new_k, new_v, extra, extra_specs, groups, v_width, n_k, causal,
                 name, **kw):
    b, ta, _ = q.shape
    past = cache_k.shape[1]
    tk = _pick_tile(past, 1024, LANES)
    nkb = past // tk
    kc = cache_k.reshape(b, past * groups, LANES)
    vc = cache_v.reshape(b, past * groups, LANES)
    kn, vn = _pad_rows(new_k, NEW_ROWS), _pad_rows(new_v, NEW_ROWS)
    whole = lambda a: pl.BlockSpec((None,) + a.shape[1:], lambda bi, n: (bi,) + (0,) * (a.ndim - 1))
    cache = pl.BlockSpec((None, tk * groups, LANES), lambda bi, n: (bi, jnp.minimum(n, nkb - 1), 0))
    out_w = q.shape[2]
    return pl.pallas_call(
        functools.partial(kernel, nkb=nkb, tk=tk, ta=ta, q_off=past, n_k=n_k, causal=causal, **kw),
        out_shape=jax.ShapeDtypeStruct((b, ta, out_w), BF16),
        grid=(b, nkb + 1),
        in_specs=[whole(q), cache, cache, whole(kn), whole(vn)] + extra_specs(tk),
        out_specs=pl.BlockSpec((None, ta, out_w), lambda bi, n: (bi, 0, 0)),
        scratch_shapes=[pltpu.VMEM((groups, ta, LANES), F32), pltpu.VMEM((groups, ta, LANES), F32),
                        pltpu.VMEM((groups, ta, v_width), F32)],
        compiler_params=_params(("parallel", "arbitrary"), 4 * tk * groups * LANES * 4 + (16 << 20)),
        name=name,
    )(q, kc, vc, kn, vn, *extra)


def _dec_mla_kernel(q_ref, ckv_ref, kp_ref, ckvn_ref, kpn_ref, wk_ref, wv_ref, gk_ref, o_ref,
                    m_ref, l_ref, acc_ref, s_ref, qp_ref, *, n_heads, **frame):
    hd, ta = LANES, frame['ta']

    @pl.when(pl.program_id(1) == 0)
    def _():
        for h in range(n_heads):
            qp_ref[h * ta:(h + 1) * ta, :] = q_ref[:, (2 * h + 1) * hd:(2 * h + 2) * hd]

    def chains(cached, n_keys, mask_fn):
        ckv = ckv_ref[...].astype(BF16) if cached else ckvn_ref[...]
        kp = kp_ref[...] if cached else kpn_ref[...]
        kvn = _dot(ckv, wk_ref[...])
        for h in range(n_heads):
            kn = _rms(kvn[:, h * hd:(h + 1) * hd], gk_ref[...]).astype(BF16)
            s_ref[h * ta:(h + 1) * ta, 0:n_keys] = _dot_nt(q_ref[:, 2 * h * hd:(2 * h + 1) * hd], kn)
        s = s_ref[:, 0:n_keys] + _dot_nt(qp_ref[...], kp)
        if mask_fn is not None:
            s = mask_fn(s)
        _attend(s, ckv, m_ref, l_ref, acc_ref, slice(None))

    def finish():
        lat = acc_ref[...] / jnp.sum(l_ref[...], axis=1, keepdims=True)
        for h in range(n_heads):
            o = _dot(lat[h * ta:(h + 1) * ta, :].astype(BF16), wv_ref[:, h * hd:(h + 1) * hd])
            o_ref[:, h * hd:(h + 1) * hd] = o.astype(o_ref.dtype)

    _cached_frame(m_ref, l_ref, acc_ref, chains, finish, **frame)


def _dec_mla(qq, cache_ckv, kp_cache, new_ckv, new_kp, w_kn, w_v, g_nope_k, n_heads, n_k):
    b, ta, _ = qq.shape
    past, c = cache_ckv.shape[1:]
    tk = _pick_tile(past, 512, LANES)
    nkb = past // tk
    rows = n_heads * ta
    ckvn, kpn = _pad_rows(new_ckv, NEW_ROWS), _pad_rows(new_kp, NEW_ROWS)
    whole = lambda a: pl.BlockSpec((None,) + a.shape[1:], lambda bi, n: (bi,) + (0,) * (a.ndim - 1))
    const = lambda a: pl.BlockSpec(a.shape, lambda bi, n: (0,) * a.ndim)
    blk = lambda w: pl.BlockSpec((None, tk, w), lambda bi, n: (bi, jnp.minimum(n, nkb - 1), 0))

    kernel = functools.partial(_dec_mla_kernel, n_heads=n_heads, nkb=nkb, tk=tk, ta=ta, q_off=past, n_k=n_k,
                               causal=False)
    gk = g_nope_k.reshape(1, LANES)
    return pl.pallas_call(
        kernel,
        out_shape=jax.ShapeDtypeStruct((b, ta, n_heads * LANES), BF16),
        grid=(b, nkb + 1),
        in_specs=[whole(qq), blk(c), blk(LANES), whole(ckvn), whole(kpn), const(w_kn), const(w_v), const(gk)],
        out_specs=pl.BlockSpec((None, ta, n_heads * LANES), lambda bi, n: (bi, 0, 0)),
        scratch_shapes=[pltpu.VMEM((rows, LANES), F32), pltpu.VMEM((rows, LANES), F32), pltpu.VMEM((rows, c), F32),
                        pltpu.VMEM((rows, tk), F32), pltpu.VMEM((rows, LANES), BF16)],
        compiler_params=_params(("parallel", "arbitrary"), 40 << 20),
        name="mla_attn_cached",
    )(qq, cache_ckv, kp_cache, ckvn, kpn, w_kn, w_v, gk)


def _cumsum_rows_kernel(x_ref, o_ref, carry_ref):
    @pl.when(pl.program_id(1) == 0)
    def _():
        carry_ref[...] = jnp.zeros_like(carry_ref)

    x = x_ref[...]
    tb = x.shape[1]
    r = lax.broadcasted_iota(jnp.int32, (tb, tb), 0)
    c = lax.broadcasted_iota(jnp.int32, (tb, tb), 1)
    tri = jnp.where(r <= c, 1.0, 0.0).astype(BF16)
    hi, mid, lo = _split3(x)
    cum = _dot(hi, tri) + _dot(mid, tri) + _dot(lo, tri) + carry_ref[:, 0:1]
    o_ref[...] = cum
    carry_ref[...] = jnp.broadcast_to(cum[:, tb - 1:tb], carry_ref.shape)


def _cumsum_rows(x, tb):
    b, g, t = x.shape
    return pl.pallas_call(
        _cumsum_rows_kernel,
        out_shape=jax.ShapeDtypeStruct((b, g, t), F32),
        grid=(b, t // tb),
        in_specs=[pl.BlockSpec((None, g, tb), lambda bi, i: (bi, 0, i))],
        out_specs=pl.BlockSpec((None, g, tb), lambda bi, i: (bi, 0, i)),
        scratch_shapes=[pltpu.VMEM((g, LANES), F32)],
        compiler_params=_params(("parallel", "arbitrary"), 32 << 20),
        name="forget_cumsum",
    )(x)


def _dec_diff(q, cache_k, cache_v, new_k, new_v, lam_p, subln, lam_init, n_heads, n_k):
    b, past = cache_k.shape[:2]
    specs = lambda tk: [pl.BlockSpec(lam_p.shape, lambda bi, n: (0, 0)),
                        pl.BlockSpec((1, 2 * LANES), lambda bi, n: (0, 0))]
    return _cached_attn(_dec_diff_kernel, q, cache_k, cache_v, new_k, new_v, [lam_p, subln.reshape(1, 2 * LANES)],
                        specs, 2 * n_heads, 2 * LANES, n_k, False, "diff_attn_cached",
                        n_heads=n_heads, lam_init=lam_init)


def _dec_fox(q, cache_k, cache_v, new_k, new_v, past_logf, new_logf, n_heads, n_k):
    b, past = cache_k.shape[:2]
    ta = q.shape[1]
    tk = _pick_tile(past, 1024, LANES)
    lf = jnp.concatenate([past_logf.astype(F32), new_logf[:, :, :n_heads]], axis=1)
    lf_rows = _pad_lanes(jnp.swapaxes(lf, 1, 2), past + tk)
    f_rows = _cumsum_rows(lf_rows, tk)
    f_q = _pad_lanes(jnp.swapaxes(f_rows[:, :, past:past + ta], 1, 2), LANES)
    specs = lambda tk: [pl.BlockSpec((None, ta, LANES), lambda bi, n: (bi, 0, 0)),
                        pl.BlockSpec((None, n_heads, tk), lambda bi, n: (bi, 0, n))]
    return _cached_attn(_dec_fox_kernel, q, cache_k, cache_v, new_k, new_v, [f_q, f_rows], specs,
                        n_heads, LANES, n_k, True, "fox_attn_cached", n_heads=n_heads)


def _rope_half(y, cos, sin, rope_dim):
    half = rope_dim // 2
    lane = lax.broadcasted_iota(jnp.int32, y.shape, 1)
    rot = jnp.where(lane < half, pltpu.roll(y, LANES - half, 1), pltpu.roll(y, half, 1))
    return y * cos + rot * sin


def _rms_low(x, gain, n):
    return x * lax.rsqrt(jnp.sum(x * x, axis=-1, keepdims=True) * (1.0 / n) + EPS) * gain


def _odd_post_kernel(z_ref, cos_ref, sin_ref, gq_ref, gkv_ref, gr_ref,
                     cq_ref, ckvf_ref, ckvb_ref, kpf_ref, kpb_ref, *, q_lora, kv_lora, rope_dim):
    cq_ref[...] = _rms(z_ref[:, 0:q_lora], gq_ref[...]).astype(BF16)
    ckv = _rms(z_ref[:, q_lora:q_lora + kv_lora], gkv_ref[...])
    ckvf_ref[...] = ckv
    ckvb_ref[...] = ckv.astype(BF16)
    kp = _rms_low(z_ref[:, q_lora + kv_lora:q_lora + kv_lora + LANES], gr_ref[...], rope_dim)
    kp = _rope_half(kp, cos_ref[...], sin_ref[...], rope_dim)
    kpf_ref[...] = kp
    kpb_ref[...] = kp.astype(BF16)


def _odd_post(z, cos, sin, g_cq, g_ckv, g_rope_k_pad, q_lora, kv_lora, rope_dim):
    b, t, n = z.shape
    tm = _pick_tile(t, 512, 16)
    row = lambda w: pl.BlockSpec((None, tm, w), lambda bi, i: (bi, i, 0))
    tab = pl.BlockSpec((tm, LANES), lambda bi, i: (i, 0))
    vec = lambda w: pl.BlockSpec((1, w), lambda bi, i: (0, 0))
    outs = [(q_lora, BF16), (kv_lora, F32), (kv_lora, BF16), (LANES, F32), (LANES, BF16)]
    return pl.pallas_call(
        functools.partial(_odd_post_kernel, q_lora=q_lora, kv_lora=kv_lora, rope_dim=rope_dim),
        out_shape=[jax.ShapeDtypeStruct((b, t, w), dt) for w, dt in outs],
        grid=(b, t // tm),
        in_specs=[row(n), tab, tab, vec(q_lora), vec(kv_lora), vec(LANES)],
        out_specs=[row(w) for w, _ in outs],
        compiler_params=_params(("parallel", "parallel"), 32 << 20),
        name="odd_post",
    )(z, cos, sin, g_cq.reshape(1, -1), g_ckv.reshape(1, -1), g_rope_k_pad)


def _qup_kernel(cq_ref, w_ref, cos_ref, sin_ref, gn_ref, gr_ref, qq_ref, *, heads, rope_dim, q_scale):
    q = _dot(cq_ref[...], w_ref[...])
    cos, sin = cos_ref[...], sin_ref[...]
    for h in range(heads):
        qn = _rms(q[:, 2 * h * LANES:(2 * h + 1) * LANES], gn_ref[...])
        qq_ref[:, 2 * h * LANES:(2 * h + 1) * LANES] = (qn * q_scale).astype(BF16)
        qp = _rms_low(q[:, (2 * h + 1) * LANES:(2 * h + 2) * LANES], gr_ref[...], rope_dim)
        qp = _rope_half(qp, cos, sin, rope_dim)
        qq_ref[:, (2 * h + 1) * LANES:(2 * h + 2) * LANES] = (qp * q_scale).astype(BF16)


def _qup(cq, w_pad, cos, sin, g_nope_q, g_rope_q_pad, n_heads, rope_dim, qk_dim):
    b, t, kq = cq.shape
    tm = _pick_tile(t, 512, 16)
    hg = 4 if n_heads % 4 == 0 else 1
    tn = hg * 2 * LANES
    return pl.pallas_call(
        functools.partial(_qup_kernel, heads=hg, rope_dim=rope_dim, q_scale=qk_dim ** -0.5 * LOG2E),
        out_shape=jax.ShapeDtypeStruct((b, t, n_heads * 2 * LANES), BF16),
        grid=(b, t // tm, n_heads // hg),
        in_specs=[
            pl.BlockSpec((None, tm, kq), lambda bi, i, j: (bi, i, 0)),
            pl.BlockSpec((kq, tn), lambda bi, i, j: (0, j)),
            pl.BlockSpec((tm, LANES), lambda bi, i, j: (i, 0)),
            pl.BlockSpec((tm, LANES), lambda bi, i, j: (i, 0)),
            pl.BlockSpec((1, LANES), lambda bi, i, j: (0, 0)),
            pl.BlockSpec((1, LANES), lambda bi, i, j: (0, 0)),
        ],
        out_specs=pl.BlockSpec((None, tm, tn), lambda bi, i, j: (bi, i, j)),
        compiler_params=_params(("parallel", "parallel", "arbitrary"), 32 << 20),
        name="mla_q_up",
    )(cq, w_pad, cos, sin, g_nope_q.reshape(1, LANES), g_rope_q_pad)


def _kvup_kernel(ckv_ref, kp_ref, w_ref, gn_ref, kk_ref, v_ref, *, heads):
    kv = _dot(ckv_ref[...], w_ref[...])
    kp = kp_ref[...]
    for h in range(heads):
        kn = _rms(kv[:, 2 * h * LANES:(2 * h + 1) * LANES], gn_ref[...])
        kk_ref[:, 2 * h * LANES:(2 * h + 1) * LANES] = kn.astype(BF16)
        kk_ref[:, (2 * h + 1) * LANES:(2 * h + 2) * LANES] = kp
        v_ref[:, 2 * h * LANES:(2 * h + 1) * LANES] = kv[:, (2 * h + 1) * LANES:(2 * h + 2) * LANES].astype(BF16)
        v_ref[:, (2 * h + 1) * LANES:(2 * h + 2) * LANES] = jnp.ones((kv.shape[0], LANES), BF16)


def _kvup(ckv, kp, w, g_nope_k, n_heads):
    b, t, kk = ckv.shape
    tm = _pick_tile(t, 512, LANES)
    hg = 4 if n_heads % 4 == 0 else 1
    tn = hg * 2 * LANES
    return pl.pallas_call(
        functools.partial(_kvup_kernel, heads=hg),
        out_shape=[jax.ShapeDtypeStruct((b, t, n_heads * 2 * LANES), BF16)] * 2,
        grid=(b, t // tm, n_heads // hg),
        in_specs=[
            pl.BlockSpec((None, tm, kk), lambda bi, i, j: (bi, i, 0)),
            pl.BlockSpec((None, tm, LANES), lambda bi, i, j: (bi, i, 0)),
            pl.BlockSpec((kk, tn), lambda bi, i, j: (0, j)),
            pl.BlockSpec((1, LANES), lambda bi, i, j: (0, 0)),
        ],
        out_specs=[pl.BlockSpec((None, tm, tn), lambda bi, i, j: (bi, i, j))] * 2,
        compiler_params=_params(("parallel", "parallel", "arbitrary"), 32 << 20),
        name="mla_kv_up",
    )(ckv, kp, w, g_nope_k.reshape(1, LANES))


def _rope_tables(pos, dim):
    half = dim // 2
    inv = ROPE_THETA ** (-jnp.arange(half, dtype=F32) * 2.0 / dim)
    ang = pos.astype(F32)[:, None] * inv[None, :]
    cos, sin = jnp.cos(ang), jnp.sin(ang)
    pad = ((0, 0), (0, LANES - dim))
    return (jnp.pad(jnp.concatenate([cos, cos], axis=-1), pad),
            jnp.pad(jnp.concatenate([-sin, sin], axis=-1), pad))


def _pad_lanes(a, width):
    return jnp.pad(a, [(0, 0)] * (a.ndim - 1) + [(0, width - a.shape[-1])])


def _pad_rows(a, rows):
    return jnp.pad(a, [(0, 0), (0, rows - a.shape[1])] + [(0, 0)] * (a.ndim - 2))


def _layer_stack(x, c_mod, tok_pos, seq_shape, caches, p):
    bx, tx, d = x.shape
    ba, ta = seq_shape
    depth = p['w_ffn_in'].shape[0]
    n_diff = p['n_diff']
    n_fox = p['n_fox']
    n_mla = p['n_mla']
    past_len = 0 if caches is None else caches[0].shape[2]
    n_k = past_len + ta
    cos128, sin128 = _rope_tables(tok_pos, LANES)
    rope_dim = p['rope_dim']
    cos_r, sin_r = _rope_tables(tok_pos, rope_dim)
    new = [[] for _ in range(7)]

    def mods(l, s):
        m = c_mod[l]
        sh, sc, gt = m[:, 3 * s], m[:, 3 * s + 1], m[:, 3 * s + 2]
        if bx == m.shape[0]:
            return tuple(a[:, None, :] for a in (sh, sc, gt))
        rep = lambda a: jnp.repeat(a, ta, axis=0).reshape(bx, tx, d)
        return rep(sh), rep(sc), rep(gt)

    seq = lambda a: a.reshape(ba, ta, a.shape[-1])
    for l in range(depth):
        i = l // 2
        g = p['norm_gains'][l]
        sh, sc, gt = mods(l, 0)
        x = _ffn(x, sh, sc, gt, g[0], p['w_ffn_in'], p['w_ffn_out'], l, 0, 0.5)
        sh, sc, gt = mods(l, 1)
        if l % 2 == 0:
            z = _modproj(x, sh, sc, g[1], p['w_in_even'][i])
            (qa, kaf, kab, vaf, vab, qb, kbf, kbb, vbf, vbb, lf) = _even_post(
                z, cos128, sin128, p['qk_norm_even'][i], p['b_forget_pad'][i], 2 * n_diff, n_fox)
            new[0].append(kaf.reshape(ba, ta, n_diff, 2, LANES))
            new[1].append(vaf.reshape(ba, ta, n_diff, 2 * LANES))
            new[2].append(kbf.reshape(ba, ta, n_fox, LANES))
            new[3].append(vbf.reshape(ba, ta, n_fox, LANES))
            new[4].append(seq(lf)[:, :, :n_fox])
            lam_init = 0.8 - 0.6 * math.exp(-0.3 * l)
            if caches is None:
                aq, kk, vv = _fox_prep(lf, kbb, vbb, n_fox)
                oa = _flash_diff(qa, kab, vab, p['diff_lambda'][i], p['diff_subln'][i], lam_init, n_diff, n_k, 0)
                ob = _flash_fox(qb, aq, kk, vv, n_fox, n_k, 0)
            else:
                past = tuple(a[i] for a in caches[:5])
                oa = _dec_diff(seq(qa), past[0], past[1], seq(kab), seq(vab), p['diff_lambda'][i],
                               p['diff_subln'][i], lam_init, n_diff, n_k)
                ob = _dec_fox(seq(qb), past[2], past[3], seq(kbb), seq(vbb), past[4], seq(lf), n_fox, n_k)
            x = _outproj(oa.reshape(bx, tx, -1), ob.reshape(bx, tx, -1), 0, 0, p['w_out_even'][i], x, gt)
        else:
            q_lora, kv_lora = p['q_lora'], p['kv_lora']
            z = _modproj(x, sh, sc, g[1], p['w_in_odd'][i])
            cq, ckvf, ckvb, kpf, kpb = _odd_post(z, cos_r, sin_r, p['mla_cq_norm'][i], p['mla_ckv_norm'][i],
                                                 p['g_rope_pad'][i, 1:2], q_lora, kv_lora, rope_dim)
            new[5].append(seq(ckvf))
            new[6].append(seq(kpf)[:, :, :rope_dim])
            qq = _qup(cq, p['w_uq_pad'][i], cos_r, sin_r, p['mla_qk_norm_nope'][i, 0],
                      p['g_rope_pad'][i, 0:1], n_mla, rope_dim, p['mla_qk_dim'])
            if caches is None:
                kk, v = _kvup(ckvb, kpb, p['w_ukv'][i], p['mla_qk_norm_nope'][i, 1], n_mla)
                o = _flash_mla(qq, kk, v, n_mla, n_k, 0)
            else:
                kp_cache = _pad_lanes(caches[6][i], LANES).astype(BF16)
                o = _dec_mla(seq(qq), caches[5][i], kp_cache, seq(ckvb), seq(kpb), p['w_kn'][i], p['w_v'][i],
                             p['mla_qk_norm_nope'][i, 1], n_mla, n_k)
            o = o.reshape(bx, tx, -1)
            x = _outproj(o, o, 0, 1, p['w_out_odd'][i], x, gt)
        sh, sc, gt = mods(l, 2)
        x = _ffn(x, sh, sc, gt, g[2], p['w_ffn_in'], p['w_ffn_out'], l, 1, 0.5, final_gain=g[3])
    return x, tuple(jnp.stack(lst) for lst in new)


def kernel(x_prompt, x_sample, c_prompt, c_sample, cache_diff_k, cache_diff_v, cache_fox_k, cache_fox_v, cache_fox_logf, cache_mla_ckv, cache_mla_kpe, w_ada, b_ada, norm_gains, w_ffn_in, w_ffn_out, w_in_even, b_forget, qk_norm_even, diff_lambda, diff_subln, w_out_even, w_in_odd, mla_cq_norm, mla_ckv_norm, w_uq, w_ukv, mla_qk_norm_nope, mla_qk_norm_rope, w_out_odd):
    d = x_prompt.shape[-1]
    n_diff, n_fox = cache_diff_k.shape[3], cache_fox_k.shape[3]
    assert cache_diff_k.shape[-1] == LANES and cache_fox_k.shape[-1] == LANES
    q_lora, kv_lora = mla_cq_norm.shape[-1], mla_ckv_norm.shape[-1]
    rope_dim, nope = cache_mla_kpe.shape[-1], mla_qk_norm_nope.shape[-1]
    n_mla = w_uq.shape[-1] // (nope + rope_dim)
    assert nope == LANES and rope_dim <= LANES and w_ukv.shape[-1] == n_mla * 2 * LANES
    n_odd = w_uq.shape[0]

    w_uq_pad = _pad_lanes(w_uq.reshape(n_odd, q_lora, n_mla, nope + rope_dim), 2 * LANES)
    p = {
        'n_diff': n_diff, 'n_fox': n_fox, 'n_mla': n_mla, 'rope_dim': rope_dim,
        'q_lora': q_lora, 'kv_lora': kv_lora, 'mla_qk_dim': nope + rope_dim,
        'norm_gains': norm_gains,
        'w_ffn_in': w_ffn_in.astype(BF16), 'w_ffn_out': w_ffn_out.astype(BF16),
        'w_in_even': _pad_lanes(w_in_even, _round_up(w_in_even.shape[-1], LANES)).astype(BF16),
        'b_forget_pad': _pad_lanes(b_forget, LANES)[:, None, :],
        'qk_norm_even': qk_norm_even, 'diff_lambda': diff_lambda, 'diff_subln': diff_subln,
        'w_out_even': w_out_even.astype(BF16),
        'w_in_odd': _pad_lanes(w_in_odd, q_lora + kv_lora + LANES).astype(BF16),
        'mla_cq_norm': mla_cq_norm, 'mla_ckv_norm': mla_ckv_norm,
        'w_uq_pad': w_uq_pad.reshape(n_odd, q_lora, n_mla * 2 * LANES).astype(BF16),
        'w_ukv': w_ukv.astype(BF16),
        'w_kn': w_ukv.reshape(n_odd, kv_lora, n_mla, 2 * LANES)[..., :LANES].reshape(n_odd, kv_lora, -1).astype(BF16),
        'w_v': w_ukv.reshape(n_odd, kv_lora, n_mla, 2 * LANES)[..., LANES:].reshape(n_odd, kv_lora, -1).astype(BF16),
        'mla_qk_norm_nope': mla_qk_norm_nope,
        'g_rope_pad': _pad_lanes(mla_qk_norm_rope, LANES),
        'w_out_odd': w_out_odd.astype(BF16),
    }

    bp, tp = x_prompt.shape[:2]
    bs, ts = x_sample.shape[:2]
    past_len = cache_diff_k.shape[2]
    mod = _ada(jnp.concatenate([c_prompt, c_sample], axis=0), w_ada, b_ada)
    mod = mod.reshape(mod.shape[0], bp + bs, N_MOD, d)

    pos_p = jnp.arange(tp, dtype=jnp.int32)
    y_prompt, st_p = _layer_stack(x_prompt, mod[:, :bp], pos_p, (bp, tp), None, p)

    pos_s = jnp.tile(past_len + jnp.arange(ts, dtype=jnp.int32), bs)
    caches = (cache_diff_k, cache_diff_v, cache_fox_k, cache_fox_v, cache_fox_logf, cache_mla_ckv, cache_mla_kpe)
    y_sample, st_s = _layer_stack(x_sample.reshape(1, bs * ts, d), mod[:, bp:], pos_s, (bs, ts), caches, p)
    return (y_prompt, y_sample.reshape(bs, ts, d)) + st_p + st_s
```

```python
import functools
import math

import numpy as np
import jax
import jax.numpy as jnp
from jax import lax
from jax.experimental import pallas as pl
from jax.experimental.pallas import tpu as pltpu

F32 = jnp.float32
BF16 = jnp.bfloat16

CHUNK = 64
ROPE_THETA = 10000.0
EPS = 1e-6
NEG_INF = -1e30
N_MOD = 9

LANES = 128
SUBLANES = 8
VMEM_CAP_BYTES = 56 * 1024 * 1024

LOG2E = math.log2(math.e)
CHUNK_SHIFT = CHUNK.bit_length() - 1
assert (1 << CHUNK_SHIFT) == CHUNK


def _round_up(n, m):
    return (n + m - 1) // m * m


def _pick_tile(n, target, quantum):
    if n <= target:
        return n
    best = None
    t = quantum
    while t <= target:
        if n % t == 0:
            best = t
        t += quantum
    assert best is not None, (n, target, quantum)
    return best


def _params(semantics, vmem_bytes):
    limit = int(min(max(vmem_bytes, 16 * 1024 * 1024), VMEM_CAP_BYTES))
    return pltpu.CompilerParams(dimension_semantics=semantics, vmem_limit_bytes=limit)


def _rms(x, gain):
    return x * lax.rsqrt(jnp.mean(x * x, axis=-1, keepdims=True) + EPS) * gain


def _silu(g):
    return g / (1.0 + jnp.exp(-g))


def _dot(a, b):
    return jnp.dot(a, b, preferred_element_type=F32)


def _dot_nt(a, b):
    return lax.dot_general(a, b, (((1,), (1,)), ((), ())), preferred_element_type=F32)


def _ada_kernel(c_ref, w_ref, b_ref, o_ref):
    a = _silu(c_ref[...]).astype(BF16)
    o_ref[...] = _dot(a, w_ref[...].astype(BF16)) + b_ref[...]


def _ada(c_all, w_ada, b_ada):
    depth, d, n = w_ada.shape
    r = c_all.shape[0]
    tn = _pick_tile(n, 1024, LANES)
    return pl.pallas_call(
        _ada_kernel,
        out_shape=jax.ShapeDtypeStruct((depth, r, n), F32),
        grid=(depth, n // tn),
        in_specs=[
            pl.BlockSpec((r, d), lambda l, j: (0, 0)),
            pl.BlockSpec((None, d, tn), lambda l, j: (l, 0, j)),
            pl.BlockSpec((None, 1, tn), lambda l, j: (l, 0, j)),
        ],
        out_specs=pl.BlockSpec((None, r, tn), lambda l, j: (l, 0, j)),
        compiler_params=_params(("arbitrary", "arbitrary"), 2 * d * tn * 4 + 3 * d * tn * 2 + (4 << 20)),
        name="ada_mod",
    )(c_all, w_ada, b_ada.reshape(depth, 1, n))


def _mod_spec(mod, tm):
    d = mod.shape[-1]
    if mod.shape[1] == 1:
        return pl.BlockSpec((None, 1, d), lambda b, i, j: (b, 0, 0))
    return pl.BlockSpec((None, tm, d), lambda b, i, j: (b, i, 0))


def _ffn_kernel(x_ref, sh_ref, sc_ref, gt_ref, g_ref, wg_ref, wu_ref, wo_ref, *rest,
                gate_mul, final_norm):
    if final_norm:
        gf_ref, o_ref, h_ref, acc_ref = rest
    else:
        o_ref, h_ref, acc_ref = rest
    f = pl.program_id(2)

    @pl.when(f == 0)
    def _():
        h = _rms(x_ref[...], g_ref[...]) * (1.0 + sc_ref[...]) + sh_ref[...]
        h_ref[...] = h.astype(BF16)
        acc_ref[...] = jnp.zeros_like(acc_ref)

    h = h_ref[...]
    g = _dot(h, wg_ref[...])
    u = _dot(h, wu_ref[...])
    a = (_silu(g) * u).astype(BF16)
    acc_ref[...] += _dot(a, wo_ref[...])

    @pl.when(f == pl.num_programs(2) - 1)
    def _():
        xn = x_ref[...] + (gate_mul * gt_ref[...]) * acc_ref[...]
        if final_norm:
            xn = _rms(xn, gf_ref[...])
        o_ref[...] = xn


def _ffn(x, sh, sc, gt, gain, w_in, w_out, layer, sub, gate_mul, final_gain=None):
    b, t, d = x.shape
    ff = w_out.shape[2]
    tm = _pick_tile(t, 512, 16)
    tf = _pick_tile(ff, 512, LANES)
    nf = ff // tf
    in_specs = [
        pl.BlockSpec((None, tm, d), lambda bi, i, f: (bi, i, 0)),
        _mod_spec(sh, tm), _mod_spec(sc, tm), _mod_spec(gt, tm),
        pl.BlockSpec((1, d), lambda bi, i, f: (0, 0)),
        pl.BlockSpec((None, None, d, tf), lambda bi, i, f: (layer, sub, 0, f)),
        pl.BlockSpec((None, None, d, tf), lambda bi, i, f: (layer, sub, 0, nf + f)),
        pl.BlockSpec((None, None, tf, d), lambda bi, i, f: (layer, sub, f, 0)),
    ]
    args = [x, sh, sc, gt, gain.reshape(1, d), w_in, w_in, w_out]
    if final_gain is not None:
        in_specs.append(pl.BlockSpec((1, d), lambda bi, i, f: (0, 0)))
        args.append(final_gain.reshape(1, d))
    vmem = (4 * tm * d * 4 + tm * d * 2 + tm * d * 4 + 6 * d * tf * 2 + 4 * tm * tf * 4
            + 6 * tm * d * 4 * (sh.shape[1] != 1) + (4 << 20))
    return pl.pallas_call(
        functools.partial(_ffn_kernel, gate_mul=gate_mul, final_norm=final_gain is not None),
        out_shape=jax.ShapeDtypeStruct((b, t, d), F32),
        grid=(b, t // tm, nf),
        in_specs=in_specs,
        out_specs=pl.BlockSpec((None, tm, d), lambda bi, i, f: (bi, i, 0)),
        scratch_shapes=[pltpu.VMEM((tm, d), BF16), pltpu.VMEM((tm, d), F32)],
        compiler_params=_params(("parallel", "parallel", "arbitrary"), vmem),
        name="ffn",
    )(*args)


def _modproj_kernel(x_ref, sh_ref, sc_ref, g_ref, w_ref, o_ref, h_ref):
    @pl.when(pl.program_id(2) == 0)
    def _():
        h = _rms(x_ref[...], g_ref[...]) * (1.0 + sc_ref[...]) + sh_ref[...]
        h_ref[...] = h.astype(BF16)

    o_ref[...] = _dot(h_ref[...], w_ref[...])


def _modproj(x, sh, sc, gain, w):
    b, t, d = x.shape
    n = w.shape[1]
    tm = _pick_tile(t, 512, 16)
    tn = _pick_tile(n, 1280, LANES)
    vmem = (2 * tm * d * 4 + tm * d * 2 + 2 * d * tn * 2 + 3 * tm * tn * 4
            + 4 * tm * d * 4 * (sh.shape[1] != 1) + 3 * tm * d * 4 + (4 << 20))
    return pl.pallas_call(
        _modproj_kernel,
        out_shape=jax.ShapeDtypeStruct((b, t, n), F32),
        grid=(b, t // tm, n // tn),
        in_specs=[
            pl.BlockSpec((None, tm, d), lambda bi, i, j: (bi, i, 0)),
            _mod_spec(sh, tm), _mod_spec(sc, tm),
            pl.BlockSpec((1, d), lambda bi, i, j: (0, 0)),
            pl.BlockSpec((d, tn), lambda bi, i, j: (0, j)),
        ],
        out_specs=pl.BlockSpec((None, tm, tn), lambda bi, i, j: (bi, i, j)),
        scratch_shapes=[pltpu.VMEM((tm, d), BF16)],
        compiler_params=_params(("parallel", "parallel", "arbitrary"), vmem),
        name="mod_proj",
    )(x, sh, sc, gain.reshape(1, d), w)


def _outproj_kernel(a1_ref, a2_ref, w1_ref, w2_ref, x_ref, gt_ref, o_ref):
    y = _dot(a1_ref[...], w1_ref[...]) + _dot(a2_ref[...], w2_ref[...])
    o_ref[...] = x_ref[...] + gt_ref[...] * y


def _outproj(a1, a2, blk1, blk2, w, x, gt):
    b, t, d = x.shape
    kh = w.shape[0] // 2
    tm = _pick_tile(t, 512, 16)
    vmem = 4 * tm * kh * 2 + 4 * kh * d * 2 + 5 * tm * d * 4 + 2 * tm * d * 4 * (gt.shape[1] != 1) + (4 << 20)
    return pl.pallas_call(
        _outproj_kernel,
        out_shape=jax.ShapeDtypeStruct((b, t, d), F32),
        grid=(b, t // tm, 1),
        in_specs=[
            pl.BlockSpec((None, tm, kh), lambda bi, i, j: (bi, i, blk1)),
            pl.BlockSpec((None, tm, kh), lambda bi, i, j: (bi, i, blk2)),
            pl.BlockSpec((kh, d), lambda bi, i, j: (0, 0)),
            pl.BlockSpec((kh, d), lambda bi, i, j: (1, 0)),
            pl.BlockSpec((None, tm, d), lambda bi, i, j: (bi, i, 0)),
            _mod_spec(gt, tm),
        ],
        out_specs=pl.BlockSpec((None, tm, d), lambda bi, i, j: (bi, i, 0)),
        compiler_params=_params(("parallel", "parallel", "arbitrary"), vmem),
        name="out_proj",
    )(a1, a2, w, w, x, gt)


def _rope128(y, cos, sin):
    return y * cos + pltpu.roll(y, LANES // 2, 1) * sin


def _log_sigmoid(x):
    return jnp.minimum(x, 0.0) - jnp.log(1.0 + jnp.exp(-jnp.abs(x)))


def _even_post_kernel(z_ref, cos_ref, sin_ref, g_ref, bf_ref,
                      qa_ref, kaf_ref, kab_ref, vaf_ref, vab_ref,
                      qb_ref, kbf_ref, kbb_ref, vbf_ref, vbb_ref, lf_ref,
                      *, n_qa, n_fox, n_forget, q_scale):
    cos = cos_ref[...]
    sin = sin_ref[...]
    hd = LANES
    wa = n_qa * hd
    wb = n_fox * hd
    for j in range(n_qa):
        q = _rope128(_rms(z_ref[:, j * hd:(j + 1) * hd], g_ref[0:1, :]), cos, sin)
        qa_ref[:, j * hd:(j + 1) * hd] = (q * q_scale).astype(BF16)
        k = _rope128(_rms(z_ref[:, wa + j * hd:wa + (j + 1) * hd], g_ref[1:2, :]), cos, sin)
        kaf_ref[:, j * hd:(j + 1) * hd] = k
        kab_ref[:, j * hd:(j + 1) * hd] = k.astype(BF16)
    va = z_ref[:, 2 * wa:3 * wa]
    vaf_ref[...] = va
    vab_ref[...] = va.astype(BF16)
    o = 3 * wa
    for j in range(n_fox):
        q = _rms(z_ref[:, o + j * hd:o + (j + 1) * hd], g_ref[2:3, :])
        qb_ref[:, j * hd:(j + 1) * hd] = (q * q_scale).astype(BF16)
        k = _rms(z_ref[:, o + wb + j * hd:o + wb + (j + 1) * hd], g_ref[3:4, :])
        kbf_ref[:, j * hd:(j + 1) * hd] = k
        kbb_ref[:, j * hd:(j + 1) * hd] = k.astype(BF16)
    vb = z_ref[:, o + 2 * wb:o + 3 * wb]
    vbf_ref[...] = vb
    vbb_ref[...] = vb.astype(BF16)
    fg = z_ref[:, o + 3 * wb:o + 3 * wb + LANES] + bf_ref[...]
    lane = lax.broadcasted_iota(jnp.int32, fg.shape, 1)
    lf_ref[...] = jnp.where(lane < n_forget, _log_sigmoid(fg), 0.0)


def _even_post(z, cos, sin, qk_gain, b_forget_pad, n_qa, n_fox):
    b, t, n = z.shape
    wa, wb = n_qa * LANES, n_fox * LANES
    tm = _pick_tile(t, 256, 16)
    row = lambda w: pl.BlockSpec((None, tm, w), lambda bi, i: (bi, i, 0))
    tab = pl.BlockSpec((tm, LANES), lambda bi, i: (i, 0))
    outs = [(wa, BF16), (wa, F32), (wa, BF16), (wa, F32), (wa, BF16),
            (wb, BF16), (wb, F32), (wb, BF16), (wb, F32), (wb, BF16), (LANES, F32)]
    vmem = 2 * tm * n * 4 + 2 * sum(tm * w * jnp.dtype(dt).itemsize for w, dt in outs) + (8 << 20)
    return pl.pallas_call(
        functools.partial(_even_post_kernel, n_qa=n_qa, n_fox=n_fox, n_forget=n_fox,
                          q_scale=LANES ** -0.5 * LOG2E),
        out_shape=[jax.ShapeDtypeStruct((b, t, w), dt) for w, dt in outs],
        grid=(b, t // tm),
        in_specs=[row(n), tab, tab,
                  pl.BlockSpec((4, LANES), lambda bi, i: (0, 0)),
                  pl.BlockSpec((1, LANES), lambda bi, i: (0, 0))],
        out_specs=[row(w) for w, _ in outs],
        compiler_params=_params(("parallel", "parallel"), vmem),
        name="even_post",
    )(z, cos, sin, qk_gain, b_forget_pad)


def _split3(x):
    hi = x.astype(BF16)
    r = x - hi.astype(F32)
    mid = r.astype(BF16)
    lo = (r - mid.astype(F32)).astype(BF16)
    return hi, mid, lo


def _fox_prep_kernel(lf_ref, k_ref, v_ref, aq_ref, kk_ref, vv_ref, carry_ref, *, n_heads):
    @pl.when(pl.program_id(1) == 0)
    def _():
        carry_ref[...] = jnp.zeros_like(carry_ref)

    x = lf_ref[...]
    tb = x.shape[0]
    r = lax.broadcasted_iota(jnp.int32, (tb, tb), 0)
    c = lax.broadcasted_iota(jnp.int32, (tb, tb), 1)
    tri = jnp.where(r >= c, 1.0, 0.0).astype(BF16)
    hi, mid, lo = _split3(x)
    cum = _dot(tri, hi) + _dot(tri, mid) + _dot(tri, lo) + carry_ref[...]
    carry_ref[...] = cum[tb - 1:tb, :]
    fh, fm, fl = (p.astype(F32) for p in _split3(cum * LOG2E))
    lane = lax.broadcasted_iota(jnp.int32, (tb, LANES), 1)
    ones_q = jnp.where((lane >= 3) & (lane < 6), 1.0, 0.0)
    ones_k = jnp.where(lane < 3, 1.0, 0.0)
    for h in range(n_heads):
        a, m, l = fh[:, h:h + 1], fm[:, h:h + 1], fl[:, h:h + 1]
        aq = jnp.where(lane == 0, a, jnp.where(lane == 1, m, jnp.where(lane == 2, l, ones_q)))
        ak = jnp.where(lane == 3, -a, jnp.where(lane == 4, -m, jnp.where(lane == 5, -l, ones_k)))
        aq_ref[:, h * LANES:(h + 1) * LANES] = aq.astype(BF16)
        kk_ref[:, 2 * h * LANES:(2 * h + 1) * LANES] = k_ref[:, h * LANES:(h + 1) * LANES]
        kk_ref[:, (2 * h + 1) * LANES:(2 * h + 2) * LANES] = ak.astype(BF16)
        vv_ref[:, 2 * h * LANES:(2 * h + 1) * LANES] = v_ref[:, h * LANES:(h + 1) * LANES]
        vv_ref[:, (2 * h + 1) * LANES:(2 * h + 2) * LANES] = jnp.ones((tb, LANES), BF16)


def _fox_prep(logf_pad, k_all, v_all, n_heads):
    b, t, _ = logf_pad.shape
    tb = _pick_tile(t, 512, LANES)
    w = n_heads * LANES
    narrow = pl.BlockSpec((None, tb, w), lambda bi, i: (bi, i, 0))
    wide = pl.BlockSpec((None, tb, 2 * w), lambda bi, i: (bi, i, 0))
    return pl.pallas_call(
        functools.partial(_fox_prep_kernel, n_heads=n_heads),
        out_shape=[jax.ShapeDtypeStruct((b, t, w), BF16), jax.ShapeDtypeStruct((b, t, 2 * w), BF16),
                   jax.ShapeDtypeStruct((b, t, 2 * w), BF16)],
        grid=(b, t // tb),
        in_specs=[pl.BlockSpec((None, tb, LANES), lambda bi, i: (bi, i, 0)), narrow, narrow],
        out_specs=[narrow, wide, wide],
        scratch_shapes=[pltpu.VMEM((1, LANES), F32)],
        compiler_params=_params(("parallel", "arbitrary"), 32 << 20),
        name="fox_prep",
    )(logf_pad, k_all, v_all)


FLAG_FIRST, FLAG_LAST = 1, 2
KIND_SHIFT = 2
KIND_FULL, KIND_MASK, KIND_DIAG = 0, 1, 2


def _pair_table(n_q, n_k, tq, tk, q_off, causal):
    tk_pad = _round_up(n_k, tk)
    aligned = tq == tk and q_off % tq == 0
    qi, kj, fl = [], [], []
    for i in range(n_q // tq):
        qmin, qmax = q_off + i * tq, q_off + (i + 1) * tq - 1
        row = []
        for j in range(tk_pad // tk):
            kmin, kmax = j * tk, min((j + 1) * tk, n_k) - 1
            if kmin >= n_k:
                continue
            if causal:
                any_vis, all_vis = kmin <= qmax, kmax <= qmin
            else:
                any_vis, all_vis = kmin // CHUNK <= qmax // CHUNK, kmax // CHUNK <= qmin // CHUNK
            all_vis = all_vis and (j + 1) * tk <= n_k
            if any_vis:
                diag = aligned and kmin == qmin and (j + 1) * tk <= n_k
                row.append((j, KIND_FULL if all_vis else KIND_DIAG if diag else KIND_MASK))
        assert row and row[0][0] == 0
        for idx, (j, kind) in enumerate(row):
            qi.append(i)
            kj.append(j)
            fl.append((kind << KIND_SHIFT) | (FLAG_FIRST if idx == 0 else 0) | (FLAG_LAST if idx == len(row) - 1 else 0))
    kinds = sorted({f >> KIND_SHIFT for f in fl})
    as_arr = lambda v: jnp.asarray(np.array(v, np.int32))
    return as_arr(qi), as_arr(kj), as_arr(fl), kinds


def _visible(shape, qpos0, kpos0, n_k, causal, row_period=None):
    if row_period is None:
        rows = qpos0 + lax.broadcasted_iota(jnp.int32, shape, 0)
    else:
        one = lax.broadcasted_iota(jnp.int32, (row_period, shape[1]), 0)
        rows = qpos0 + jnp.concatenate([one] * (shape[0] // row_period), axis=0)
    cols = kpos0 + lax.broadcasted_iota(jnp.int32, shape, 1)
    if causal:
        ok = cols <= rows
    else:
        ok = (cols >> CHUNK_SHIFT) <= (rows >> CHUNK_SHIFT)
    return ok if n_k is None else ok & (cols < n_k)


def _lane_tile(x, n):
    return x if n == LANES else jnp.concatenate([x] * (n // LANES), axis=1)


def _lane_fold(p):
    acc = p[:, 0:LANES]
    for c in range(1, p.shape[1] // LANES):
        acc = acc + p[:, c * LANES:(c + 1) * LANES]
    return acc


def _attend(s, v, m_ref, l_ref, acc_ref, idx):
    m_prev = m_ref[idx]
    m_new = jnp.maximum(m_prev, jnp.max(s, axis=1, keepdims=True))
    alpha = jnp.exp2(m_prev - m_new)
    p = jnp.exp2(s - _lane_tile(m_new, s.shape[1]))
    if l_ref is not None:
        l_ref[idx] = alpha * l_ref[idx] + _lane_fold(p)
    acc_ref[idx] = _lane_tile(alpha, v.shape[1]) * acc_ref[idx] + _dot(p.astype(v.dtype), v)
    m_ref[idx] = m_new


def _flash_frame(qi_ref, kj_ref, fl_ref, m_ref, l_ref, acc_ref, step, finish, first=None, *,
                 tq, tk, rs, q_off, n_k, causal, kinds):
    n = pl.program_id(2)
    flags = fl_ref[n]
    kind = flags >> KIND_SHIFT

    @pl.when((flags & FLAG_FIRST) != 0)
    def _():
        m_ref[...] = jnp.full_like(m_ref, NEG_INF)
        if l_ref is not None:
            l_ref[...] = jnp.zeros_like(l_ref)
        acc_ref[...] = jnp.zeros_like(acc_ref)
        if first is not None:
            first()

    qpos0 = q_off + qi_ref[n] * tq
    kpos0 = kj_ref[n] * tk

    def run(k):
        for r in range(tq // rs):
            n_keys = (r + 1) * rs if k == KIND_DIAG else tk
            if k == KIND_FULL:
                mask_fn = None
            elif k == KIND_DIAG:
                def mask_fn(s, lo=r * rs):
                    vis = _visible((rs, rs), qpos0 + lo, kpos0 + lo, None, causal)
                    blk = jnp.where(vis, s[:, lo:], NEG_INF)
                    return blk if lo == 0 else jnp.concatenate([s[:, :lo], blk], axis=1)
            else:
                mask_fn = lambda s, r=r: jnp.where(_visible(s.shape, qpos0 + r * rs, kpos0, n_k, causal), s, NEG_INF)
            step(r, n_keys, mask_fn)

    for k in kinds:
        pl.when(kind == k)(functools.partial(run, k))

    @pl.when((flags & FLAG_LAST) != 0)
    def _():
        finish()


def _diff_kernel(qi_ref, kj_ref, fl_ref, q_ref, k_ref, v_ref, lam_ref, sub_ref, o_ref,
                 m_ref, l_ref, acc_ref, *, rs, lam_init, **frame):
    hd = LANES

    def step(r, n_keys, mask_fn):
        rows = slice(r * rs, (r + 1) * rs)
        v = v_ref[0:n_keys, :]
        for c in range(2):
            s = _dot_nt(q_ref[rows, c * hd:(c + 1) * hd], k_ref[0:n_keys, c * hd:(c + 1) * hd])
            if mask_fn is not None:
                s = mask_fn(s)
            _attend(s, v, m_ref, l_ref, acc_ref, (c, rows))

    def finish():
        lp = lam_ref[...]
        lam = (jnp.exp(jnp.sum(lp[0:1] * lp[1:2], axis=-1, keepdims=True))
               - jnp.exp(jnp.sum(lp[2:3] * lp[3:4], axis=-1, keepdims=True)) + lam_init)
        l0 = jnp.sum(l_ref[0], axis=1, keepdims=True)
        l1 = jnp.sum(l_ref[1], axis=1, keepdims=True)
        o = acc_ref[0] / l0 - lam * (acc_ref[1] / l1)
        o_ref[...] = (_rms(o, sub_ref[...]) * (1.0 - lam_init)).astype(o_ref.dtype)

    _flash_frame(qi_ref, kj_ref, fl_ref, m_ref, l_ref, acc_ref, step, finish, rs=rs, **frame)


def _ones_finish(acc_ref, o_ref):
    o_ref[...] = (acc_ref[:, 0:LANES] / acc_ref[:, LANES:2 * LANES]).astype(o_ref.dtype)


def _fox_kernel(qi_ref, kj_ref, fl_ref, q_ref, aq_ref, k_ref, v_ref, o_ref,
                m_ref, acc_ref, qq_ref, *, rs, **frame):
    def first():
        qq_ref[:, 0:LANES] = q_ref[...]
        qq_ref[:, LANES:2 * LANES] = aq_ref[...]

    def step(r, n_keys, mask_fn):
        rows = slice(r * rs, (r + 1) * rs)
        s = _dot_nt(qq_ref[rows, :], k_ref[0:n_keys, :])
        if mask_fn is not None:
            s = mask_fn(s)
        _attend(s, v_ref[0:n_keys, :], m_ref, None, acc_ref, rows)

    _flash_frame(qi_ref, kj_ref, fl_ref, m_ref, None, acc_ref, step,
                 functools.partial(_ones_finish, acc_ref, o_ref), first, rs=rs, **frame)


def _mla_kernel(qi_ref, kj_ref, fl_ref, q_ref, k_ref, v_ref, o_ref,
                m_ref, acc_ref, *, rs, **frame):
    def step(r, n_keys, mask_fn):
        rows = slice(r * rs, (r + 1) * rs)
        s = _dot_nt(q_ref[rows, :], k_ref[0:n_keys, :])
        if mask_fn is not None:
            s = mask_fn(s)
        _attend(s, v_ref[0:n_keys, :], m_ref, None, acc_ref, rows)

    _flash_frame(qi_ref, kj_ref, fl_ref, m_ref, None, acc_ref, step,
                 functools.partial(_ones_finish, acc_ref, o_ref), rs=rs, **frame)


def _flash_tiles(n_q, n_k_pad, tile):
    tq = _pick_tile(n_q, tile, 16)
    tk = next((t for t in (tile, tile // 2, tile // 4) if n_k_pad % t == 0), None) or _pick_tile(n_k_pad, tile, LANES)
    rs = _pick_tile(tq, 256, 16)
    return tq, tk, rs


def _flash_call(kernel, n_heads, operands, n_q, n_k_pad, out_width, scratch, n_k, q_off, causal, name,
                tile=2048, **kw):
    b = operands[0][0].shape[0]
    tq, tk, rs = _flash_tiles(n_q, n_k_pad, tile)
    qi, kj, fl, kinds = _pair_table(n_q, n_k, tq, tk, q_off, causal)
    if KIND_DIAG in kinds:
        assert rs % CHUNK == 0 and rs % LANES == 0

    in_specs, args = [], []
    for op in operands:
        a = op[0]
        if len(op) == 1:
            in_specs.append(pl.BlockSpec(a.shape, lambda bi, h, n, qi, kj, fl: (0, 0)))
        elif op[2]:
            in_specs.append(pl.BlockSpec((None, tq, op[1]), lambda bi, h, n, qi, kj, fl: (bi, qi[n], h)))
        else:
            in_specs.append(pl.BlockSpec((None, tk, op[1]), lambda bi, h, n, qi, kj, fl: (bi, kj[n], h)))
        args.append(a)
    vmem = 2 * tq * tk * 4 + 8 * max(tq, tk) * 2 * LANES * 2 * len(operands) + 10 * tq * 2 * LANES * 4 + (8 << 20)
    grid_spec = pltpu.PrefetchScalarGridSpec(
        num_scalar_prefetch=3,
        grid=(b, n_heads, int(qi.shape[0])),
        in_specs=in_specs,
        out_specs=pl.BlockSpec((None, tq, out_width), lambda bi, h, n, qi, kj, fl: (bi, qi[n], h)),
        scratch_shapes=scratch(tq),
    )
    return pl.pallas_call(
        functools.partial(kernel, tq=tq, tk=tk, rs=rs, q_off=q_off, n_k=n_k, causal=causal, kinds=kinds, **kw),
        out_shape=jax.ShapeDtypeStruct((b, n_q, n_heads * out_width), BF16),
        grid_spec=grid_spec,
        compiler_params=_params(("parallel", "parallel", "arbitrary"), vmem),
        name=name,
    )(qi, kj, fl, *args)


def _flash_diff(q, k, v, lam_p, subln, lam_init, n_heads, n_k, q_off):
    w = 2 * LANES
    scratch = lambda tq: [pltpu.VMEM((2, tq, LANES), F32), pltpu.VMEM((2, tq, LANES), F32), pltpu.VMEM((2, tq, w), F32)]
    ops = [(q, w, True), (k, w, False), (v, w, False), (lam_p,), (subln.reshape(1, w),)]
    return _flash_call(_diff_kernel, n_heads, ops, q.shape[1], k.shape[1], w, scratch, n_k, q_off, False,
                       "diff_attn", tile=2048, lam_init=lam_init)


def _flash_fox(q, aq, kk, vv, n_heads, n_k, q_off):
    w = LANES
    scratch = lambda tq: [pltpu.VMEM((tq, LANES), F32), pltpu.VMEM((tq, 2 * w), F32), pltpu.VMEM((tq, 2 * w), BF16)]
    ops = [(q, w, True), (aq, w, True), (kk, 2 * w, False), (vv, 2 * w, False)]
    return _flash_call(_fox_kernel, n_heads, ops, q.shape[1], kk.shape[1], w, scratch, n_k, q_off, True, "fox_attn")


def _flash_mla(qq, kk, vv, n_heads, n_k, q_off):
    w = LANES
    scratch = lambda tq: [pltpu.VMEM((tq, LANES), F32), pltpu.VMEM((tq, 2 * w), F32)]
    ops = [(qq, 2 * w, True), (kk, 2 * w, False), (vv, 2 * w, False)]
    return _flash_call(_mla_kernel, n_heads, ops, qq.shape[1], kk.shape[1], w, scratch, n_k, q_off, False, "mla_attn")


NEW_ROWS = LANES


def _cached_frame(m_ref, l_ref, acc_ref, chains, finish, *, nkb, tk, ta, q_off, n_k, causal):
    n = pl.program_id(1)

    @pl.when(n == 0)
    def _():
        m_ref[...] = jnp.full_like(m_ref, NEG_INF)
        l_ref[...] = jnp.zeros_like(l_ref)
        acc_ref[...] = jnp.zeros_like(acc_ref)

    @pl.when(n < nkb)
    def _():
        chains(True, tk, None)

    @pl.when(n == nkb)
    def _():
        mask_fn = lambda s: jnp.where(_visible(s.shape, q_off, q_off, n_k, causal, ta), s, NEG_INF)
        chains(False, NEW_ROWS, mask_fn)
        finish()


def _dec_diff_kernel(q_ref, kc_ref, vc_ref, kn_ref, vn_ref, lam_ref, sub_ref, o_ref,
                     m_ref, l_ref, acc_ref, *, n_heads, lam_init, **frame):
    hd, g, tk = LANES, 2 * n_heads, frame['tk']

    def chains(cached, n_keys, mask_fn):
        for h in range(n_heads):
            if cached:
                v = jnp.concatenate([vc_ref[pl.ds(h, tk, stride=g), :],
                                     vc_ref[pl.ds(n_heads + h, tk, stride=g), :]],
                                    axis=1).astype(BF16)
            else:
                v = vn_ref[:, 2 * h * hd:(2 * h + 2) * hd]
            for c in range(2):
                j = 2 * h + c
                k = kc_ref[pl.ds(j, tk, stride=g), :].astype(BF16) if cached else kn_ref[:, j * hd:(j + 1) * hd]
                s = _dot_nt(q_ref[:, j * hd:(j + 1) * hd], k)
                if mask_fn is not None:
                    s = mask_fn(s)
                _attend(s, v, m_ref, l_ref, acc_ref, j)

    def finish():
        lp = lam_ref[...]
        lam = (jnp.exp(jnp.sum(lp[0:1] * lp[1:2], axis=-1, keepdims=True))
               - jnp.exp(jnp.sum(lp[2:3] * lp[3:4], axis=-1, keepdims=True)) + lam_init)
        for h in range(n_heads):
            l0 = jnp.sum(l_ref[2 * h], axis=1, keepdims=True)
            l1 = jnp.sum(l_ref[2 * h + 1], axis=1, keepdims=True)
            o = acc_ref[2 * h] / l0 - lam * (acc_ref[2 * h + 1] / l1)
            o_ref[:, 2 * h * hd:(2 * h + 2) * hd] = (_rms(o, sub_ref[...]) * (1.0 - lam_init)).astype(o_ref.dtype)

    _cached_frame(m_ref, l_ref, acc_ref, chains, finish, **frame)


def _dec_fox_kernel(q_ref, kc_ref, vc_ref, kn_ref, vn_ref, fq_ref, fk_ref, o_ref,
                    m_ref, l_ref, acc_ref, *, n_heads, **frame):
    hd, g, tk = LANES, n_heads, frame['tk']

    def chains(cached, n_keys, mask_fn):
        for h in range(n_heads):
            if cached:
                k = kc_ref[pl.ds(h, tk, stride=g), :].astype(BF16)
                v = vc_ref[pl.ds(h, tk, stride=g), :].astype(BF16)
            else:
                k = kn_ref[:, h * hd:(h + 1) * hd]
                v = vn_ref[:, h * hd:(h + 1) * hd]
            bias = (fq_ref[:, h:h + 1] - fk_ref[h:h + 1, 0:n_keys]) * LOG2E
            s = _dot_nt(q_ref[:, h * hd:(h + 1) * hd], k) + bias
            if mask_fn is not None:
                s = mask_fn(s)
            _attend(s, v, m_ref, l_ref, acc_ref, h)

    def finish():
        for h in range(n_heads):
            o = acc_ref[h] / jnp.sum(l_ref[h], axis=1, keepdims=True)
            o_ref[:, h * hd:(h + 1) * hd] = o.astype(o_ref.dtype)

    _cached_frame(m_ref, l_ref, acc_ref, chains, finish, **frame)


def _cached_attn(kernel, q, cache_k, cache_v, new_k, new_v, extra, extra_specs, groups, v_width, n_k, causal,
                 name, **kw):
    b, ta, _ = q.shape
    past = cache_k.shape[1]
    tk = _pick_tile(past, 1024, LANES)
    nkb = past // tk
    kc = cache_k.reshape(b, past * groups, LANES)
    if cache_v.shape[-1] == 2 * LANES:
        vc = cache_v.reshape(b, past, groups // 2, 2, LANES).swapaxes(2, 3).reshape(b, past * groups, LANES)
    else:
        vc = cache_v.reshape(b, past * groups, LANES)
    kn, vn = _pad_rows(new_k, NEW_ROWS), _pad_rows(new_v, NEW_ROWS)
    whole = lambda a: pl.BlockSpec((None,) + a.shape[1:], lambda bi, n: (bi,) + (0,) * (a.ndim - 1))
    cache = pl.BlockSpec((None, tk * groups, LANES), lambda bi, n: (bi, jnp.minimum(n, nkb - 1), 0))
    out_w = q.shape[2]
    return pl.pallas_call(
        functools.partial(kernel, nkb=nkb, tk=tk, ta=ta, q_off=past, n_k=n_k, causal=causal, **kw),
        out_shape=jax.ShapeDtypeStruct((b, ta, out_w), BF16),
        grid=(b, nkb + 1),
        in_specs=[whole(q), cache, cache, whole(kn), whole(vn)] + extra_specs(tk),
        out_specs=pl.BlockSpec((None, ta, out_w), lambda bi, n: (bi, 0, 0)),
        scratch_shapes=[pltpu.VMEM((groups, ta, LANES), F32), pltpu.VMEM((groups, ta, LANES), F32),
                        pltpu.VMEM((groups, ta, v_width), F32)],
        compiler_params=_params(("parallel", "arbitrary"), 4 * tk * groups * LANES * 4 + (16 << 20)),
        name=name,
    )(q, kc, vc, kn, vn, *extra)


def _dec_mla_kernel(q_ref, ckv_ref, kp_ref, ckvn_ref, kpn_ref, wk_ref, wv_ref, gk_ref, o_ref,
                    m_ref, l_ref, acc_ref, s_ref, qp_ref, *, n_heads, **frame):
    hd, ta = LANES, frame['ta']

    @pl.when(pl.program_id(1) == 0)
    def _():
        for h in range(n_heads):
            qp_ref[h * ta:(h + 1) * ta, :] = q_ref[:, (2 * h + 1) * hd:(2 * h + 2) * hd]

    def chains(cached, n_keys, mask_fn):
        ckv = ckv_ref[...].astype(BF16) if cached else ckvn_ref[...]
        kp = kp_ref[...] if cached else kpn_ref[...]
        kvn = _dot(ckv, wk_ref[...])
        for h in range(n_heads):
            kn = _rms(kvn[:, h * hd:(h + 1) * hd], gk_ref[...]).astype(BF16)
            s_ref[h * ta:(h + 1) * ta, 0:n_keys] = _dot_nt(q_ref[:, 2 * h * hd:(2 * h + 1) * hd], kn)
        s = s_ref[:, 0:n_keys] + _dot_nt(qp_ref[...], kp)
        if mask_fn is not None:
            s = mask_fn(s)
        _attend(s, ckv, m_ref, l_ref, acc_ref, slice(None))

    def finish():
        lat = acc_ref[...] / jnp.sum(l_ref[...], axis=1, keepdims=True)
        for h in range(n_heads):
            o = _dot(lat[h * ta:(h + 1) * ta, :].astype(BF16), wv_ref[:, h * hd:(h + 1) * hd])
            o_ref[:, h * hd:(h + 1) * hd] = o.astype(o_ref.dtype)

    _cached_frame(m_ref, l_ref, acc_ref, chains, finish, **frame)


def _dec_mla(qq, cache_ckv, kp_cache, new_ckv, new_kp, w_kn, w_v, g_nope_k, n_heads, n_k):
    b, ta, _ = qq.shape
    past, c = cache_ckv.shape[1:]
    tk = _pick_tile(past, 512, LANES)
    nkb = past // tk
    rows = n_heads * ta
    ckvn, kpn = _pad_rows(new_ckv, NEW_ROWS), _pad_rows(new_kp, NEW_ROWS)
    whole = lambda a: pl.BlockSpec((None,) + a.shape[1:], lambda bi, n: (bi,) + (0,) * (a.ndim - 1))
    const = lambda a: pl.BlockSpec(a.shape, lambda bi, n: (0,) * a.ndim)
    blk = lambda w: pl.BlockSpec((None, tk, w), lambda bi, n: (bi, jnp.minimum(n, nkb - 1), 0))

    kernel = functools.partial(_dec_mla_kernel, n_heads=n_heads, nkb=nkb, tk=tk, ta=ta, q_off=past, n_k=n_k,
                               causal=False)
    gk = g_nope_k.reshape(1, LANES)
    return pl.pallas_call(
        kernel,
        out_shape=jax.ShapeDtypeStruct((b, ta, n_heads * LANES), BF16),
        grid=(b, nkb + 1),
        in_specs=[whole(qq), blk(c), blk(LANES), whole(ckvn), whole(kpn), const(w_kn), const(w_v), const(gk)],
        out_specs=pl.BlockSpec((None, ta, n_heads * LANES), lambda bi, n: (bi, 0, 0)),
        scratch_shapes=[pltpu.VMEM((rows, LANES), F32), pltpu.VMEM((rows, LANES), F32), pltpu.VMEM((rows, c), F32),
                        pltpu.VMEM((rows, tk), F32), pltpu.VMEM((rows, LANES), BF16)],
        compiler_params=_params(("parallel", "arbitrary"), 40 << 20),
        name="mla_attn_cached",
    )(qq, cache_ckv, kp_cache, ckvn, kpn, w_kn, w_v, gk)


def _cumsum_rows_kernel(x_ref, o_ref, carry_ref):
    @pl.when(pl.program_id(1) == 0)
    def _():
        carry_ref[...] = jnp.zeros_like(carry_ref)

    x = x_ref[...]
    tb = x.shape[1]
    r = lax.broadcasted_iota(jnp.int32, (tb, tb), 0)
    c = lax.broadcasted_iota(jnp.int32, (tb, tb), 1)
    tri = jnp.where(r <= c, 1.0, 0.0).astype(BF16)
    hi, mid, lo = _split3(x)
    cum = _dot(hi, tri) + _dot(mid, tri) + _dot(lo, tri) + carry_ref[:, 0:1]
    o_ref[...] = cum
    carry_ref[...] = jnp.broadcast_to(cum[:, tb - 1:tb], carry_ref.shape)


def _cumsum_rows(x, tb):
    b, g, t = x.shape
    return pl.pallas_call(
        _cumsum_rows_kernel,
        out_shape=jax.ShapeDtypeStruct((b, g, t), F32),
        grid=(b, t // tb),
        in_specs=[pl.BlockSpec((None, g, tb), lambda bi, i: (bi, 0, i))],
        out_specs=pl.BlockSpec((None, g, tb), lambda bi, i: (bi, 0, i)),
        scratch_shapes=[pltpu.VMEM((g, LANES), F32)],
        compiler_params=_params(("parallel", "arbitrary"), 32 << 20),
        name="forget_cumsum",
    )(x)


def _dec_diff(q, cache_k, cache_v, new_k, new_v, lam_p, subln, lam_init, n_heads, n_k):
    b, past = cache_k.shape[:2]
    specs = lambda tk: [pl.BlockSpec(lam_p.shape, lambda bi, n: (0, 0)),
                        pl.BlockSpec((1, 2 * LANES), lambda bi, n: (0, 0))]
    return _cached_attn(_dec_diff_kernel, q, cache_k, cache_v, new_k, new_v, [lam_p, subln.reshape(1, 2 * LANES)],
                        specs, 2 * n_heads, 2 * LANES, n_k, False, "diff_attn_cached",
                        n_heads=n_heads, lam_init=lam_init)


def _dec_fox(q, cache_k, cache_v, new_k, new_v, past_logf, new_logf, n_heads, n_k):
    b, past = cache_k.shape[:2]
    ta = q.shape[1]
    tk = _pick_tile(past, 1024, LANES)
    lf = jnp.concatenate([past_logf.astype(F32), new_logf[:, :, :n_heads]], axis=1)
    lf_rows = _pad_lanes(jnp.swapaxes(lf, 1, 2), past + tk)
    f_rows = _cumsum_rows(lf_rows, tk)
    f_q = _pad_lanes(jnp.swapaxes(f_rows[:, :, past:past + ta], 1, 2), LANES)
    specs = lambda tk: [pl.BlockSpec((None, ta, LANES), lambda bi, n: (bi, 0, 0)),
                        pl.BlockSpec((None, n_heads, tk), lambda bi, n: (bi, 0, n))]
    return _cached_attn(_dec_fox_kernel, q, cache_k, cache_v, new_k, new_v, [f_q, f_rows], specs,
                        n_heads, LANES, n_k, True, "fox_attn_cached", n_heads=n_heads)


def _rope_half(y, cos, sin, rope_dim):
    half = rope_dim // 2
    lane = lax.broadcasted_iota(jnp.int32, y.shape, 1)
    rot = jnp.where(lane < half, pltpu.roll(y, LANES - half, 1), pltpu.roll(y, half, 1))
    return y * cos + rot * sin


def _rms_low(x, gain, n):
    return x * lax.rsqrt(jnp.sum(x * x, axis=-1, keepdims=True) * (1.0 / n) + EPS) * gain


def _odd_post_kernel(z_ref, cos_ref, sin_ref, gq_ref, gkv_ref, gr_ref,
                     cq_ref, ckvf_ref, ckvb_ref, kpf_ref, kpb_ref, *, q_lora, kv_lora, rope_dim):
    cq_ref[...] = _rms(z_ref[:, 0:q_lora], gq_ref[...]).astype(BF16)
    ckv = _rms(z_ref[:, q_lora:q_lora + kv_lora], gkv_ref[...])
    ckvf_ref[...] = ckv
    ckvb_ref[...] = ckv.astype(BF16)
    kp = _rms_low(z_ref[:, q_lora + kv_lora:q_lora + kv_lora + LANES], gr_ref[...], rope_dim)
    kp = _rope_half(kp, cos_ref[...], sin_ref[...], rope_dim)
    kpf_ref[...] = kp
    kpb_ref[...] = kp.astype(BF16)


def _odd_post(z, cos, sin, g_cq, g_ckv, g_rope_k_pad, q_lora, kv_lora, rope_dim):
    b, t, n = z.shape
    tm = _pick_tile(t, 512, 16)
    row = lambda w: pl.BlockSpec((None, tm, w), lambda bi, i: (bi, i, 0))
    tab = pl.BlockSpec((tm, LANES), lambda bi, i: (i, 0))
    vec = lambda w: pl.BlockSpec((1, w), lambda bi, i: (0, 0))
    outs = [(q_lora, BF16), (kv_lora, F32), (kv_lora, BF16), (LANES, F32), (LANES, BF16)]
    return pl.pallas_call(
        functools.partial(_odd_post_kernel, q_lora=q_lora, kv_lora=kv_lora, rope_dim=rope_dim),
        out_shape=[jax.ShapeDtypeStruct((b, t, w), dt) for w, dt in outs],
        grid=(b, t // tm),
        in_specs=[row(n), tab, tab, vec(q_lora), vec(kv_lora), vec(LANES)],
        out_specs=[row(w) for w, _ in outs],
        compiler_params=_params(("parallel", "parallel"), 32 << 20),
        name="odd_post",
    )(z, cos, sin, g_cq.reshape(1, -1), g_ckv.reshape(1, -1), g_rope_k_pad)


def _qup_kernel(cq_ref, w_ref, cos_ref, sin_ref, gn_ref, gr_ref, qq_ref, *, heads, rope_dim, q_scale):
    q = _dot(cq_ref[...], w_ref[...])
    cos, sin = cos_ref[...], sin_ref[...]
    for h in range(heads):
        qn = _rms(q[:, 2 * h * LANES:(2 * h + 1) * LANES], gn_ref[...])
        qq_ref[:, 2 * h * LANES:(2 * h + 1) * LANES] = (qn * q_scale).astype(BF16)
        qp = _rms_low(q[:, (2 * h + 1) * LANES:(2 * h + 2) * LANES], gr_ref[...], rope_dim)
        qp = _rope_half(qp, cos, sin, rope_dim)
        qq_ref[:, (2 * h + 1) * LANES:(2 * h + 2) * LANES] = (qp * q_scale).astype(BF16)


def _qup(cq, w_pad, cos, sin, g_nope_q, g_rope_q_pad, n_heads, rope_dim, qk_dim):
    b, t, kq = cq.shape
    tm = _pick_tile(t, 512, 16)
    hg = 4 if n_heads % 4 == 0 else 1
    tn = hg * 2 * LANES
    return pl.pallas_call(
        functools.partial(_qup_kernel, heads=hg, rope_dim=rope_dim, q_scale=qk_dim ** -0.5 * LOG2E),
        out_shape=jax.ShapeDtypeStruct((b, t, n_heads * 2 * LANES), BF16),
        grid=(b, t // tm, n_heads // hg),
        in_specs=[
            pl.BlockSpec((None, tm, kq), lambda bi, i, j: (bi, i, 0)),
            pl.BlockSpec((kq, tn), lambda bi, i, j: (0, j)),
            pl.BlockSpec((tm, LANES), lambda bi, i, j: (i, 0)),
            pl.BlockSpec((tm, LANES), lambda bi, i, j: (i, 0)),
            pl.BlockSpec((1, LANES), lambda bi, i, j: (0, 0)),
            pl.BlockSpec((1, LANES), lambda bi, i, j: (0, 0)),
        ],
        out_specs=pl.BlockSpec((None, tm, tn), lambda bi, i, j: (bi, i, j)),
        compiler_params=_params(("parallel", "parallel", "arbitrary"), 32 << 20),
        name="mla_q_up",
    )(cq, w_pad, cos, sin, g_nope_q.reshape(1, LANES), g_rope_q_pad)


def _kvup_kernel(ckv_ref, kp_ref, w_ref, gn_ref, kk_ref, v_ref, *, heads):
    kv = _dot(ckv_ref[...], w_ref[...])
    kp = kp_ref[...]
    for h in range(heads):
        kn = _rms(kv[:, 2 * h * LANES:(2 * h + 1) * LANES], gn_ref[...])
        kk_ref[:, 2 * h * LANES:(2 * h + 1) * LANES] = kn.astype(BF16)
        kk_ref[:, (2 * h + 1) * LANES:(2 * h + 2) * LANES] = kp
        v_ref[:, 2 * h * LANES:(2 * h + 1) * LANES] = kv[:, (2 * h + 1) * LANES:(2 * h + 2) * LANES].astype(BF16)
        v_ref[:, (2 * h + 1) * LANES:(2 * h + 2) * LANES] = jnp.ones((kv.shape[0], LANES), BF16)


def _kvup(ckv, kp, w, g_nope_k, n_heads):
    b, t, kk = ckv.shape
    tm = _pick_tile(t, 512, LANES)
    hg = 4 if n_heads % 4 == 0 else 1
    tn = hg * 2 * LANES
    return pl.pallas_call(
        functools.partial(_kvup_kernel, heads=hg),
        out_shape=[jax.ShapeDtypeStruct((b, t, n_heads * 2 * LANES), BF16)] * 2,
        grid=(b, t // tm, n_heads // hg),
        in_specs=[
            pl.BlockSpec((None, tm, kk), lambda bi, i, j: (bi, i, 0)),
            pl.BlockSpec((None, tm, LANES), lambda bi, i, j: (bi, i, 0)),
            pl.BlockSpec((kk, tn), lambda bi, i, j: (0, j)),
            pl.BlockSpec((1, LANES), lambda bi, i, j: (0, 0)),
        ],
        out_specs=[pl.BlockSpec((None, tm, tn), lambda bi, i, j: (bi, i, j))] * 2,
        compiler_params=_params(("parallel", "parallel", "arbitrary"), 32 << 20),
        name="mla_kv_up",
    )(ckv, kp, w, g_nope_k.reshape(1, LANES))


def _rope_tables(pos, dim):
    half = dim // 2
    inv = ROPE_THETA ** (-jnp.arange(half, dtype=F32) * 2.0 / dim)
    ang = pos.astype(F32)[:, None] * inv[None, :]
    cos, sin = jnp.cos(ang), jnp.sin(ang)
    pad = ((0, 0), (0, LANES - dim))
    return (jnp.pad(jnp.concatenate([cos, cos], axis=-1), pad),
            jnp.pad(jnp.concatenate([-sin, sin], axis=-1), pad))


def _pad_lanes(a, width):
    return jnp.pad(a, [(0, 0)] * (a.ndim - 1) + [(0, width - a.shape[-1])])


def _pad_rows(a, rows):
    return jnp.pad(a, [(0, 0), (0, rows - a.shape[1])] + [(0, 0)] * (a.ndim - 2))


def _layer_stack(x, c_mod, tok_pos, seq_shape, caches, p):
    bx, tx, d = x.shape
    ba, ta = seq_shape
    depth = p['w_ffn_in'].shape[0]
    n_diff = p['n_diff']
    n_fox = p['n_fox']
    n_mla = p['n_mla']
    past_len = 0 if caches is None else caches[0].shape[2]
    n_k = past_len + ta
    cos128, sin128 = _rope_tables(tok_pos, LANES)
    rope_dim = p['rope_dim']
    cos_r, sin_r = _rope_tables(tok_pos, rope_dim)
    new = [[] for _ in range(7)]

    def mods(l, s):
        m = c_mod[l]
        sh, sc, gt = m[:, 3 * s], m[:, 3 * s + 1], m[:, 3 * s + 2]
        if bx == m.shape[0]:
            return tuple(a[:, None, :] for a in (sh, sc, gt))
        rep = lambda a: jnp.repeat(a, ta, axis=0).reshape(bx, tx, d)
        return rep(sh), rep(sc), rep(gt)

    seq = lambda a: a.reshape(ba, ta, a.shape[-1])
    for l in range(depth):
        i = l // 2
        g = p['norm_gains'][l]
        sh, sc, gt = mods(l, 0)
        x = _ffn(x, sh, sc, gt, g[0], p['w_ffn_in'], p['w_ffn_out'], l, 0, 0.5)
        sh, sc, gt = mods(l, 1)
        if l % 2 == 0:
            z = _modproj(x, sh, sc, g[1], p['w_in_even'][i])
            (qa, kaf, kab, vaf, vab, qb, kbf, kbb, vbf, vbb, lf) = _even_post(
                z, cos128, sin128, p['qk_norm_even'][i], p['b_forget_pad'][i], 2 * n_diff, n_fox)
            new[0].append(kaf.reshape(ba, ta, n_diff, 2, LANES))
            new[1].append(vaf.reshape(ba, ta, n_diff, 2 * LANES))
            new[2].append(kbf.reshape(ba, ta, n_fox, LANES))
            new[3].append(vbf.reshape(ba, ta, n_fox, LANES))
            new[4].append(seq(lf)[:, :, :n_fox])
            lam_init = 0.8 - 0.6 * math.exp(-0.3 * l)
            if caches is None:
                aq, kk, vv = _fox_prep(lf, kbb, vbb, n_fox)
                oa = _flash_diff(qa, kab, vab, p['diff_lambda'][i], p['diff_subln'][i], lam_init, n_diff, n_k, 0)
                ob = _flash_fox(qb, aq, kk, vv, n_fox, n_k, 0)
            else:
                past = tuple(a[i] for a in caches[:5])
                oa = _dec_diff(seq(qa), past[0], past[1], seq(kab), seq(vab), p['diff_lambda'][i],
                               p['diff_subln'][i], lam_init, n_diff, n_k)
                ob = _dec_fox(seq(qb), past[2], past[3], seq(kbb), seq(vbb), past[4], seq(lf), n_fox, n_k)
            x = _outproj(oa.reshape(bx, tx, -1), ob.reshape(bx, tx, -1), 0, 0, p['w_out_even'][i], x, gt)
        else:
            q_lora, kv_lora = p['q_lora'], p['kv_lora']
            z = _modproj(x, sh, sc, g[1], p['w_in_odd'][i])
            cq, ckvf, ckvb, kpf, kpb = _odd_post(z, cos_r, sin_r, p['mla_cq_norm'][i], p['mla_ckv_norm'][i],
                                                 p['g_rope_pad'][i, 1:2], q_lora, kv_lora, rope_dim)
            new[5].append(seq(ckvf))
            new[6].append(seq(kpf)[:, :, :rope_dim])
            qq = _qup(cq, p['w_uq_pad'][i], cos_r, sin_r, p['mla_qk_norm_nope'][i, 0],
                      p['g_rope_pad'][i, 0:1], n_mla, rope_dim, p['mla_qk_dim'])
            if caches is None:
                kk, v = _kvup(ckvb, kpb, p['w_ukv'][i], p['mla_qk_norm_nope'][i, 1], n_mla)
                o = _flash_mla(qq, kk, v, n_mla, n_k, 0)
            else:
                kp_cache = _pad_lanes(caches[6][i], LANES).astype(BF16)
                o = _dec_mla(seq(qq), caches[5][i], kp_cache, seq(ckvb), seq(kpb), p['w_kn'][i], p['w_v'][i],
                             p['mla_qk_norm_nope'][i, 1], n_mla, n_k)
            o = o.reshape(bx, tx, -1)
            x = _outproj(o, o, 0, 1, p['w_out_odd'][i], x, gt)
        sh, sc, gt = mods(l, 2)
        x = _ffn(x, sh, sc, gt, g[2], p['w_ffn_in'], p['w_ffn_out'], l, 1, 0.5, final_gain=g[3])
    return x, tuple(jnp.stack(lst) for lst in new)


def kernel(x_prompt, x_sample, c_prompt, c_sample, cache_diff_k, cache_diff_v, cache_fox_k, cache_fox_v, cache_fox_logf, cache_mla_ckv, cache_mla_kpe, w_ada, b_ada, norm_gains, w_ffn_in, w_ffn_out, w_in_even, b_forget, qk_norm_even, diff_lambda, diff_subln, w_out_even, w_in_odd, mla_cq_norm, mla_ckv_norm, w_uq, w_ukv, mla_qk_norm_nope, mla_qk_norm_rope, w_out_odd):
    d = x_prompt.shape[-1]
    n_diff, n_fox = cache_diff_k.shape[3], cache_fox_k.shape[3]
    assert cache_diff_k.shape[-1] == LANES and cache_fox_k.shape[-1] == LANES
    q_lora, kv_lora = mla_cq_norm.shape[-1], mla_ckv_norm.shape[-1]
    rope_dim, nope = cache_mla_kpe.shape[-1], mla_qk_norm_nope.shape[-1]
    n_mla = w_uq.shape[-1] // (nope + rope_dim)
    assert nope == LANES and rope_dim <= LANES and w_ukv.shape[-1] == n_mla * 2 * LANES
    n_odd = w_uq.shape[0]

    w_uq_pad = _pad_lanes(w_uq.reshape(n_odd, q_lora, n_mla, nope + rope_dim), 2 * LANES)
    p = {
        'n_diff': n_diff, 'n_fox': n_fox, 'n_mla': n_mla, 'rope_dim': rope_dim,
        'q_lora': q_lora, 'kv_lora': kv_lora, 'mla_qk_dim': nope + rope_dim,
        'norm_gains': norm_gains,
        'w_ffn_in': w_ffn_in.astype(BF16), 'w_ffn_out': w_ffn_out.astype(BF16),
        'w_in_even': _pad_lanes(w_in_even, _round_up(w_in_even.shape[-1], LANES)).astype(BF16),
        'b_forget_pad': _pad_lanes(b_forget, LANES)[:, None, :],
        'qk_norm_even': qk_norm_even, 'diff_lambda': diff_lambda, 'diff_subln': diff_subln,
        'w_out_even': w_out_even.astype(BF16),
        'w_in_odd': _pad_lanes(w_in_odd, q_lora + kv_lora + LANES).astype(BF16),
        'mla_cq_norm': mla_cq_norm, 'mla_ckv_norm': mla_ckv_norm,
        'w_uq_pad': w_uq_pad.reshape(n_odd, q_lora, n_mla * 2 * LANES).astype(BF16),
        'w_ukv': w_ukv.astype(BF16),
        'w_kn': w_ukv.reshape(n_odd, kv_lora, n_mla, 2 * LANES)[..., :LANES].reshape(n_odd, kv_lora, -1).astype(BF16),
        'w_v': w_ukv.reshape(n_odd, kv_lora, n_mla, 2 * LANES)[..., LANES:].reshape(n_odd, kv_lora, -1).astype(BF16),
        'mla_qk_norm_nope': mla_qk_norm_nope,
        'g_rope_pad': _pad_lanes(mla_qk_norm_rope, LANES),
        'w_out_odd': w_out_odd.astype(BF16),
    }

    bp, tp = x_prompt.shape[:2]
    bs, ts = x_sample.shape[:2]
    past_len = cache_diff_k.shape[2]
    mod = _ada(jnp.concatenate([c_prompt, c_sample], axis=0), w_ada, b_ada)
    mod = mod.reshape(mod.shape[0], bp + bs, N_MOD, d)

    pos_p = jnp.arange(tp, dtype=jnp.int32)
    y_prompt, st_p = _layer_stack(x_prompt, mod[:, :bp], pos_p, (bp, tp), None, p)

    pos_s = jnp.tile(past_len + jnp.arange(ts, dtype=jnp.int32), bs)
    caches = (cache_diff_k, cache_diff_v, cache_fox_k, cache_fox_v, cache_fox_logf, cache_mla_ckv, cache_mla_kpe)
    y_sample, st_s = _layer_stack(x_sample.reshape(1, bs * ts, d), mod[:, bp:], pos_s, (bs, ts), caches, p)
    return (y_prompt, y_sample.reshape(bs, ts, d)) + st_p + st_s
```

```python
import functools
import math

import numpy as np
import jax
import jax.numpy as jnp
from jax import lax
from jax.experimental import pallas as pl
from jax.experimental.pallas import tpu as pltpu

F32 = jnp.float32
BF16 = jnp.bfloat16

CHUNK = 64
ROPE_THETA = 10000.0
EPS = 1e-6
NEG_INF = -1e30
N_MOD = 9

LANES = 128
SUBLANES = 8
VMEM_CAP_BYTES = 56 * 1024 * 1024

LOG2E = math.log2(math.e)
CHUNK_SHIFT = CHUNK.bit_length() - 1
assert (1 << CHUNK_SHIFT) == CHUNK


def _round_up(n, m):
    return (n + m - 1) // m * m


def _pick_tile(n, target, quantum):
    if n <= target:
        return n
    best = None
    t = quantum
    while t <= target:
        if n % t == 0:
            best = t
        t += quantum
    assert best is not None, (n, target, quantum)
    return best


def _params(semantics, vmem_bytes):
    limit = int(min(max(vmem_bytes, 16 * 1024 * 1024), VMEM_CAP_BYTES))
    return pltpu.CompilerParams(dimension_semantics=semantics, vmem_limit_bytes=limit)


def _rms(x, gain):
    return x * lax.rsqrt(jnp.mean(x * x, axis=-1, keepdims=True) + EPS) * gain


def _silu(g):
    return g / (1.0 + jnp.exp(-g))


def _dot(a, b):
    return jnp.dot(a, b, preferred_element_type=F32)


def _dot_nt(a, b):
    return lax.dot_general(a, b, (((1,), (1,)), ((), ())), preferred_element_type=F32)


def _ada_kernel(c_ref, w_ref, b_ref, o_ref):
    a = _silu(c_ref[...]).astype(BF16)
    o_ref[...] = _dot(a, w_ref[...].astype(BF16)) + b_ref[...]


def _ada(c_all, w_ada, b_ada):
    depth, d, n = w_ada.shape
    r = c_all.shape[0]
    tn = _pick_tile(n, 1024, LANES)
    return pl.pallas_call(
        _ada_kernel,
        out_shape=jax.ShapeDtypeStruct((depth, r, n), F32),
        grid=(depth, n // tn),
        in_specs=[
            pl.BlockSpec((r, d), lambda l, j: (0, 0)),
            pl.BlockSpec((None, d, tn), lambda l, j: (l, 0, j)),
            pl.BlockSpec((None, 1, tn), lambda l, j: (l, 0, j)),
        ],
        out_specs=pl.BlockSpec((None, r, tn), lambda l, j: (l, 0, j)),
        compiler_params=_params(("arbitrary", "arbitrary"), 2 * d * tn * 4 + 3 * d * tn * 2 + (4 << 20)),
        name="ada_mod",
    )(c_all, w_ada, b_ada.reshape(depth, 1, n))


def _mod_spec(mod, tm):
    d = mod.shape[-1]
    if mod.shape[1] == 1:
        return pl.BlockSpec((None, 1, d), lambda b, i, j: (b, 0, 0))
    return pl.BlockSpec((None, tm, d), lambda b, i, j: (b, i, 0))


def _ffn_kernel(x_ref, sh_ref, sc_ref, gt_ref, g_ref, wg_ref, wu_ref, wo_ref, *rest,
                gate_mul, final_norm):
    if final_norm:
        gf_ref, o_ref, h_ref, acc_ref, a_ref = rest
    else:
        o_ref, h_ref, acc_ref, a_ref = rest
    s = pl.program_id(2)
    nf = pl.num_programs(2) - 1

    def up(h):
        return (_silu(_dot(h, wg_ref[...])) * _dot(h, wu_ref[...])).astype(BF16)

    @pl.when(s == 0)
    def _():
        h = (_rms(x_ref[...], g_ref[...]) * (1.0 + sc_ref[...]) + sh_ref[...]).astype(BF16)
        h_ref[...] = h
        acc_ref[...] = jnp.zeros_like(acc_ref)
        a_ref[0] = up(h)

    @pl.when((s > 0) & (s < nf))
    def _():
        a_new = up(h_ref[...])
        acc_ref[...] += _dot(a_ref[(s - 1) % 2], wo_ref[...])
        a_ref[s % 2] = a_new

    @pl.when(s == nf)
    def _():
        xn = x_ref[...] + (gate_mul * gt_ref[...]) * (acc_ref[...] + _dot(a_ref[(s - 1) % 2], wo_ref[...]))
        if final_norm:
            xn = _rms(xn, gf_ref[...])
        o_ref[...] = xn


def _ffn(x, sh, sc, gt, gain, w_in, w_out, layer, sub, gate_mul, final_gain=None):
    b, t, d = x.shape
    ff = w_out.shape[2]
    tm = _pick_tile(t, 512, 16)
    tf = _pick_tile(ff, 512, LANES)
    nf = ff // tf
    in_specs = [
        pl.BlockSpec((None, tm, d), lambda bi, i, f: (bi, i, 0)),
        _mod_spec(sh, tm), _mod_spec(sc, tm), _mod_spec(gt, tm),
        pl.BlockSpec((1, d), lambda bi, i, f: (0, 0)),
        pl.BlockSpec((None, None, d, tf), lambda bi, i, f: (layer, sub, 0, jnp.minimum(f, nf - 1))),
        pl.BlockSpec((None, None, d, tf), lambda bi, i, f: (layer, sub, 0, nf + jnp.minimum(f, nf - 1))),
        pl.BlockSpec((None, None, tf, d), lambda bi, i, f: (layer, sub, jnp.maximum(f - 1, 0), 0)),
    ]
    args = [x, sh, sc, gt, gain.reshape(1, d), w_in, w_in, w_out]
    if final_gain is not None:
        in_specs.append(pl.BlockSpec((1, d), lambda bi, i, f: (0, 0)))
        args.append(final_gain.reshape(1, d))
    vmem = (4 * tm * d * 4 + tm * d * 2 + tm * d * 4 + 6 * d * tf * 2 + 4 * tm * tf * 4
            + 6 * tm * d * 4 * (sh.shape[1] != 1) + (4 << 20))
    return pl.pallas_call(
        functools.partial(_ffn_kernel, gate_mul=gate_mul, final_norm=final_gain is not None),
        out_shape=jax.ShapeDtypeStruct((b, t, d), F32),
        grid=(b, t // tm, nf + 1),
        in_specs=in_specs,
        out_specs=pl.BlockSpec((None, tm, d), lambda bi, i, f: (bi, i, 0)),
        scratch_shapes=[pltpu.VMEM((tm, d), BF16), pltpu.VMEM((tm, d), F32), pltpu.VMEM((2, tm, tf), BF16)],
        compiler_params=_params(("parallel", "parallel", "arbitrary"), vmem),
        name="ffn",
    )(*args)


def _modproj_kernel(x_ref, sh_ref, sc_ref, g_ref, w_ref, o_ref, h_ref):
    @pl.when(pl.program_id(2) == 0)
    def _():
        h = _rms(x_ref[...], g_ref[...]) * (1.0 + sc_ref[...]) + sh_ref[...]
        h_ref[...] = h.astype(BF16)

    o_ref[...] = _dot(h_ref[...], w_ref[...])


def _modproj(x, sh, sc, gain, w):
    b, t, d = x.shape
    n = w.shape[1]
    tm = _pick_tile(t, 512, 16)
    tn = _pick_tile(n, 1280, LANES)
    vmem = (2 * tm * d * 4 + tm * d * 2 + 2 * d * tn * 2 + 3 * tm * tn * 4
            + 4 * tm * d * 4 * (sh.shape[1] != 1) + 3 * tm * d * 4 + (4 << 20))
    return pl.pallas_call(
        _modproj_kernel,
        out_shape=jax.ShapeDtypeStruct((b, t, n), F32),
        grid=(b, t // tm, n // tn),
        in_specs=[
            pl.BlockSpec((None, tm, d), lambda bi, i, j: (bi, i, 0)),
            _mod_spec(sh, tm), _mod_spec(sc, tm),
            pl.BlockSpec((1, d), lambda bi, i, j: (0, 0)),
            pl.BlockSpec((d, tn), lambda bi, i, j: (0, j)),
        ],
        out_specs=pl.BlockSpec((None, tm, tn), lambda bi, i, j: (bi, i, j)),
        scratch_shapes=[pltpu.VMEM((tm, d), BF16)],
        compiler_params=_params(("parallel", "parallel", "arbitrary"), vmem),
        name="mod_proj",
    )(x, sh, sc, gain.reshape(1, d), w)


def _outproj_kernel(a1_ref, a2_ref, w1_ref, w2_ref, x_ref, gt_ref, o_ref):
    y = _dot(a1_ref[...], w1_ref[...]) + _dot(a2_ref[...], w2_ref[...])
    o_ref[...] = x_ref[...] + gt_ref[...] * y


def _outproj(a1, a2, blk1, blk2, w, x, gt):
    b, t, d = x.shape
    kh = w.shape[0] // 2
    tm = _pick_tile(t, 512, 16)
    vmem = 4 * tm * kh * 2 + 4 * kh * d * 2 + 5 * tm * d * 4 + 2 * tm * d * 4 * (gt.shape[1] != 1) + (4 << 20)
    return pl.pallas_call(
        _outproj_kernel,
        out_shape=jax.ShapeDtypeStruct((b, t, d), F32),
        grid=(b, t // tm, 1),
        in_specs=[
            pl.BlockSpec((None, tm, kh), lambda bi, i, j: (bi, i, blk1)),
            pl.BlockSpec((None, tm, kh), lambda bi, i, j: (bi, i, blk2)),
            pl.BlockSpec((kh, d), lambda bi, i, j: (0, 0)),
            pl.BlockSpec((kh, d), lambda bi, i, j: (1, 0)),
            pl.BlockSpec((None, tm, d), lambda bi, i, j: (bi, i, 0)),
            _mod_spec(gt, tm),
        ],
        out_specs=pl.BlockSpec((None, tm, d), lambda bi, i, j: (bi, i, 0)),
        compiler_params=_params(("parallel", "parallel", "arbitrary"), vmem),
        name="out_proj",
    )(a1, a2, w, w, x, gt)


def _rope128(y, cos, sin):
    return y * cos + pltpu.roll(y, LANES // 2, 1) * sin


def _log_sigmoid(x):
    return jnp.minimum(x, 0.0) - jnp.log(1.0 + jnp.exp(-jnp.abs(x)))


def _even_post_kernel(z_ref, cos_ref, sin_ref, g_ref, bf_ref,
                      qa_ref, kaf_ref, kab_ref, vaf_ref, vab_ref,
                      qb_ref, kbf_ref, kbb_ref, vbf_ref, vbb_ref, lf_ref,
                      *, n_qa, n_fox, n_forget, q_scale):
    cos = cos_ref[...]
    sin = sin_ref[...]
    hd = LANES
    tm = z_ref.shape[0]
    wa = n_qa * hd
    wb = n_fox * hd
    for j in range(n_qa):
        q = _rope128(_rms(z_ref[:, j * hd:(j + 1) * hd], g_ref[0:1, :]), cos, sin)
        qa_ref[:, j * hd:(j + 1) * hd] = (q * q_scale).astype(BF16)
        k = _rope128(_rms(z_ref[:, wa + j * hd:wa + (j + 1) * hd], g_ref[1:2, :]), cos, sin)
        kaf_ref[pl.ds(j, tm, stride=n_qa), :] = k
        kab_ref[:, j * hd:(j + 1) * hd] = k.astype(BF16)
    for j in range(n_qa):
        head, half = j // 2, j % 2
        vaf_ref[pl.ds(half * (n_qa // 2) + head, tm, stride=n_qa), :] = z_ref[:, 2 * wa + j * hd:2 * wa + (j + 1) * hd]
    vab_ref[...] = z_ref[:, 2 * wa:3 * wa].astype(BF16)
    o = 3 * wa
    for j in range(n_fox):
        q = _rms(z_ref[:, o + j * hd:o + (j + 1) * hd], g_ref[2:3, :])
        qb_ref[:, j * hd:(j + 1) * hd] = (q * q_scale).astype(BF16)
        k = _rms(z_ref[:, o + wb + j * hd:o + wb + (j + 1) * hd], g_ref[3:4, :])
        kbf_ref[:, j * hd:(j + 1) * hd] = k
        kbb_ref[:, j * hd:(j + 1) * hd] = k.astype(BF16)
    vb = z_ref[:, o + 2 * wb:o + 3 * wb]
    vbf_ref[...] = vb
    vbb_ref[...] = vb.astype(BF16)
    fg = z_ref[:, o + 3 * wb:o + 3 * wb + LANES] + bf_ref[...]
    lane = lax.broadcasted_iota(jnp.int32, fg.shape, 1)
    lf_ref[...] = jnp.where(lane < n_forget, _log_sigmoid(fg), 0.0)


def _even_post(z, cos, sin, qk_gain, b_forget_pad, n_qa, n_fox):
    b, t, n = z.shape
    wa, wb = n_qa * LANES, n_fox * LANES
    tm = _pick_tile(t, 256, 16)
    row = lambda w: pl.BlockSpec((None, tm, w), lambda bi, i: (bi, i, 0))
    tab = pl.BlockSpec((tm, LANES), lambda bi, i: (i, 0))
    outs = [(wa, BF16), (wa, F32), (wa, BF16), (wa, F32), (wa, BF16),
            (wb, BF16), (wb, F32), (wb, BF16), (wb, F32), (wb, BF16), (LANES, F32)]
    vmem = 2 * tm * n * 4 + 2 * sum(tm * w * jnp.dtype(dt).itemsize for w, dt in outs) + (8 << 20)
    out_shape = [jax.ShapeDtypeStruct((b, t, w), dt) for w, dt in outs]
    out_specs = [row(w) for w, _ in outs]
    for idx in (1, 3):
        out_shape[idx] = jax.ShapeDtypeStruct((b, t * n_qa, LANES), F32)
        out_specs[idx] = pl.BlockSpec((None, tm * n_qa, LANES), lambda bi, i: (bi, i, 0))
    return pl.pallas_call(
        functools.partial(_even_post_kernel, n_qa=n_qa, n_fox=n_fox, n_forget=n_fox,
                          q_scale=LANES ** -0.5 * LOG2E),
        out_shape=out_shape,
        grid=(b, t // tm),
        in_specs=[row(n), tab, tab,
                  pl.BlockSpec((4, LANES), lambda bi, i: (0, 0)),
                  pl.BlockSpec((1, LANES), lambda bi, i: (0, 0))],
        out_specs=out_specs,
        compiler_params=_params(("parallel", "parallel"), vmem),
        name="even_post",
    )(z, cos, sin, qk_gain, b_forget_pad)


def _split3(x):
    hi = x.astype(BF16)
    r = x - hi.astype(F32)
    mid = r.astype(BF16)
    lo = (r - mid.astype(F32)).astype(BF16)
    return hi, mid, lo


def _fox_prep_kernel(lf_ref, k_ref, v_ref, aq_ref, kk_ref, vv_ref, carry_ref, *, n_heads):
    @pl.when(pl.program_id(1) == 0)
    def _():
        carry_ref[...] = jnp.zeros_like(carry_ref)

    x = lf_ref[...]
    tb = x.shape[0]
    r = lax.broadcasted_iota(jnp.int32, (tb, tb), 0)
    c = lax.broadcasted_iota(jnp.int32, (tb, tb), 1)
    tri = jnp.where(r >= c, 1.0, 0.0).astype(BF16)
    hi, mid, lo = _split3(x)
    cum = _dot(tri, hi) + _dot(tri, mid) + _dot(tri, lo) + carry_ref[...]
    carry_ref[...] = cum[tb - 1:tb, :]
    fh, fm, fl = (p.astype(F32) for p in _split3(cum * LOG2E))
    lane = lax.broadcasted_iota(jnp.int32, (tb, LANES), 1)
    ones_q = jnp.where((lane >= 3) & (lane < 6), 1.0, 0.0)
    ones_k = jnp.where(lane < 3, 1.0, 0.0)
    for h in range(n_heads):
        a, m, l = fh[:, h:h + 1], fm[:, h:h + 1], fl[:, h:h + 1]
        aq = jnp.where(lane == 0, a, jnp.where(lane == 1, m, jnp.where(lane == 2, l, ones_q)))
        ak = jnp.where(lane == 3, -a, jnp.where(lane == 4, -m, jnp.where(lane == 5, -l, ones_k)))
        aq_ref[:, h * LANES:(h + 1) * LANES] = aq.astype(BF16)
        kk_ref[:, 2 * h * LANES:(2 * h + 1) * LANES] = k_ref[:, h * LANES:(h + 1) * LANES]
        kk_ref[:, (2 * h + 1) * LANES:(2 * h + 2) * LANES] = ak.astype(BF16)
        vv_ref[:, 2 * h * LANES:(2 * h + 1) * LANES] = v_ref[:, h * LANES:(h + 1) * LANES]
        vv_ref[:, (2 * h + 1) * LANES:(2 * h + 2) * LANES] = jnp.ones((tb, LANES), BF16)


def _fox_prep(logf_pad, k_all, v_all, n_heads):
    b, t, _ = logf_pad.shape
    tb = _pick_tile(t, 512, LANES)
    w = n_heads * LANES
    narrow = pl.BlockSpec((None, tb, w), lambda bi, i: (bi, i, 0))
    wide = pl.BlockSpec((None, tb, 2 * w), lambda bi, i: (bi, i, 0))
    return pl.pallas_call(
        functools.partial(_fox_prep_kernel, n_heads=n_heads),
        out_shape=[jax.ShapeDtypeStruct((b, t, w), BF16), jax.ShapeDtypeStruct((b, t, 2 * w), BF16),
                   jax.ShapeDtypeStruct((b, t, 2 * w), BF16)],
        grid=(b, t // tb),
        in_specs=[pl.BlockSpec((None, tb, LANES), lambda bi, i: (bi, i, 0)), narrow, narrow],
        out_specs=[narrow, wide, wide],
        scratch_shapes=[pltpu.VMEM((1, LANES), F32)],
        compiler_params=_params(("parallel", "arbitrary"), 32 << 20),
        name="fox_prep",
    )(logf_pad, k_all, v_all)


FLAG_FIRST, FLAG_LAST = 1, 2
KIND_SHIFT = 2
KIND_FULL, KIND_MASK, KIND_DIAG = 0, 1, 2


def _pair_table(n_q, n_k, tq, tk, q_off, causal):
    tk_pad = _round_up(n_k, tk)
    aligned = tq == tk and q_off % tq == 0
    qi, kj, fl = [], [], []
    for i in range(n_q // tq):
        qmin, qmax = q_off + i * tq, q_off + (i + 1) * tq - 1
        row = []
        for j in range(tk_pad // tk):
            kmin, kmax = j * tk, min((j + 1) * tk, n_k) - 1
            if kmin >= n_k:
                continue
            if causal:
                any_vis, all_vis = kmin <= qmax, kmax <= qmin
            else:
                any_vis, all_vis = kmin // CHUNK <= qmax // CHUNK, kmax // CHUNK <= qmin // CHUNK
            all_vis = all_vis and (j + 1) * tk <= n_k
            if any_vis:
                diag = aligned and kmin == qmin and (j + 1) * tk <= n_k
                row.append((j, KIND_FULL if all_vis else KIND_DIAG if diag else KIND_MASK))
        assert row and row[0][0] == 0
        for idx, (j, kind) in enumerate(row):
            qi.append(i)
            kj.append(j)
            fl.append((kind << KIND_SHIFT) | (FLAG_FIRST if idx == 0 else 0) | (FLAG_LAST if idx == len(row) - 1 else 0))
    kinds = sorted({f >> KIND_SHIFT for f in fl})
    as_arr = lambda v: jnp.asarray(np.array(v, np.int32))
    return as_arr(qi), as_arr(kj), as_arr(fl), kinds


def _visible(shape, qpos0, kpos0, n_k, causal, row_period=None):
    if row_period is None:
        rows = qpos0 + lax.broadcasted_iota(jnp.int32, shape, 0)
    else:
        one = lax.broadcasted_iota(jnp.int32, (row_period, shape[1]), 0)
        rows = qpos0 + jnp.concatenate([one] * (shape[0] // row_period), axis=0)
    cols = kpos0 + lax.broadcasted_iota(jnp.int32, shape, 1)
    if causal:
        ok = cols <= rows
    else:
        ok = (cols >> CHUNK_SHIFT) <= (rows >> CHUNK_SHIFT)
    return ok if n_k is None else ok & (cols < n_k)


def _lane_tile(x, n):
    return x if n == LANES else jnp.concatenate([x] * (n // LANES), axis=1)


def _lane_fold(p):
    acc = p[:, 0:LANES]
    for c in range(1, p.shape[1] // LANES):
        acc = acc + p[:, c * LANES:(c + 1) * LANES]
    return acc


def _attend(s, v, m_ref, l_ref, acc_ref, idx):
    m_prev = m_ref[idx]
    m_new = jnp.maximum(m_prev, jnp.max(s, axis=1, keepdims=True))
    alpha = jnp.exp2(m_prev - m_new)
    p = jnp.exp2(s - _lane_tile(m_new, s.shape[1]))
    if l_ref is not None:
        l_ref[idx] = alpha * l_ref[idx] + _lane_fold(p)
    acc_ref[idx] = _lane_tile(alpha, v.shape[1]) * acc_ref[idx] + _dot(p.astype(v.dtype), v)
    m_ref[idx] = m_new


def _flash_frame(qi_ref, kj_ref, fl_ref, m_ref, l_ref, acc_ref, step, finish, first=None, *,
                 tq, tk, rs, q_off, n_k, causal, kinds):
    n = pl.program_id(2)
    flags = fl_ref[n]
    kind = flags >> KIND_SHIFT

    @pl.when((flags & FLAG_FIRST) != 0)
    def _():
        m_ref[...] = jnp.full_like(m_ref, NEG_INF)
        if l_ref is not None:
            l_ref[...] = jnp.zeros_like(l_ref)
        acc_ref[...] = jnp.zeros_like(acc_ref)
        if first is not None:
            first()

    qpos0 = q_off + qi_ref[n] * tq
    kpos0 = kj_ref[n] * tk

    def run(k):
        order = range(tq // rs)
        for r in (reversed(order) if k == KIND_DIAG else order):
            n_keys = (r + 1) * rs if k == KIND_DIAG else tk
            if k == KIND_FULL:
                mask_fn = None
            elif k == KIND_DIAG:
                def mask_fn(s, lo=r * rs):
                    vis = _visible((rs, rs), qpos0 + lo, kpos0 + lo, None, causal)
                    blk = jnp.where(vis, s[:, lo:], NEG_INF)
                    return blk if lo == 0 else jnp.concatenate([s[:, :lo], blk], axis=1)
            else:
                mask_fn = lambda s, r=r: jnp.where(_visible(s.shape, qpos0 + r * rs, kpos0, n_k, causal), s, NEG_INF)
            step(r, n_keys, mask_fn)

    for k in kinds:
        pl.when(kind == k)(functools.partial(run, k))

    @pl.when((flags & FLAG_LAST) != 0)
    def _():
        finish()


def _diff_kernel(qi_ref, kj_ref, fl_ref, q_ref, k_ref, v_ref, lam_ref, sub_ref, o_ref,
                 m_ref, l_ref, acc_ref, *, rs, lam_init, **frame):
    hd = LANES

    def step(r, n_keys, mask_fn):
        rows = slice(r * rs, (r + 1) * rs)
        v = v_ref[0:n_keys, :]
        for c in range(2):
            s = _dot_nt(q_ref[rows, c * hd:(c + 1) * hd], k_ref[0:n_keys, c * hd:(c + 1) * hd])
            if mask_fn is not None:
                s = mask_fn(s)
            _attend(s, v, m_ref, l_ref, acc_ref, (c, rows))

    def finish():
        lp = lam_ref[...]
        lam = (jnp.exp(jnp.sum(lp[0:1] * lp[1:2], axis=-1, keepdims=True))
               - jnp.exp(jnp.sum(lp[2:3] * lp[3:4], axis=-1, keepdims=True)) + lam_init)
        l0 = jnp.sum(l_ref[0], axis=1, keepdims=True)
        l1 = jnp.sum(l_ref[1], axis=1, keepdims=True)
        o = acc_ref[0] / l0 - lam * (acc_ref[1] / l1)
        o_ref[...] = (_rms(o, sub_ref[...]) * (1.0 - lam_init)).astype(o_ref.dtype)

    _flash_frame(qi_ref, kj_ref, fl_ref, m_ref, l_ref, acc_ref, step, finish, rs=rs, **frame)


def _ones_finish(acc_ref, o_ref):
    o_ref[...] = (acc_ref[:, 0:LANES] / acc_ref[:, LANES:2 * LANES]).astype(o_ref.dtype)


def _fox_kernel(qi_ref, kj_ref, fl_ref, q_ref, aq_ref, k_ref, v_ref, o_ref,
                m_ref, acc_ref, qq_ref, *, rs, **frame):
    def first():
        qq_ref[:, 0:LANES] = q_ref[...]
        qq_ref[:, LANES:2 * LANES] = aq_ref[...]

    def step(r, n_keys, mask_fn):
        rows = slice(r * rs, (r + 1) * rs)
        s = _dot_nt(qq_ref[rows, :], k_ref[0:n_keys, :])
        if mask_fn is not None:
            s = mask_fn(s)
        _attend(s, v_ref[0:n_keys, :], m_ref, None, acc_ref, rows)

    _flash_frame(qi_ref, kj_ref, fl_ref, m_ref, None, acc_ref, step,
                 functools.partial(_ones_finish, acc_ref, o_ref), first, rs=rs, **frame)


def _mla_kernel(qi_ref, kj_ref, fl_ref, q_ref, k_ref, v_ref, o_ref,
                m_ref, acc_ref, *, rs, **frame):
    def step(r, n_keys, mask_fn):
        rows = slice(r * rs, (r + 1) * rs)
        s = _dot_nt(q_ref[rows, :], k_ref[0:n_keys, :])
        if mask_fn is not None:
            s = mask_fn(s)
        _attend(s, v_ref[0:n_keys, :], m_ref, None, acc_ref, rows)

    _flash_frame(qi_ref, kj_ref, fl_ref, m_ref, None, acc_ref, step,
                 functools.partial(_ones_finish, acc_ref, o_ref), rs=rs, **frame)


def _flash_tiles(n_q, n_k_pad, tile):
    tq = _pick_tile(n_q, tile, 16)
    tk = next((t for t in (tile, tile // 2, tile // 4) if n_k_pad % t == 0), None) or _pick_tile(n_k_pad, tile, LANES)
    rs = _pick_tile(tq, 256, 16)
    return tq, tk, rs


def _flash_call(kernel, n_heads, operands, n_q, n_k_pad, out_width, scratch, n_k, q_off, causal, name,
                tile=2048, **kw):
    b = operands[0][0].shape[0]
    tq, tk, rs = _flash_tiles(n_q, n_k_pad, tile)
    qi, kj, fl, kinds = _pair_table(n_q, n_k, tq, tk, q_off, causal)
    if KIND_DIAG in kinds:
        assert rs % CHUNK == 0 and rs % LANES == 0

    in_specs, args = [], []
    for op in operands:
        a = op[0]
        if len(op) == 1:
            in_specs.append(pl.BlockSpec(a.shape, lambda bi, h, n, qi, kj, fl: (0, 0)))
        elif op[2]:
            in_specs.append(pl.BlockSpec((None, tq, op[1]), lambda bi, h, n, qi, kj, fl: (bi, qi[n], h)))
        else:
            in_specs.append(pl.BlockSpec((None, tk, op[1]), lambda bi, h, n, qi, kj, fl: (bi, kj[n], h)))
        args.append(a)
    vmem = 2 * tq * tk * 4 + 8 * max(tq, tk) * 2 * LANES * 2 * len(operands) + 10 * tq * 2 * LANES * 4 + (8 << 20)
    grid_spec = pltpu.PrefetchScalarGridSpec(
        num_scalar_prefetch=3,
        grid=(b, n_heads, int(qi.shape[0])),
        in_specs=in_specs,
        out_specs=pl.BlockSpec((None, tq, out_width), lambda bi, h, n, qi, kj, fl: (bi, qi[n], h)),
        scratch_shapes=scratch(tq),
    )
    return pl.pallas_call(
        functools.partial(kernel, tq=tq, tk=tk, rs=rs, q_off=q_off, n_k=n_k, causal=causal, kinds=kinds, **kw),
        out_shape=jax.ShapeDtypeStruct((b, n_q, n_heads * out_width), BF16),
        grid_spec=grid_spec,
        compiler_params=_params(("parallel", "parallel", "arbitrary"), vmem),
        name=name,
    )(qi, kj, fl, *args)


def _flash_diff(q, k, v, lam_p, subln, lam_init, n_heads, n_k, q_off):
    w = 2 * LANES
    scratch = lambda tq: [pltpu.VMEM((2, tq, LANES), F32), pltpu.VMEM((2, tq, LANES), F32), pltpu.VMEM((2, tq, w), F32)]
    ops = [(q, w, True), (k, w, False), (v, w, False), (lam_p,), (subln.reshape(1, w),)]
    return _flash_call(_diff_kernel, n_heads, ops, q.shape[1], k.shape[1], w, scratch, n_k, q_off, False,
                       "diff_attn", tile=2048, lam_init=lam_init)


def _flash_fox(q, aq, kk, vv, n_heads, n_k, q_off):
    w = LANES
    scratch = lambda tq: [pltpu.VMEM((tq, LANES), F32), pltpu.VMEM((tq, 2 * w), F32), pltpu.VMEM((tq, 2 * w), BF16)]
    ops = [(q, w, True), (aq, w, True), (kk, 2 * w, False), (vv, 2 * w, False)]
    return _flash_call(_fox_kernel, n_heads, ops, q.shape[1], kk.shape[1], w, scratch, n_k, q_off, True, "fox_attn")


def _flash_mla(qq, kk, vv, n_heads, n_k, q_off):
    w = LANES
    scratch = lambda tq: [pltpu.VMEM((tq, LANES), F32), pltpu.VMEM((tq, 2 * w), F32)]
    ops = [(qq, 2 * w, True), (kk, 2 * w, False), (vv, 2 * w, False)]
    return _flash_call(_mla_kernel, n_heads, ops, qq.shape[1], kk.shape[1], w, scratch, n_k, q_off, False, "mla_attn")


NEW_ROWS = LANES


def _cached_frame(m_ref, l_ref, acc_ref, chains, finish, *, nkb, tk, ta, q_off, n_k, causal):
    n = pl.program_id(1)

    @pl.when(n == 0)
    def _():
        m_ref[...] = jnp.full_like(m_ref, NEG_INF)
        l_ref[...] = jnp.zeros_like(l_ref)
        acc_ref[...] = jnp.zeros_like(acc_ref)

    @pl.when(n < nkb)
    def _():
        chains(True, tk, None)

    @pl.when(n == nkb)
    def _():
        mask_fn = lambda s: jnp.where(_visible(s.shape, q_off, q_off, n_k, causal, ta), s, NEG_INF)
        chains(False, NEW_ROWS, mask_fn)
        finish()


def _dec_diff_kernel(q_ref, kc_ref, vc_ref, kn_ref, vn_ref, lam_ref, sub_ref, o_ref,
                     m_ref, l_ref, acc_ref, *, n_heads, lam_init, **frame):
    hd, g, tk = LANES, 2 * n_heads, frame['tk']

    def chains(cached, n_keys, mask_fn):
        for h in range(n_heads):
            if cached:
                v = jnp.concatenate([vc_ref[pl.ds(h, tk, stride=g), :],
                                     vc_ref[pl.ds(n_heads + h, tk, stride=g), :]],
                                    axis=1).astype(BF16)
            else:
                v = vn_ref[:, 2 * h * hd:(2 * h + 2) * hd]
            for c in range(2):
                j = 2 * h + c
                k = kc_ref[pl.ds(j, tk, stride=g), :].astype(BF16) if cached else kn_ref[:, j * hd:(j + 1) * hd]
                s = _dot_nt(q_ref[:, j * hd:(j + 1) * hd], k)
                if mask_fn is not None:
                    s = mask_fn(s)
                _attend(s, v, m_ref, l_ref, acc_ref, j)

    def finish():
        lp = lam_ref[...]
        lam = (jnp.exp(jnp.sum(lp[0:1] * lp[1:2], axis=-1, keepdims=True))
               - jnp.exp(jnp.sum(lp[2:3] * lp[3:4], axis=-1, keepdims=True)) + lam_init)
        for h in range(n_heads):
            l0 = jnp.sum(l_ref[2 * h], axis=1, keepdims=True)
            l1 = jnp.sum(l_ref[2 * h + 1], axis=1, keepdims=True)
            o = acc_ref[2 * h] / l0 - lam * (acc_ref[2 * h + 1] / l1)
            o_ref[:, 2 * h * hd:(2 * h + 2) * hd] = (_rms(o, sub_ref[...]) * (1.0 - lam_init)).astype(o_ref.dtype)

    _cached_frame(m_ref, l_ref, acc_ref, chains, finish, **frame)


def _dec_fox_kernel(q_ref, kc_ref, vc_ref, kn_ref, vn_ref, fq_ref, fk_ref, o_ref,
                    m_ref, l_ref, acc_ref, *, n_heads, **frame):
    hd, g, tk = LANES, n_heads, frame['tk']

    def chains(cached, n_keys, mask_fn):
        for h in range(n_heads):
            if cached:
                k = kc_ref[pl.ds(h, tk, stride=g), :].astype(BF16)
                v = vc_ref[pl.ds(h, tk, stride=g), :].astype(BF16)
            else:
                k = kn_ref[:, h * hd:(h + 1) * hd]
                v = vn_ref[:, h * hd:(h + 1) * hd]
            bias = (fq_ref[:, h:h + 1] - fk_ref[h:h + 1, 0:n_keys]) * LOG2E
            s = _dot_nt(q_ref[:, h * hd:(h + 1) * hd], k) + bias
            if mask_fn is not None:
                s = mask_fn(s)
            _attend(s, v, m_ref, l_ref, acc_ref, h)

    def finish():
        for h in range(n_heads):
            o = acc_ref[h] / jnp.sum(l_ref[h], axis=1, keepdims=True)
            o_ref[:, h * hd:(h + 1) * hd] = o.astype(o_ref.dtype)

    _cached_frame(m_ref, l_ref, acc_ref, chains, finish, **frame)


def _cached_attn(kernel, q, cache_k, cache_v, new_k, new_v, extra, extra_specs, groups, v_width, n_k, causal,
                 name, **kw):
    b, ta, _ = q.shape
    past = cache_k.shape[1]
    tk = _pick_tile(past, 1024, LANES)
    nkb = past // tk
    kc = cache_k.reshape(b, past * groups, LANES)
    if cache_v.shape[-1] == 2 * LANES:
        vc = cache_v.reshape(b, past, groups // 2, 2, LANES).swapaxes(2, 3).reshape(b, past * groups, LANES)
    else:
        vc = cache_v.reshape(b, past * groups, LANES)
    kn, vn = _pad_rows(new_k, NEW_ROWS), _pad_rows(new_v, NEW_ROWS)
    whole = lambda a: pl.BlockSpec((None,) + a.shape[1:], lambda bi, n: (bi,) + (0,) * (a.ndim - 1))
    cache = pl.BlockSpec((None, tk * groups, LANES), lambda bi, n: (bi, jnp.minimum(n, nkb - 1), 0))
    out_w = q.shape[2]
    return pl.pallas_call(
        functools.partial(kernel, nkb=nkb, tk=tk, ta=ta, q_off=past, n_k=n_k, causal=causal, **kw),
        out_shape=jax.ShapeDtypeStruct((b, ta, out_w), BF16),
        grid=(b, nkb + 1),
        in_specs=[whole(q), cache, cache, whole(kn), whole(vn)] + extra_specs(tk),
        out_specs=pl.BlockSpec((None, ta, out_w), lambda bi, n: (bi, 0, 0)),
        scratch_shapes=[pltpu.VMEM((groups, ta, LANES), F32), pltpu.VMEM((groups, ta, LANES), F32),
                        pltpu.VMEM((groups, ta, v_width), F32)],
        compiler_params=_params(("parallel", "arbitrary"), 4 * tk * groups * LANES * 4 + (16 << 20)),
        name=name,
    )(q, kc, vc, kn, vn, *extra)


def _dec_mla_kernel(q_ref, ckv_ref, kp_ref, ckvn_ref, kpn_ref, wk_ref, wv_ref, gk_ref, o_ref,
                    m_ref, l_ref, acc_ref, s_ref, qp_ref, *, n_heads, **frame):
    hd, ta = LANES, frame['ta']

    @pl.when(pl.program_id(1) == 0)
    def _():
        for h in range(n_heads):
            qp_ref[h * ta:(h + 1) * ta, :] = q_ref[:, (2 * h + 1) * hd:(2 * h + 2) * hd]

    def chains(cached, n_keys, mask_fn):
        ckv = ckv_ref[...].astype(BF16) if cached else ckvn_ref[...]
        kp = kp_ref[...] if cached else kpn_ref[...]
        kvn = _dot(ckv, wk_ref[...])
        for h in range(n_heads):
            kn = _rms(kvn[:, h * hd:(h + 1) * hd], gk_ref[...]).astype(BF16)
            s_ref[h * ta:(h + 1) * ta, 0:n_keys] = _dot_nt(q_ref[:, 2 * h * hd:(2 * h + 1) * hd], kn)
        s = s_ref[:, 0:n_keys] + _dot_nt(qp_ref[...], kp)
        if mask_fn is not None:
            s = mask_fn(s)
        _attend(s, ckv, m_ref, l_ref, acc_ref, slice(None))

    def finish():
        lat = acc_ref[...] / jnp.sum(l_ref[...], axis=1, keepdims=True)
        for h in range(n_heads):
            o = _dot(lat[h * ta:(h + 1) * ta, :].astype(BF16), wv_ref[:, h * hd:(h + 1) * hd])
            o_ref[:, h * hd:(h + 1) * hd] = o.astype(o_ref.dtype)

    _cached_frame(m_ref, l_ref, acc_ref, chains, finish, **frame)


def _dec_mla(qq, cache_ckv, kp_cache, new_ckv, new_kp, w_kn, w_v, g_nope_k, n_heads, n_k):
    b, ta, _ = qq.shape
    past, c = cache_ckv.shape[1:]
    tk = _pick_tile(past, 512, LANES)
    nkb = past // tk
    rows = n_heads * ta
    ckvn, kpn = _pad_rows(new_ckv, NEW_ROWS), _pad_rows(new_kp, NEW_ROWS)
    whole = lambda a: pl.BlockSpec((None,) + a.shape[1:], lambda bi, n: (bi,) + (0,) * (a.ndim - 1))
    const = lambda a: pl.BlockSpec(a.shape, lambda bi, n: (0,) * a.ndim)
    blk = lambda w: pl.BlockSpec((None, tk, w), lambda bi, n: (bi, jnp.minimum(n, nkb - 1), 0))

    kernel = functools.partial(_dec_mla_kernel, n_heads=n_heads, nkb=nkb, tk=tk, ta=ta, q_off=past, n_k=n_k,
                               causal=False)
    gk = g_nope_k.reshape(1, LANES)
    return pl.pallas_call(
        kernel,
        out_shape=jax.ShapeDtypeStruct((b, ta, n_heads * LANES), BF16),
        grid=(b, nkb + 1),
        in_specs=[whole(qq), blk(c), blk(LANES), whole(ckvn), whole(kpn), const(w_kn), const(w_v), const(gk)],
        out_specs=pl.BlockSpec((None, ta, n_heads * LANES), lambda bi, n: (bi, 0, 0)),
        scratch_shapes=[pltpu.VMEM((rows, LANES), F32), pltpu.VMEM((rows, LANES), F32), pltpu.VMEM((rows, c), F32),
                        pltpu.VMEM((rows, tk), F32), pltpu.VMEM((rows, LANES), BF16)],
        compiler_params=_params(("parallel", "arbitrary"), 40 << 20),
        name="mla_attn_cached",
    )(qq, cache_ckv, kp_cache, ckvn, kpn, w_kn, w_v, gk)


def _cumsum_rows_kernel(x_ref, o_ref, carry_ref):
    @pl.when(pl.program_id(1) == 0)
    def _():
        carry_ref[...] = jnp.zeros_like(carry_ref)

    x = x_ref[...]
    tb = x.shape[1]
    r = lax.broadcasted_iota(jnp.int32, (tb, tb), 0)
    c = lax.broadcasted_iota(jnp.int32, (tb, tb), 1)
    tri = jnp.where(r <= c, 1.0, 0.0).astype(BF16)
    hi, mid, lo = _split3(x)
    cum = _dot(hi, tri) + _dot(mid, tri) + _dot(lo, tri) + carry_ref[:, 0:1]
    o_ref[...] = cum
    carry_ref[...] = jnp.broadcast_to(cum[:, tb - 1:tb], carry_ref.shape)


def _cumsum_rows(x, tb):
    b, g, t = x.shape
    return pl.pallas_call(
        _cumsum_rows_kernel,
        out_shape=jax.ShapeDtypeStruct((b, g, t), F32),
        grid=(b, t // tb),
        in_specs=[pl.BlockSpec((None, g, tb), lambda bi, i: (bi, 0, i))],
        out_specs=pl.BlockSpec((None, g, tb), lambda bi, i: (bi, 0, i)),
        scratch_shapes=[pltpu.VMEM((g, LANES), F32)],
        compiler_params=_params(("parallel", "arbitrary"), 32 << 20),
        name="forget_cumsum",
    )(x)


def _dec_diff(q, cache_k, cache_v, new_k, new_v, lam_p, subln, lam_init, n_heads, n_k):
    b, past = cache_k.shape[:2]
    specs = lambda tk: [pl.BlockSpec(lam_p.shape, lambda bi, n: (0, 0)),
                        pl.BlockSpec((1, 2 * LANES), lambda bi, n: (0, 0))]
    return _cached_attn(_dec_diff_kernel, q, cache_k, cache_v, new_k, new_v, [lam_p, subln.reshape(1, 2 * LANES)],
                        specs, 2 * n_heads, 2 * LANES, n_k, False, "diff_attn_cached",
                        n_heads=n_heads, lam_init=lam_init)


def _dec_fox(q, cache_k, cache_v, new_k, new_v, past_logf, new_logf, n_heads, n_k):
    b, past = cache_k.shape[:2]
    ta = q.shape[1]
    tk = _pick_tile(past, 1024, LANES)
    lf = jnp.concatenate([past_logf.astype(F32), new_logf[:, :, :n_heads]], axis=1)
    lf_rows = _pad_lanes(jnp.swapaxes(lf, 1, 2), past + tk)
    f_rows = _cumsum_rows(lf_rows, tk)
    f_q = _pad_lanes(jnp.swapaxes(f_rows[:, :, past:past + ta], 1, 2), LANES)
    specs = lambda tk: [pl.BlockSpec((None, ta, LANES), lambda bi, n: (bi, 0, 0)),
                        pl.BlockSpec((None, n_heads, tk), lambda bi, n: (bi, 0, n))]
    return _cached_attn(_dec_fox_kernel, q, cache_k, cache_v, new_k, new_v, [f_q, f_rows], specs,
                        n_heads, LANES, n_k, True, "fox_attn_cached", n_heads=n_heads)


def _rope_half(y, cos, sin, rope_dim):
    half = rope_dim // 2
    lane = lax.broadcasted_iota(jnp.int32, y.shape, 1)
    rot = jnp.where(lane < half, pltpu.roll(y, LANES - half, 1), pltpu.roll(y, half, 1))
    return y * cos + rot * sin


def _rms_low(x, gain, n):
    return x * lax.rsqrt(jnp.sum(x * x, axis=-1, keepdims=True) * (1.0 / n) + EPS) * gain


def _odd_post_kernel(z_ref, cos_ref, sin_ref, gq_ref, gkv_ref, gr_ref,
                     cq_ref, ckvf_ref, ckvb_ref, kpf_ref, kpb_ref, *, q_lora, kv_lora, rope_dim):
    cq_ref[...] = _rms(z_ref[:, 0:q_lora], gq_ref[...]).astype(BF16)
    ckv = _rms(z_ref[:, q_lora:q_lora + kv_lora], gkv_ref[...])
    ckvf_ref[...] = ckv
    ckvb_ref[...] = ckv.astype(BF16)
    kp = _rms_low(z_ref[:, q_lora + kv_lora:q_lora + kv_lora + LANES], gr_ref[...], rope_dim)
    kp = _rope_half(kp, cos_ref[...], sin_ref[...], rope_dim)
    kpf_ref[...] = kp
    kpb_ref[...] = kp.astype(BF16)


def _odd_post(z, cos, sin, g_cq, g_ckv, g_rope_k_pad, q_lora, kv_lora, rope_dim):
    b, t, n = z.shape
    tm = _pick_tile(t, 512, 16)
    row = lambda w: pl.BlockSpec((None, tm, w), lambda bi, i: (bi, i, 0))
    tab = pl.BlockSpec((tm, LANES), lambda bi, i: (i, 0))
    vec = lambda w: pl.BlockSpec((1, w), lambda bi, i: (0, 0))
    outs = [(q_lora, BF16), (kv_lora, F32), (kv_lora, BF16), (LANES, F32), (LANES, BF16)]
    return pl.pallas_call(
        functools.partial(_odd_post_kernel, q_lora=q_lora, kv_lora=kv_lora, rope_dim=rope_dim),
        out_shape=[jax.ShapeDtypeStruct((b, t, w), dt) for w, dt in outs],
        grid=(b, t // tm),
        in_specs=[row(n), tab, tab, vec(q_lora), vec(kv_lora), vec(LANES)],
        out_specs=[row(w) for w, _ in outs],
        compiler_params=_params(("parallel", "parallel"), 32 << 20),
        name="odd_post",
    )(z, cos, sin, g_cq.reshape(1, -1), g_ckv.reshape(1, -1), g_rope_k_pad)


def _qup_kernel(cq_ref, w_ref, cos_ref, sin_ref, gn_ref, gr_ref, qq_ref, *, heads, rope_dim, q_scale):
    q = _dot(cq_ref[...], w_ref[...])
    cos, sin = cos_ref[...], sin_ref[...]
    for h in range(heads):
        qn = _rms(q[:, 2 * h * LANES:(2 * h + 1) * LANES], gn_ref[...])
        qq_ref[:, 2 * h * LANES:(2 * h + 1) * LANES] = (qn * q_scale).astype(BF16)
        qp = _rms_low(q[:, (2 * h + 1) * LANES:(2 * h + 2) * LANES], gr_ref[...], rope_dim)
        qp = _rope_half(qp, cos, sin, rope_dim)
        qq_ref[:, (2 * h + 1) * LANES:(2 * h + 2) * LANES] = (qp * q_scale).astype(BF16)


def _qup(cq, w_pad, cos, sin, g_nope_q, g_rope_q_pad, n_heads, rope_dim, qk_dim):
    b, t, kq = cq.shape
    tm = _pick_tile(t, 512, 16)
    hg = 4 if n_heads % 4 == 0 else 1
    tn = hg * 2 * LANES
    return pl.pallas_call(
        functools.partial(_qup_kernel, heads=hg, rope_dim=rope_dim, q_scale=qk_dim ** -0.5 * LOG2E),
        out_shape=jax.ShapeDtypeStruct((b, t, n_heads * 2 * LANES), BF16),
        grid=(b, t // tm, n_heads // hg),
        in_specs=[
            pl.BlockSpec((None, tm, kq), lambda bi, i, j: (bi, i, 0)),
            pl.BlockSpec((kq, tn), lambda bi, i, j: (0, j)),
            pl.BlockSpec((tm, LANES), lambda bi, i, j: (i, 0)),
            pl.BlockSpec((tm, LANES), lambda bi, i, j: (i, 0)),
            pl.BlockSpec((1, LANES), lambda bi, i, j: (0, 0)),
            pl.BlockSpec((1, LANES), lambda bi, i, j: (0, 0)),
        ],
        out_specs=pl.BlockSpec((None, tm, tn), lambda bi, i, j: (bi, i, j)),
        compiler_params=_params(("parallel", "parallel", "arbitrary"), 32 << 20),
        name="mla_q_up",
    )(cq, w_pad, cos, sin, g_nope_q.reshape(1, LANES), g_rope_q_pad)


def _kvup_kernel(ckv_ref, kp_ref, w_ref, gn_ref, kk_ref, v_ref, *, heads):
    kv = _dot(ckv_ref[...], w_ref[...])
    kp = kp_ref[...]
    for h in range(heads):
        kn = _rms(kv[:, 2 * h * LANES:(2 * h + 1) * LANES], gn_ref[...])
        kk_ref[:, 2 * h * LANES:(2 * h + 1) * LANES] = kn.astype(BF16)
        kk_ref[:, (2 * h + 1) * LANES:(2 * h + 2) * LANES] = kp
        v_ref[:, 2 * h * LANES:(2 * h + 1) * LANES] = kv[:, (2 * h + 1) * LANES:(2 * h + 2) * LANES].astype(BF16)
        v_ref[:, (2 * h + 1) * LANES:(2 * h + 2) * LANES] = jnp.ones((kv.shape[0], LANES), BF16)


def _kvup(ckv, kp, w, g_nope_k, n_heads):
    b, t, kk = ckv.shape
    tm = _pick_tile(t, 512, LANES)
    hg = 4 if n_heads % 4 == 0 else 1
    tn = hg * 2 * LANES
    return pl.pallas_call(
        functools.partial(_kvup_kernel, heads=hg),
        out_shape=[jax.ShapeDtypeStruct((b, t, n_heads * 2 * LANES), BF16)] * 2,
        grid=(b, t // tm, n_heads // hg),
        in_specs=[
            pl.BlockSpec((None, tm, kk), lambda bi, i, j: (bi, i, 0)),
            pl.BlockSpec((None, tm, LANES), lambda bi, i, j: (bi, i, 0)),
            pl.BlockSpec((kk, tn), lambda bi, i, j: (0, j)),
            pl.BlockSpec((1, LANES), lambda bi, i, j: (0, 0)),
        ],
        out_specs=[pl.BlockSpec((None, tm, tn), lambda bi, i, j: (bi, i, j))] * 2,
        compiler_params=_params(("parallel", "parallel", "arbitrary"), 32 << 20),
        name="mla_kv_up",
    )(ckv, kp, w, g_nope_k.reshape(1, LANES))


def _rope_tables(pos, dim):
    half = dim // 2
    inv = ROPE_THETA ** (-jnp.arange(half, dtype=F32) * 2.0 / dim)
    ang = pos.astype(F32)[:, None] * inv[None, :]
    cos, sin = jnp.cos(ang), jnp.sin(ang)
    pad = ((0, 0), (0, LANES - dim))
    return (jnp.pad(jnp.concatenate([cos, cos], axis=-1), pad),
            jnp.pad(jnp.concatenate([-sin, sin], axis=-1), pad))


def _pad_lanes(a, width):
    return jnp.pad(a, [(0, 0)] * (a.ndim - 1) + [(0, width - a.shape[-1])])


def _pad_rows(a, rows):
    return jnp.pad(a, [(0, 0), (0, rows - a.shape[1])] + [(0, 0)] * (a.ndim - 2))


def _layer_stack(x, c_mod, tok_pos, seq_shape, caches, p):
    bx, tx, d = x.shape
    ba, ta = seq_shape
    depth = p['w_ffn_in'].shape[0]
    n_diff = p['n_diff']
    n_fox = p['n_fox']
    n_mla = p['n_mla']
    past_len = 0 if caches is None else caches[0].shape[2]
    n_k = past_len + ta
    cos128, sin128 = _rope_tables(tok_pos, LANES)
    rope_dim = p['rope_dim']
    cos_r, sin_r = _rope_tables(tok_pos, rope_dim)
    new = [[] for _ in range(7)]

    def mods(l, s):
        m = c_mod[l]
        sh, sc, gt = m[:, 3 * s], m[:, 3 * s + 1], m[:, 3 * s + 2]
        if bx == m.shape[0]:
            return tuple(a[:, None, :] for a in (sh, sc, gt))
        rep = lambda a: jnp.repeat(a, ta, axis=0).reshape(bx, tx, d)
        return rep(sh), rep(sc), rep(gt)

    seq = lambda a: a.reshape(ba, ta, a.shape[-1])
    for l in range(depth):
        i = l // 2
        g = p['norm_gains'][l]
        sh, sc, gt = mods(l, 0)
        x = _ffn(x, sh, sc, gt, g[0], p['w_ffn_in'], p['w_ffn_out'], l, 0, 0.5)
        sh, sc, gt = mods(l, 1)
        if l % 2 == 0:
            z = _modproj(x, sh, sc, g[1], p['w_in_even'][i])
            (qa, kaf, kab, vaf, vab, qb, kbf, kbb, vbf, vbb, lf) = _even_post(
                z, cos128, sin128, p['qk_norm_even'][i], p['b_forget_pad'][i], 2 * n_diff, n_fox)
            new[0].append(kaf.reshape(ba, ta, n_diff, 2, LANES))
            new[1].append(vaf.reshape(ba, ta, 2, n_diff, LANES).swapaxes(2, 3).reshape(ba, ta, n_diff, 2 * LANES))
            new[2].append(kbf.reshape(ba, ta, n_fox, LANES))
            new[3].append(vbf.reshape(ba, ta, n_fox, LANES))
            new[4].append(seq(lf)[:, :, :n_fox])
            lam_init = 0.8 - 0.6 * math.exp(-0.3 * l)
            if caches is None:
                aq, kk, vv = _fox_prep(lf, kbb, vbb, n_fox)
                oa = _flash_diff(qa, kab, vab, p['diff_lambda'][i], p['diff_subln'][i], lam_init, n_diff, n_k, 0)
                ob = _flash_fox(qb, aq, kk, vv, n_fox, n_k, 0)
            else:
                past = tuple(a[i] for a in caches[:5])
                oa = _dec_diff(seq(qa), past[0], past[1], seq(kab), seq(vab), p['diff_lambda'][i],
                               p['diff_subln'][i], lam_init, n_diff, n_k)
                ob = _dec_fox(seq(qb), past[2], past[3], seq(kbb), seq(vbb), past[4], seq(lf), n_fox, n_k)
            x = _outproj(oa.reshape(bx, tx, -1), ob.reshape(bx, tx, -1), 0, 0, p['w_out_even'][i], x, gt)
        else:
            q_lora, kv_lora = p['q_lora'], p['kv_lora']
            z = _modproj(x, sh, sc, g[1], p['w_in_odd'][i])
            cq, ckvf, ckvb, kpf, kpb = _odd_post(z, cos_r, sin_r, p['mla_cq_norm'][i], p['mla_ckv_norm'][i],
                                                 p['g_rope_pad'][i, 1:2], q_lora, kv_lora, rope_dim)
            new[5].append(seq(ckvf))
            new[6].append(seq(kpf)[:, :, :rope_dim])
            qq = _qup(cq, p['w_uq_pad'][i], cos_r, sin_r, p['mla_qk_norm_nope'][i, 0],
                      p['g_rope_pad'][i, 0:1], n_mla, rope_dim, p['mla_qk_dim'])
            if caches is None:
                kk, v = _kvup(ckvb, kpb, p['w_ukv'][i], p['mla_qk_norm_nope'][i, 1], n_mla)
                o = _flash_mla(qq, kk, v, n_mla, n_k, 0)
            else:
                kp_cache = _pad_lanes(caches[6][i], LANES).astype(BF16)
                o = _dec_mla(seq(qq), caches[5][i], kp_cache, seq(ckvb), seq(kpb), p['w_kn'][i], p['w_v'][i],
                             p['mla_qk_norm_nope'][i, 1], n_mla, n_k)
            o = o.reshape(bx, tx, -1)
            x = _outproj(o, o, 0, 1, p['w_out_odd'][i], x, gt)
        sh, sc, gt = mods(l, 2)
        x = _ffn(x, sh, sc, gt, g[2], p['w_ffn_in'], p['w_ffn_out'], l, 1, 0.5, final_gain=g[3])
    return x, tuple(jnp.stack(lst) for lst in new)


def kernel(x_prompt, x_sample, c_prompt, c_sample, cache_diff_k, cache_diff_v, cache_fox_k, cache_fox_v, cache_fox_logf, cache_mla_ckv, cache_mla_kpe, w_ada, b_ada, norm_gains, w_ffn_in, w_ffn_out, w_in_even, b_forget, qk_norm_even, diff_lambda, diff_subln, w_out_even, w_in_odd, mla_cq_norm, mla_ckv_norm, w_uq, w_ukv, mla_qk_norm_nope, mla_qk_norm_rope, w_out_odd):
    d = x_prompt.shape[-1]
    n_diff, n_fox = cache_diff_k.shape[3], cache_fox_k.shape[3]
    assert cache_diff_k.shape[-1] == LANES and cache_fox_k.shape[-1] == LANES
    q_lora, kv_lora = mla_cq_norm.shape[-1], mla_ckv_norm.shape[-1]
    rope_dim, nope = cache_mla_kpe.shape[-1], mla_qk_norm_nope.shape[-1]
    n_mla = w_uq.shape[-1] // (nope + rope_dim)
    assert nope == LANES and rope_dim <= LANES and w_ukv.shape[-1] == n_mla * 2 * LANES
    n_odd = w_uq.shape[0]

    w_uq_pad = _pad_lanes(w_uq.reshape(n_odd, q_lora, n_mla, nope + rope_dim), 2 * LANES)
    p = {
        'n_diff': n_diff, 'n_fox': n_fox, 'n_mla': n_mla, 'rope_dim': rope_dim,
        'q_lora': q_lora, 'kv_lora': kv_lora, 'mla_qk_dim': nope + rope_dim,
        'norm_gains': norm_gains,
        'w_ffn_in': w_ffn_in.astype(BF16), 'w_ffn_out': w_ffn_out.astype(BF16),
        'w_in_even': _pad_lanes(w_in_even, _round_up(w_in_even.shape[-1], LANES)).astype(BF16),
        'b_forget_pad': _pad_lanes(b_forget, LANES)[:, None, :],
        'qk_norm_even': qk_norm_even, 'diff_lambda': diff_lambda, 'diff_subln': diff_subln,
        'w_out_even': w_out_even.astype(BF16),
        'w_in_odd': _pad_lanes(w_in_odd, q_lora + kv_lora + LANES).astype(BF16),
        'mla_cq_norm': mla_cq_norm, 'mla_ckv_norm': mla_ckv_norm,
        'w_uq_pad': w_uq_pad.reshape(n_odd, q_lora, n_mla * 2 * LANES).astype(BF16),
        'w_ukv': w_ukv.astype(BF16),
        'w_kn': w_ukv.reshape(n_odd, kv_lora, n_mla, 2 * LANES)[..., :LANES].reshape(n_odd, kv_lora, -1).astype(BF16),
        'w_v': w_ukv.reshape(n_odd, kv_lora, n_mla, 2 * LANES)[..., LANES:].reshape(n_odd, kv_lora, -1).astype(BF16),
        'mla_qk_norm_nope': mla_qk_norm_nope,
        'g_rope_pad': _pad_lanes(mla_qk_norm_rope, LANES),
        'w_out_odd': w_out_odd.astype(BF16),
    }

    bp, tp = x_prompt.shape[:2]
    bs, ts = x_sample.shape[:2]
    past_len = cache_diff_k.shape[2]
    mod = _ada(jnp.concatenate([c_prompt, c_sample], axis=0), w_ada, b_ada)
    mod = mod.reshape(mod.shape[0], bp + bs, N_MOD, d)

    pos_p = jnp.arange(tp, dtype=jnp.int32)
    y_prompt, st_p = _layer_stack(x_prompt, mod[:, :bp], pos_p, (bp, tp), None, p)

    pos_s = jnp.tile(past_len + jnp.arange(ts, dtype=jnp.int32), bs)
    caches = (cache_diff_k, cache_diff_v, cache_fox_k, cache_fox_v, cache_fox_logf, cache_mla_ckv, cache_mla_kpe)
    y_sample, st_s = _layer_stack(x_sample.reshape(1, bs * ts, d), mod[:, bp:], pos_s, (bs, ts), caches, p)
    return (y_prompt, y_sample.reshape(bs, ts, d)) + st_p + st_s
```

```python
import functools
import math

import numpy as np
import jax
import jax.numpy as jnp
from jax import lax
from jax.experimental import pallas as pl
from jax.experimental.pallas import tpu as pltpu

F32 = jnp.float32
BF16 = jnp.bfloat16

CHUNK = 64
ROPE_THETA = 10000.0
EPS = 1e-6
NEG_INF = -1e30
N_MOD = 9

LANES = 128
SUBLANES = 8
VMEM_CAP_BYTES = 56 * 1024 * 1024

LOG2E = math.log2(math.e)
CHUNK_SHIFT = CHUNK.bit_length() - 1
assert (1 << CHUNK_SHIFT) == CHUNK


def _round_up(n, m):
    return (n + m - 1) // m * m


def _pick_tile(n, target, quantum):
    if n <= target:
        return n
    best = None
    t = quantum
    while t <= target:
        if n % t == 0:
            best = t
        t += quantum
    assert best is not None, (n, target, quantum)
    return best


def _params(semantics, vmem_bytes):
    limit = int(min(max(vmem_bytes, 16 * 1024 * 1024), VMEM_CAP_BYTES))
    return pltpu.CompilerParams(dimension_semantics=semantics, vmem_limit_bytes=limit)


def _rms(x, gain):
    return x * lax.rsqrt(jnp.mean(x * x, axis=-1, keepdims=True) + EPS) * gain


def _silu(g):
    return g / (1.0 + jnp.exp(-g))


def _dot(a, b):
    return jnp.dot(a, b, preferred_element_type=F32)


def _dot_nt(a, b):
    return lax.dot_general(a, b, (((1,), (1,)), ((), ())), preferred_element_type=F32)


def _ada_kernel(c_ref, w_ref, b_ref, o_ref):
    a = _silu(c_ref[...]).astype(BF16)
    o_ref[...] = _dot(a, w_ref[...].astype(BF16)) + b_ref[...]


def _ada(c_all, w_ada, b_ada):
    depth, d, n = w_ada.shape
    r = c_all.shape[0]
    tn = _pick_tile(n, 1024, LANES)
    return pl.pallas_call(
        _ada_kernel,
        out_shape=jax.ShapeDtypeStruct((depth, r, n), F32),
        grid=(depth, n // tn),
        in_specs=[
            pl.BlockSpec((r, d), lambda l, j: (0, 0)),
            pl.BlockSpec((None, d, tn), lambda l, j: (l, 0, j)),
            pl.BlockSpec((None, 1, tn), lambda l, j: (l, 0, j)),
        ],
        out_specs=pl.BlockSpec((None, r, tn), lambda l, j: (l, 0, j)),
        compiler_params=_params(("arbitrary", "arbitrary"), 2 * d * tn * 4 + 3 * d * tn * 2 + (4 << 20)),
        name="ada_mod",
    )(c_all, w_ada, b_ada.reshape(depth, 1, n))


def _mod_spec(mod, tm):
    d = mod.shape[-1]
    if mod.shape[1] == 1:
        return pl.BlockSpec((None, 1, d), lambda b, i, j: (b, 0, 0))
    return pl.BlockSpec((None, tm, d), lambda b, i, j: (b, i, 0))


def _ffn_kernel(x_ref, sh_ref, sc_ref, gt_ref, g_ref, wg_ref, wu_ref, wo_ref, *rest,
                gate_mul, final_norm):
    if final_norm:
        gf_ref, o_ref, h_ref, acc_ref = rest
    else:
        o_ref, h_ref, acc_ref = rest
    f = pl.program_id(2)

    @pl.when(f == 0)
    def _():
        h = _rms(x_ref[...], g_ref[...]) * (1.0 + sc_ref[...]) + sh_ref[...]
        h_ref[...] = h.astype(BF16)
        acc_ref[...] = jnp.zeros_like(acc_ref)

    h = h_ref[...]
    g = _dot(h, wg_ref[...])
    u = _dot(h, wu_ref[...])
    a = (_silu(g) * u).astype(BF16)
    acc_ref[...] += _dot(a, wo_ref[...])

    @pl.when(f == pl.num_programs(2) - 1)
    def _():
        xn = x_ref[...] + (gate_mul * gt_ref[...]) * acc_ref[...]
        if final_norm:
            xn = _rms(xn, gf_ref[...])
        o_ref[...] = xn


def _ffn(x, sh, sc, gt, gain, w_in, w_out, layer, sub, gate_mul, final_gain=None):
    b, t, d = x.shape
    ff = w_out.shape[2]
    tm = _pick_tile(t, 512, 16)
    tf = _pick_tile(ff, 512, LANES)
    nf = ff // tf
    in_specs = [
        pl.BlockSpec((None, tm, d), lambda bi, i, f: (bi, i, 0)),
        _mod_spec(sh, tm), _mod_spec(sc, tm), _mod_spec(gt, tm),
        pl.BlockSpec((1, d), lambda bi, i, f: (0, 0)),
        pl.BlockSpec((None, None, d, tf), lambda bi, i, f: (layer, sub, 0, f)),
        pl.BlockSpec((None, None, d, tf), lambda bi, i, f: (layer, sub, 0, nf + f)),
        pl.BlockSpec((None, None, tf, d), lambda bi, i, f: (layer, sub, f, 0)),
    ]
    args = [x, sh, sc, gt, gain.reshape(1, d), w_in, w_in, w_out]
    if final_gain is not None:
        in_specs.append(pl.BlockSpec((1, d), lambda bi, i, f: (0, 0)))
        args.append(final_gain.reshape(1, d))
    vmem = (4 * tm * d * 4 + tm * d * 2 + tm * d * 4 + 6 * d * tf * 2 + 4 * tm * tf * 4
            + 6 * tm * d * 4 * (sh.shape[1] != 1) + (4 << 20))
    return pl.pallas_call(
        functools.partial(_ffn_kernel, gate_mul=gate_mul, final_norm=final_gain is not None),
        out_shape=jax.ShapeDtypeStruct((b, t, d), F32),
        grid=(b, t // tm, nf),
        in_specs=in_specs,
        out_specs=pl.BlockSpec((None, tm, d), lambda bi, i, f: (bi, i, 0)),
        scratch_shapes=[pltpu.VMEM((tm, d), BF16), pltpu.VMEM((tm, d), F32)],
        compiler_params=_params(("parallel", "parallel", "arbitrary"), vmem),
        name="ffn",
    )(*args)


def _modproj_kernel(x_ref, sh_ref, sc_ref, g_ref, w_ref, o_ref, h_ref):
    @pl.when(pl.program_id(2) == 0)
    def _():
        h = _rms(x_ref[...], g_ref[...]) * (1.0 + sc_ref[...]) + sh_ref[...]
        h_ref[...] = h.astype(BF16)

    o_ref[...] = _dot(h_ref[...], w_ref[...])


def _modproj(x, sh, sc, gain, w):
    b, t, d = x.shape
    n = w.shape[1]
    tm = _pick_tile(t, 1024, 16)
    tn = _pick_tile(n, 1280, LANES)
    vmem = (2 * tm * d * 4 + tm * d * 2 + 2 * d * tn * 2 + 3 * tm * tn * 4
            + 4 * tm * d * 4 * (sh.shape[1] != 1) + 3 * tm * d * 4 + (4 << 20))
    return pl.pallas_call(
        _modproj_kernel,
        out_shape=jax.ShapeDtypeStruct((b, t, n), F32),
        grid=(b, t // tm, n // tn),
        in_specs=[
            pl.BlockSpec((None, tm, d), lambda bi, i, j: (bi, i, 0)),
            _mod_spec(sh, tm), _mod_spec(sc, tm),
            pl.BlockSpec((1, d), lambda bi, i, j: (0, 0)),
            pl.BlockSpec((d, tn), lambda bi, i, j: (0, j)),
        ],
        out_specs=pl.BlockSpec((None, tm, tn), lambda bi, i, j: (bi, i, j)),
        scratch_shapes=[pltpu.VMEM((tm, d), BF16)],
        compiler_params=_params(("parallel", "parallel", "arbitrary"), vmem),
        name="mod_proj",
    )(x, sh, sc, gain.reshape(1, d), w)


def _outproj_kernel(a1_ref, a2_ref, w1_ref, w2_ref, x_ref, gt_ref, o_ref):
    y = _dot(a1_ref[...], w1_ref[...]) + _dot(a2_ref[...], w2_ref[...])
    o_ref[...] = x_ref[...] + gt_ref[...] * y


def _outproj(a1, a2, blk1, blk2, w, x, gt):
    b, t, d = x.shape
    kh = w.shape[0] // 2
    tm = _pick_tile(t, 512, 16)
    vmem = 4 * tm * kh * 2 + 4 * kh * d * 2 + 5 * tm * d * 4 + 2 * tm * d * 4 * (gt.shape[1] != 1) + (4 << 20)
    return pl.pallas_call(
        _outproj_kernel,
        out_shape=jax.ShapeDtypeStruct((b, t, d), F32),
        grid=(b, t // tm, 1),
        in_specs=[
            pl.BlockSpec((None, tm, kh), lambda bi, i, j: (bi, i, blk1)),
            pl.BlockSpec((None, tm, kh), lambda bi, i, j: (bi, i, blk2)),
            pl.BlockSpec((kh, d), lambda bi, i, j: (0, 0)),
            pl.BlockSpec((kh, d), lambda bi, i, j: (1, 0)),
            pl.BlockSpec((None, tm, d), lambda bi, i, j: (bi, i, 0)),
            _mod_spec(gt, tm),
        ],
        out_specs=pl.BlockSpec((None, tm, d), lambda bi, i, j: (bi, i, 0)),
        compiler_params=_params(("parallel", "parallel", "arbitrary"), vmem),
        name="out_proj",
    )(a1, a2, w, w, x, gt)


def _rope128(y, cos, sin):
    return y * cos + pltpu.roll(y, LANES // 2, 1) * sin


def _log_sigmoid(x):
    return jnp.minimum(x, 0.0) - jnp.log(1.0 + jnp.exp(-jnp.abs(x)))


def _even_post_kernel(z_ref, cos_ref, sin_ref, g_ref, bf_ref,
                      qa_ref, kaf_ref, kab_ref, vaf_ref, vab_ref,
                      qb_ref, kbf_ref, kbb_ref, vbf_ref, vbb_ref, lf_ref,
                      *, n_qa, n_fox, n_forget, q_scale):
    cos = cos_ref[...]
    sin = sin_ref[...]
    hd = LANES
    tm = z_ref.shape[0]
    wa = n_qa * hd
    wb = n_fox * hd
    for j in range(n_qa):
        q = _rope128(_rms(z_ref[:, j * hd:(j + 1) * hd], g_ref[0:1, :]), cos, sin)
        qa_ref[:, j * hd:(j + 1) * hd] = (q * q_scale).astype(BF16)
        k = _rope128(_rms(z_ref[:, wa + j * hd:wa + (j + 1) * hd], g_ref[1:2, :]), cos, sin)
        kaf_ref[pl.ds(j, tm, stride=n_qa), :] = k
        kab_ref[:, j * hd:(j + 1) * hd] = k.astype(BF16)
    for j in range(n_qa):
        head, half = j // 2, j % 2
        vaf_ref[pl.ds(half * (n_qa // 2) + head, tm, stride=n_qa), :] = z_ref[:, 2 * wa + j * hd:2 * wa + (j + 1) * hd]
    vab_ref[...] = z_ref[:, 2 * wa:3 * wa].astype(BF16)
    o = 3 * wa
    for j in range(n_fox):
        q = _rms(z_ref[:, o + j * hd:o + (j + 1) * hd], g_ref[2:3, :])
        qb_ref[:, j * hd:(j + 1) * hd] = (q * q_scale).astype(BF16)
        k = _rms(z_ref[:, o + wb + j * hd:o + wb + (j + 1) * hd], g_ref[3:4, :])
        kbf_ref[:, j * hd:(j + 1) * hd] = k
        kbb_ref[:, j * hd:(j + 1) * hd] = k.astype(BF16)
    vb = z_ref[:, o + 2 * wb:o + 3 * wb]
    vbf_ref[...] = vb
    vbb_ref[...] = vb.astype(BF16)
    fg = z_ref[:, o + 3 * wb:o + 3 * wb + LANES] + bf_ref[...]
    lane = lax.broadcasted_iota(jnp.int32, fg.shape, 1)
    lf_ref[...] = jnp.where(lane < n_forget, _log_sigmoid(fg), 0.0)


def _even_post(z, cos, sin, qk_gain, b_forget_pad, n_qa, n_fox):
    b, t, n = z.shape
    wa, wb = n_qa * LANES, n_fox * LANES
    tm = _pick_tile(t, 256, 16)
    row = lambda w: pl.BlockSpec((None, tm, w), lambda bi, i: (bi, i, 0))
    tab = pl.BlockSpec((tm, LANES), lambda bi, i: (i, 0))
    outs = [(wa, BF16), (wa, F32), (wa, BF16), (wa, F32), (wa, BF16),
            (wb, BF16), (wb, F32), (wb, BF16), (wb, F32), (wb, BF16), (LANES, F32)]
    vmem = 2 * tm * n * 4 + 2 * sum(tm * w * jnp.dtype(dt).itemsize for w, dt in outs) + (8 << 20)
    out_shape = [jax.ShapeDtypeStruct((b, t, w), dt) for w, dt in outs]
    out_specs = [row(w) for w, _ in outs]
    for idx in (1, 3):
        out_shape[idx] = jax.ShapeDtypeStruct((b, t * n_qa, LANES), F32)
        out_specs[idx] = pl.BlockSpec((None, tm * n_qa, LANES), lambda bi, i: (bi, i, 0))
    return pl.pallas_call(
        functools.partial(_even_post_kernel, n_qa=n_qa, n_fox=n_fox, n_forget=n_fox,
                          q_scale=LANES ** -0.5 * LOG2E),
        out_shape=out_shape,
        grid=(b, t // tm),
        in_specs=[row(n), tab, tab,
                  pl.BlockSpec((4, LANES), lambda bi, i: (0, 0)),
                  pl.BlockSpec((1, LANES), lambda bi, i: (0, 0))],
        out_specs=out_specs,
        compiler_params=_params(("parallel", "parallel"), vmem),
        name="even_post",
    )(z, cos, sin, qk_gain, b_forget_pad)


def _split3(x):
    hi = x.astype(BF16)
    r = x - hi.astype(F32)
    mid = r.astype(BF16)
    lo = (r - mid.astype(F32)).astype(BF16)
    return hi, mid, lo


def _fox_prep_kernel(lf_ref, k_ref, v_ref, aq_ref, kk_ref, vv_ref, carry_ref, *, n_heads):
    @pl.when(pl.program_id(1) == 0)
    def _():
        carry_ref[...] = jnp.zeros_like(carry_ref)

    x = lf_ref[...]
    tb = x.shape[0]
    r = lax.broadcasted_iota(jnp.int32, (tb, tb), 0)
    c = lax.broadcasted_iota(jnp.int32, (tb, tb), 1)
    tri = jnp.where(r >= c, 1.0, 0.0).astype(BF16)
    hi, mid, lo = _split3(x)
    cum = _dot(tri, hi) + _dot(tri, mid) + _dot(tri, lo) + carry_ref[...]
    carry_ref[...] = cum[tb - 1:tb, :]
    fh, fm, fl = (p.astype(F32) for p in _split3(cum * LOG2E))
    lane = lax.broadcasted_iota(jnp.int32, (tb, LANES), 1)
    ones_q = jnp.where((lane >= 3) & (lane < 6), 1.0, 0.0)
    ones_k = jnp.where(lane < 3, 1.0, 0.0)
    for h in range(n_heads):
        a, m, l = fh[:, h:h + 1], fm[:, h:h + 1], fl[:, h:h + 1]
        aq = jnp.where(lane == 0, a, jnp.where(lane == 1, m, jnp.where(lane == 2, l, ones_q)))
        ak = jnp.where(lane == 3, -a, jnp.where(lane == 4, -m, jnp.where(lane == 5, -l, ones_k)))
        aq_ref[:, h * LANES:(h + 1) * LANES] = aq.astype(BF16)
        kk_ref[:, 2 * h * LANES:(2 * h + 1) * LANES] = k_ref[:, h * LANES:(h + 1) * LANES]
        kk_ref[:, (2 * h + 1) * LANES:(2 * h + 2) * LANES] = ak.astype(BF16)
        vv_ref[:, 2 * h * LANES:(2 * h + 1) * LANES] = v_ref[:, h * LANES:(h + 1) * LANES]
        vv_ref[:, (2 * h + 1) * LANES:(2 * h + 2) * LANES] = jnp.ones((tb, LANES), BF16)


def _fox_prep(logf_pad, k_all, v_all, n_heads):
    b, t, _ = logf_pad.shape
    tb = _pick_tile(t, 512, LANES)
    w = n_heads * LANES
    narrow = pl.BlockSpec((None, tb, w), lambda bi, i: (bi, i, 0))
    wide = pl.BlockSpec((None, tb, 2 * w), lambda bi, i: (bi, i, 0))
    return pl.pallas_call(
        functools.partial(_fox_prep_kernel, n_heads=n_heads),
        out_shape=[jax.ShapeDtypeStruct((b, t, w), BF16), jax.ShapeDtypeStruct((b, t, 2 * w), BF16),
                   jax.ShapeDtypeStruct((b, t, 2 * w), BF16)],
        grid=(b, t // tb),
        in_specs=[pl.BlockSpec((None, tb, LANES), lambda bi, i: (bi, i, 0)), narrow, narrow],
        out_specs=[narrow, wide, wide],
        scratch_shapes=[pltpu.VMEM((1, LANES), F32)],
        compiler_params=_params(("parallel", "arbitrary"), 32 << 20),
        name="fox_prep",
    )(logf_pad, k_all, v_all)


FLAG_FIRST, FLAG_LAST = 1, 2
KIND_SHIFT = 2
KIND_FULL, KIND_MASK, KIND_DIAG = 0, 1, 2


def _pair_table(n_q, n_k, tq, tk, q_off, causal):
    tk_pad = _round_up(n_k, tk)
    aligned = tq == tk and q_off % tq == 0
    qi, kj, fl = [], [], []
    for i in range(n_q // tq):
        qmin, qmax = q_off + i * tq, q_off + (i + 1) * tq - 1
        row = []
        for j in range(tk_pad // tk):
            kmin, kmax = j * tk, min((j + 1) * tk, n_k) - 1
            if kmin >= n_k:
                continue
            if causal:
                any_vis, all_vis = kmin <= qmax, kmax <= qmin
            else:
                any_vis, all_vis = kmin // CHUNK <= qmax // CHUNK, kmax // CHUNK <= qmin // CHUNK
            all_vis = all_vis and (j + 1) * tk <= n_k
            if any_vis:
                diag = aligned and kmin == qmin and (j + 1) * tk <= n_k
                row.append((j, KIND_FULL if all_vis else KIND_DIAG if diag else KIND_MASK))
        assert row and row[0][0] == 0
        for idx, (j, kind) in enumerate(row):
            qi.append(i)
            kj.append(j)
            fl.append((kind << KIND_SHIFT) | (FLAG_FIRST if idx == 0 else 0) | (FLAG_LAST if idx == len(row) - 1 else 0))
    kinds = sorted({f >> KIND_SHIFT for f in fl})
    as_arr = lambda v: jnp.asarray(np.array(v, np.int32))
    return as_arr(qi), as_arr(kj), as_arr(fl), kinds


def _visible(shape, qpos0, kpos0, n_k, causal, row_period=None):
    if row_period is None:
        rows = qpos0 + lax.broadcasted_iota(jnp.int32, shape, 0)
    else:
        one = lax.broadcasted_iota(jnp.int32, (row_period, shape[1]), 0)
        rows = qpos0 + jnp.concatenate([one] * (shape[0] // row_period), axis=0)
    cols = kpos0 + lax.broadcasted_iota(jnp.int32, shape, 1)
    if causal:
        ok = cols <= rows
    else:
        ok = (cols >> CHUNK_SHIFT) <= (rows >> CHUNK_SHIFT)
    return ok if n_k is None else ok & (cols < n_k)


def _lane_tile(x, n):
    return x if n == LANES else jnp.concatenate([x] * (n // LANES), axis=1)


def _lane_fold(p):
    acc = p[:, 0:LANES]
    for c in range(1, p.shape[1] // LANES):
        acc = acc + p[:, c * LANES:(c + 1) * LANES]
    return acc


def _attend(s, v, m_ref, l_ref, acc_ref, idx):
    m_prev = m_ref[idx]
    m_new = jnp.maximum(m_prev, jnp.max(s, axis=1, keepdims=True))
    alpha = jnp.exp2(m_prev - m_new)
    p = jnp.exp2(s - _lane_tile(m_new, s.shape[1]))
    if l_ref is not None:
        l_ref[idx] = alpha * l_ref[idx] + _lane_fold(p)
    acc_ref[idx] = _lane_tile(alpha, v.shape[1]) * acc_ref[idx] + _dot(p.astype(v.dtype), v)
    m_ref[idx] = m_new


def _flash_frame(qi_ref, kj_ref, fl_ref, m_ref, l_ref, acc_ref, scores, update, finish, first=None, *,
                 tq, tk, rs, q_off, n_k, causal, kinds):
    n = pl.program_id(2)
    flags = fl_ref[n]
    kind = flags >> KIND_SHIFT

    @pl.when((flags & FLAG_FIRST) != 0)
    def _():
        m_ref[...] = jnp.full_like(m_ref, NEG_INF)
        if l_ref is not None:
            l_ref[...] = jnp.zeros_like(l_ref)
        acc_ref[...] = jnp.zeros_like(acc_ref)
        if first is not None:
            first()

    qpos0 = q_off + qi_ref[n] * tq
    kpos0 = kj_ref[n] * tk

    def run(k):
        def masked_scores(r):
            n_keys = (r + 1) * rs if k == KIND_DIAG else tk
            tiles = scores(r, n_keys)
            if k == KIND_DIAG:
                lo = r * rs
                vis = _visible((rs, rs), qpos0 + lo, kpos0 + lo, None, causal)
                blks = [jnp.where(vis, s[:, lo:], NEG_INF) for s in tiles]
                tiles = blks if lo == 0 else [jnp.concatenate([s[:, :lo], b], axis=1) for s, b in zip(tiles, blks)]
            elif k == KIND_MASK:
                vis = _visible(tiles[0].shape, qpos0 + r * rs, kpos0, n_k, causal)
                tiles = [jnp.where(vis, s, NEG_INF) for s in tiles]
            return n_keys, tiles

        order = list(range(tq // rs))
        if k == KIND_DIAG:
            order.reverse()
        pending = masked_scores(order[0])
        for idx, r in enumerate(order):
            n_keys, tiles = pending
            if idx + 1 < len(order):
                pending = masked_scores(order[idx + 1])
            update(r, n_keys, tiles)

    for k in kinds:
        pl.when(kind == k)(functools.partial(run, k))

    @pl.when((flags & FLAG_LAST) != 0)
    def _():
        finish()


def _diff_kernel(qi_ref, kj_ref, fl_ref, q_ref, k_ref, v_ref, lam_ref, sub_ref, o_ref,
                 m_ref, l_ref, acc_ref, *, rs, lam_init, **frame):
    hd = LANES

    def scores(r, n_keys):
        rows = slice(r * rs, (r + 1) * rs)
        return [_dot_nt(q_ref[rows, c * hd:(c + 1) * hd], k_ref[0:n_keys, c * hd:(c + 1) * hd]) for c in range(2)]

    def update(r, n_keys, tiles):
        v = v_ref[0:n_keys, :]
        for c in range(2):
            _attend(tiles[c], v, m_ref, l_ref, acc_ref, (c, slice(r * rs, (r + 1) * rs)))

    def finish():
        lp = lam_ref[...]
        lam = (jnp.exp(jnp.sum(lp[0:1] * lp[1:2], axis=-1, keepdims=True))
               - jnp.exp(jnp.sum(lp[2:3] * lp[3:4], axis=-1, keepdims=True)) + lam_init)
        l0 = jnp.sum(l_ref[0], axis=1, keepdims=True)
        l1 = jnp.sum(l_ref[1], axis=1, keepdims=True)
        o = acc_ref[0] / l0 - lam * (acc_ref[1] / l1)
        o_ref[...] = (_rms(o, sub_ref[...]) * (1.0 - lam_init)).astype(o_ref.dtype)

    _flash_frame(qi_ref, kj_ref, fl_ref, m_ref, l_ref, acc_ref, scores, update, finish, rs=rs, **frame)


def _ones_finish(acc_ref, o_ref):
    o_ref[...] = (acc_ref[:, 0:LANES] / acc_ref[:, LANES:2 * LANES]).astype(o_ref.dtype)


def _fox_kernel(qi_ref, kj_ref, fl_ref, q_ref, aq_ref, k_ref, v_ref, o_ref,
                m_ref, acc_ref, qq_ref, *, rs, **frame):
    def first():
        qq_ref[:, 0:LANES] = q_ref[...]
        qq_ref[:, LANES:2 * LANES] = aq_ref[...]

    def scores(r, n_keys):
        return [_dot_nt(qq_ref[r * rs:(r + 1) * rs, :], k_ref[0:n_keys, :])]

    def update(r, n_keys, tiles):
        _attend(tiles[0], v_ref[0:n_keys, :], m_ref, None, acc_ref, slice(r * rs, (r + 1) * rs))

    _flash_frame(qi_ref, kj_ref, fl_ref, m_ref, None, acc_ref, scores, update,
                 functools.partial(_ones_finish, acc_ref, o_ref), first, rs=rs, **frame)


def _mla_kernel(qi_ref, kj_ref, fl_ref, q_ref, k_ref, v_ref, o_ref,
                m_ref, acc_ref, *, rs, **frame):
    def scores(r, n_keys):
        return [_dot_nt(q_ref[r * rs:(r + 1) * rs, :], k_ref[0:n_keys, :])]

    def update(r, n_keys, tiles):
        _attend(tiles[0], v_ref[0:n_keys, :], m_ref, None, acc_ref, slice(r * rs, (r + 1) * rs))

    _flash_frame(qi_ref, kj_ref, fl_ref, m_ref, None, acc_ref, scores, update,
                 functools.partial(_ones_finish, acc_ref, o_ref), rs=rs, **frame)


def _flash_tiles(n_q, n_k_pad, tile):
    tq = _pick_tile(n_q, tile, 16)
    tk = next((t for t in (tile, tile // 2, tile // 4) if n_k_pad % t == 0), None) or _pick_tile(n_k_pad, tile, LANES)
    rs = _pick_tile(tq, 256, 16)
    return tq, tk, rs


def _flash_call(kernel, n_heads, operands, n_q, n_k_pad, out_width, scratch, n_k, q_off, causal, name,
                tile=2048, **kw):
    b = operands[0][0].shape[0]
    tq, tk, rs = _flash_tiles(n_q, n_k_pad, tile)
    qi, kj, fl, kinds = _pair_table(n_q, n_k, tq, tk, q_off, causal)
    if KIND_DIAG in kinds:
        assert rs % CHUNK == 0 and rs % LANES == 0

    in_specs, args = [], []
    for op in operands:
        a = op[0]
        if len(op) == 1:
            in_specs.append(pl.BlockSpec(a.shape, lambda bi, h, n, qi, kj, fl: (0, 0)))
        elif op[2]:
            in_specs.append(pl.BlockSpec((None, tq, op[1]), lambda bi, h, n, qi, kj, fl: (bi, qi[n], h)))
        else:
            in_specs.append(pl.BlockSpec((None, tk, op[1]), lambda bi, h, n, qi, kj, fl: (bi, kj[n], h)))
        args.append(a)
    vmem = 2 * tq * tk * 4 + 8 * max(tq, tk) * 2 * LANES * 2 * len(operands) + 10 * tq * 2 * LANES * 4 + (8 << 20)
    grid_spec = pltpu.PrefetchScalarGridSpec(
        num_scalar_prefetch=3,
        grid=(b, n_heads, int(qi.shape[0])),
        in_specs=in_specs,
        out_specs=pl.BlockSpec((None, tq, out_width), lambda bi, h, n, qi, kj, fl: (bi, qi[n], h)),
        scratch_shapes=scratch(tq),
    )
    return pl.pallas_call(
        functools.partial(kernel, tq=tq, tk=tk, rs=rs, q_off=q_off, n_k=n_k, causal=causal, kinds=kinds, **kw),
        out_shape=jax.ShapeDtypeStruct((b, n_q, n_heads * out_width), BF16),
        grid_spec=grid_spec,
        compiler_params=_params(("parallel", "parallel", "arbitrary"), vmem),
        name=name,
    )(qi, kj, fl, *args)


def _flash_diff(q, k, v, lam_p, subln, lam_init, n_heads, n_k, q_off):
    w = 2 * LANES
    scratch = lambda tq: [pltpu.VMEM((2, tq, LANES), F32), pltpu.VMEM((2, tq, LANES), F32), pltpu.VMEM((2, tq, w), F32)]
    ops = [(q, w, True), (k, w, False), (v, w, False), (lam_p,), (subln.reshape(1, w),)]
    return _flash_call(_diff_kernel, n_heads, ops, q.shape[1], k.shape[1], w, scratch, n_k, q_off, False,
                       "diff_attn", tile=2048, lam_init=lam_init)


def _flash_fox(q, aq, kk, vv, n_heads, n_k, q_off):
    w = LANES
    scratch = lambda tq: [pltpu.VMEM((tq, LANES), F32), pltpu.VMEM((tq, 2 * w), F32), pltpu.VMEM((tq, 2 * w), BF16)]
    ops = [(q, w, True), (aq, w, True), (kk, 2 * w, False), (vv, 2 * w, False)]
    return _flash_call(_fox_kernel, n_heads, ops, q.shape[1], kk.shape[1], w, scratch, n_k, q_off, True, "fox_attn")


def _flash_mla(qq, kk, vv, n_heads, n_k, q_off):
    w = LANES
    scratch = lambda tq: [pltpu.VMEM((tq, LANES), F32), pltpu.VMEM((tq, 2 * w), F32)]
    ops = [(qq, 2 * w, True), (kk, 2 * w, False), (vv, 2 * w, False)]
    return _flash_call(_mla_kernel, n_heads, ops, qq.shape[1], kk.shape[1], w, scratch, n_k, q_off, False, "mla_attn")


NEW_ROWS = LANES


def _cached_frame(m_ref, l_ref, acc_ref, chains, finish, *, nkb, tk, ta, q_off, n_k, causal):
    n = pl.program_id(1)

    @pl.when(n == 0)
    def _():
        m_ref[...] = jnp.full_like(m_ref, NEG_INF)
        l_ref[...] = jnp.zeros_like(l_ref)
        acc_ref[...] = jnp.zeros_like(acc_ref)

    @pl.when(n < nkb)
    def _():
        chains(True, tk, None)

    @pl.when(n == nkb)
    def _():
        mask_fn = lambda s: jnp.where(_visible(s.shape, q_off, q_off, n_k, causal, ta), s, NEG_INF)
        chains(False, NEW_ROWS, mask_fn)
        finish()


def _dec_diff_kernel(q_ref, kc_ref, vc_ref, kn_ref, vn_ref, lam_ref, sub_ref, o_ref,
                     m_ref, l_ref, acc_ref, *, n_heads, lam_init, **frame):
    hd, g, tk = LANES, 2 * n_heads, frame['tk']

    def chains(cached, n_keys, mask_fn):
        for h in range(n_heads):
            if cached:
                v = jnp.concatenate([vc_ref[pl.ds(h, tk, stride=g), :],
                                     vc_ref[pl.ds(n_heads + h, tk, stride=g), :]],
                                    axis=1).astype(BF16)
            else:
                v = vn_ref[:, 2 * h * hd:(2 * h + 2) * hd]
            for c in range(2):
                j = 2 * h + c
                k = kc_ref[pl.ds(j, tk, stride=g), :].astype(BF16) if cached else kn_ref[:, j * hd:(j + 1) * hd]
                s = _dot_nt(q_ref[:, j * hd:(j + 1) * hd], k)
                if mask_fn is not None:
                    s = mask_fn(s)
                _attend(s, v, m_ref, l_ref, acc_ref, j)

    def finish():
        lp = lam_ref[...]
        lam = (jnp.exp(jnp.sum(lp[0:1] * lp[1:2], axis=-1, keepdims=True))
               - jnp.exp(jnp.sum(lp[2:3] * lp[3:4], axis=-1, keepdims=True)) + lam_init)
        for h in range(n_heads):
            l0 = jnp.sum(l_ref[2 * h], axis=1, keepdims=True)
            l1 = jnp.sum(l_ref[2 * h + 1], axis=1, keepdims=True)
            o = acc_ref[2 * h] / l0 - lam * (acc_ref[2 * h + 1] / l1)
            o_ref[:, 2 * h * hd:(2 * h + 2) * hd] = (_rms(o, sub_ref[...]) * (1.0 - lam_init)).astype(o_ref.dtype)

    _cached_frame(m_ref, l_ref, acc_ref, chains, finish, **frame)


def _dec_fox_kernel(q_ref, kc_ref, vc_ref, kn_ref, vn_ref, fq_ref, fk_ref, o_ref,
                    m_ref, l_ref, acc_ref, *, n_heads, **frame):
    hd, g, tk = LANES, n_heads, frame['tk']

    def chains(cached, n_keys, mask_fn):
        for h in range(n_heads):
            if cached:
                k = kc_ref[pl.ds(h, tk, stride=g), :].astype(BF16)
                v = vc_ref[pl.ds(h, tk, stride=g), :].astype(BF16)
            else:
                k = kn_ref[:, h * hd:(h + 1) * hd]
                v = vn_ref[:, h * hd:(h + 1) * hd]
            bias = (fq_ref[:, h:h + 1] - fk_ref[h:h + 1, 0:n_keys]) * LOG2E
            s = _dot_nt(q_ref[:, h * hd:(h + 1) * hd], k) + bias
            if mask_fn is not None:
                s = mask_fn(s)
            _attend(s, v, m_ref, l_ref, acc_ref, h)

    def finish():
        for h in range(n_heads):
            o = acc_ref[h] / jnp.sum(l_ref[h], axis=1, keepdims=True)
            o_ref[:, h * hd:(h + 1) * hd] = o.astype(o_ref.dtype)

    _cached_frame(m_ref, l_ref, acc_ref, chains, finish, **frame)


def _cached_attn(kernel, q, cache_k, cache_v, new_k, new_v, extra, extra_specs, groups, v_width, n_k, causal,
                 name, **kw):
    b, ta, _ = q.shape
    past = cache_k.shape[1]
    tk = _pick_tile(past, 1024, LANES)
    nkb = past // tk
    kc = cache_k.reshape(b, past * groups, LANES)
    if cache_v.shape[-1] == 2 * LANES:
        vc = cache_v.reshape(b, past, groups // 2, 2, LANES).swapaxes(2, 3).reshape(b, past * groups, LANES)
    else:
        vc = cache_v.reshape(b, past * groups, LANES)
    kn, vn = _pad_rows(new_k, NEW_ROWS), _pad_rows(new_v, NEW_ROWS)
    whole = lambda a: pl.BlockSpec((None,) + a.shape[1:], lambda bi, n: (bi,) + (0,) * (a.ndim - 1))
    cache = pl.BlockSpec((None, tk * groups, LANES), lambda bi, n: (bi, jnp.minimum(n, nkb - 1), 0))
    out_w = q.shape[2]
    return pl.pallas_call(
        functools.partial(kernel, nkb=nkb, tk=tk, ta=ta, q_off=past, n_k=n_k, causal=causal, **kw),
        out_shape=jax.ShapeDtypeStruct((b, ta, out_w), BF16),
        grid=(b, nkb + 1),
        in_specs=[whole(q), cache, cache, whole(kn), whole(vn)] + extra_specs(tk),
        out_specs=pl.BlockSpec((None, ta, out_w), lambda bi, n: (bi, 0, 0)),
        scratch_shapes=[pltpu.VMEM((groups, ta, LANES), F32), pltpu.VMEM((groups, ta, LANES), F32),
                        pltpu.VMEM((groups, ta, v_width), F32)],
        compiler_params=_params(("parallel", "arbitrary"), 4 * tk * groups * LANES * 4 + (16 << 20)),
        name=name,
    )(q, kc, vc, kn, vn, *extra)


def _dec_mla_kernel(q_ref, ckv_ref, kp_ref, ckvn_ref, kpn_ref, wk_ref, wv_ref, gk_ref, o_ref,
                    m_ref, l_ref, acc_ref, s_ref, qp_ref, *, n_heads, **frame):
    hd, ta = LANES, frame['ta']

    @pl.when(pl.program_id(1) == 0)
    def _():
        for h in range(n_heads):
            qp_ref[h * ta:(h + 1) * ta, :] = q_ref[:, (2 * h + 1) * hd:(2 * h + 2) * hd]

    def chains(cached, n_keys, mask_fn):
        ckv = ckv_ref[...].astype(BF16) if cached else ckvn_ref[...]
        kp = kp_ref[...] if cached else kpn_ref[...]
        kvn = _dot(ckv, wk_ref[...])
        for h in range(n_heads):
            kn = _rms(kvn[:, h * hd:(h + 1) * hd], gk_ref[...]).astype(BF16)
            s_ref[h * ta:(h + 1) * ta, 0:n_keys] = _dot_nt(q_ref[:, 2 * h * hd:(2 * h + 1) * hd], kn)
        s = s_ref[:, 0:n_keys] + _dot_nt(qp_ref[...], kp)
        if mask_fn is not None:
            s = mask_fn(s)
        _attend(s, ckv, m_ref, l_ref, acc_ref, slice(None))

    def finish():
        lat = acc_ref[...] / jnp.sum(l_ref[...], axis=1, keepdims=True)
        for h in range(n_heads):
            o = _dot(lat[h * ta:(h + 1) * ta, :].astype(BF16), wv_ref[:, h * hd:(h + 1) * hd])
            o_ref[:, h * hd:(h + 1) * hd] = o.astype(o_ref.dtype)

    _cached_frame(m_ref, l_ref, acc_ref, chains, finish, **frame)


def _dec_mla(qq, cache_ckv, kp_cache, new_ckv, new_kp, w_kn, w_v, g_nope_k, n_heads, n_k):
    b, ta, _ = qq.shape
    past, c = cache_ckv.shape[1:]
    tk = _pick_tile(past, 512, LANES)
    nkb = past // tk
    rows = n_heads * ta
    ckvn, kpn = _pad_rows(new_ckv, NEW_ROWS), _pad_rows(new_kp, NEW_ROWS)
    whole = lambda a: pl.BlockSpec((None,) + a.shape[1:], lambda bi, n: (bi,) + (0,) * (a.ndim - 1))
    const = lambda a: pl.BlockSpec(a.shape, lambda bi, n: (0,) * a.ndim)
    blk = lambda w: pl.BlockSpec((None, tk, w), lambda bi, n: (bi, jnp.minimum(n, nkb - 1), 0))

    kernel = functools.partial(_dec_mla_kernel, n_heads=n_heads, nkb=nkb, tk=tk, ta=ta, q_off=past, n_k=n_k,
                               causal=False)
    gk = g_nope_k.reshape(1, LANES)
    return pl.pallas_call(
        kernel,
        out_shape=jax.ShapeDtypeStruct((b, ta, n_heads * LANES), BF16),
        grid=(b, nkb + 1),
        in_specs=[whole(qq), blk(c), blk(LANES), whole(ckvn), whole(kpn), const(w_kn), const(w_v), const(gk)],
        out_specs=pl.BlockSpec((None, ta, n_heads * LANES), lambda bi, n: (bi, 0, 0)),
        scratch_shapes=[pltpu.VMEM((rows, LANES), F32), pltpu.VMEM((rows, LANES), F32), pltpu.VMEM((rows, c), F32),
                        pltpu.VMEM((rows, tk), F32), pltpu.VMEM((rows, LANES), BF16)],
        compiler_params=_params(("parallel", "arbitrary"), 40 << 20),
        name="mla_attn_cached",
    )(qq, cache_ckv, kp_cache, ckvn, kpn, w_kn, w_v, gk)


def _cumsum_rows_kernel(x_ref, o_ref, carry_ref):
    @pl.when(pl.program_id(1) == 0)
    def _():
        carry_ref[...] = jnp.zeros_like(carry_ref)

    x = x_ref[...]
    tb = x.shape[1]
    r = lax.broadcasted_iota(jnp.int32, (tb, tb), 0)
    c = lax.broadcasted_iota(jnp.int32, (tb, tb), 1)
    tri = jnp.where(r <= c, 1.0, 0.0).astype(BF16)
    hi, mid, lo = _split3(x)
    cum = _dot(hi, tri) + _dot(mid, tri) + _dot(lo, tri) + carry_ref[:, 0:1]
    o_ref[...] = cum
    carry_ref[...] = jnp.broadcast_to(cum[:, tb - 1:tb], carry_ref.shape)


def _cumsum_rows(x, tb):
    b, g, t = x.shape
    return pl.pallas_call(
        _cumsum_rows_kernel,
        out_shape=jax.ShapeDtypeStruct((b, g, t), F32),
        grid=(b, t // tb),
        in_specs=[pl.BlockSpec((None, g, tb), lambda bi, i: (bi, 0, i))],
        out_specs=pl.BlockSpec((None, g, tb), lambda bi, i: (bi, 0, i)),
        scratch_shapes=[pltpu.VMEM((g, LANES), F32)],
        compiler_params=_params(("parallel", "arbitrary"), 32 << 20),
        name="forget_cumsum",
    )(x)


def _dec_diff(q, cache_k, cache_v, new_k, new_v, lam_p, subln, lam_init, n_heads, n_k):
    b, past = cache_k.shape[:2]
    specs = lambda tk: [pl.BlockSpec(lam_p.shape, lambda bi, n: (0, 0)),
                        pl.BlockSpec((1, 2 * LANES), lambda bi, n: (0, 0))]
    return _cached_attn(_dec_diff_kernel, q, cache_k, cache_v, new_k, new_v, [lam_p, subln.reshape(1, 2 * LANES)],
                        specs, 2 * n_heads, 2 * LANES, n_k, False, "diff_attn_cached",
                        n_heads=n_heads, lam_init=lam_init)


def _dec_fox(q, cache_k, cache_v, new_k, new_v, past_logf, new_logf, n_heads, n_k):
    b, past = cache_k.shape[:2]
    ta = q.shape[1]
    tk = _pick_tile(past, 1024, LANES)
    lf = jnp.concatenate([past_logf.astype(F32), new_logf[:, :, :n_heads]], axis=1)
    lf_rows = _pad_lanes(jnp.swapaxes(lf, 1, 2), past + tk)
    f_rows = _cumsum_rows(lf_rows, tk)
    f_q = _pad_lanes(jnp.swapaxes(f_rows[:, :, past:past + ta], 1, 2), LANES)
    specs = lambda tk: [pl.BlockSpec((None, ta, LANES), lambda bi, n: (bi, 0, 0)),
                        pl.BlockSpec((None, n_heads, tk), lambda bi, n: (bi, 0, n))]
    return _cached_attn(_dec_fox_kernel, q, cache_k, cache_v, new_k, new_v, [f_q, f_rows], specs,
                        n_heads, LANES, n_k, True, "fox_attn_cached", n_heads=n_heads)


def _rope_half(y, cos, sin, rope_dim):
    half = rope_dim // 2
    lane = lax.broadcasted_iota(jnp.int32, y.shape, 1)
    rot = jnp.where(lane < half, pltpu.roll(y, LANES - half, 1), pltpu.roll(y, half, 1))
    return y * cos + rot * sin


def _rms_low(x, gain, n):
    return x * lax.rsqrt(jnp.sum(x * x, axis=-1, keepdims=True) * (1.0 / n) + EPS) * gain


def _odd_post_kernel(z_ref, cos_ref, sin_ref, gq_ref, gkv_ref, gr_ref,
                     cq_ref, ckvf_ref, ckvb_ref, kpf_ref, kpb_ref, *, q_lora, kv_lora, rope_dim):
    cq_ref[...] = _rms(z_ref[:, 0:q_lora], gq_ref[...]).astype(BF16)
    ckv = _rms(z_ref[:, q_lora:q_lora + kv_lora], gkv_ref[...])
    ckvf_ref[...] = ckv
    ckvb_ref[...] = ckv.astype(BF16)
    kp = _rms_low(z_ref[:, q_lora + kv_lora:q_lora + kv_lora + LANES], gr_ref[...], rope_dim)
    kp = _rope_half(kp, cos_ref[...], sin_ref[...], rope_dim)
    kpf_ref[...] = kp
    kpb_ref[...] = kp.astype(BF16)


def _odd_post(z, cos, sin, g_cq, g_ckv, g_rope_k_pad, q_lora, kv_lora, rope_dim):
    b, t, n = z.shape
    tm = _pick_tile(t, 512, 16)
    row = lambda w: pl.BlockSpec((None, tm, w), lambda bi, i: (bi, i, 0))
    tab = pl.BlockSpec((tm, LANES), lambda bi, i: (i, 0))
    vec = lambda w: pl.BlockSpec((1, w), lambda bi, i: (0, 0))
    outs = [(q_lora, BF16), (kv_lora, F32), (kv_lora, BF16), (LANES, F32), (LANES, BF16)]
    return pl.pallas_call(
        functools.partial(_odd_post_kernel, q_lora=q_lora, kv_lora=kv_lora, rope_dim=rope_dim),
        out_shape=[jax.ShapeDtypeStruct((b, t, w), dt) for w, dt in outs],
        grid=(b, t // tm),
        in_specs=[row(n), tab, tab, vec(q_lora), vec(kv_lora), vec(LANES)],
        out_specs=[row(w) for w, _ in outs],
        compiler_params=_params(("parallel", "parallel"), 32 << 20),
        name="odd_post",
    )(z, cos, sin, g_cq.reshape(1, -1), g_ckv.reshape(1, -1), g_rope_k_pad)


def _qup_kernel(cq_ref, w_ref, cos_ref, sin_ref, gn_ref, gr_ref, qq_ref, *, heads, rope_dim, q_scale):
    q = _dot(cq_ref[...], w_ref[...])
    cos, sin = cos_ref[...], sin_ref[...]
    for h in range(heads):
        qn = _rms(q[:, 2 * h * LANES:(2 * h + 1) * LANES], gn_ref[...])
        qq_ref[:, 2 * h * LANES:(2 * h + 1) * LANES] = (qn * q_scale).astype(BF16)
        qp = _rms_low(q[:, (2 * h + 1) * LANES:(2 * h + 2) * LANES], gr_ref[...], rope_dim)
        qp = _rope_half(qp, cos, sin, rope_dim)
        qq_ref[:, (2 * h + 1) * LANES:(2 * h + 2) * LANES] = (qp * q_scale).astype(BF16)


def _qup(cq, w_pad, cos, sin, g_nope_q, g_rope_q_pad, n_heads, rope_dim, qk_dim):
    b, t, kq = cq.shape
    tm = _pick_tile(t, 512, 16)
    hg = 4 if n_heads % 4 == 0 else 1
    tn = hg * 2 * LANES
    return pl.pallas_call(
        functools.partial(_qup_kernel, heads=hg, rope_dim=rope_dim, q_scale=qk_dim ** -0.5 * LOG2E),
        out_shape=jax.ShapeDtypeStruct((b, t, n_heads * 2 * LANES), BF16),
        grid=(b, t // tm, n_heads // hg),
        in_specs=[
            pl.BlockSpec((None, tm, kq), lambda bi, i, j: (bi, i, 0)),
            pl.BlockSpec((kq, tn), lambda bi, i, j: (0, j)),
            pl.BlockSpec((tm, LANES), lambda bi, i, j: (i, 0)),
            pl.BlockSpec((tm, LANES), lambda bi, i, j: (i, 0)),
            pl.BlockSpec((1, LANES), lambda bi, i, j: (0, 0)),
            pl.BlockSpec((1, LANES), lambda bi, i, j: (0, 0)),
        ],
        out_specs=pl.BlockSpec((None, tm, tn), lambda bi, i, j: (bi, i, j)),
        compiler_params=_params(("parallel", "parallel", "arbitrary"), 32 << 20),
        name="mla_q_up",
    )(cq, w_pad, cos, sin, g_nope_q.reshape(1, LANES), g_rope_q_pad)


def _kvup_kernel(ckv_ref, kp_ref, w_ref, gn_ref, kk_ref, v_ref, *, heads):
    kv = _dot(ckv_ref[...], w_ref[...])
    kp = kp_ref[...]
    for h in range(heads):
        kn = _rms(kv[:, 2 * h * LANES:(2 * h + 1) * LANES], gn_ref[...])
        kk_ref[:, 2 * h * LANES:(2 * h + 1) * LANES] = kn.astype(BF16)
        kk_ref[:, (2 * h + 1) * LANES:(2 * h + 2) * LANES] = kp
        v_ref[:, 2 * h * LANES:(2 * h + 1) * LANES] = kv[:, (2 * h + 1) * LANES:(2 * h + 2) * LANES].astype(BF16)
        v_ref[:, (2 * h + 1) * LANES:(2 * h + 2) * LANES] = jnp.ones((kv.shape[0], LANES), BF16)


def _kvup(ckv, kp, w, g_nope_k, n_heads):
    b, t, kk = ckv.shape
    tm = _pick_tile(t, 512, LANES)
    hg = 4 if n_heads % 4 == 0 else 1
    tn = hg * 2 * LANES
    return pl.pallas_call(
        functools.partial(_kvup_kernel, heads=hg),
        out_shape=[jax.ShapeDtypeStruct((b, t, n_heads * 2 * LANES), BF16)] * 2,
        grid=(b, t // tm, n_heads // hg),
        in_specs=[
            pl.BlockSpec((None, tm, kk), lambda bi, i, j: (bi, i, 0)),
            pl.BlockSpec((None, tm, LANES), lambda bi, i, j: (bi, i, 0)),
            pl.BlockSpec((kk, tn), lambda bi, i, j: (0, j)),
            pl.BlockSpec((1, LANES), lambda bi, i, j: (0, 0)),
        ],
        out_specs=[pl.BlockSpec((None, tm, tn), lambda bi, i, j: (bi, i, j))] * 2,
        compiler_params=_params(("parallel", "parallel", "arbitrary"), 32 << 20),
        name="mla_kv_up",
    )(ckv, kp, w, g_nope_k.reshape(1, LANES))


def _rope_tables(pos, dim):
    half = dim // 2
    inv = ROPE_THETA ** (-jnp.arange(half, dtype=F32) * 2.0 / dim)
    ang = pos.astype(F32)[:, None] * inv[None, :]
    cos, sin = jnp.cos(ang), jnp.sin(ang)
    pad = ((0, 0), (0, LANES - dim))
    return (jnp.pad(jnp.concatenate([cos, cos], axis=-1), pad),
            jnp.pad(jnp.concatenate([-sin, sin], axis=-1), pad))


def _pad_lanes(a, width):
    return jnp.pad(a, [(0, 0)] * (a.ndim - 1) + [(0, width - a.shape[-1])])


def _pad_rows(a, rows):
    return jnp.pad(a, [(0, 0), (0, rows - a.shape[1])] + [(0, 0)] * (a.ndim - 2))


def _layer_stack(x, c_mod, tok_pos, seq_shape, caches, p):
    bx, tx, d = x.shape
    ba, ta = seq_shape
    depth = p['w_ffn_in'].shape[0]
    n_diff = p['n_diff']
    n_fox = p['n_fox']
    n_mla = p['n_mla']
    past_len = 0 if caches is None else caches[0].shape[2]
    n_k = past_len + ta
    cos128, sin128 = _rope_tables(tok_pos, LANES)
    rope_dim = p['rope_dim']
    cos_r, sin_r = _rope_tables(tok_pos, rope_dim)
    new = [[] for _ in range(7)]

    def mods(l, s):
        m = c_mod[l]
        sh, sc, gt = m[:, 3 * s], m[:, 3 * s + 1], m[:, 3 * s + 2]
        if bx == m.shape[0]:
            return tuple(a[:, None, :] for a in (sh, sc, gt))
        rep = lambda a: jnp.repeat(a, ta, axis=0).reshape(bx, tx, d)
        return rep(sh), rep(sc), rep(gt)

    seq = lambda a: a.reshape(ba, ta, a.shape[-1])
    for l in range(depth):
        i = l // 2
        g = p['norm_gains'][l]
        sh, sc, gt = mods(l, 0)
        x = _ffn(x, sh, sc, gt, g[0], p['w_ffn_in'], p['w_ffn_out'], l, 0, 0.5)
        sh, sc, gt = mods(l, 1)
        if l % 2 == 0:
            z = _modproj(x, sh, sc, g[1], p['w_in_even'][i])
            (qa, kaf, kab, vaf, vab, qb, kbf, kbb, vbf, vbb, lf) = _even_post(
                z, cos128, sin128, p['qk_norm_even'][i], p['b_forget_pad'][i], 2 * n_diff, n_fox)
            new[0].append(kaf.reshape(ba, ta, n_diff, 2, LANES))
            new[1].append(vaf.reshape(ba, ta, 2, n_diff, LANES).swapaxes(2, 3).reshape(ba, ta, n_diff, 2 * LANES))
            new[2].append(kbf.reshape(ba, ta, n_fox, LANES))
            new[3].append(vbf.reshape(ba, ta, n_fox, LANES))
            new[4].append(seq(lf)[:, :, :n_fox])
            lam_init = 0.8 - 0.6 * math.exp(-0.3 * l)
            if caches is None:
                aq, kk, vv = _fox_prep(lf, kbb, vbb, n_fox)
                oa = _flash_diff(qa, kab, vab, p['diff_lambda'][i], p['diff_subln'][i], lam_init, n_diff, n_k, 0)
                ob = _flash_fox(qb, aq, kk, vv, n_fox, n_k, 0)
            else:
                past = tuple(a[i] for a in caches[:5])
                oa = _dec_diff(seq(qa), past[0], past[1], seq(kab), seq(vab), p['diff_lambda'][i],
                               p['diff_subln'][i], lam_init, n_diff, n_k)
                ob = _dec_fox(seq(qb), past[2], past[3], seq(kbb), seq(vbb), past[4], seq(lf), n_fox, n_k)
            x = _outproj(oa.reshape(bx, tx, -1), ob.reshape(bx, tx, -1), 0, 0, p['w_out_even'][i], x, gt)
        else:
            q_lora, kv_lora = p['q_lora'], p['kv_lora']
            z = _modproj(x, sh, sc, g[1], p['w_in_odd'][i])
            cq, ckvf, ckvb, kpf, kpb = _odd_post(z, cos_r, sin_r, p['mla_cq_norm'][i], p['mla_ckv_norm'][i],
                                                 p['g_rope_pad'][i, 1:2], q_lora, kv_lora, rope_dim)
            new[5].append(seq(ckvf))
            new[6].append(seq(kpf)[:, :, :rope_dim])
            qq = _qup(cq, p['w_uq_pad'][i], cos_r, sin_r, p['mla_qk_norm_nope'][i, 0],
                      p['g_rope_pad'][i, 0:1], n_mla, rope_dim, p['mla_qk_dim'])
            if caches is None:
                kk, v = _kvup(ckvb, kpb, p['w_ukv'][i], p['mla_qk_norm_nope'][i, 1], n_mla)
                o = _flash_mla(qq, kk, v, n_mla, n_k, 0)
            else:
                kp_cache = _pad_lanes(caches[6][i], LANES).astype(BF16)
                o = _dec_mla(seq(qq), caches[5][i], kp_cache, seq(ckvb), seq(kpb), p['w_kn'][i], p['w_v'][i],
                             p['mla_qk_norm_nope'][i, 1], n_mla, n_k)
            o = o.reshape(bx, tx, -1)
            x = _outproj(o, o, 0, 1, p['w_out_odd'][i], x, gt)
        sh, sc, gt = mods(l, 2)
        x = _ffn(x, sh, sc, gt, g[2], p['w_ffn_in'], p['w_ffn_out'], l, 1, 0.5, final_gain=g[3])
    return x, tuple(jnp.stack(lst) for lst in new)


def kernel(x_prompt, x_sample, c_prompt, c_sample, cache_diff_k, cache_diff_v, cache_fox_k, cache_fox_v, cache_fox_logf, cache_mla_ckv, cache_mla_kpe, w_ada, b_ada, norm_gains, w_ffn_in, w_ffn_out, w_in_even, b_forget, qk_norm_even, diff_lambda, diff_subln, w_out_even, w_in_odd, mla_cq_norm, mla_ckv_norm, w_uq, w_ukv, mla_qk_norm_nope, mla_qk_norm_rope, w_out_odd):
    d = x_prompt.shape[-1]
    n_diff, n_fox = cache_diff_k.shape[3], cache_fox_k.shape[3]
    assert cache_diff_k.shape[-1] == LANES and cache_fox_k.shape[-1] == LANES
    q_lora, kv_lora = mla_cq_norm.shape[-1], mla_ckv_norm.shape[-1]
    rope_dim, nope = cache_mla_kpe.shape[-1], mla_qk_norm_nope.shape[-1]
    n_mla = w_uq.shape[-1] // (nope + rope_dim)
    assert nope == LANES and rope_dim <= LANES and w_ukv.shape[-1] == n_mla * 2 * LANES
    n_odd = w_uq.shape[0]

    w_uq_pad = _pad_lanes(w_uq.reshape(n_odd, q_lora, n_mla, nope + rope_dim), 2 * LANES)
    p = {
        'n_diff': n_diff, 'n_fox': n_fox, 'n_mla': n_mla, 'rope_dim': rope_dim,
        'q_lora': q_lora, 'kv_lora': kv_lora, 'mla_qk_dim': nope + rope_dim,
        'norm_gains': norm_gains,
        'w_ffn_in': w_ffn_in.astype(BF16), 'w_ffn_out': w_ffn_out.astype(BF16),
        'w_in_even': _pad_lanes(w_in_even, _round_up(w_in_even.shape[-1], LANES)).astype(BF16),
        'b_forget_pad': _pad_lanes(b_forget, LANES)[:, None, :],
        'qk_norm_even': qk_norm_even, 'diff_lambda': diff_lambda, 'diff_subln': diff_subln,
        'w_out_even': w_out_even.astype(BF16),
        'w_in_odd': _pad_lanes(w_in_odd, q_lora + kv_lora + LANES).astype(BF16),
        'mla_cq_norm': mla_cq_norm, 'mla_ckv_norm': mla_ckv_norm,
        'w_uq_pad': w_uq_pad.reshape(n_odd, q_lora, n_mla * 2 * LANES).astype(BF16),
        'w_ukv': w_ukv.astype(BF16),
        'w_kn': w_ukv.reshape(n_odd, kv_lora, n_mla, 2 * LANES)[..., :LANES].reshape(n_odd, kv_lora, -1).astype(BF16),
        'w_v': w_ukv.reshape(n_odd, kv_lora, n_mla, 2 * LANES)[..., LANES:].reshape(n_odd, kv_lora, -1).astype(BF16),
        'mla_qk_norm_nope': mla_qk_norm_nope,
        'g_rope_pad': _pad_lanes(mla_qk_norm_rope, LANES),
        'w_out_odd': w_out_odd.astype(BF16),
    }

    bp, tp = x_prompt.shape[:2]
    bs, ts = x_sample.shape[:2]
    past_len = cache_diff_k.shape[2]
    mod = _ada(jnp.concatenate([c_prompt, c_sample], axis=0), w_ada, b_ada)
    mod = mod.reshape(mod.shape[0], bp + bs, N_MOD, d)

    pos_p = jnp.arange(tp, dtype=jnp.int32)
    y_prompt, st_p = _layer_stack(x_prompt, mod[:, :bp], pos_p, (bp, tp), None, p)

    pos_s = jnp.tile(past_len + jnp.arange(ts, dtype=jnp.int32), bs)
    caches = (cache_diff_k, cache_diff_v, cache_fox_k, cache_fox_v, cache_fox_logf, cache_mla_ckv, cache_mla_kpe)
    y_sample, st_s = _layer_stack(x_sample.reshape(1, bs * ts, d), mod[:, bp:], pos_s, (bs, ts), caches, p)
    return (y_prompt, y_sample.reshape(bs, ts, d)) + st_p + st_s
```

```python
import functools
import math

import numpy as np
import jax
import jax.numpy as jnp
from jax import lax
from jax.experimental import pallas as pl
from jax.experimental.pallas import tpu as pltpu

F32 = jnp.float32
BF16 = jnp.bfloat16

CHUNK = 64
ROPE_THETA = 10000.0
EPS = 1e-6
NEG_INF = -1e30
N_MOD = 9

LANES = 128
SUBLANES = 8
VMEM_CAP_BYTES = 56 * 1024 * 1024

LOG2E = math.log2(math.e)
CHUNK_SHIFT = CHUNK.bit_length() - 1
assert (1 << CHUNK_SHIFT) == CHUNK


def _round_up(n, m):
    return (n + m - 1) // m * m


def _pick_tile(n, target, quantum):
    if n <= target:
        return n
    best = None
    t = quantum
    while t <= target:
        if n % t == 0:
            best = t
        t += quantum
    assert best is not None, (n, target, quantum)
    return best


def _params(semantics, vmem_bytes):
    limit = int(min(max(vmem_bytes, 16 * 1024 * 1024), VMEM_CAP_BYTES))
    return pltpu.CompilerParams(dimension_semantics=semantics, vmem_limit_bytes=limit)


def _rms(x, gain):
    return x * lax.rsqrt(jnp.mean(x * x, axis=-1, keepdims=True) + EPS) * gain


def _silu(g):
    return g / (1.0 + jnp.exp(-g))


def _dot(a, b):
    return jnp.dot(a, b, preferred_element_type=F32)


def _dot_nt(a, b):
    return lax.dot_general(a, b, (((1,), (1,)), ((), ())), preferred_element_type=F32)


def _ada_kernel(c_ref, w_ref, b_ref, o_ref):
    a = _silu(c_ref[...]).astype(BF16)
    o_ref[...] = _dot(a, w_ref[...].astype(BF16)) + b_ref[...]


def _ada(c_all, w_ada, b_ada):
    depth, d, n = w_ada.shape
    r = c_all.shape[0]
    tn = _pick_tile(n, 1024, LANES)
    return pl.pallas_call(
        _ada_kernel,
        out_shape=jax.ShapeDtypeStruct((depth, r, n), F32),
        grid=(depth, n // tn),
        in_specs=[
            pl.BlockSpec((r, d), lambda l, j: (0, 0)),
            pl.BlockSpec((None, d, tn), lambda l, j: (l, 0, j)),
            pl.BlockSpec((None, 1, tn), lambda l, j: (l, 0, j)),
        ],
        out_specs=pl.BlockSpec((None, r, tn), lambda l, j: (l, 0, j)),
        compiler_params=_params(("arbitrary", "arbitrary"), 2 * d * tn * 4 + 3 * d * tn * 2 + (4 << 20)),
        name="ada_mod",
    )(c_all, w_ada, b_ada.reshape(depth, 1, n))


def _mod_spec(mod, tm):
    d = mod.shape[-1]
    if mod.shape[1] == 1:
        return pl.BlockSpec((None, 1, d), lambda b, i, j: (b, 0, 0))
    return pl.BlockSpec((None, tm, d), lambda b, i, j: (b, i, 0))


def _ffn_kernel(x_ref, sh_ref, sc_ref, gt_ref, g_ref, wg_ref, wu_ref, wo_ref, *rest,
                gate_mul, final_norm):
    if final_norm:
        gf_ref, o_ref, h_ref, acc_ref = rest
    else:
        o_ref, h_ref, acc_ref = rest
    f = pl.program_id(2)

    @pl.when(f == 0)
    def _():
        h = _rms(x_ref[...], g_ref[...]) * (1.0 + sc_ref[...]) + sh_ref[...]
        h_ref[...] = h.astype(BF16)
        acc_ref[...] = jnp.zeros_like(acc_ref)

    h = h_ref[...]
    g = _dot(h, wg_ref[...])
    u = _dot(h, wu_ref[...])
    a = (_silu(g) * u).astype(BF16)
    acc_ref[...] += _dot(a, wo_ref[...])

    @pl.when(f == pl.num_programs(2) - 1)
    def _():
        xn = x_ref[...] + (gate_mul * gt_ref[...]) * acc_ref[...]
        if final_norm:
            xn = _rms(xn, gf_ref[...])
        o_ref[...] = xn


def _ffn(x, sh, sc, gt, gain, w_in, w_out, layer, sub, gate_mul, final_gain=None):
    b, t, d = x.shape
    ff = w_out.shape[2]
    tm = _pick_tile(t, 512, 16)
    tf = _pick_tile(ff, 512, LANES)
    nf = ff // tf
    in_specs = [
        pl.BlockSpec((None, tm, d), lambda bi, i, f: (bi, i, 0)),
        _mod_spec(sh, tm), _mod_spec(sc, tm), _mod_spec(gt, tm),
        pl.BlockSpec((1, d), lambda bi, i, f: (0, 0)),
        pl.BlockSpec((None, None, d, tf), lambda bi, i, f: (layer, sub, 0, f)),
        pl.BlockSpec((None, None, d, tf), lambda bi, i, f: (layer, sub, 0, nf + f)),
        pl.BlockSpec((None, None, tf, d), lambda bi, i, f: (layer, sub, f, 0)),
    ]
    args = [x, sh, sc, gt, gain.reshape(1, d), w_in, w_in, w_out]
    if final_gain is not None:
        in_specs.append(pl.BlockSpec((1, d), lambda bi, i, f: (0, 0)))
        args.append(final_gain.reshape(1, d))
    vmem = (4 * tm * d * 4 + tm * d * 2 + tm * d * 4 + 6 * d * tf * 2 + 4 * tm * tf * 4
            + 6 * tm * d * 4 * (sh.shape[1] != 1) + (4 << 20))
    return pl.pallas_call(
        functools.partial(_ffn_kernel, gate_mul=gate_mul, final_norm=final_gain is not None),
        out_shape=jax.ShapeDtypeStruct((b, t, d), F32),
        grid=(b, t // tm, nf),
        in_specs=in_specs,
        out_specs=pl.BlockSpec((None, tm, d), lambda bi, i, f: (bi, i, 0)),
        scratch_shapes=[pltpu.VMEM((tm, d), BF16), pltpu.VMEM((tm, d), F32)],
        compiler_params=_params(("parallel", "parallel", "arbitrary"), vmem),
        name="ffn",
    )(*args)


def _modproj_kernel(x_ref, sh_ref, sc_ref, g_ref, w_ref, o_ref, h_ref):
    @pl.when(pl.program_id(2) == 0)
    def _():
        h = _rms(x_ref[...], g_ref[...]) * (1.0 + sc_ref[...]) + sh_ref[...]
        h_ref[...] = h.astype(BF16)

    o_ref[...] = _dot(h_ref[...], w_ref[...])


def _modproj(x, sh, sc, gain, w):
    b, t, d = x.shape
    n = w.shape[1]
    tm = _pick_tile(t, 1024, 16)
    tn = _pick_tile(n, 1280, LANES)
    vmem = (2 * tm * d * 4 + tm * d * 2 + 2 * d * tn * 2 + 3 * tm * tn * 4
            + 4 * tm * d * 4 * (sh.shape[1] != 1) + 3 * tm * d * 4 + (4 << 20))
    return pl.pallas_call(
        _modproj_kernel,
        out_shape=jax.ShapeDtypeStruct((b, t, n), F32),
        grid=(b, t // tm, n // tn),
        in_specs=[
            pl.BlockSpec((None, tm, d), lambda bi, i, j: (bi, i, 0)),
            _mod_spec(sh, tm), _mod_spec(sc, tm),
            pl.BlockSpec((1, d), lambda bi, i, j: (0, 0)),
            pl.BlockSpec((d, tn), lambda bi, i, j: (0, j)),
        ],
        out_specs=pl.BlockSpec((None, tm, tn), lambda bi, i, j: (bi, i, j)),
        scratch_shapes=[pltpu.VMEM((tm, d), BF16)],
        compiler_params=_params(("parallel", "parallel", "arbitrary"), vmem),
        name="mod_proj",
    )(x, sh, sc, gain.reshape(1, d), w)


def _outproj_kernel(a1_ref, a2_ref, w1_ref, w2_ref, x_ref, gt_ref, o_ref):
    y = _dot(a1_ref[...], w1_ref[...]) + _dot(a2_ref[...], w2_ref[...])
    o_ref[...] = x_ref[...] + gt_ref[...] * y


def _outproj(a1, a2, blk1, blk2, w, x, gt):
    b, t, d = x.shape
    kh = w.shape[0] // 2
    tm = _pick_tile(t, 512, 16)
    vmem = 4 * tm * kh * 2 + 4 * kh * d * 2 + 5 * tm * d * 4 + 2 * tm * d * 4 * (gt.shape[1] != 1) + (4 << 20)
    return pl.pallas_call(
        _outproj_kernel,
        out_shape=jax.ShapeDtypeStruct((b, t, d), F32),
        grid=(b, t // tm, 1),
        in_specs=[
            pl.BlockSpec((None, tm, kh), lambda bi, i, j: (bi, i, blk1)),
            pl.BlockSpec((None, tm, kh), lambda bi, i, j: (bi, i, blk2)),
            pl.BlockSpec((kh, d), lambda bi, i, j: (0, 0)),
            pl.BlockSpec((kh, d), lambda bi, i, j: (1, 0)),
            pl.BlockSpec((None, tm, d), lambda bi, i, j: (bi, i, 0)),
            _mod_spec(gt, tm),
        ],
        out_specs=pl.BlockSpec((None, tm, d), lambda bi, i, j: (bi, i, 0)),
        compiler_params=_params(("parallel", "parallel", "arbitrary"), vmem),
        name="out_proj",
    )(a1, a2, w, w, x, gt)


def _rope128(y, cos, sin):
    return y * cos + pltpu.roll(y, LANES // 2, 1) * sin


def _log_sigmoid(x):
    return jnp.minimum(x, 0.0) - jnp.log(1.0 + jnp.exp(-jnp.abs(x)))


def _even_proj_kernel(x_ref, sh_ref, sc_ref, g_ref, w_ref, wf_ref, cos_ref, sin_ref, gn_ref, bf_ref,
                      qa_ref, kaf_ref, kab_ref, vaf_ref, vab_ref,
                      qb_ref, kbf_ref, kbb_ref, vbf_ref, vbb_ref, lf_ref, h_ref,
                      *, n_grp, n_forget, q_scale):
    j = pl.program_id(2)
    tm, hd = x_ref.shape[0], LANES

    @pl.when(j == 0)
    def _():
        h = (_rms(x_ref[...], g_ref[...]) * (1.0 + sc_ref[...]) + sh_ref[...]).astype(BF16)
        h_ref[...] = h
        fg = _dot(h, wf_ref[...]) + bf_ref[...]
        lane = lax.broadcasted_iota(jnp.int32, fg.shape, 1)
        lf_ref[...] = jnp.where(lane < n_forget, _log_sigmoid(fg), 0.0)

    def heads():
        z = _dot(h_ref[...], w_ref[...])
        return [z[:, i * hd:(i + 1) * hd] for i in range(n_grp)]

    @pl.when(j == 0)
    def _():
        for i, z in enumerate(heads()):
            q = _rope128(_rms(z, gn_ref[0:1, :]), cos_ref[...], sin_ref[...])
            qa_ref[:, i * hd:(i + 1) * hd] = (q * q_scale).astype(BF16)

    @pl.when(j == 1)
    def _():
        for i, z in enumerate(heads()):
            k = _rope128(_rms(z, gn_ref[1:2, :]), cos_ref[...], sin_ref[...])
            kaf_ref[pl.ds(i, tm, stride=n_grp), :] = k
            kab_ref[:, i * hd:(i + 1) * hd] = k.astype(BF16)

    @pl.when(j == 2)
    def _():
        for i, z in enumerate(heads()):
            vaf_ref[pl.ds((i % 2) * (n_grp // 2) + i // 2, tm, stride=n_grp), :] = z
            vab_ref[:, i * hd:(i + 1) * hd] = z.astype(BF16)

    @pl.when(j == 3)
    def _():
        for i, z in enumerate(heads()):
            qb_ref[:, i * hd:(i + 1) * hd] = (_rms(z, gn_ref[2:3, :]) * q_scale).astype(BF16)

    @pl.when(j == 4)
    def _():
        for i, z in enumerate(heads()):
            k = _rms(z, gn_ref[3:4, :])
            kbf_ref[:, i * hd:(i + 1) * hd] = k
            kbb_ref[:, i * hd:(i + 1) * hd] = k.astype(BF16)

    @pl.when(j == 5)
    def _():
        for i, z in enumerate(heads()):
            vbf_ref[:, i * hd:(i + 1) * hd] = z
            vbb_ref[:, i * hd:(i + 1) * hd] = z.astype(BF16)


def _even_proj(x, sh, sc, gain, w_main, w_forget, cos, sin, qk_gain, b_forget_pad, n_grp, n_fox):
    b, t, d = x.shape
    wg = n_grp * LANES
    assert w_main.shape[1] == 6 * wg and n_fox == n_grp
    tm = _pick_tile(t, 512, 16)
    row = lambda w: pl.BlockSpec((None, tm, w), lambda bi, i, j: (bi, i, 0))
    tab = pl.BlockSpec((tm, LANES), lambda bi, i, j: (i, 0))
    const = lambda shape: pl.BlockSpec(shape, lambda bi, i, j: (0,) * len(shape))
    outs = [(wg, BF16), (wg, F32), (wg, BF16), (wg, F32), (wg, BF16),
            (wg, BF16), (wg, F32), (wg, BF16), (wg, F32), (wg, BF16), (LANES, F32)]
    out_shape = [jax.ShapeDtypeStruct((b, t, w), dt) for w, dt in outs]
    out_specs = [row(w) for w, _ in outs]
    for idx in (1, 3):
        out_shape[idx] = jax.ShapeDtypeStruct((b, t * n_grp, LANES), F32)
        out_specs[idx] = pl.BlockSpec((None, tm * n_grp, LANES), lambda bi, i, j: (bi, i, 0))
    vmem = (2 * tm * d * 4 + tm * d * 2 + 4 * d * wg * 2 + 2 * sum(tm * w * jnp.dtype(dt).itemsize for w, dt in outs)
            + 4 * tm * wg * 4 + 4 * tm * d * 4 * (sh.shape[1] != 1) + (4 << 20))
    return pl.pallas_call(
        functools.partial(_even_proj_kernel, n_grp=n_grp, n_forget=n_fox, q_scale=LANES ** -0.5 * LOG2E),
        out_shape=out_shape,
        grid=(b, t // tm, 6),
        in_specs=[pl.BlockSpec((None, tm, d), lambda bi, i, j: (bi, i, 0)),
                  _mod_spec(sh, tm), _mod_spec(sc, tm), const((1, d)),
                  pl.BlockSpec((d, wg), lambda bi, i, j: (0, j)), const((d, LANES)),
                  tab, tab, const((4, LANES)), const((1, LANES))],
        out_specs=out_specs,
        scratch_shapes=[pltpu.VMEM((tm, d), BF16)],
        compiler_params=_params(("parallel", "parallel", "arbitrary"), vmem),
        name="even_proj",
    )(x, sh, sc, gain.reshape(1, d), w_main, w_forget, cos, sin, qk_gain, b_forget_pad)


def _split3(x):
    hi = x.astype(BF16)
    r = x - hi.astype(F32)
    mid = r.astype(BF16)
    lo = (r - mid.astype(F32)).astype(BF16)
    return hi, mid, lo


def _fox_prep_kernel(lf_ref, k_ref, v_ref, aq_ref, kk_ref, vv_ref, carry_ref, *, n_heads):
    @pl.when(pl.program_id(1) == 0)
    def _():
        carry_ref[...] = jnp.zeros_like(carry_ref)

    x = lf_ref[...]
    tb = x.shape[0]
    r = lax.broadcasted_iota(jnp.int32, (tb, tb), 0)
    c = lax.broadcasted_iota(jnp.int32, (tb, tb), 1)
    tri = jnp.where(r >= c, 1.0, 0.0).astype(BF16)
    hi, mid, lo = _split3(x)
    cum = _dot(tri, hi) + _dot(tri, mid) + _dot(tri, lo) + carry_ref[...]
    carry_ref[...] = cum[tb - 1:tb, :]
    fh, fm, fl = (p.astype(F32) for p in _split3(cum * LOG2E))
    lane = lax.broadcasted_iota(jnp.int32, (tb, LANES), 1)
    ones_q = jnp.where((lane >= 3) & (lane < 6), 1.0, 0.0)
    ones_k = jnp.where(lane < 3, 1.0, 0.0)
    for h in range(n_heads):
        a, m, l = fh[:, h:h + 1], fm[:, h:h + 1], fl[:, h:h + 1]
        aq = jnp.where(lane == 0, a, jnp.where(lane == 1, m, jnp.where(lane == 2, l, ones_q)))
        ak = jnp.where(lane == 3, -a, jnp.where(lane == 4, -m, jnp.where(lane == 5, -l, ones_k)))
        aq_ref[:, h * LANES:(h + 1) * LANES] = aq.astype(BF16)
        kk_ref[:, 2 * h * LANES:(2 * h + 1) * LANES] = k_ref[:, h * LANES:(h + 1) * LANES]
        kk_ref[:, (2 * h + 1) * LANES:(2 * h + 2) * LANES] = ak.astype(BF16)
        vv_ref[:, 2 * h * LANES:(2 * h + 1) * LANES] = v_ref[:, h * LANES:(h + 1) * LANES]
        vv_ref[:, (2 * h + 1) * LANES:(2 * h + 2) * LANES] = jnp.ones((tb, LANES), BF16)


def _fox_prep(logf_pad, k_all, v_all, n_heads):
    b, t, _ = logf_pad.shape
    tb = _pick_tile(t, 512, LANES)
    w = n_heads * LANES
    narrow = pl.BlockSpec((None, tb, w), lambda bi, i: (bi, i, 0))
    wide = pl.BlockSpec((None, tb, 2 * w), lambda bi, i: (bi, i, 0))
    return pl.pallas_call(
        functools.partial(_fox_prep_kernel, n_heads=n_heads),
        out_shape=[jax.ShapeDtypeStruct((b, t, w), BF16), jax.ShapeDtypeStruct((b, t, 2 * w), BF16),
                   jax.ShapeDtypeStruct((b, t, 2 * w), BF16)],
        grid=(b, t // tb),
        in_specs=[pl.BlockSpec((None, tb, LANES), lambda bi, i: (bi, i, 0)), narrow, narrow],
        out_specs=[narrow, wide, wide],
        scratch_shapes=[pltpu.VMEM((1, LANES), F32)],
        compiler_params=_params(("parallel", "arbitrary"), 32 << 20),
        name="fox_prep",
    )(logf_pad, k_all, v_all)


FLAG_FIRST, FLAG_LAST = 1, 2
KIND_SHIFT = 2
KIND_FULL, KIND_MASK, KIND_DIAG = 0, 1, 2
SCORE_LOOKAHEAD = 1


def _pair_table(n_q, n_k, tq, tk, q_off, causal):
    tk_pad = _round_up(n_k, tk)
    aligned = tq == tk and q_off % tq == 0
    qi, kj, fl = [], [], []
    for i in range(n_q // tq):
        qmin, qmax = q_off + i * tq, q_off + (i + 1) * tq - 1
        row = []
        for j in range(tk_pad // tk):
            kmin, kmax = j * tk, min((j + 1) * tk, n_k) - 1
            if kmin >= n_k:
                continue
            if causal:
                any_vis, all_vis = kmin <= qmax, kmax <= qmin
            else:
                any_vis, all_vis = kmin // CHUNK <= qmax // CHUNK, kmax // CHUNK <= qmin // CHUNK
            all_vis = all_vis and (j + 1) * tk <= n_k
            if any_vis:
                diag = aligned and kmin == qmin and (j + 1) * tk <= n_k
                row.append((j, KIND_FULL if all_vis else KIND_DIAG if diag else KIND_MASK))
        assert row and row[0][0] == 0
        for idx, (j, kind) in enumerate(row):
            qi.append(i)
            kj.append(j)
            fl.append((kind << KIND_SHIFT) | (FLAG_FIRST if idx == 0 else 0) | (FLAG_LAST if idx == len(row) - 1 else 0))
    kinds = sorted({f >> KIND_SHIFT for f in fl})
    as_arr = lambda v: jnp.asarray(np.array(v, np.int32))
    return as_arr(qi), as_arr(kj), as_arr(fl), kinds


def _visible(shape, qpos0, kpos0, n_k, causal, row_period=None):
    if row_period is None:
        rows = qpos0 + lax.broadcasted_iota(jnp.int32, shape, 0)
    else:
        one = lax.broadcasted_iota(jnp.int32, (row_period, shape[1]), 0)
        rows = qpos0 + jnp.concatenate([one] * (shape[0] // row_period), axis=0)
    cols = kpos0 + lax.broadcasted_iota(jnp.int32, shape, 1)
    if causal:
        ok = cols <= rows
    else:
        ok = (cols >> CHUNK_SHIFT) <= (rows >> CHUNK_SHIFT)
    return ok if n_k is None else ok & (cols < n_k)


def _lane_tile(x, n):
    return x if n == LANES else jnp.concatenate([x] * (n // LANES), axis=1)


def _lane_fold(p):
    acc = p[:, 0:LANES]
    for c in range(1, p.shape[1] // LANES):
        acc = acc + p[:, c * LANES:(c + 1) * LANES]
    return acc


def _attend(s, v, m_ref, l_ref, acc_ref, idx):
    m_prev = m_ref[idx]
    m_new = jnp.maximum(m_prev, jnp.max(s, axis=1, keepdims=True))
    alpha = jnp.exp2(m_prev - m_new)
    p = jnp.exp2(s - _lane_tile(m_new, s.shape[1]))
    if l_ref is not None:
        l_ref[idx] = alpha * l_ref[idx] + _lane_fold(p)
    acc_ref[idx] = _lane_tile(alpha, v.shape[1]) * acc_ref[idx] + _dot(p.astype(v.dtype), v)
    m_ref[idx] = m_new


def _flash_frame(qi_ref, kj_ref, fl_ref, m_ref, l_ref, acc_ref, scores, update, finish, first=None, *,
                 tq, tk, rs, q_off, n_k, causal, kinds):
    n = pl.program_id(2)
    flags = fl_ref[n]
    kind = flags >> KIND_SHIFT

    @pl.when((flags & FLAG_FIRST) != 0)
    def _():
        m_ref[...] = jnp.full_like(m_ref, NEG_INF)
        if l_ref is not None:
            l_ref[...] = jnp.zeros_like(l_ref)
        acc_ref[...] = jnp.zeros_like(acc_ref)
        if first is not None:
            first()

    qpos0 = q_off + qi_ref[n] * tq
    kpos0 = kj_ref[n] * tk

    def run(k):
        def masked_scores(r):
            n_keys = (r + 1) * rs if k == KIND_DIAG else tk
            tiles = scores(r, n_keys)
            if k == KIND_DIAG:
                lo = r * rs
                vis = _visible((rs, rs), qpos0 + lo, kpos0 + lo, None, causal)
                blks = [jnp.where(vis, s[:, lo:], NEG_INF) for s in tiles]
                tiles = blks if lo == 0 else [jnp.concatenate([s[:, :lo], b], axis=1) for s, b in zip(tiles, blks)]
            elif k == KIND_MASK:
                vis = _visible(tiles[0].shape, qpos0 + r * rs, kpos0, n_k, causal)
                tiles = [jnp.where(vis, s, NEG_INF) for s in tiles]
            return n_keys, tiles

        order = list(range(tq // rs))
        if k == KIND_DIAG:
            order.reverse()
        pending = [masked_scores(r) for r in order[:SCORE_LOOKAHEAD]]
        for idx, r in enumerate(order):
            n_keys, tiles = pending.pop(0)
            if idx + SCORE_LOOKAHEAD < len(order):
                pending.append(masked_scores(order[idx + SCORE_LOOKAHEAD]))
            update(r, n_keys, tiles)

    for k in kinds:
        pl.when(kind == k)(functools.partial(run, k))

    @pl.when((flags & FLAG_LAST) != 0)
    def _():
        finish()


def _diff_kernel(qi_ref, kj_ref, fl_ref, q_ref, k_ref, v_ref, lam_ref, sub_ref, o_ref,
                 m_ref, l_ref, acc_ref, *, rs, lam_init, **frame):
    hd = LANES

    def scores(r, n_keys):
        rows = slice(r * rs, (r + 1) * rs)
        return [_dot_nt(q_ref[rows, c * hd:(c + 1) * hd], k_ref[0:n_keys, c * hd:(c + 1) * hd]) for c in range(2)]

    def update(r, n_keys, tiles):
        v = v_ref[0:n_keys, :]
        for c in range(2):
            _attend(tiles[c], v, m_ref, l_ref, acc_ref, (c, slice(r * rs, (r + 1) * rs)))

    def finish():
        lp = lam_ref[...]
        lam = (jnp.exp(jnp.sum(lp[0:1] * lp[1:2], axis=-1, keepdims=True))
               - jnp.exp(jnp.sum(lp[2:3] * lp[3:4], axis=-1, keepdims=True)) + lam_init)
        l0 = jnp.sum(l_ref[0], axis=1, keepdims=True)
        l1 = jnp.sum(l_ref[1], axis=1, keepdims=True)
        o = acc_ref[0] / l0 - lam * (acc_ref[1] / l1)
        o_ref[...] = (_rms(o, sub_ref[...]) * (1.0 - lam_init)).astype(o_ref.dtype)

    _flash_frame(qi_ref, kj_ref, fl_ref, m_ref, l_ref, acc_ref, scores, update, finish, rs=rs, **frame)


def _ones_finish(acc_ref, o_ref):
    o_ref[...] = (acc_ref[:, 0:LANES] / acc_ref[:, LANES:2 * LANES]).astype(o_ref.dtype)


def _fox_kernel(qi_ref, kj_ref, fl_ref, q_ref, aq_ref, k_ref, v_ref, o_ref,
                m_ref, acc_ref, qq_ref, *, rs, **frame):
    def first():
        qq_ref[:, 0:LANES] = q_ref[...]
        qq_ref[:, LANES:2 * LANES] = aq_ref[...]

    def scores(r, n_keys):
        return [_dot_nt(qq_ref[r * rs:(r + 1) * rs, :], k_ref[0:n_keys, :])]

    def update(r, n_keys, tiles):
        _attend(tiles[0], v_ref[0:n_keys, :], m_ref, None, acc_ref, slice(r * rs, (r + 1) * rs))

    _flash_frame(qi_ref, kj_ref, fl_ref, m_ref, None, acc_ref, scores, update,
                 functools.partial(_ones_finish, acc_ref, o_ref), first, rs=rs, **frame)


def _mla_kernel(qi_ref, kj_ref, fl_ref, q_ref, k_ref, v_ref, o_ref,
                m_ref, acc_ref, *, rs, **frame):
    def scores(r, n_keys):
        return [_dot_nt(q_ref[r * rs:(r + 1) * rs, :], k_ref[0:n_keys, :])]

    def update(r, n_keys, tiles):
        _attend(tiles[0], v_ref[0:n_keys, :], m_ref, None, acc_ref, slice(r * rs, (r + 1) * rs))

    _flash_frame(qi_ref, kj_ref, fl_ref, m_ref, None, acc_ref, scores, update,
                 functools.partial(_ones_finish, acc_ref, o_ref), rs=rs, **frame)


def _flash_tiles(n_q, n_k_pad, tile):
    tq = _pick_tile(n_q, tile, 16)
    tk = next((t for t in (tile, tile // 2, tile // 4) if n_k_pad % t == 0), None) or _pick_tile(n_k_pad, tile, LANES)
    rs = _pick_tile(tq, 256, 16)
    return tq, tk, rs


def _flash_call(kernel, n_heads, operands, n_q, n_k_pad, out_width, scratch, n_k, q_off, causal, name,
                tile=2048, **kw):
    b = operands[0][0].shape[0]
    tq, tk, rs = _flash_tiles(n_q, n_k_pad, tile)
    qi, kj, fl, kinds = _pair_table(n_q, n_k, tq, tk, q_off, causal)
    if KIND_DIAG in kinds:
        assert rs % CHUNK == 0 and rs % LANES == 0

    in_specs, args = [], []
    for op in operands:
        a = op[0]
        if len(op) == 1:
            in_specs.append(pl.BlockSpec(a.shape, lambda bi, h, n, qi, kj, fl: (0, 0)))
        elif op[2]:
            in_specs.append(pl.BlockSpec((None, tq, op[1]), lambda bi, h, n, qi, kj, fl: (bi, qi[n], h)))
        else:
            in_specs.append(pl.BlockSpec((None, tk, op[1]), lambda bi, h, n, qi, kj, fl: (bi, kj[n], h)))
        args.append(a)
    vmem = 2 * tq * tk * 4 + 8 * max(tq, tk) * 2 * LANES * 2 * len(operands) + 10 * tq * 2 * LANES * 4 + (8 << 20)
    grid_spec = pltpu.PrefetchScalarGridSpec(
        num_scalar_prefetch=3,
        grid=(b, n_heads, int(qi.shape[0])),
        in_specs=in_specs,
        out_specs=pl.BlockSpec((None, tq, out_width), lambda bi, h, n, qi, kj, fl: (bi, qi[n], h)),
        scratch_shapes=scratch(tq),
    )
    return pl.pallas_call(
        functools.partial(kernel, tq=tq, tk=tk, rs=rs, q_off=q_off, n_k=n_k, causal=causal, kinds=kinds, **kw),
        out_shape=jax.ShapeDtypeStruct((b, n_q, n_heads * out_width), BF16),
        grid_spec=grid_spec,
        compiler_params=_params(("parallel", "parallel", "arbitrary"), vmem),
        name=name,
    )(qi, kj, fl, *args)


def _flash_diff(q, k, v, lam_p, subln, lam_init, n_heads, n_k, q_off):
    w = 2 * LANES
    scratch = lambda tq: [pltpu.VMEM((2, tq, LANES), F32), pltpu.VMEM((2, tq, LANES), F32), pltpu.VMEM((2, tq, w), F32)]
    ops = [(q, w, True), (k, w, False), (v, w, False), (lam_p,), (subln.reshape(1, w),)]
    return _flash_call(_diff_kernel, n_heads, ops, q.shape[1], k.shape[1], w, scratch, n_k, q_off, False,
                       "diff_attn", tile=2048, lam_init=lam_init)


def _flash_fox(q, aq, kk, vv, n_heads, n_k, q_off):
    w = LANES
    scratch = lambda tq: [pltpu.VMEM((tq, LANES), F32), pltpu.VMEM((tq, 2 * w), F32), pltpu.VMEM((tq, 2 * w), BF16)]
    ops = [(q, w, True), (aq, w, True), (kk, 2 * w, False), (vv, 2 * w, False)]
    return _flash_call(_fox_kernel, n_heads, ops, q.shape[1], kk.shape[1], w, scratch, n_k, q_off, True, "fox_attn")


def _flash_mla(qq, kk, vv, n_heads, n_k, q_off):
    w = LANES
    scratch = lambda tq: [pltpu.VMEM((tq, LANES), F32), pltpu.VMEM((tq, 2 * w), F32)]
    ops = [(qq, 2 * w, True), (kk, 2 * w, False), (vv, 2 * w, False)]
    return _flash_call(_mla_kernel, n_heads, ops, qq.shape[1], kk.shape[1], w, scratch, n_k, q_off, False, "mla_attn")


NEW_ROWS = LANES


def _cached_frame(m_ref, l_ref, acc_ref, chains, finish, *, nkb, tk, ta, q_off, n_k, causal):
    n = pl.program_id(1)

    @pl.when(n == 0)
    def _():
        m_ref[...] = jnp.full_like(m_ref, NEG_INF)
        l_ref[...] = jnp.zeros_like(l_ref)
        acc_ref[...] = jnp.zeros_like(acc_ref)

    @pl.when(n < nkb)
    def _():
        chains(True, tk, None)

    @pl.when(n == nkb)
    def _():
        mask_fn = lambda s: jnp.where(_visible(s.shape, q_off, q_off, n_k, causal, ta), s, NEG_INF)
        chains(False, NEW_ROWS, mask_fn)
        finish()


def _dec_diff_kernel(q_ref, kc_ref, vc_ref, kn_ref, vn_ref, lam_ref, sub_ref, o_ref,
                     m_ref, l_ref, acc_ref, *, n_heads, lam_init, **frame):
    hd, g, tk = LANES, 2 * n_heads, frame['tk']

    def chains(cached, n_keys, mask_fn):
        for h in range(n_heads):
            if cached:
                v = jnp.concatenate([vc_ref[pl.ds(h, tk, stride=g), :],
                                     vc_ref[pl.ds(n_heads + h, tk, stride=g), :]],
                                    axis=1).astype(BF16)
            else:
                v = vn_ref[:, 2 * h * hd:(2 * h + 2) * hd]
            for c in range(2):
                j = 2 * h + c
                k = kc_ref[pl.ds(j, tk, stride=g), :].astype(BF16) if cached else kn_ref[:, j * hd:(j + 1) * hd]
                s = _dot_nt(q_ref[:, j * hd:(j + 1) * hd], k)
                if mask_fn is not None:
                    s = mask_fn(s)
                _attend(s, v, m_ref, l_ref, acc_ref, j)

    def finish():
        lp = lam_ref[...]
        lam = (jnp.exp(jnp.sum(lp[0:1] * lp[1:2], axis=-1, keepdims=True))
               - jnp.exp(jnp.sum(lp[2:3] * lp[3:4], axis=-1, keepdims=True)) + lam_init)
        for h in range(n_heads):
            l0 = jnp.sum(l_ref[2 * h], axis=1, keepdims=True)
            l1 = jnp.sum(l_ref[2 * h + 1], axis=1, keepdims=True)
            o = acc_ref[2 * h] / l0 - lam * (acc_ref[2 * h + 1] / l1)
            o_ref[:, 2 * h * hd:(2 * h + 2) * hd] = (_rms(o, sub_ref[...]) * (1.0 - lam_init)).astype(o_ref.dtype)

    _cached_frame(m_ref, l_ref, acc_ref, chains, finish, **frame)


def _dec_fox_kernel(q_ref, kc_ref, vc_ref, kn_ref, vn_ref, fq_ref, fk_ref, o_ref,
                    m_ref, l_ref, acc_ref, *, n_heads, **frame):
    hd, g, tk = LANES, n_heads, frame['tk']

    def chains(cached, n_keys, mask_fn):
        for h in range(n_heads):
            if cached:
                k = kc_ref[pl.ds(h, tk, stride=g), :].astype(BF16)
                v = vc_ref[pl.ds(h, tk, stride=g), :].astype(BF16)
            else:
                k = kn_ref[:, h * hd:(h + 1) * hd]
                v = vn_ref[:, h * hd:(h + 1) * hd]
            bias = (fq_ref[:, h:h + 1] - fk_ref[h:h + 1, 0:n_keys]) * LOG2E
            s = _dot_nt(q_ref[:, h * hd:(h + 1) * hd], k) + bias
            if mask_fn is not None:
                s = mask_fn(s)
            _attend(s, v, m_ref, l_ref, acc_ref, h)

    def finish():
        for h in range(n_heads):
            o = acc_ref[h] / jnp.sum(l_ref[h], axis=1, keepdims=True)
            o_ref[:, h * hd:(h + 1) * hd] = o.astype(o_ref.dtype)

    _cached_frame(m_ref, l_ref, acc_ref, chains, finish, **frame)


def _cached_attn(kernel, q, cache_k, cache_v, new_k, new_v, extra, extra_specs, groups, v_width, n_k, causal,
                 name, **kw):
    b, ta, _ = q.shape
    past = cache_k.shape[1]
    tk = _pick_tile(past, 1024, LANES)
    nkb = past // tk
    kc = cache_k.reshape(b, past * groups, LANES)
    if cache_v.shape[-1] == 2 * LANES:
        vc = cache_v.reshape(b, past, groups // 2, 2, LANES).swapaxes(2, 3).reshape(b, past * groups, LANES)
    else:
        vc = cache_v.reshape(b, past * groups, LANES)
    kn, vn = _pad_rows(new_k, NEW_ROWS), _pad_rows(new_v, NEW_ROWS)
    whole = lambda a: pl.BlockSpec((None,) + a.shape[1:], lambda bi, n: (bi,) + (0,) * (a.ndim - 1))
    cache = pl.BlockSpec((None, tk * groups, LANES), lambda bi, n: (bi, jnp.minimum(n, nkb - 1), 0))
    out_w = q.shape[2]
    return pl.pallas_call(
        functools.partial(kernel, nkb=nkb, tk=tk, ta=ta, q_off=past, n_k=n_k, causal=causal, **kw),
        out_shape=jax.ShapeDtypeStruct((b, ta, out_w), BF16),
        grid=(b, nkb + 1),
        in_specs=[whole(q), cache, cache, whole(kn), whole(vn)] + extra_specs(tk),
        out_specs=pl.BlockSpec((None, ta, out_w), lambda bi, n: (bi, 0, 0)),
        scratch_shapes=[pltpu.VMEM((groups, ta, LANES), F32), pltpu.VMEM((groups, ta, LANES), F32),
                        pltpu.VMEM((groups, ta, v_width), F32)],
        compiler_params=_params(("parallel", "arbitrary"), 4 * tk * groups * LANES * 4 + (16 << 20)),
        name=name,
    )(q, kc, vc, kn, vn, *extra)


def _dec_mla_kernel(q_ref, ckv_ref, kp_ref, ckvn_ref, kpn_ref, wk_ref, wv_ref, gk_ref, o_ref,
                    m_ref, l_ref, acc_ref, s_ref, qp_ref, *, n_heads, **frame):
    hd, ta = LANES, frame['ta']

    @pl.when(pl.program_id(1) == 0)
    def _():
        for h in range(n_heads):
            qp_ref[h * ta:(h + 1) * ta, :] = q_ref[:, (2 * h + 1) * hd:(2 * h + 2) * hd]

    def chains(cached, n_keys, mask_fn):
        ckv = ckv_ref[...].astype(BF16) if cached else ckvn_ref[...]
        kp = kp_ref[...] if cached else kpn_ref[...]
        kvn = _dot(ckv, wk_ref[...])
        for h in range(n_heads):
            kn = _rms(kvn[:, h * hd:(h + 1) * hd], gk_ref[...]).astype(BF16)
            s_ref[h * ta:(h + 1) * ta, 0:n_keys] = _dot_nt(q_ref[:, 2 * h * hd:(2 * h + 1) * hd], kn)
        s = s_ref[:, 0:n_keys] + _dot_nt(qp_ref[...], kp)
        if mask_fn is not None:
            s = mask_fn(s)
        _attend(s, ckv, m_ref, l_ref, acc_ref, slice(None))

    def finish():
        lat = acc_ref[...] / jnp.sum(l_ref[...], axis=1, keepdims=True)
        for h in range(n_heads):
            o = _dot(lat[h * ta:(h + 1) * ta, :].astype(BF16), wv_ref[:, h * hd:(h + 1) * hd])
            o_ref[:, h * hd:(h + 1) * hd] = o.astype(o_ref.dtype)

    _cached_frame(m_ref, l_ref, acc_ref, chains, finish, **frame)


def _dec_mla(qq, cache_ckv, kp_cache, new_ckv, new_kp, w_kn, w_v, g_nope_k, n_heads, n_k):
    b, ta, _ = qq.shape
    past, c = cache_ckv.shape[1:]
    tk = _pick_tile(past, 512, LANES)
    nkb = past // tk
    rows = n_heads * ta
    ckvn, kpn = _pad_rows(new_ckv, NEW_ROWS), _pad_rows(new_kp, NEW_ROWS)
    whole = lambda a: pl.BlockSpec((None,) + a.shape[1:], lambda bi, n: (bi,) + (0,) * (a.ndim - 1))
    const = lambda a: pl.BlockSpec(a.shape, lambda bi, n: (0,) * a.ndim)
    blk = lambda w: pl.BlockSpec((None, tk, w), lambda bi, n: (bi, jnp.minimum(n, nkb - 1), 0))

    kernel = functools.partial(_dec_mla_kernel, n_heads=n_heads, nkb=nkb, tk=tk, ta=ta, q_off=past, n_k=n_k,
                               causal=False)
    gk = g_nope_k.reshape(1, LANES)
    return pl.pallas_call(
        kernel,
        out_shape=jax.ShapeDtypeStruct((b, ta, n_heads * LANES), BF16),
        grid=(b, nkb + 1),
        in_specs=[whole(qq), blk(c), blk(LANES), whole(ckvn), whole(kpn), const(w_kn), const(w_v), const(gk)],
        out_specs=pl.BlockSpec((None, ta, n_heads * LANES), lambda bi, n: (bi, 0, 0)),
        scratch_shapes=[pltpu.VMEM((rows, LANES), F32), pltpu.VMEM((rows, LANES), F32), pltpu.VMEM((rows, c), F32),
                        pltpu.VMEM((rows, tk), F32), pltpu.VMEM((rows, LANES), BF16)],
        compiler_params=_params(("parallel", "arbitrary"), 40 << 20),
        name="mla_attn_cached",
    )(qq, cache_ckv, kp_cache, ckvn, kpn, w_kn, w_v, gk)


def _cumsum_rows_kernel(x_ref, o_ref, carry_ref):
    @pl.when(pl.program_id(1) == 0)
    def _():
        carry_ref[...] = jnp.zeros_like(carry_ref)

    x = x_ref[...]
    tb = x.shape[1]
    r = lax.broadcasted_iota(jnp.int32, (tb, tb), 0)
    c = lax.broadcasted_iota(jnp.int32, (tb, tb), 1)
    tri = jnp.where(r <= c, 1.0, 0.0).astype(BF16)
    hi, mid, lo = _split3(x)
    cum = _dot(hi, tri) + _dot(mid, tri) + _dot(lo, tri) + carry_ref[:, 0:1]
    o_ref[...] = cum
    carry_ref[...] = jnp.broadcast_to(cum[:, tb - 1:tb], carry_ref.shape)


def _cumsum_rows(x, tb):
    b, g, t = x.shape
    return pl.pallas_call(
        _cumsum_rows_kernel,
        out_shape=jax.ShapeDtypeStruct((b, g, t), F32),
        grid=(b, t // tb),
        in_specs=[pl.BlockSpec((None, g, tb), lambda bi, i: (bi, 0, i))],
        out_specs=pl.BlockSpec((None, g, tb), lambda bi, i: (bi, 0, i)),
        scratch_shapes=[pltpu.VMEM((g, LANES), F32)],
        compiler_params=_params(("parallel", "arbitrary"), 32 << 20),
        name="forget_cumsum",
    )(x)


def _dec_diff(q, cache_k, cache_v, new_k, new_v, lam_p, subln, lam_init, n_heads, n_k):
    b, past = cache_k.shape[:2]
    specs = lambda tk: [pl.BlockSpec(lam_p.shape, lambda bi, n: (0, 0)),
                        pl.BlockSpec((1, 2 * LANES), lambda bi, n: (0, 0))]
    return _cached_attn(_dec_diff_kernel, q, cache_k, cache_v, new_k, new_v, [lam_p, subln.reshape(1, 2 * LANES)],
                        specs, 2 * n_heads, 2 * LANES, n_k, False, "diff_attn_cached",
                        n_heads=n_heads, lam_init=lam_init)


def _dec_fox(q, cache_k, cache_v, new_k, new_v, past_logf, new_logf, n_heads, n_k):
    b, past = cache_k.shape[:2]
    ta = q.shape[1]
    tk = _pick_tile(past, 1024, LANES)
    lf = jnp.concatenate([past_logf.astype(F32), new_logf[:, :, :n_heads]], axis=1)
    lf_rows = _pad_lanes(jnp.swapaxes(lf, 1, 2), past + tk)
    f_rows = _cumsum_rows(lf_rows, tk)
    f_q = _pad_lanes(jnp.swapaxes(f_rows[:, :, past:past + ta], 1, 2), LANES)
    specs = lambda tk: [pl.BlockSpec((None, ta, LANES), lambda bi, n: (bi, 0, 0)),
                        pl.BlockSpec((None, n_heads, tk), lambda bi, n: (bi, 0, n))]
    return _cached_attn(_dec_fox_kernel, q, cache_k, cache_v, new_k, new_v, [f_q, f_rows], specs,
                        n_heads, LANES, n_k, True, "fox_attn_cached", n_heads=n_heads)


def _rope_half(y, cos, sin, rope_dim):
    half = rope_dim // 2
    lane = lax.broadcasted_iota(jnp.int32, y.shape, 1)
    rot = jnp.where(lane < half, pltpu.roll(y, LANES - half, 1), pltpu.roll(y, half, 1))
    return y * cos + rot * sin


def _rms_low(x, gain, n):
    return x * lax.rsqrt(jnp.sum(x * x, axis=-1, keepdims=True) * (1.0 / n) + EPS) * gain


def _odd_post_kernel(z_ref, cos_ref, sin_ref, gq_ref, gkv_ref, gr_ref,
                     cq_ref, ckvf_ref, ckvb_ref, kpf_ref, kpb_ref, *, q_lora, kv_lora, rope_dim):
    cq_ref[...] = _rms(z_ref[:, 0:q_lora], gq_ref[...]).astype(BF16)
    ckv = _rms(z_ref[:, q_lora:q_lora + kv_lora], gkv_ref[...])
    ckvf_ref[...] = ckv
    ckvb_ref[...] = ckv.astype(BF16)
    kp = _rms_low(z_ref[:, q_lora + kv_lora:q_lora + kv_lora + LANES], gr_ref[...], rope_dim)
    kp = _rope_half(kp, cos_ref[...], sin_ref[...], rope_dim)
    kpf_ref[...] = kp
    kpb_ref[...] = kp.astype(BF16)


def _odd_post(z, cos, sin, g_cq, g_ckv, g_rope_k_pad, q_lora, kv_lora, rope_dim):
    b, t, n = z.shape
    tm = _pick_tile(t, 512, 16)
    row = lambda w: pl.BlockSpec((None, tm, w), lambda bi, i: (bi, i, 0))
    tab = pl.BlockSpec((tm, LANES), lambda bi, i: (i, 0))
    vec = lambda w: pl.BlockSpec((1, w), lambda bi, i: (0, 0))
    outs = [(q_lora, BF16), (kv_lora, F32), (kv_lora, BF16), (LANES, F32), (LANES, BF16)]
    return pl.pallas_call(
        functools.partial(_odd_post_kernel, q_lora=q_lora, kv_lora=kv_lora, rope_dim=rope_dim),
        out_shape=[jax.ShapeDtypeStruct((b, t, w), dt) for w, dt in outs],
        grid=(b, t // tm),
        in_specs=[row(n), tab, tab, vec(q_lora), vec(kv_lora), vec(LANES)],
        out_specs=[row(w) for w, _ in outs],
        compiler_params=_params(("parallel", "parallel"), 32 << 20),
        name="odd_post",
    )(z, cos, sin, g_cq.reshape(1, -1), g_ckv.reshape(1, -1), g_rope_k_pad)


def _qup_kernel(cq_ref, w_ref, cos_ref, sin_ref, gn_ref, gr_ref, qq_ref, *, heads, rope_dim, q_scale):
    q = _dot(cq_ref[...], w_ref[...])
    cos, sin = cos_ref[...], sin_ref[...]
    for h in range(heads):
        qn = _rms(q[:, 2 * h * LANES:(2 * h + 1) * LANES], gn_ref[...])
        qq_ref[:, 2 * h * LANES:(2 * h + 1) * LANES] = (qn * q_scale).astype(BF16)
        qp = _rms_low(q[:, (2 * h + 1) * LANES:(2 * h + 2) * LANES], gr_ref[...], rope_dim)
        qp = _rope_half(qp, cos, sin, rope_dim)
        qq_ref[:, (2 * h + 1) * LANES:(2 * h + 2) * LANES] = (qp * q_scale).astype(BF16)


def _qup(cq, w_pad, cos, sin, g_nope_q, g_rope_q_pad, n_heads, rope_dim, qk_dim):
    b, t, kq = cq.shape
    tm = _pick_tile(t, 512, 16)
    hg = 4 if n_heads % 4 == 0 else 1
    tn = hg * 2 * LANES
    return pl.pallas_call(
        functools.partial(_qup_kernel, heads=hg, rope_dim=rope_dim, q_scale=qk_dim ** -0.5 * LOG2E),
        out_shape=jax.ShapeDtypeStruct((b, t, n_heads * 2 * LANES), BF16),
        grid=(b, t // tm, n_heads // hg),
        in_specs=[
            pl.BlockSpec((None, tm, kq), lambda bi, i, j: (bi, i, 0)),
            pl.BlockSpec((kq, tn), lambda bi, i, j: (0, j)),
            pl.BlockSpec((tm, LANES), lambda bi, i, j: (i, 0)),
            pl.BlockSpec((tm, LANES), lambda bi, i, j: (i, 0)),
            pl.BlockSpec((1, LANES), lambda bi, i, j: (0, 0)),
            pl.BlockSpec((1, LANES), lambda bi, i, j: (0, 0)),
        ],
        out_specs=pl.BlockSpec((None, tm, tn), lambda bi, i, j: (bi, i, j)),
        compiler_params=_params(("parallel", "parallel", "arbitrary"), 32 << 20),
        name="mla_q_up",
    )(cq, w_pad, cos, sin, g_nope_q.reshape(1, LANES), g_rope_q_pad)


def _kvup_kernel(ckv_ref, kp_ref, w_ref, gn_ref, kk_ref, v_ref, *, heads):
    kv = _dot(ckv_ref[...], w_ref[...])
    kp = kp_ref[...]
    for h in range(heads):
        kn = _rms(kv[:, 2 * h * LANES:(2 * h + 1) * LANES], gn_ref[...])
        kk_ref[:, 2 * h * LANES:(2 * h + 1) * LANES] = kn.astype(BF16)
        kk_ref[:, (2 * h + 1) * LANES:(2 * h + 2) * LANES] = kp
        v_ref[:, 2 * h * LANES:(2 * h + 1) * LANES] = kv[:, (2 * h + 1) * LANES:(2 * h + 2) * LANES].astype(BF16)
        v_ref[:, (2 * h + 1) * LANES:(2 * h + 2) * LANES] = jnp.ones((kv.shape[0], LANES), BF16)


def _kvup(ckv, kp, w, g_nope_k, n_heads):
    b, t, kk = ckv.shape
    tm = _pick_tile(t, 512, LANES)
    hg = 4 if n_heads % 4 == 0 else 1
    tn = hg * 2 * LANES
    return pl.pallas_call(
        functools.partial(_kvup_kernel, heads=hg),
        out_shape=[jax.ShapeDtypeStruct((b, t, n_heads * 2 * LANES), BF16)] * 2,
        grid=(b, t // tm, n_heads // hg),
        in_specs=[
            pl.BlockSpec((None, tm, kk), lambda bi, i, j: (bi, i, 0)),
            pl.BlockSpec((None, tm, LANES), lambda bi, i, j: (bi, i, 0)),
            pl.BlockSpec((kk, tn), lambda bi, i, j: (0, j)),
            pl.BlockSpec((1, LANES), lambda bi, i, j: (0, 0)),
        ],
        out_specs=[pl.BlockSpec((None, tm, tn), lambda bi, i, j: (bi, i, j))] * 2,
        compiler_params=_params(("parallel", "parallel", "arbitrary"), 32 << 20),
        name="mla_kv_up",
    )(ckv, kp, w, g_nope_k.reshape(1, LANES))


def _rope_tables(pos, dim):
    half = dim // 2
    inv = ROPE_THETA ** (-jnp.arange(half, dtype=F32) * 2.0 / dim)
    ang = pos.astype(F32)[:, None] * inv[None, :]
    cos, sin = jnp.cos(ang), jnp.sin(ang)
    pad = ((0, 0), (0, LANES - dim))
    return (jnp.pad(jnp.concatenate([cos, cos], axis=-1), pad),
            jnp.pad(jnp.concatenate([-sin, sin], axis=-1), pad))


def _pad_lanes(a, width):
    return jnp.pad(a, [(0, 0)] * (a.ndim - 1) + [(0, width - a.shape[-1])])


def _pad_rows(a, rows):
    return jnp.pad(a, [(0, 0), (0, rows - a.shape[1])] + [(0, 0)] * (a.ndim - 2))


def _layer_stack(x, c_mod, tok_pos, seq_shape, caches, p):
    bx, tx, d = x.shape
    ba, ta = seq_shape
    depth = p['w_ffn_in'].shape[0]
    n_diff = p['n_diff']
    n_fox = p['n_fox']
    n_mla = p['n_mla']
    past_len = 0 if caches is None else caches[0].shape[2]
    n_k = past_len + ta
    cos128, sin128 = _rope_tables(tok_pos, LANES)
    rope_dim = p['rope_dim']
    cos_r, sin_r = _rope_tables(tok_pos, rope_dim)
    new = [[] for _ in range(7)]

    def mods(l, s):
        m = c_mod[l]
        sh, sc, gt = m[:, 3 * s], m[:, 3 * s + 1], m[:, 3 * s + 2]
        if bx == m.shape[0]:
            return tuple(a[:, None, :] for a in (sh, sc, gt))
        rep = lambda a: jnp.repeat(a, ta, axis=0).reshape(bx, tx, d)
        return rep(sh), rep(sc), rep(gt)

    seq = lambda a: a.reshape(ba, ta, a.shape[-1])
    for l in range(depth):
        i = l // 2
        g = p['norm_gains'][l]
        sh, sc, gt = mods(l, 0)
        x = _ffn(x, sh, sc, gt, g[0], p['w_ffn_in'], p['w_ffn_out'], l, 0, 0.5)
        sh, sc, gt = mods(l, 1)
        if l % 2 == 0:
            (qa, kaf, kab, vaf, vab, qb, kbf, kbb, vbf, vbb, lf) = _even_proj(
                x, sh, sc, g[1], p['w_in_even'][i], p['w_in_forget'][i], cos128, sin128,
                p['qk_norm_even'][i], p['b_forget_pad'][i], 2 * n_diff, n_fox)
            new[0].append(kaf.reshape(ba, ta, n_diff, 2, LANES))
            new[1].append(vaf.reshape(ba, ta, 2, n_diff, LANES).swapaxes(2, 3).reshape(ba, ta, n_diff, 2 * LANES))
            new[2].append(kbf.reshape(ba, ta, n_fox, LANES))
            new[3].append(vbf.reshape(ba, ta, n_fox, LANES))
            new[4].append(seq(lf)[:, :, :n_fox])
            lam_init = 0.8 - 0.6 * math.exp(-0.3 * l)
            if caches is None:
                aq, kk, vv = _fox_prep(lf, kbb, vbb, n_fox)
                oa = _flash_diff(qa, kab, vab, p['diff_lambda'][i], p['diff_subln'][i], lam_init, n_diff, n_k, 0)
                ob = _flash_fox(qb, aq, kk, vv, n_fox, n_k, 0)
            else:
                past = tuple(a[i] for a in caches[:5])
                oa = _dec_diff(seq(qa), past[0], past[1], seq(kab), seq(vab), p['diff_lambda'][i],
                               p['diff_subln'][i], lam_init, n_diff, n_k)
                ob = _dec_fox(seq(qb), past[2], past[3], seq(kbb), seq(vbb), past[4], seq(lf), n_fox, n_k)
            x = _outproj(oa.reshape(bx, tx, -1), ob.reshape(bx, tx, -1), 0, 0, p['w_out_even'][i], x, gt)
        else:
            q_lora, kv_lora = p['q_lora'], p['kv_lora']
            z = _modproj(x, sh, sc, g[1], p['w_in_odd'][i])
            cq, ckvf, ckvb, kpf, kpb = _odd_post(z, cos_r, sin_r, p['mla_cq_norm'][i], p['mla_ckv_norm'][i],
                                                 p['g_rope_pad'][i, 1:2], q_lora, kv_lora, rope_dim)
            new[5].append(seq(ckvf))
            new[6].append(seq(kpf)[:, :, :rope_dim])
            qq = _qup(cq, p['w_uq_pad'][i], cos_r, sin_r, p['mla_qk_norm_nope'][i, 0],
                      p['g_rope_pad'][i, 0:1], n_mla, rope_dim, p['mla_qk_dim'])
            if caches is None:
                kk, v = _kvup(ckvb, kpb, p['w_ukv'][i], p['mla_qk_norm_nope'][i, 1], n_mla)
                o = _flash_mla(qq, kk, v, n_mla, n_k, 0)
            else:
                kp_cache = _pad_lanes(caches[6][i], LANES).astype(BF16)
                o = _dec_mla(seq(qq), caches[5][i], kp_cache, seq(ckvb), seq(kpb), p['w_kn'][i], p['w_v'][i],
                             p['mla_qk_norm_nope'][i, 1], n_mla, n_k)
            o = o.reshape(bx, tx, -1)
            x = _outproj(o, o, 0, 1, p['w_out_odd'][i], x, gt)
        sh, sc, gt = mods(l, 2)
        x = _ffn(x, sh, sc, gt, g[2], p['w_ffn_in'], p['w_ffn_out'], l, 1, 0.5, final_gain=g[3])
    return x, tuple(jnp.stack(lst) for lst in new)


def kernel(x_prompt, x_sample, c_prompt, c_sample, cache_diff_k, cache_diff_v, cache_fox_k, cache_fox_v, cache_fox_logf, cache_mla_ckv, cache_mla_kpe, w_ada, b_ada, norm_gains, w_ffn_in, w_ffn_out, w_in_even, b_forget, qk_norm_even, diff_lambda, diff_subln, w_out_even, w_in_odd, mla_cq_norm, mla_ckv_norm, w_uq, w_ukv, mla_qk_norm_nope, mla_qk_norm_rope, w_out_odd):
    d = x_prompt.shape[-1]
    n_diff, n_fox = cache_diff_k.shape[3], cache_fox_k.shape[3]
    assert cache_diff_k.shape[-1] == LANES and cache_fox_k.shape[-1] == LANES
    q_lora, kv_lora = mla_cq_norm.shape[-1], mla_ckv_norm.shape[-1]
    rope_dim, nope = cache_mla_kpe.shape[-1], mla_qk_norm_nope.shape[-1]
    n_mla = w_uq.shape[-1] // (nope + rope_dim)
    assert nope == LANES and rope_dim <= LANES and w_ukv.shape[-1] == n_mla * 2 * LANES
    n_odd = w_uq.shape[0]
    n_main = w_in_even.shape[-1] - n_fox
    assert n_main == 6 * 2 * n_diff * LANES and n_fox <= LANES

    w_uq_pad = _pad_lanes(w_uq.reshape(n_odd, q_lora, n_mla, nope + rope_dim), 2 * LANES)
    p = {
        'n_diff': n_diff, 'n_fox': n_fox, 'n_mla': n_mla, 'rope_dim': rope_dim,
        'q_lora': q_lora, 'kv_lora': kv_lora, 'mla_qk_dim': nope + rope_dim,
        'norm_gains': norm_gains,
        'w_ffn_in': w_ffn_in.astype(BF16), 'w_ffn_out': w_ffn_out.astype(BF16),
        'w_in_even': w_in_even[:, :, :n_main].astype(BF16),
        'w_in_forget': _pad_lanes(w_in_even[:, :, n_main:], LANES).astype(BF16),
        'b_forget_pad': _pad_lanes(b_forget, LANES)[:, None, :],
        'qk_norm_even': qk_norm_even, 'diff_lambda': diff_lambda, 'diff_subln': diff_subln,
        'w_out_even': w_out_even.astype(BF16),
        'w_in_odd': _pad_lanes(w_in_odd, q_lora + kv_lora + LANES).astype(BF16),
        'mla_cq_norm': mla_cq_norm, 'mla_ckv_norm': mla_ckv_norm,
        'w_uq_pad': w_uq_pad.reshape(n_odd, q_lora, n_mla * 2 * LANES).astype(BF16),
        'w_ukv': w_ukv.astype(BF16),
        'w_kn': w_ukv.reshape(n_odd, kv_lora, n_mla, 2 * LANES)[..., :LANES].reshape(n_odd, kv_lora, -1).astype(BF16),
        'w_v': w_ukv.reshape(n_odd, kv_lora, n_mla, 2 * LANES)[..., LANES:].reshape(n_odd, kv_lora, -1).astype(BF16),
        'mla_qk_norm_nope': mla_qk_norm_nope,
        'g_rope_pad': _pad_lanes(mla_qk_norm_rope, LANES),
        'w_out_odd': w_out_odd.astype(BF16),
    }

    bp, tp = x_prompt.shape[:2]
    bs, ts = x_sample.shape[:2]
    past_len = cache_diff_k.shape[2]
    mod = _ada(jnp.concatenate([c_prompt, c_sample], axis=0), w_ada, b_ada)
    mod = mod.reshape(mod.shape[0], bp + bs, N_MOD, d)

    pos_p = jnp.arange(tp, dtype=jnp.int32)
    y_prompt, st_p = _layer_stack(x_prompt, mod[:, :bp], pos_p, (bp, tp), None, p)

    pos_s = jnp.tile(past_len + jnp.arange(ts, dtype=jnp.int32), bs)
    caches = (cache_diff_k, cache_diff_v, cache_fox_k, cache_fox_v, cache_fox_logf, cache_mla_ckv, cache_mla_kpe)
    y_sample, st_s = _layer_stack(x_sample.reshape(1, bs * ts, d), mod[:, bp:], pos_s, (bs, ts), caches, p)
    return (y_prompt, y_sample.reshape(bs, ts, d)) + st_p + st_s
```

```python
import functools
import math

import numpy as np
import jax
import jax.numpy as jnp
from jax import lax
from jax.experimental import pallas as pl
from jax.experimental.pallas import tpu as pltpu

F32 = jnp.float32
BF16 = jnp.bfloat16

CHUNK = 64
ROPE_THETA = 10000.0
EPS = 1e-6
NEG_INF = -1e30
N_MOD = 9

LANES = 128
SUBLANES = 8
VMEM_CAP_BYTES = 56 * 1024 * 1024

LOG2E = math.log2(math.e)
CHUNK_SHIFT = CHUNK.bit_length() - 1
assert (1 << CHUNK_SHIFT) == CHUNK


def _round_up(n, m):
    return (n + m - 1) // m * m


def _pick_tile(n, target, quantum):
    if n <= target:
        return n
    best = None
    t = quantum
    while t <= target:
        if n % t == 0:
            best = t
        t += quantum
    assert best is not None, (n, target, quantum)
    return best


def _params(semantics, vmem_bytes):
    limit = int(min(max(vmem_bytes, 16 * 1024 * 1024), VMEM_CAP_BYTES))
    return pltpu.CompilerParams(dimension_semantics=semantics, vmem_limit_bytes=limit)


def _rms(x, gain):
    return x * lax.rsqrt(jnp.mean(x * x, axis=-1, keepdims=True) + EPS) * gain


def _silu(g):
    return g / (1.0 + jnp.exp(-g))


def _dot(a, b):
    return jnp.dot(a, b, preferred_element_type=F32)


def _dot_nt(a, b):
    return lax.dot_general(a, b, (((1,), (1,)), ((), ())), preferred_element_type=F32)


def _ada_kernel(c_ref, w_ref, b_ref, o_ref):
    a = _silu(c_ref[...]).astype(BF16)
    o_ref[...] = _dot(a, w_ref[...].astype(BF16)) + b_ref[...]


def _ada(c_all, w_ada, b_ada):
    depth, d, n = w_ada.shape
    r = c_all.shape[0]
    tn = _pick_tile(n, 1024, LANES)
    return pl.pallas_call(
        _ada_kernel,
        out_shape=jax.ShapeDtypeStruct((depth, r, n), F32),
        grid=(depth, n // tn),
        in_specs=[
            pl.BlockSpec((r, d), lambda l, j: (0, 0)),
            pl.BlockSpec((None, d, tn), lambda l, j: (l, 0, j)),
            pl.BlockSpec((None, 1, tn), lambda l, j: (l, 0, j)),
        ],
        out_specs=pl.BlockSpec((None, r, tn), lambda l, j: (l, 0, j)),
        compiler_params=_params(("arbitrary", "arbitrary"), 2 * d * tn * 4 + 3 * d * tn * 2 + (4 << 20)),
        name="ada_mod",
    )(c_all, w_ada, b_ada.reshape(depth, 1, n))


def _mod_spec(mod, tm):
    d = mod.shape[-1]
    if mod.shape[1] == 1:
        return pl.BlockSpec((None, 1, d), lambda b, i, j: (b, 0, 0))
    return pl.BlockSpec((None, tm, d), lambda b, i, j: (b, i, 0))


def _ffn_kernel(x_ref, sh_ref, sc_ref, gt_ref, g_ref, wg_ref, wu_ref, wo_ref, *rest,
                gate_mul, final_norm):
    if final_norm:
        gf_ref, o_ref, h_ref, acc_ref = rest
    else:
        o_ref, h_ref, acc_ref = rest
    f = pl.program_id(2)

    @pl.when(f == 0)
    def _():
        h = _rms(x_ref[...], g_ref[...]) * (1.0 + sc_ref[...]) + sh_ref[...]
        h_ref[...] = h.astype(BF16)
        acc_ref[...] = jnp.zeros_like(acc_ref)

    h = h_ref[...]
    g = _dot(h, wg_ref[...])
    u = _dot(h, wu_ref[...])
    a = (_silu(g) * u).astype(BF16)
    acc_ref[...] += _dot(a, wo_ref[...])

    @pl.when(f == pl.num_programs(2) - 1)
    def _():
        xn = x_ref[...] + (gate_mul * gt_ref[...]) * acc_ref[...]
        if final_norm:
            xn = _rms(xn, gf_ref[...])
        o_ref[...] = xn


def _ffn(x, sh, sc, gt, gain, w_in, w_out, layer, sub, gate_mul, final_gain=None):
    b, t, d = x.shape
    ff = w_out.shape[2]
    tm = _pick_tile(t, 512, 16)
    tf = _pick_tile(ff, 512, LANES)
    nf = ff // tf
    in_specs = [
        pl.BlockSpec((None, tm, d), lambda bi, i, f: (bi, i, 0)),
        _mod_spec(sh, tm), _mod_spec(sc, tm), _mod_spec(gt, tm),
        pl.BlockSpec((1, d), lambda bi, i, f: (0, 0)),
        pl.BlockSpec((None, None, d, tf), lambda bi, i, f: (layer, sub, 0, f)),
        pl.BlockSpec((None, None, d, tf), lambda bi, i, f: (layer, sub, 0, nf + f)),
        pl.BlockSpec((None, None, tf, d), lambda bi, i, f: (layer, sub, f, 0)),
    ]
    args = [x, sh, sc, gt, gain.reshape(1, d), w_in, w_in, w_out]
    if final_gain is not None:
        in_specs.append(pl.BlockSpec((1, d), lambda bi, i, f: (0, 0)))
        args.append(final_gain.reshape(1, d))
    vmem = (4 * tm * d * 4 + tm * d * 2 + tm * d * 4 + 6 * d * tf * 2 + 4 * tm * tf * 4
            + 6 * tm * d * 4 * (sh.shape[1] != 1) + (4 << 20))
    return pl.pallas_call(
        functools.partial(_ffn_kernel, gate_mul=gate_mul, final_norm=final_gain is not None),
        out_shape=jax.ShapeDtypeStruct((b, t, d), F32),
        grid=(b, t // tm, nf),
        in_specs=in_specs,
        out_specs=pl.BlockSpec((None, tm, d), lambda bi, i, f: (bi, i, 0)),
        scratch_shapes=[pltpu.VMEM((tm, d), BF16), pltpu.VMEM((tm, d), F32)],
        compiler_params=_params(("parallel", "parallel", "arbitrary"), vmem),
        name="ffn",
    )(*args)


def _modproj_kernel(x_ref, sh_ref, sc_ref, g_ref, w_ref, o_ref, h_ref):
    @pl.when(pl.program_id(2) == 0)
    def _():
        h = _rms(x_ref[...], g_ref[...]) * (1.0 + sc_ref[...]) + sh_ref[...]
        h_ref[...] = h.astype(BF16)

    o_ref[...] = _dot(h_ref[...], w_ref[...])


def _modproj(x, sh, sc, gain, w):
    b, t, d = x.shape
    n = w.shape[1]
    tm = _pick_tile(t, 1024, 16)
    tn = _pick_tile(n, 1280, LANES)
    vmem = (2 * tm * d * 4 + tm * d * 2 + 2 * d * tn * 2 + 3 * tm * tn * 4
            + 4 * tm * d * 4 * (sh.shape[1] != 1) + 3 * tm * d * 4 + (4 << 20))
    return pl.pallas_call(
        _modproj_kernel,
        out_shape=jax.ShapeDtypeStruct((b, t, n), F32),
        grid=(b, t // tm, n // tn),
        in_specs=[
            pl.BlockSpec((None, tm, d), lambda bi, i, j: (bi, i, 0)),
            _mod_spec(sh, tm), _mod_spec(sc, tm),
            pl.BlockSpec((1, d), lambda bi, i, j: (0, 0)),
            pl.BlockSpec((d, tn), lambda bi, i, j: (0, j)),
        ],
        out_specs=pl.BlockSpec((None, tm, tn), lambda bi, i, j: (bi, i, j)),
        scratch_shapes=[pltpu.VMEM((tm, d), BF16)],
        compiler_params=_params(("parallel", "parallel", "arbitrary"), vmem),
        name="mod_proj",
    )(x, sh, sc, gain.reshape(1, d), w)


def _outproj_kernel(a1_ref, a2_ref, w1_ref, w2_ref, x_ref, gt_ref, o_ref):
    y = _dot(a1_ref[...], w1_ref[...]) + _dot(a2_ref[...], w2_ref[...])
    o_ref[...] = x_ref[...] + gt_ref[...] * y


def _outproj(a1, a2, blk1, blk2, w, x, gt):
    b, t, d = x.shape
    kh = w.shape[0] // 2
    tm = _pick_tile(t, 512, 16)
    vmem = 4 * tm * kh * 2 + 4 * kh * d * 2 + 5 * tm * d * 4 + 2 * tm * d * 4 * (gt.shape[1] != 1) + (4 << 20)
    return pl.pallas_call(
        _outproj_kernel,
        out_shape=jax.ShapeDtypeStruct((b, t, d), F32),
        grid=(b, t // tm, 1),
        in_specs=[
            pl.BlockSpec((None, tm, kh), lambda bi, i, j: (bi, i, blk1)),
            pl.BlockSpec((None, tm, kh), lambda bi, i, j: (bi, i, blk2)),
            pl.BlockSpec((kh, d), lambda bi, i, j: (0, 0)),
            pl.BlockSpec((kh, d), lambda bi, i, j: (1, 0)),
            pl.BlockSpec((None, tm, d), lambda bi, i, j: (bi, i, 0)),
            _mod_spec(gt, tm),
        ],
        out_specs=pl.BlockSpec((None, tm, d), lambda bi, i, j: (bi, i, 0)),
        compiler_params=_params(("parallel", "parallel", "arbitrary"), vmem),
        name="out_proj",
    )(a1, a2, w, w, x, gt)


def _rope128(y, cos, sin):
    return y * cos + pltpu.roll(y, LANES // 2, 1) * sin


def _log_sigmoid(x):
    return jnp.minimum(x, 0.0) - jnp.log(1.0 + jnp.exp(-jnp.abs(x)))


def _even_proj_kernel(x_ref, sh_ref, sc_ref, g_ref, w_ref, wf_ref, cos_ref, sin_ref, gn_ref, bf_ref,
                      qa_ref, kaf_ref, kab_ref, vaf_ref, vab_ref,
                      qb_ref, kbf_ref, kbb_ref, vbf_ref, vbb_ref, lf_ref, h_ref,
                      *, n_grp, n_forget, q_scale):
    j = pl.program_id(2)
    tm, hd = x_ref.shape[0], LANES

    @pl.when(j == 0)
    def _():
        h = (_rms(x_ref[...], g_ref[...]) * (1.0 + sc_ref[...]) + sh_ref[...]).astype(BF16)
        h_ref[...] = h
        fg = _dot(h, wf_ref[...]) + bf_ref[...]
        lane = lax.broadcasted_iota(jnp.int32, fg.shape, 1)
        lf_ref[...] = jnp.where(lane < n_forget, _log_sigmoid(fg), 0.0)

    def heads():
        h = h_ref[...]
        for c in range(n_grp // 2):
            z = _dot(h, w_ref[:, 2 * c * hd:(2 * c + 2) * hd])
            yield z[:, 0:hd]
            yield z[:, hd:2 * hd]

    @pl.when(j == 0)
    def _():
        for i, z in enumerate(heads()):
            q = _rope128(_rms(z, gn_ref[0:1, :]), cos_ref[...], sin_ref[...])
            qa_ref[:, i * hd:(i + 1) * hd] = (q * q_scale).astype(BF16)

    @pl.when(j == 1)
    def _():
        for i, z in enumerate(heads()):
            k = _rope128(_rms(z, gn_ref[1:2, :]), cos_ref[...], sin_ref[...])
            kaf_ref[pl.ds(i, tm, stride=n_grp), :] = k
            kab_ref[:, i * hd:(i + 1) * hd] = k.astype(BF16)

    @pl.when(j == 2)
    def _():
        for i, z in enumerate(heads()):
            vaf_ref[pl.ds((i % 2) * (n_grp // 2) + i // 2, tm, stride=n_grp), :] = z
            vab_ref[:, i * hd:(i + 1) * hd] = z.astype(BF16)

    @pl.when(j == 3)
    def _():
        for i, z in enumerate(heads()):
            qb_ref[:, i * hd:(i + 1) * hd] = (_rms(z, gn_ref[2:3, :]) * q_scale).astype(BF16)

    @pl.when(j == 4)
    def _():
        for i, z in enumerate(heads()):
            k = _rms(z, gn_ref[3:4, :])
            kbf_ref[:, i * hd:(i + 1) * hd] = k
            kbb_ref[:, i * hd:(i + 1) * hd] = k.astype(BF16)

    @pl.when(j == 5)
    def _():
        for i, z in enumerate(heads()):
            vbf_ref[:, i * hd:(i + 1) * hd] = z
            vbb_ref[:, i * hd:(i + 1) * hd] = z.astype(BF16)


def _even_proj(x, sh, sc, gain, w_main, w_forget, cos, sin, qk_gain, b_forget_pad, n_grp, n_fox):
    b, t, d = x.shape
    wg = n_grp * LANES
    assert w_main.shape[1] == 6 * wg and n_fox == n_grp
    tm = _pick_tile(t, 512, 16)
    row = lambda w: pl.BlockSpec((None, tm, w), lambda bi, i, j: (bi, i, 0))
    tab = pl.BlockSpec((tm, LANES), lambda bi, i, j: (i, 0))
    const = lambda shape: pl.BlockSpec(shape, lambda bi, i, j: (0,) * len(shape))
    outs = [(wg, BF16), (wg, F32), (wg, BF16), (wg, F32), (wg, BF16),
            (wg, BF16), (wg, F32), (wg, BF16), (wg, F32), (wg, BF16), (LANES, F32)]
    out_shape = [jax.ShapeDtypeStruct((b, t, w), dt) for w, dt in outs]
    out_specs = [row(w) for w, _ in outs]
    for idx in (1, 3):
        out_shape[idx] = jax.ShapeDtypeStruct((b, t * n_grp, LANES), F32)
        out_specs[idx] = pl.BlockSpec((None, tm * n_grp, LANES), lambda bi, i, j: (bi, i, 0))
    vmem = (2 * tm * d * 4 + tm * d * 2 + 4 * d * wg * 2 + 2 * sum(tm * w * jnp.dtype(dt).itemsize for w, dt in outs)
            + 4 * tm * wg * 4 + 4 * tm * d * 4 * (sh.shape[1] != 1) + (4 << 20))
    return pl.pallas_call(
        functools.partial(_even_proj_kernel, n_grp=n_grp, n_forget=n_fox, q_scale=LANES ** -0.5 * LOG2E),
        out_shape=out_shape,
        grid=(b, t // tm, 6),
        in_specs=[pl.BlockSpec((None, tm, d), lambda bi, i, j: (bi, i, 0)),
                  _mod_spec(sh, tm), _mod_spec(sc, tm), const((1, d)),
                  pl.BlockSpec((d, wg), lambda bi, i, j: (0, j)), const((d, LANES)),
                  tab, tab, const((4, LANES)), const((1, LANES))],
        out_specs=out_specs,
        scratch_shapes=[pltpu.VMEM((tm, d), BF16)],
        compiler_params=_params(("parallel", "parallel", "arbitrary"), vmem),
        name="even_proj",
    )(x, sh, sc, gain.reshape(1, d), w_main, w_forget, cos, sin, qk_gain, b_forget_pad)


def _split3(x):
    hi = x.astype(BF16)
    r = x - hi.astype(F32)
    mid = r.astype(BF16)
    lo = (r - mid.astype(F32)).astype(BF16)
    return hi, mid, lo


def _fox_prep_kernel(lf_ref, k_ref, v_ref, aq_ref, kk_ref, vv_ref, carry_ref, *, n_heads):
    @pl.when(pl.program_id(1) == 0)
    def _():
        carry_ref[...] = jnp.zeros_like(carry_ref)

    x = lf_ref[...]
    tb = x.shape[0]
    r = lax.broadcasted_iota(jnp.int32, (tb, tb), 0)
    c = lax.broadcasted_iota(jnp.int32, (tb, tb), 1)
    tri = jnp.where(r >= c, 1.0, 0.0).astype(BF16)
    hi, mid, lo = _split3(x)
    cum = _dot(tri, hi) + _dot(tri, mid) + _dot(tri, lo) + carry_ref[...]
    carry_ref[...] = cum[tb - 1:tb, :]
    fh, fm, fl = (p.astype(F32) for p in _split3(cum * LOG2E))
    lane = lax.broadcasted_iota(jnp.int32, (tb, LANES), 1)
    ones_q = jnp.where((lane >= 3) & (lane < 6), 1.0, 0.0)
    ones_k = jnp.where(lane < 3, 1.0, 0.0)
    for h in range(n_heads):
        a, m, l = fh[:, h:h + 1], fm[:, h:h + 1], fl[:, h:h + 1]
        aq = jnp.where(lane == 0, a, jnp.where(lane == 1, m, jnp.where(lane == 2, l, ones_q)))
        ak = jnp.where(lane == 3, -a, jnp.where(lane == 4, -m, jnp.where(lane == 5, -l, ones_k)))
        aq_ref[:, h * LANES:(h + 1) * LANES] = aq.astype(BF16)
        kk_ref[:, 2 * h * LANES:(2 * h + 1) * LANES] = k_ref[:, h * LANES:(h + 1) * LANES]
        kk_ref[:, (2 * h + 1) * LANES:(2 * h + 2) * LANES] = ak.astype(BF16)
        vv_ref[:, 2 * h * LANES:(2 * h + 1) * LANES] = v_ref[:, h * LANES:(h + 1) * LANES]
        vv_ref[:, (2 * h + 1) * LANES:(2 * h + 2) * LANES] = jnp.ones((tb, LANES), BF16)


def _fox_prep(logf_pad, k_all, v_all, n_heads):
    b, t, _ = logf_pad.shape
    tb = _pick_tile(t, 512, LANES)
    w = n_heads * LANES
    narrow = pl.BlockSpec((None, tb, w), lambda bi, i: (bi, i, 0))
    wide = pl.BlockSpec((None, tb, 2 * w), lambda bi, i: (bi, i, 0))
    return pl.pallas_call(
        functools.partial(_fox_prep_kernel, n_heads=n_heads),
        out_shape=[jax.ShapeDtypeStruct((b, t, w), BF16), jax.ShapeDtypeStruct((b, t, 2 * w), BF16),
                   jax.ShapeDtypeStruct((b, t, 2 * w), BF16)],
        grid=(b, t // tb),
        in_specs=[pl.BlockSpec((None, tb, LANES), lambda bi, i: (bi, i, 0)), narrow, narrow],
        out_specs=[narrow, wide, wide],
        scratch_shapes=[pltpu.VMEM((1, LANES), F32)],
        compiler_params=_params(("parallel", "arbitrary"), 32 << 20),
        name="fox_prep",
    )(logf_pad, k_all, v_all)


FLAG_FIRST, FLAG_LAST = 1, 2
KIND_SHIFT = 2
KIND_FULL, KIND_MASK, KIND_DIAG = 0, 1, 2
SCORE_LOOKAHEAD = 1


def _pair_table(n_q, n_k, tq, tk, q_off, causal):
    tk_pad = _round_up(n_k, tk)
    aligned = tq == tk and q_off % tq == 0
    qi, kj, fl = [], [], []
    for i in range(n_q // tq):
        qmin, qmax = q_off + i * tq, q_off + (i + 1) * tq - 1
        row = []
        for j in range(tk_pad // tk):
            kmin, kmax = j * tk, min((j + 1) * tk, n_k) - 1
            if kmin >= n_k:
                continue
            if causal:
                any_vis, all_vis = kmin <= qmax, kmax <= qmin
            else:
                any_vis, all_vis = kmin // CHUNK <= qmax // CHUNK, kmax // CHUNK <= qmin // CHUNK
            all_vis = all_vis and (j + 1) * tk <= n_k
            if any_vis:
                diag = aligned and kmin == qmin and (j + 1) * tk <= n_k
                row.append((j, KIND_FULL if all_vis else KIND_DIAG if diag else KIND_MASK))
        assert row and row[0][0] == 0
        for idx, (j, kind) in enumerate(row):
            qi.append(i)
            kj.append(j)
            fl.append((kind << KIND_SHIFT) | (FLAG_FIRST if idx == 0 else 0) | (FLAG_LAST if idx == len(row) - 1 else 0))
    kinds = sorted({f >> KIND_SHIFT for f in fl})
    as_arr = lambda v: jnp.asarray(np.array(v, np.int32))
    return as_arr(qi), as_arr(kj), as_arr(fl), kinds


def _visible(shape, qpos0, kpos0, n_k, causal, row_period=None):
    if row_period is None:
        rows = qpos0 + lax.broadcasted_iota(jnp.int32, shape, 0)
    else:
        one = lax.broadcasted_iota(jnp.int32, (row_period, shape[1]), 0)
        rows = qpos0 + jnp.concatenate([one] * (shape[0] // row_period), axis=0)
    cols = kpos0 + lax.broadcasted_iota(jnp.int32, shape, 1)
    if causal:
        ok = cols <= rows
    else:
        ok = (cols >> CHUNK_SHIFT) <= (rows >> CHUNK_SHIFT)
    return ok if n_k is None else ok & (cols < n_k)


def _lane_tile(x, n):
    return x if n == LANES else jnp.concatenate([x] * (n // LANES), axis=1)


def _lane_fold(p):
    acc = p[:, 0:LANES]
    for c in range(1, p.shape[1] // LANES):
        acc = acc + p[:, c * LANES:(c + 1) * LANES]
    return acc


def _attend(s, v, m_ref, l_ref, acc_ref, idx):
    m_prev = m_ref[idx]
    m_new = jnp.maximum(m_prev, jnp.max(s, axis=1, keepdims=True))
    alpha = jnp.exp2(m_prev - m_new)
    p = jnp.exp2(s - _lane_tile(m_new, s.shape[1]))
    if l_ref is not None:
        l_ref[idx] = alpha * l_ref[idx] + _lane_fold(p)
    acc_ref[idx] = _lane_tile(alpha, v.shape[1]) * acc_ref[idx] + _dot(p.astype(v.dtype), v)
    m_ref[idx] = m_new


def _flash_frame(qi_ref, kj_ref, fl_ref, m_ref, l_ref, acc_ref, scores, update, finish, first=None, *,
                 tq, tk, rs, rs_full, q_off, n_k, causal, kinds):
    n = pl.program_id(2)
    flags = fl_ref[n]
    kind = flags >> KIND_SHIFT

    @pl.when((flags & FLAG_FIRST) != 0)
    def _():
        m_ref[...] = jnp.full_like(m_ref, NEG_INF)
        if l_ref is not None:
            l_ref[...] = jnp.zeros_like(l_ref)
        acc_ref[...] = jnp.zeros_like(acc_ref)
        if first is not None:
            first()

    qpos0 = q_off + qi_ref[n] * tq
    kpos0 = kj_ref[n] * tk
    rs_masked = rs

    def run(k):
        rs = rs_full if k == KIND_FULL else rs_masked

        def masked_scores(r):
            n_keys = (r + 1) * rs if k == KIND_DIAG else tk
            rows = slice(r * rs, (r + 1) * rs)
            tiles = scores(rows, n_keys)
            if k == KIND_DIAG:
                lo = r * rs
                vis = _visible((rs, rs), qpos0 + lo, kpos0 + lo, None, causal)
                blks = [jnp.where(vis, s[:, lo:], NEG_INF) for s in tiles]
                tiles = blks if lo == 0 else [jnp.concatenate([s[:, :lo], b], axis=1) for s, b in zip(tiles, blks)]
            elif k == KIND_MASK:
                vis = _visible(tiles[0].shape, qpos0 + r * rs, kpos0, n_k, causal)
                tiles = [jnp.where(vis, s, NEG_INF) for s in tiles]
            return rows, n_keys, tiles

        order = list(range(tq // rs))
        if k == KIND_DIAG:
            order.reverse()
        pending = [masked_scores(r) for r in order[:SCORE_LOOKAHEAD]]
        for idx in range(len(order)):
            rows, n_keys, tiles = pending.pop(0)
            if idx + SCORE_LOOKAHEAD < len(order):
                pending.append(masked_scores(order[idx + SCORE_LOOKAHEAD]))
            update(rows, n_keys, tiles)

    for k in kinds:
        pl.when(kind == k)(functools.partial(run, k))

    @pl.when((flags & FLAG_LAST) != 0)
    def _():
        finish()


def _diff_kernel(qi_ref, kj_ref, fl_ref, q_ref, k_ref, v_ref, lam_ref, sub_ref, o_ref,
                 m_ref, l_ref, acc_ref, *, lam_init, **frame):
    hd = LANES

    def scores(rows, n_keys):
        return [_dot_nt(q_ref[rows, c * hd:(c + 1) * hd], k_ref[0:n_keys, c * hd:(c + 1) * hd]) for c in range(2)]

    def update(rows, n_keys, tiles):
        v = v_ref[0:n_keys, :]
        for c in range(2):
            _attend(tiles[c], v, m_ref, l_ref, acc_ref, (c, rows))

    def finish():
        lp = lam_ref[...]
        lam = (jnp.exp(jnp.sum(lp[0:1] * lp[1:2], axis=-1, keepdims=True))
               - jnp.exp(jnp.sum(lp[2:3] * lp[3:4], axis=-1, keepdims=True)) + lam_init)
        l0 = jnp.sum(l_ref[0], axis=1, keepdims=True)
        l1 = jnp.sum(l_ref[1], axis=1, keepdims=True)
        o = acc_ref[0] / l0 - lam * (acc_ref[1] / l1)
        o_ref[...] = (_rms(o, sub_ref[...]) * (1.0 - lam_init)).astype(o_ref.dtype)

    _flash_frame(qi_ref, kj_ref, fl_ref, m_ref, l_ref, acc_ref, scores, update, finish, **frame)


def _ones_finish(acc_ref, o_ref):
    o_ref[...] = (acc_ref[:, 0:LANES] / acc_ref[:, LANES:2 * LANES]).astype(o_ref.dtype)


def _fox_kernel(qi_ref, kj_ref, fl_ref, q_ref, aq_ref, k_ref, v_ref, o_ref,
                m_ref, acc_ref, qq_ref, **frame):
    def first():
        qq_ref[:, 0:LANES] = q_ref[...]
        qq_ref[:, LANES:2 * LANES] = aq_ref[...]

    def scores(rows, n_keys):
        return [_dot_nt(qq_ref[rows, :], k_ref[0:n_keys, :])]

    def update(rows, n_keys, tiles):
        _attend(tiles[0], v_ref[0:n_keys, :], m_ref, None, acc_ref, rows)

    _flash_frame(qi_ref, kj_ref, fl_ref, m_ref, None, acc_ref, scores, update,
                 functools.partial(_ones_finish, acc_ref, o_ref), first, **frame)


def _mla_kernel(qi_ref, kj_ref, fl_ref, q_ref, k_ref, v_ref, o_ref,
                m_ref, acc_ref, **frame):
    def scores(rows, n_keys):
        return [_dot_nt(q_ref[rows, :], k_ref[0:n_keys, :])]

    def update(rows, n_keys, tiles):
        _attend(tiles[0], v_ref[0:n_keys, :], m_ref, None, acc_ref, rows)

    _flash_frame(qi_ref, kj_ref, fl_ref, m_ref, None, acc_ref, scores, update,
                 functools.partial(_ones_finish, acc_ref, o_ref), **frame)


def _flash_tiles(n_q, n_k_pad, tile):
    tq = _pick_tile(n_q, tile, 16)
    tk = next((t for t in (tile, tile // 2, tile // 4) if n_k_pad % t == 0), None) or _pick_tile(n_k_pad, tile, LANES)
    rs = _pick_tile(tq, 256, 16)
    return tq, tk, rs


def _flash_call(kernel, n_heads, operands, n_q, n_k_pad, out_width, scratch, n_k, q_off, causal, name,
                tile=2048, **kw):
    b = operands[0][0].shape[0]
    tq, tk, rs = _flash_tiles(n_q, n_k_pad, tile)
    qi, kj, fl, kinds = _pair_table(n_q, n_k, tq, tk, q_off, causal)
    if KIND_DIAG in kinds:
        assert rs % CHUNK == 0 and rs % LANES == 0

    in_specs, args = [], []
    for op in operands:
        a = op[0]
        if len(op) == 1:
            in_specs.append(pl.BlockSpec(a.shape, lambda bi, h, n, qi, kj, fl: (0, 0)))
        elif op[2]:
            in_specs.append(pl.BlockSpec((None, tq, op[1]), lambda bi, h, n, qi, kj, fl: (bi, qi[n], h)))
        else:
            in_specs.append(pl.BlockSpec((None, tk, op[1]), lambda bi, h, n, qi, kj, fl: (bi, kj[n], h)))
        args.append(a)
    vmem = 2 * tq * tk * 4 + 8 * max(tq, tk) * 2 * LANES * 2 * len(operands) + 10 * tq * 2 * LANES * 4 + (8 << 20)
    grid_spec = pltpu.PrefetchScalarGridSpec(
        num_scalar_prefetch=3,
        grid=(b, n_heads, int(qi.shape[0])),
        in_specs=in_specs,
        out_specs=pl.BlockSpec((None, tq, out_width), lambda bi, h, n, qi, kj, fl: (bi, qi[n], h)),
        scratch_shapes=scratch(tq),
    )
    return pl.pallas_call(
        functools.partial(kernel, tq=tq, tk=tk, rs=rs, rs_full=_pick_tile(tq, 2 * rs, 16), q_off=q_off, n_k=n_k, causal=causal, kinds=kinds, **kw),
        out_shape=jax.ShapeDtypeStruct((b, n_q, n_heads * out_width), BF16),
        grid_spec=grid_spec,
        compiler_params=_params(("parallel", "parallel", "arbitrary"), vmem),
        name=name,
    )(qi, kj, fl, *args)


def _flash_diff(q, k, v, lam_p, subln, lam_init, n_heads, n_k, q_off):
    w = 2 * LANES
    scratch = lambda tq: [pltpu.VMEM((2, tq, LANES), F32), pltpu.VMEM((2, tq, LANES), F32), pltpu.VMEM((2, tq, w), F32)]
    ops = [(q, w, True), (k, w, False), (v, w, False), (lam_p,), (subln.reshape(1, w),)]
    return _flash_call(_diff_kernel, n_heads, ops, q.shape[1], k.shape[1], w, scratch, n_k, q_off, False,
                       "diff_attn", tile=2048, lam_init=lam_init)


def _flash_fox(q, aq, kk, vv, n_heads, n_k, q_off):
    w = LANES
    scratch = lambda tq: [pltpu.VMEM((tq, LANES), F32), pltpu.VMEM((tq, 2 * w), F32), pltpu.VMEM((tq, 2 * w), BF16)]
    ops = [(q, w, True), (aq, w, True), (kk, 2 * w, False), (vv, 2 * w, False)]
    return _flash_call(_fox_kernel, n_heads, ops, q.shape[1], kk.shape[1], w, scratch, n_k, q_off, True, "fox_attn")


def _flash_mla(qq, kk, vv, n_heads, n_k, q_off):
    w = LANES
    scratch = lambda tq: [pltpu.VMEM((tq, LANES), F32), pltpu.VMEM((tq, 2 * w), F32)]
    ops = [(qq, 2 * w, True), (kk, 2 * w, False), (vv, 2 * w, False)]
    return _flash_call(_mla_kernel, n_heads, ops, qq.shape[1], kk.shape[1], w, scratch, n_k, q_off, False, "mla_attn")


NEW_ROWS = LANES


def _cached_frame(m_ref, l_ref, acc_ref, chains, finish, *, nkb, tk, ta, q_off, n_k, causal):
    n = pl.program_id(1)

    @pl.when(n == 0)
    def _():
        m_ref[...] = jnp.full_like(m_ref, NEG_INF)
        l_ref[...] = jnp.zeros_like(l_ref)
        acc_ref[...] = jnp.zeros_like(acc_ref)

    @pl.when(n < nkb)
    def _():
        chains(True, tk, None)

    @pl.when(n == nkb)
    def _():
        mask_fn = lambda s: jnp.where(_visible(s.shape, q_off, q_off, n_k, causal, ta), s, NEG_INF)
        chains(False, NEW_ROWS, mask_fn)
        finish()


def _dec_diff_kernel(q_ref, kc_ref, vc_ref, kn_ref, vn_ref, lam_ref, sub_ref, o_ref,
                     m_ref, l_ref, acc_ref, *, n_heads, lam_init, **frame):
    hd, g, tk = LANES, 2 * n_heads, frame['tk']

    def chains(cached, n_keys, mask_fn):
        for h in range(n_heads):
            if cached:
                v = jnp.concatenate([vc_ref[pl.ds(h, tk, stride=g), :],
                                     vc_ref[pl.ds(n_heads + h, tk, stride=g), :]],
                                    axis=1).astype(BF16)
            else:
                v = vn_ref[:, 2 * h * hd:(2 * h + 2) * hd]
            for c in range(2):
                j = 2 * h + c
                k = kc_ref[pl.ds(j, tk, stride=g), :].astype(BF16) if cached else kn_ref[:, j * hd:(j + 1) * hd]
                s = _dot_nt(q_ref[:, j * hd:(j + 1) * hd], k)
                if mask_fn is not None:
                    s = mask_fn(s)
                _attend(s, v, m_ref, l_ref, acc_ref, j)

    def finish():
        lp = lam_ref[...]
        lam = (jnp.exp(jnp.sum(lp[0:1] * lp[1:2], axis=-1, keepdims=True))
               - jnp.exp(jnp.sum(lp[2:3] * lp[3:4], axis=-1, keepdims=True)) + lam_init)
        for h in range(n_heads):
            l0 = jnp.sum(l_ref[2 * h], axis=1, keepdims=True)
            l1 = jnp.sum(l_ref[2 * h + 1], axis=1, keepdims=True)
            o = acc_ref[2 * h] / l0 - lam * (acc_ref[2 * h + 1] / l1)
            o_ref[:, 2 * h * hd:(2 * h + 2) * hd] = (_rms(o, sub_ref[...]) * (1.0 - lam_init)).astype(o_ref.dtype)

    _cached_frame(m_ref, l_ref, acc_ref, chains, finish, **frame)


def _dec_fox_kernel(q_ref, kc_ref, vc_ref, kn_ref, vn_ref, fq_ref, fk_ref, o_ref,
                    m_ref, l_ref, acc_ref, *, n_heads, **frame):
    hd, g, tk = LANES, n_heads, frame['tk']

    def chains(cached, n_keys, mask_fn):
        for h in range(n_heads):
            if cached:
                k = kc_ref[pl.ds(h, tk, stride=g), :].astype(BF16)
                v = vc_ref[pl.ds(h, tk, stride=g), :].astype(BF16)
            else:
                k = kn_ref[:, h * hd:(h + 1) * hd]
                v = vn_ref[:, h * hd:(h + 1) * hd]
            bias = (fq_ref[:, h:h + 1] - fk_ref[h:h + 1, 0:n_keys]) * LOG2E
            s = _dot_nt(q_ref[:, h * hd:(h + 1) * hd], k) + bias
            if mask_fn is not None:
                s = mask_fn(s)
            _attend(s, v, m_ref, l_ref, acc_ref, h)

    def finish():
        for h in range(n_heads):
            o = acc_ref[h] / jnp.sum(l_ref[h], axis=1, keepdims=True)
            o_ref[:, h * hd:(h + 1) * hd] = o.astype(o_ref.dtype)

    _cached_frame(m_ref, l_ref, acc_ref, chains, finish, **frame)


def _cached_attn(kernel, q, cache_k, cache_v, new_k, new_v, extra, extra_specs, groups, v_width, n_k, causal,
                 name, **kw):
    b, ta, _ = q.shape
    past = cache_k.shape[1]
    tk = _pick_tile(past, 1024, LANES)
    nkb = past // tk
    kc = cache_k.reshape(b, past * groups, LANES)
    if cache_v.shape[-1] == 2 * LANES:
        vc = cache_v.reshape(b, past, groups // 2, 2, LANES).swapaxes(2, 3).reshape(b, past * groups, LANES)
    else:
        vc = cache_v.reshape(b, past * groups, LANES)
    kn, vn = _pad_rows(new_k, NEW_ROWS), _pad_rows(new_v, NEW_ROWS)
    whole = lambda a: pl.BlockSpec((None,) + a.shape[1:], lambda bi, n: (bi,) + (0,) * (a.ndim - 1))
    cache = pl.BlockSpec((None, tk * groups, LANES), lambda bi, n: (bi, jnp.minimum(n, nkb - 1), 0))
    out_w = q.shape[2]
    return pl.pallas_call(
        functools.partial(kernel, nkb=nkb, tk=tk, ta=ta, q_off=past, n_k=n_k, causal=causal, **kw),
        out_shape=jax.ShapeDtypeStruct((b, ta, out_w), BF16),
        grid=(b, nkb + 1),
        in_specs=[whole(q), cache, cache, whole(kn), whole(vn)] + extra_specs(tk),
        out_specs=pl.BlockSpec((None, ta, out_w), lambda bi, n: (bi, 0, 0)),
        scratch_shapes=[pltpu.VMEM((groups, ta, LANES), F32), pltpu.VMEM((groups, ta, LANES), F32),
                        pltpu.VMEM((groups, ta, v_width), F32)],
        compiler_params=_params(("parallel", "arbitrary"), 4 * tk * groups * LANES * 4 + (16 << 20)),
        name=name,
    )(q, kc, vc, kn, vn, *extra)


def _dec_mla_kernel(q_ref, ckv_ref, kp_ref, ckvn_ref, kpn_ref, wk_ref, wv_ref, gk_ref, o_ref,
                    m_ref, l_ref, acc_ref, s_ref, qp_ref, *, n_heads, **frame):
    hd, ta = LANES, frame['ta']

    @pl.when(pl.program_id(1) == 0)
    def _():
        for h in range(n_heads):
            qp_ref[h * ta:(h + 1) * ta, :] = q_ref[:, (2 * h + 1) * hd:(2 * h + 2) * hd]

    def chains(cached, n_keys, mask_fn):
        ckv = ckv_ref[...].astype(BF16) if cached else ckvn_ref[...]
        kp = kp_ref[...] if cached else kpn_ref[...]
        kvn = _dot(ckv, wk_ref[...])
        for h in range(n_heads):
            kn = _rms(kvn[:, h * hd:(h + 1) * hd], gk_ref[...]).astype(BF16)
            s_ref[h * ta:(h + 1) * ta, 0:n_keys] = _dot_nt(q_ref[:, 2 * h * hd:(2 * h + 1) * hd], kn)
        s = s_ref[:, 0:n_keys] + _dot_nt(qp_ref[...], kp)
        if mask_fn is not None:
            s = mask_fn(s)
        _attend(s, ckv, m_ref, l_ref, acc_ref, slice(None))

    def finish():
        lat = acc_ref[...] / jnp.sum(l_ref[...], axis=1, keepdims=True)
        for h in range(n_heads):
            o = _dot(lat[h * ta:(h + 1) * ta, :].astype(BF16), wv_ref[:, h * hd:(h + 1) * hd])
            o_ref[:, h * hd:(h + 1) * hd] = o.astype(o_ref.dtype)

    _cached_frame(m_ref, l_ref, acc_ref, chains, finish, **frame)


def _dec_mla(qq, cache_ckv, kp_cache, new_ckv, new_kp, w_kn, w_v, g_nope_k, n_heads, n_k):
    b, ta, _ = qq.shape
    past, c = cache_ckv.shape[1:]
    tk = _pick_tile(past, 1024, LANES)
    nkb = past // tk
    rows = n_heads * ta
    ckvn, kpn = _pad_rows(new_ckv, NEW_ROWS), _pad_rows(new_kp, NEW_ROWS)
    whole = lambda a: pl.BlockSpec((None,) + a.shape[1:], lambda bi, n: (bi,) + (0,) * (a.ndim - 1))
    const = lambda a: pl.BlockSpec(a.shape, lambda bi, n: (0,) * a.ndim)
    blk = lambda w: pl.BlockSpec((None, tk, w), lambda bi, n: (bi, jnp.minimum(n, nkb - 1), 0))

    kernel = functools.partial(_dec_mla_kernel, n_heads=n_heads, nkb=nkb, tk=tk, ta=ta, q_off=past, n_k=n_k,
                               causal=False)
    gk = g_nope_k.reshape(1, LANES)
    return pl.pallas_call(
        kernel,
        out_shape=jax.ShapeDtypeStruct((b, ta, n_heads * LANES), BF16),
        grid=(b, nkb + 1),
        in_specs=[whole(qq), blk(c), blk(LANES), whole(ckvn), whole(kpn), const(w_kn), const(w_v), const(gk)],
        out_specs=pl.BlockSpec((None, ta, n_heads * LANES), lambda bi, n: (bi, 0, 0)),
        scratch_shapes=[pltpu.VMEM((rows, LANES), F32), pltpu.VMEM((rows, LANES), F32), pltpu.VMEM((rows, c), F32),
                        pltpu.VMEM((rows, tk), F32), pltpu.VMEM((rows, LANES), BF16)],
        compiler_params=_params(("parallel", "arbitrary"), 40 << 20),
        name="mla_attn_cached",
    )(qq, cache_ckv, kp_cache, ckvn, kpn, w_kn, w_v, gk)


def _cumsum_rows_kernel(x_ref, o_ref, carry_ref):
    @pl.when(pl.program_id(1) == 0)
    def _():
        carry_ref[...] = jnp.zeros_like(carry_ref)

    x = x_ref[...]
    tb = x.shape[1]
    r = lax.broadcasted_iota(jnp.int32, (tb, tb), 0)
    c = lax.broadcasted_iota(jnp.int32, (tb, tb), 1)
    tri = jnp.where(r <= c, 1.0, 0.0).astype(BF16)
    hi, mid, lo = _split3(x)
    cum = _dot(hi, tri) + _dot(mid, tri) + _dot(lo, tri) + carry_ref[:, 0:1]
    o_ref[...] = cum
    carry_ref[...] = jnp.broadcast_to(cum[:, tb - 1:tb], carry_ref.shape)


def _cumsum_rows(x, tb):
    b, g, t = x.shape
    return pl.pallas_call(
        _cumsum_rows_kernel,
        out_shape=jax.ShapeDtypeStruct((b, g, t), F32),
        grid=(b, t // tb),
        in_specs=[pl.BlockSpec((None, g, tb), lambda bi, i: (bi, 0, i))],
        out_specs=pl.BlockSpec((None, g, tb), lambda bi, i: (bi, 0, i)),
        scratch_shapes=[pltpu.VMEM((g, LANES), F32)],
        compiler_params=_params(("parallel", "arbitrary"), 32 << 20),
        name="forget_cumsum",
    )(x)


def _dec_diff(q, cache_k, cache_v, new_k, new_v, lam_p, subln, lam_init, n_heads, n_k):
    b, past = cache_k.shape[:2]
    specs = lambda tk: [pl.BlockSpec(lam_p.shape, lambda bi, n: (0, 0)),
                        pl.BlockSpec((1, 2 * LANES), lambda bi, n: (0, 0))]
    return _cached_attn(_dec_diff_kernel, q, cache_k, cache_v, new_k, new_v, [lam_p, subln.reshape(1, 2 * LANES)],
                        specs, 2 * n_heads, 2 * LANES, n_k, False, "diff_attn_cached",
                        n_heads=n_heads, lam_init=lam_init)


def _dec_fox(q, cache_k, cache_v, new_k, new_v, past_logf, new_logf, n_heads, n_k):
    b, past = cache_k.shape[:2]
    ta = q.shape[1]
    tk = _pick_tile(past, 1024, LANES)
    lf = jnp.concatenate([past_logf.astype(F32), new_logf[:, :, :n_heads]], axis=1)
    lf_rows = _pad_lanes(jnp.swapaxes(lf, 1, 2), past + tk)
    f_rows = _cumsum_rows(lf_rows, tk)
    f_q = _pad_lanes(jnp.swapaxes(f_rows[:, :, past:past + ta], 1, 2), LANES)
    specs = lambda tk: [pl.BlockSpec((None, ta, LANES), lambda bi, n: (bi, 0, 0)),
                        pl.BlockSpec((None, n_heads, tk), lambda bi, n: (bi, 0, n))]
    return _cached_attn(_dec_fox_kernel, q, cache_k, cache_v, new_k, new_v, [f_q, f_rows], specs,
                        n_heads, LANES, n_k, True, "fox_attn_cached", n_heads=n_heads)


def _rope_half(y, cos, sin, rope_dim):
    half = rope_dim // 2
    lane = lax.broadcasted_iota(jnp.int32, y.shape, 1)
    rot = jnp.where(lane < half, pltpu.roll(y, LANES - half, 1), pltpu.roll(y, half, 1))
    return y * cos + rot * sin


def _rms_low(x, gain, n):
    return x * lax.rsqrt(jnp.sum(x * x, axis=-1, keepdims=True) * (1.0 / n) + EPS) * gain


def _odd_post_kernel(z_ref, cos_ref, sin_ref, gq_ref, gkv_ref, gr_ref,
                     cq_ref, ckvf_ref, ckvb_ref, kpf_ref, kpb_ref, *, q_lora, kv_lora, rope_dim):
    cq_ref[...] = _rms(z_ref[:, 0:q_lora], gq_ref[...]).astype(BF16)
    ckv = _rms(z_ref[:, q_lora:q_lora + kv_lora], gkv_ref[...])
    ckvf_ref[...] = ckv
    ckvb_ref[...] = ckv.astype(BF16)
    kp = _rms_low(z_ref[:, q_lora + kv_lora:q_lora + kv_lora + LANES], gr_ref[...], rope_dim)
    kp = _rope_half(kp, cos_ref[...], sin_ref[...], rope_dim)
    kpf_ref[...] = kp
    kpb_ref[...] = kp.astype(BF16)


def _odd_post(z, cos, sin, g_cq, g_ckv, g_rope_k_pad, q_lora, kv_lora, rope_dim):
    b, t, n = z.shape
    tm = _pick_tile(t, 512, 16)
    row = lambda w: pl.BlockSpec((None, tm, w), lambda bi, i: (bi, i, 0))
    tab = pl.BlockSpec((tm, LANES), lambda bi, i: (i, 0))
    vec = lambda w: pl.BlockSpec((1, w), lambda bi, i: (0, 0))
    outs = [(q_lora, BF16), (kv_lora, F32), (kv_lora, BF16), (LANES, F32), (LANES, BF16)]
    return pl.pallas_call(
        functools.partial(_odd_post_kernel, q_lora=q_lora, kv_lora=kv_lora, rope_dim=rope_dim),
        out_shape=[jax.ShapeDtypeStruct((b, t, w), dt) for w, dt in outs],
        grid=(b, t // tm),
        in_specs=[row(n), tab, tab, vec(q_lora), vec(kv_lora), vec(LANES)],
        out_specs=[row(w) for w, _ in outs],
        compiler_params=_params(("parallel", "parallel"), 32 << 20),
        name="odd_post",
    )(z, cos, sin, g_cq.reshape(1, -1), g_ckv.reshape(1, -1), g_rope_k_pad)


def _qup_kernel(cq_ref, w_ref, cos_ref, sin_ref, gn_ref, gr_ref, qq_ref, *, heads, rope_dim, q_scale):
    cq = cq_ref[...]
    cos, sin = cos_ref[...], sin_ref[...]
    for h in range(heads):
        q = _dot(cq, w_ref[:, 2 * h * LANES:(2 * h + 2) * LANES])
        qn = _rms(q[:, 0:LANES], gn_ref[...])
        qq_ref[:, 2 * h * LANES:(2 * h + 1) * LANES] = (qn * q_scale).astype(BF16)
        qp = _rms_low(q[:, LANES:2 * LANES], gr_ref[...], rope_dim)
        qp = _rope_half(qp, cos, sin, rope_dim)
        qq_ref[:, (2 * h + 1) * LANES:(2 * h + 2) * LANES] = (qp * q_scale).astype(BF16)


def _qup(cq, w_pad, cos, sin, g_nope_q, g_rope_q_pad, n_heads, rope_dim, qk_dim):
    b, t, kq = cq.shape
    tm = _pick_tile(t, 512, 16)
    hg = 4 if n_heads % 4 == 0 else 1
    tn = hg * 2 * LANES
    return pl.pallas_call(
        functools.partial(_qup_kernel, heads=hg, rope_dim=rope_dim, q_scale=qk_dim ** -0.5 * LOG2E),
        out_shape=jax.ShapeDtypeStruct((b, t, n_heads * 2 * LANES), BF16),
        grid=(b, t // tm, n_heads // hg),
        in_specs=[
            pl.BlockSpec((None, tm, kq), lambda bi, i, j: (bi, i, 0)),
            pl.BlockSpec((kq, tn), lambda bi, i, j: (0, j)),
            pl.BlockSpec((tm, LANES), lambda bi, i, j: (i, 0)),
            pl.BlockSpec((tm, LANES), lambda bi, i, j: (i, 0)),
            pl.BlockSpec((1, LANES), lambda bi, i, j: (0, 0)),
            pl.BlockSpec((1, LANES), lambda bi, i, j: (0, 0)),
        ],
        out_specs=pl.BlockSpec((None, tm, tn), lambda bi, i, j: (bi, i, j)),
        compiler_params=_params(("parallel", "parallel", "arbitrary"), 32 << 20),
        name="mla_q_up",
    )(cq, w_pad, cos, sin, g_nope_q.reshape(1, LANES), g_rope_q_pad)


def _kvup_kernel(ckv_ref, kp_ref, w_ref, gn_ref, kk_ref, v_ref, *, heads):
    ckv = ckv_ref[...]
    kp = kp_ref[...]
    for h in range(heads):
        kv = _dot(ckv, w_ref[:, 2 * h * LANES:(2 * h + 2) * LANES])
        kn = _rms(kv[:, 0:LANES], gn_ref[...])
        kk_ref[:, 2 * h * LANES:(2 * h + 1) * LANES] = kn.astype(BF16)
        kk_ref[:, (2 * h + 1) * LANES:(2 * h + 2) * LANES] = kp
        v_ref[:, 2 * h * LANES:(2 * h + 1) * LANES] = kv[:, LANES:2 * LANES].astype(BF16)
        v_ref[:, (2 * h + 1) * LANES:(2 * h + 2) * LANES] = jnp.ones((kv.shape[0], LANES), BF16)


def _kvup(ckv, kp, w, g_nope_k, n_heads):
    b, t, kk = ckv.shape
    tm = _pick_tile(t, 512, LANES)
    hg = 4 if n_heads % 4 == 0 else 1
    tn = hg * 2 * LANES
    return pl.pallas_call(
        functools.partial(_kvup_kernel, heads=hg),
        out_shape=[jax.ShapeDtypeStruct((b, t, n_heads * 2 * LANES), BF16)] * 2,
        grid=(b, t // tm, n_heads // hg),
        in_specs=[
            pl.BlockSpec((None, tm, kk), lambda bi, i, j: (bi, i, 0)),
            pl.BlockSpec((None, tm, LANES), lambda bi, i, j: (bi, i, 0)),
            pl.BlockSpec((kk, tn), lambda bi, i, j: (0, j)),
            pl.BlockSpec((1, LANES), lambda bi, i, j: (0, 0)),
        ],
        out_specs=[pl.BlockSpec((None, tm, tn), lambda bi, i, j: (bi, i, j))] * 2,
        compiler_params=_params(("parallel", "parallel", "arbitrary"), 32 << 20),
        name="mla_kv_up",
    )(ckv, kp, w, g_nope_k.reshape(1, LANES))


def _rope_tables(pos, dim):
    half = dim // 2
    inv = ROPE_THETA ** (-jnp.arange(half, dtype=F32) * 2.0 / dim)
    ang = pos.astype(F32)[:, None] * inv[None, :]
    cos, sin = jnp.cos(ang), jnp.sin(ang)
    pad = ((0, 0), (0, LANES - dim))
    return (jnp.pad(jnp.concatenate([cos, cos], axis=-1), pad),
            jnp.pad(jnp.concatenate([-sin, sin], axis=-1), pad))


def _pad_lanes(a, width):
    return jnp.pad(a, [(0, 0)] * (a.ndim - 1) + [(0, width - a.shape[-1])])


def _pad_rows(a, rows):
    return jnp.pad(a, [(0, 0), (0, rows - a.shape[1])] + [(0, 0)] * (a.ndim - 2))


def _layer_stack(x, c_mod, tok_pos, seq_shape, caches, p):
    bx, tx, d = x.shape
    ba, ta = seq_shape
    depth = p['w_ffn_in'].shape[0]
    n_diff = p['n_diff']
    n_fox = p['n_fox']
    n_mla = p['n_mla']
    past_len = 0 if caches is None else caches[0].shape[2]
    n_k = past_len + ta
    cos128, sin128 = _rope_tables(tok_pos, LANES)
    rope_dim = p['rope_dim']
    cos_r, sin_r = _rope_tables(tok_pos, rope_dim)
    new = [[] for _ in range(7)]

    def mods(l, s):
        m = c_mod[l]
        sh, sc, gt = m[:, 3 * s], m[:, 3 * s + 1], m[:, 3 * s + 2]
        if bx == m.shape[0]:
            return tuple(a[:, None, :] for a in (sh, sc, gt))
        rep = lambda a: jnp.repeat(a, ta, axis=0).reshape(bx, tx, d)
        return rep(sh), rep(sc), rep(gt)

    seq = lambda a: a.reshape(ba, ta, a.shape[-1])
    for l in range(depth):
        i = l // 2
        g = p['norm_gains'][l]
        sh, sc, gt = mods(l, 0)
        x = _ffn(x, sh, sc, gt, g[0], p['w_ffn_in'], p['w_ffn_out'], l, 0, 0.5)
        sh, sc, gt = mods(l, 1)
        if l % 2 == 0:
            (qa, kaf, kab, vaf, vab, qb, kbf, kbb, vbf, vbb, lf) = _even_proj(
                x, sh, sc, g[1], p['w_in_even'][i], p['w_in_forget'][i], cos128, sin128,
                p['qk_norm_even'][i], p['b_forget_pad'][i], 2 * n_diff, n_fox)
            new[0].append(kaf.reshape(ba, ta, n_diff, 2, LANES))
            new[1].append(vaf.reshape(ba, ta, 2, n_diff, LANES).swapaxes(2, 3).reshape(ba, ta, n_diff, 2 * LANES))
            new[2].append(kbf.reshape(ba, ta, n_fox, LANES))
            new[3].append(vbf.reshape(ba, ta, n_fox, LANES))
            new[4].append(seq(lf)[:, :, :n_fox])
            lam_init = 0.8 - 0.6 * math.exp(-0.3 * l)
            if caches is None:
                aq, kk, vv = _fox_prep(lf, kbb, vbb, n_fox)
                oa = _flash_diff(qa, kab, vab, p['diff_lambda'][i], p['diff_subln'][i], lam_init, n_diff, n_k, 0)
                ob = _flash_fox(qb, aq, kk, vv, n_fox, n_k, 0)
            else:
                past = tuple(a[i] for a in caches[:5])
                oa = _dec_diff(seq(qa), past[0], past[1], seq(kab), seq(vab), p['diff_lambda'][i],
                               p['diff_subln'][i], lam_init, n_diff, n_k)
                ob = _dec_fox(seq(qb), past[2], past[3], seq(kbb), seq(vbb), past[4], seq(lf), n_fox, n_k)
            x = _outproj(oa.reshape(bx, tx, -1), ob.reshape(bx, tx, -1), 0, 0, p['w_out_even'][i], x, gt)
        else:
            q_lora, kv_lora = p['q_lora'], p['kv_lora']
            z = _modproj(x, sh, sc, g[1], p['w_in_odd'][i])
            cq, ckvf, ckvb, kpf, kpb = _odd_post(z, cos_r, sin_r, p['mla_cq_norm'][i], p['mla_ckv_norm'][i],
                                                 p['g_rope_pad'][i, 1:2], q_lora, kv_lora, rope_dim)
            new[5].append(seq(ckvf))
            new[6].append(seq(kpf)[:, :, :rope_dim])
            qq = _qup(cq, p['w_uq_pad'][i], cos_r, sin_r, p['mla_qk_norm_nope'][i, 0],
                      p['g_rope_pad'][i, 0:1], n_mla, rope_dim, p['mla_qk_dim'])
            if caches is None:
                kk, v = _kvup(ckvb, kpb, p['w_ukv'][i], p['mla_qk_norm_nope'][i, 1], n_mla)
                o = _flash_mla(qq, kk, v, n_mla, n_k, 0)
            else:
                kp_cache = _pad_lanes(caches[6][i], LANES).astype(BF16)
                o = _dec_mla(seq(qq), caches[5][i], kp_cache, seq(ckvb), seq(kpb), p['w_kn'][i], p['w_v'][i],
                             p['mla_qk_norm_nope'][i, 1], n_mla, n_k)
            o = o.reshape(bx, tx, -1)
            x = _outproj(o, o, 0, 1, p['w_out_odd'][i], x, gt)
        sh, sc, gt = mods(l, 2)
        x = _ffn(x, sh, sc, gt, g[2], p['w_ffn_in'], p['w_ffn_out'], l, 1, 0.5, final_gain=g[3])
    return x, tuple(jnp.stack(lst) for lst in new)


def kernel(x_prompt, x_sample, c_prompt, c_sample, cache_diff_k, cache_diff_v, cache_fox_k, cache_fox_v, cache_fox_logf, cache_mla_ckv, cache_mla_kpe, w_ada, b_ada, norm_gains, w_ffn_in, w_ffn_out, w_in_even, b_forget, qk_norm_even, diff_lambda, diff_subln, w_out_even, w_in_odd, mla_cq_norm, mla_ckv_norm, w_uq, w_ukv, mla_qk_norm_nope, mla_qk_norm_rope, w_out_odd):
    d = x_prompt.shape[-1]
    n_diff, n_fox = cache_diff_k.shape[3], cache_fox_k.shape[3]
    assert cache_diff_k.shape[-1] == LANES and cache_fox_k.shape[-1] == LANES
    q_lora, kv_lora = mla_cq_norm.shape[-1], mla_ckv_norm.shape[-1]
    rope_dim, nope = cache_mla_kpe.shape[-1], mla_qk_norm_nope.shape[-1]
    n_mla = w_uq.shape[-1] // (nope + rope_dim)
    assert nope == LANES and rope_dim <= LANES and w_ukv.shape[-1] == n_mla * 2 * LANES
    n_odd = w_uq.shape[0]
    n_main = w_in_even.shape[-1] - n_fox
    assert n_main == 6 * 2 * n_diff * LANES and n_fox <= LANES

    w_uq_pad = _pad_lanes(w_uq.reshape(n_odd, q_lora, n_mla, nope + rope_dim), 2 * LANES)
    p = {
        'n_diff': n_diff, 'n_fox': n_fox, 'n_mla': n_mla, 'rope_dim': rope_dim,
        'q_lora': q_lora, 'kv_lora': kv_lora, 'mla_qk_dim': nope + rope_dim,
        'norm_gains': norm_gains,
        'w_ffn_in': w_ffn_in.astype(BF16), 'w_ffn_out': w_ffn_out.astype(BF16),
        'w_in_even': w_in_even[:, :, :n_main].astype(BF16),
        'w_in_forget': _pad_lanes(w_in_even[:, :, n_main:], LANES).astype(BF16),
        'b_forget_pad': _pad_lanes(b_forget, LANES)[:, None, :],
        'qk_norm_even': qk_norm_even, 'diff_lambda': diff_lambda, 'diff_subln': diff_subln,
        'w_out_even': w_out_even.astype(BF16),
        'w_in_odd': _pad_lanes(w_in_odd, q_lora + kv_lora + LANES).astype(BF16),
        'mla_cq_norm': mla_cq_norm, 'mla_ckv_norm': mla_ckv_norm,
        'w_uq_pad': w_uq_pad.reshape(n_odd, q_lora, n_mla * 2 * LANES).astype(BF16),
        'w_ukv': w_ukv.astype(BF16),
        'w_kn': w_ukv.reshape(n_odd, kv_lora, n_mla, 2 * LANES)[..., :LANES].reshape(n_odd, kv_lora, -1).astype(BF16),
        'w_v': w_ukv.reshape(n_odd, kv_lora, n_mla, 2 * LANES)[..., LANES:].reshape(n_odd, kv_lora, -1).astype(BF16),
        'mla_qk_norm_nope': mla_qk_norm_nope,
        'g_rope_pad': _pad_lanes(mla_qk_norm_rope, LANES),
        'w_out_odd': w_out_odd.astype(BF16),
    }

    bp, tp = x_prompt.shape[:2]
    bs, ts = x_sample.shape[:2]
    past_len = cache_diff_k.shape[2]
    mod = _ada(jnp.concatenate([c_prompt, c_sample], axis=0), w_ada, b_ada)
    mod = mod.reshape(mod.shape[0], bp + bs, N_MOD, d)

    pos_p = jnp.arange(tp, dtype=jnp.int32)
    y_prompt, st_p = _layer_stack(x_prompt, mod[:, :bp], pos_p, (bp, tp), None, p)

    pos_s = jnp.tile(past_len + jnp.arange(ts, dtype=jnp.int32), bs)
    caches = (cache_diff_k, cache_diff_v, cache_fox_k, cache_fox_v, cache_fox_logf, cache_mla_ckv, cache_mla_kpe)
    y_sample, st_s = _layer_stack(x_sample.reshape(1, bs * ts, d), mod[:, bp:], pos_s, (bs, ts), caches, p)
    return (y_prompt, y_sample.reshape(bs, ts, d)) + st_p + st_s
```

```python
import functools
import math

import numpy as np
import jax
import jax.numpy as jnp
from jax import lax
from jax.experimental import pallas as pl
from jax.experimental.pallas import tpu as pltpu

F32 = jnp.float32
BF16 = jnp.bfloat16

CHUNK = 64
ROPE_THETA = 10000.0
EPS = 1e-6
NEG_INF = -1e30
N_MOD = 9

LANES = 128
SUBLANES = 8
VMEM_CAP_BYTES = 56 * 1024 * 1024

LOG2E = math.log2(math.e)
CHUNK_SHIFT = CHUNK.bit_length() - 1
assert (1 << CHUNK_SHIFT) == CHUNK


def _round_up(n, m):
    return (n + m - 1) // m * m


def _pick_tile(n, target, quantum):
    if n <= target:
        return n
    best = None
    t = quantum
    while t <= target:
        if n % t == 0:
            best = t
        t += quantum
    assert best is not None, (n, target, quantum)
    return best


def _params(semantics, vmem_bytes):
    limit = int(min(max(vmem_bytes, 16 * 1024 * 1024), VMEM_CAP_BYTES))
    return pltpu.CompilerParams(dimension_semantics=semantics, vmem_limit_bytes=limit)


def _rms(x, gain):
    return x * lax.rsqrt(jnp.mean(x * x, axis=-1, keepdims=True) + EPS) * gain


def _silu(g):
    return g / (1.0 + jnp.exp(-g))


def _dot(a, b):
    return jnp.dot(a, b, preferred_element_type=F32)


def _dot_nt(a, b):
    return lax.dot_general(a, b, (((1,), (1,)), ((), ())), preferred_element_type=F32)


def _ada_kernel(c_ref, w_ref, b_ref, o_ref):
    a = _silu(c_ref[...]).astype(BF16)
    o_ref[...] = _dot(a, w_ref[...].astype(BF16)) + b_ref[...]


def _ada(c_all, w_ada, b_ada):
    depth, d, n = w_ada.shape
    r = c_all.shape[0]
    tn = _pick_tile(n, 1024, LANES)
    return pl.pallas_call(
        _ada_kernel,
        out_shape=jax.ShapeDtypeStruct((depth, r, n), F32),
        grid=(depth, n // tn),
        in_specs=[
            pl.BlockSpec((r, d), lambda l, j: (0, 0)),
            pl.BlockSpec((None, d, tn), lambda l, j: (l, 0, j)),
            pl.BlockSpec((None, 1, tn), lambda l, j: (l, 0, j)),
        ],
        out_specs=pl.BlockSpec((None, r, tn), lambda l, j: (l, 0, j)),
        compiler_params=_params(("arbitrary", "arbitrary"), 2 * d * tn * 4 + 3 * d * tn * 2 + (4 << 20)),
        name="ada_mod",
    )(c_all, w_ada, b_ada.reshape(depth, 1, n))


def _mod_spec(mod, tm):
    d = mod.shape[-1]
    if mod.shape[1] == 1:
        return pl.BlockSpec((None, 1, d), lambda b, i, j: (b, 0, 0))
    return pl.BlockSpec((None, tm, d), lambda b, i, j: (b, i, 0))


def _ffn_kernel(x_ref, sh_ref, sc_ref, gt_ref, g_ref, wgu_ref, wo_ref, *rest,
                gate_mul, final_norm):
    if final_norm:
        gf_ref, o_ref, h_ref, acc_ref = rest
    else:
        o_ref, h_ref, acc_ref = rest
    f = pl.program_id(2)

    @pl.when(f == 0)
    def _():
        h = _rms(x_ref[...], g_ref[...]) * (1.0 + sc_ref[...]) + sh_ref[...]
        h_ref[...] = h.astype(BF16)
        acc_ref[...] = jnp.zeros_like(acc_ref)

    tf = wo_ref.shape[0]
    gu = _dot(h_ref[...], wgu_ref[...])
    a = (_silu(gu[:, 0:tf]) * gu[:, tf:2 * tf]).astype(BF16)
    acc_ref[...] += _dot(a, wo_ref[...])

    @pl.when(f == pl.num_programs(2) - 1)
    def _():
        xn = x_ref[...] + (gate_mul * gt_ref[...]) * acc_ref[...]
        if final_norm:
            xn = _rms(xn, gf_ref[...])
        o_ref[...] = xn


def _ffn_tile(ff):
    return _pick_tile(ff, 512, LANES)


def _gate_up_kernel(g_ref, u_ref, o_ref):
    tf = g_ref.shape[1]
    o_ref[:, 0:tf] = g_ref[...].astype(BF16)
    o_ref[:, tf:2 * tf] = u_ref[...].astype(BF16)


def _interleave_gate_up(w_in):
    nl, ns, d, ff2 = w_in.shape
    tf = _ffn_tile(ff2 // 2)
    nf = ff2 // 2 // tf
    return pl.pallas_call(
        _gate_up_kernel,
        out_shape=jax.ShapeDtypeStruct(w_in.shape, BF16),
        grid=(nl, ns, nf),
        in_specs=[pl.BlockSpec((None, None, d, tf), lambda l, s, f: (l, s, 0, f)),
                  pl.BlockSpec((None, None, d, tf), lambda l, s, f: (l, s, 0, nf + f))],
        out_specs=pl.BlockSpec((None, None, d, 2 * tf), lambda l, s, f: (l, s, 0, f)),
        compiler_params=_params(("parallel", "parallel", "parallel"), 4 * d * tf * 4 + 2 * d * 2 * tf * 2 + (8 << 20)),
        name="ffn_weight_prep",
    )(w_in, w_in)


def _ffn(x, sh, sc, gt, gain, w_in, w_out, layer, sub, gate_mul, final_gain=None):
    b, t, d = x.shape
    ff = w_out.shape[2]
    tm = _pick_tile(t, 512, 16)
    tf = _ffn_tile(ff)
    nf = ff // tf
    in_specs = [
        pl.BlockSpec((None, tm, d), lambda bi, i, f: (bi, i, 0)),
        _mod_spec(sh, tm), _mod_spec(sc, tm), _mod_spec(gt, tm),
        pl.BlockSpec((1, d), lambda bi, i, f: (0, 0)),
        pl.BlockSpec((None, None, d, 2 * tf), lambda bi, i, f: (layer, sub, 0, f)),
        pl.BlockSpec((None, None, tf, d), lambda bi, i, f: (layer, sub, f, 0)),
    ]
    args = [x, sh, sc, gt, gain.reshape(1, d), w_in, w_out]
    if final_gain is not None:
        in_specs.append(pl.BlockSpec((1, d), lambda bi, i, f: (0, 0)))
        args.append(final_gain.reshape(1, d))
    vmem = (4 * tm * d * 4 + tm * d * 2 + tm * d * 4 + 6 * d * tf * 2 + 4 * tm * tf * 4
            + 6 * tm * d * 4 * (sh.shape[1] != 1) + (4 << 20))
    return pl.pallas_call(
        functools.partial(_ffn_kernel, gate_mul=gate_mul, final_norm=final_gain is not None),
        out_shape=jax.ShapeDtypeStruct((b, t, d), F32),
        grid=(b, t // tm, nf),
        in_specs=in_specs,
        out_specs=pl.BlockSpec((None, tm, d), lambda bi, i, f: (bi, i, 0)),
        scratch_shapes=[pltpu.VMEM((tm, d), BF16), pltpu.VMEM((tm, d), F32)],
        compiler_params=_params(("parallel", "parallel", "arbitrary"), vmem),
        name="ffn",
    )(*args)


def _modproj_kernel(x_ref, sh_ref, sc_ref, g_ref, w_ref, o_ref, h_ref):
    @pl.when(pl.program_id(2) == 0)
    def _():
        h = _rms(x_ref[...], g_ref[...]) * (1.0 + sc_ref[...]) + sh_ref[...]
        h_ref[...] = h.astype(BF16)

    o_ref[...] = _dot(h_ref[...], w_ref[...])


def _modproj(x, sh, sc, gain, w):
    b, t, d = x.shape
    n = w.shape[1]
    tm = _pick_tile(t, 1024, 16)
    tn = _pick_tile(n, 1280, LANES)
    vmem = (2 * tm * d * 4 + tm * d * 2 + 2 * d * tn * 2 + 3 * tm * tn * 4
            + 4 * tm * d * 4 * (sh.shape[1] != 1) + 3 * tm * d * 4 + (4 << 20))
    return pl.pallas_call(
        _modproj_kernel,
        out_shape=jax.ShapeDtypeStruct((b, t, n), F32),
        grid=(b, t // tm, n // tn),
        in_specs=[
            pl.BlockSpec((None, tm, d), lambda bi, i, j: (bi, i, 0)),
            _mod_spec(sh, tm), _mod_spec(sc, tm),
            pl.BlockSpec((1, d), lambda bi, i, j: (0, 0)),
            pl.BlockSpec((d, tn), lambda bi, i, j: (0, j)),
        ],
        out_specs=pl.BlockSpec((None, tm, tn), lambda bi, i, j: (bi, i, j)),
        scratch_shapes=[pltpu.VMEM((tm, d), BF16)],
        compiler_params=_params(("parallel", "parallel", "arbitrary"), vmem),
        name="mod_proj",
    )(x, sh, sc, gain.reshape(1, d), w)


def _outproj_kernel(a1_ref, a2_ref, w1_ref, w2_ref, x_ref, gt_ref, o_ref):
    y = _dot(a1_ref[...], w1_ref[...]) + _dot(a2_ref[...], w2_ref[...])
    o_ref[...] = x_ref[...] + gt_ref[...] * y


def _outproj(a1, a2, blk1, blk2, w, x, gt):
    b, t, d = x.shape
    kh = w.shape[0] // 2
    tm = _pick_tile(t, 512, 16)
    vmem = 4 * tm * kh * 2 + 4 * kh * d * 2 + 5 * tm * d * 4 + 2 * tm * d * 4 * (gt.shape[1] != 1) + (4 << 20)
    return pl.pallas_call(
        _outproj_kernel,
        out_shape=jax.ShapeDtypeStruct((b, t, d), F32),
        grid=(b, t // tm, 1),
        in_specs=[
            pl.BlockSpec((None, tm, kh), lambda bi, i, j: (bi, i, blk1)),
            pl.BlockSpec((None, tm, kh), lambda bi, i, j: (bi, i, blk2)),
            pl.BlockSpec((kh, d), lambda bi, i, j: (0, 0)),
            pl.BlockSpec((kh, d), lambda bi, i, j: (1, 0)),
            pl.BlockSpec((None, tm, d), lambda bi, i, j: (bi, i, 0)),
            _mod_spec(gt, tm),
        ],
        out_specs=pl.BlockSpec((None, tm, d), lambda bi, i, j: (bi, i, 0)),
        compiler_params=_params(("parallel", "parallel", "arbitrary"), vmem),
        name="out_proj",
    )(a1, a2, w, w, x, gt)


def _rope128(y, cos, sin):
    return y * cos + pltpu.roll(y, LANES // 2, 1) * sin


def _log_sigmoid(x):
    return jnp.minimum(x, 0.0) - jnp.log(1.0 + jnp.exp(-jnp.abs(x)))


def _even_proj_kernel(x_ref, sh_ref, sc_ref, g_ref, w_ref, wf_ref, cos_ref, sin_ref, gn_ref, bf_ref,
                      qa_ref, kaf_ref, kab_ref, vaf_ref, vab_ref,
                      qb_ref, kbf_ref, kbb_ref, vbf_ref, vbb_ref, lf_ref, h_ref,
                      *, n_grp, n_forget, q_scale):
    j = pl.program_id(2)
    tm, hd = x_ref.shape[0], LANES

    @pl.when(j == 0)
    def _():
        h = (_rms(x_ref[...], g_ref[...]) * (1.0 + sc_ref[...]) + sh_ref[...]).astype(BF16)
        h_ref[...] = h
        fg = _dot(h, wf_ref[...]) + bf_ref[...]
        lane = lax.broadcasted_iota(jnp.int32, fg.shape, 1)
        lf_ref[...] = jnp.where(lane < n_forget, _log_sigmoid(fg), 0.0)

    def heads():
        h = h_ref[...]
        for c in range(n_grp // 2):
            z = _dot(h, w_ref[:, 2 * c * hd:(2 * c + 2) * hd])
            yield z[:, 0:hd]
            yield z[:, hd:2 * hd]

    @pl.when(j == 0)
    def _():
        for i, z in enumerate(heads()):
            q = _rope128(_rms(z, gn_ref[0:1, :]), cos_ref[...], sin_ref[...])
            qa_ref[:, i * hd:(i + 1) * hd] = (q * q_scale).astype(BF16)

    @pl.when(j == 1)
    def _():
        for i, z in enumerate(heads()):
            k = _rope128(_rms(z, gn_ref[1:2, :]), cos_ref[...], sin_ref[...])
            kaf_ref[pl.ds(i, tm, stride=n_grp), :] = k
            kab_ref[:, i * hd:(i + 1) * hd] = k.astype(BF16)

    @pl.when(j == 2)
    def _():
        for i, z in enumerate(heads()):
            vaf_ref[pl.ds((i % 2) * (n_grp // 2) + i // 2, tm, stride=n_grp), :] = z
            vab_ref[:, i * hd:(i + 1) * hd] = z.astype(BF16)

    @pl.when(j == 3)
    def _():
        for i, z in enumerate(heads()):
            qb_ref[:, i * hd:(i + 1) * hd] = (_rms(z, gn_ref[2:3, :]) * q_scale).astype(BF16)

    @pl.when(j == 4)
    def _():
        for i, z in enumerate(heads()):
            k = _rms(z, gn_ref[3:4, :])
            kbf_ref[:, i * hd:(i + 1) * hd] = k
            kbb_ref[:, i * hd:(i + 1) * hd] = k.astype(BF16)

    @pl.when(j == 5)
    def _():
        for i, z in enumerate(heads()):
            vbf_ref[:, i * hd:(i + 1) * hd] = z
            vbb_ref[:, i * hd:(i + 1) * hd] = z.astype(BF16)


def _even_proj(x, sh, sc, gain, w_main, w_forget, cos, sin, qk_gain, b_forget_pad, n_grp, n_fox):
    b, t, d = x.shape
    wg = n_grp * LANES
    assert w_main.shape[1] == 6 * wg and n_fox == n_grp
    tm = _pick_tile(t, 512, 16)
    row = lambda w: pl.BlockSpec((None, tm, w), lambda bi, i, j: (bi, i, 0))
    tab = pl.BlockSpec((tm, LANES), lambda bi, i, j: (i, 0))
    const = lambda shape: pl.BlockSpec(shape, lambda bi, i, j: (0,) * len(shape))
    outs = [(wg, BF16), (wg, F32), (wg, BF16), (wg, F32), (wg, BF16),
            (wg, BF16), (wg, F32), (wg, BF16), (wg, F32), (wg, BF16), (LANES, F32)]
    out_shape = [jax.ShapeDtypeStruct((b, t, w), dt) for w, dt in outs]
    out_specs = [row(w) for w, _ in outs]
    for idx in (1, 3):
        out_shape[idx] = jax.ShapeDtypeStruct((b, t * n_grp, LANES), F32)
        out_specs[idx] = pl.BlockSpec((None, tm * n_grp, LANES), lambda bi, i, j: (bi, i, 0))
    vmem = (2 * tm * d * 4 + tm * d * 2 + 4 * d * wg * 2 + 2 * sum(tm * w * jnp.dtype(dt).itemsize for w, dt in outs)
            + 4 * tm * wg * 4 + 4 * tm * d * 4 * (sh.shape[1] != 1) + (4 << 20))
    return pl.pallas_call(
        functools.partial(_even_proj_kernel, n_grp=n_grp, n_forget=n_fox, q_scale=LANES ** -0.5 * LOG2E),
        out_shape=out_shape,
        grid=(b, t // tm, 6),
        in_specs=[pl.BlockSpec((None, tm, d), lambda bi, i, j: (bi, i, 0)),
                  _mod_spec(sh, tm), _mod_spec(sc, tm), const((1, d)),
                  pl.BlockSpec((d, wg), lambda bi, i, j: (0, j)), const((d, LANES)),
                  tab, tab, const((4, LANES)), const((1, LANES))],
        out_specs=out_specs,
        scratch_shapes=[pltpu.VMEM((tm, d), BF16)],
        compiler_params=_params(("parallel", "parallel", "arbitrary"), vmem),
        name="even_proj",
    )(x, sh, sc, gain.reshape(1, d), w_main, w_forget, cos, sin, qk_gain, b_forget_pad)


def _split3(x):
    hi = x.astype(BF16)
    r = x - hi.astype(F32)
    mid = r.astype(BF16)
    lo = (r - mid.astype(F32)).astype(BF16)
    return hi, mid, lo


def _fox_prep_kernel(lf_ref, k_ref, v_ref, aq_ref, kk_ref, vv_ref, carry_ref, *, n_heads):
    @pl.when(pl.program_id(1) == 0)
    def _():
        carry_ref[...] = jnp.zeros_like(carry_ref)

    x = lf_ref[...]
    tb = x.shape[0]
    r = lax.broadcasted_iota(jnp.int32, (tb, tb), 0)
    c = lax.broadcasted_iota(jnp.int32, (tb, tb), 1)
    tri = jnp.where(r >= c, 1.0, 0.0).astype(BF16)
    hi, mid, lo = _split3(x)
    cum = _dot(tri, hi) + _dot(tri, mid) + _dot(tri, lo) + carry_ref[...]
    carry_ref[...] = cum[tb - 1:tb, :]
    fh, fm, fl = (p.astype(F32) for p in _split3(cum * LOG2E))
    lane = lax.broadcasted_iota(jnp.int32, (tb, LANES), 1)
    ones_q = jnp.where((lane >= 3) & (lane < 6), 1.0, 0.0)
    ones_k = jnp.where(lane < 3, 1.0, 0.0)
    for h in range(n_heads):
        a, m, l = fh[:, h:h + 1], fm[:, h:h + 1], fl[:, h:h + 1]
        aq = jnp.where(lane == 0, a, jnp.where(lane == 1, m, jnp.where(lane == 2, l, ones_q)))
        ak = jnp.where(lane == 3, -a, jnp.where(lane == 4, -m, jnp.where(lane == 5, -l, ones_k)))
        aq_ref[:, h * LANES:(h + 1) * LANES] = aq.astype(BF16)
        kk_ref[:, 2 * h * LANES:(2 * h + 1) * LANES] = k_ref[:, h * LANES:(h + 1) * LANES]
        kk_ref[:, (2 * h + 1) * LANES:(2 * h + 2) * LANES] = ak.astype(BF16)
        vv_ref[:, 2 * h * LANES:(2 * h + 1) * LANES] = v_ref[:, h * LANES:(h + 1) * LANES]
        vv_ref[:, (2 * h + 1) * LANES:(2 * h + 2) * LANES] = jnp.ones((tb, LANES), BF16)


def _fox_prep(logf_pad, k_all, v_all, n_heads):
    b, t, _ = logf_pad.shape
    tb = _pick_tile(t, 512, LANES)
    w = n_heads * LANES
    narrow = pl.BlockSpec((None, tb, w), lambda bi, i: (bi, i, 0))
    wide = pl.BlockSpec((None, tb, 2 * w), lambda bi, i: (bi, i, 0))
    return pl.pallas_call(
        functools.partial(_fox_prep_kernel, n_heads=n_heads),
        out_shape=[jax.ShapeDtypeStruct((b, t, w), BF16), jax.ShapeDtypeStruct((b, t, 2 * w), BF16),
                   jax.ShapeDtypeStruct((b, t, 2 * w), BF16)],
        grid=(b, t // tb),
        in_specs=[pl.BlockSpec((None, tb, LANES), lambda bi, i: (bi, i, 0)), narrow, narrow],
        out_specs=[narrow, wide, wide],
        scratch_shapes=[pltpu.VMEM((1, LANES), F32)],
        compiler_params=_params(("parallel", "arbitrary"), 32 << 20),
        name="fox_prep",
    )(logf_pad, k_all, v_all)


FLAG_FIRST, FLAG_LAST = 1, 2
KIND_SHIFT = 2
KIND_FULL, KIND_MASK, KIND_DIAG = 0, 1, 2
SCORE_LOOKAHEAD = 1


def _pair_table(n_q, n_k, tq, tk, q_off, causal):
    tk_pad = _round_up(n_k, tk)
    aligned = tq == tk and q_off % tq == 0
    qi, kj, fl = [], [], []
    for i in range(n_q // tq):
        qmin, qmax = q_off + i * tq, q_off + (i + 1) * tq - 1
        row = []
        for j in range(tk_pad // tk):
            kmin, kmax = j * tk, min((j + 1) * tk, n_k) - 1
            if kmin >= n_k:
                continue
            if causal:
                any_vis, all_vis = kmin <= qmax, kmax <= qmin
            else:
                any_vis, all_vis = kmin // CHUNK <= qmax // CHUNK, kmax // CHUNK <= qmin // CHUNK
            all_vis = all_vis and (j + 1) * tk <= n_k
            if any_vis:
                diag = aligned and kmin == qmin and (j + 1) * tk <= n_k
                row.append((j, KIND_FULL if all_vis else KIND_DIAG if diag else KIND_MASK))
        assert row and row[0][0] == 0
        for idx, (j, kind) in enumerate(row):
            qi.append(i)
            kj.append(j)
            fl.append((kind << KIND_SHIFT) | (FLAG_FIRST if idx == 0 else 0) | (FLAG_LAST if idx == len(row) - 1 else 0))
    kinds = sorted({f >> KIND_SHIFT for f in fl})
    as_arr = lambda v: jnp.asarray(np.array(v, np.int32))
    return as_arr(qi), as_arr(kj), as_arr(fl), kinds


def _visible(shape, qpos0, kpos0, n_k, causal, row_period=None):
    if row_period is None:
        rows = qpos0 + lax.broadcasted_iota(jnp.int32, shape, 0)
    else:
        one = lax.broadcasted_iota(jnp.int32, (row_period, shape[1]), 0)
        rows = qpos0 + jnp.concatenate([one] * (shape[0] // row_period), axis=0)
    cols = kpos0 + lax.broadcasted_iota(jnp.int32, shape, 1)
    if causal:
        ok = cols <= rows
    else:
        ok = (cols >> CHUNK_SHIFT) <= (rows >> CHUNK_SHIFT)
    return ok if n_k is None else ok & (cols < n_k)


def _lane_tile(x, n):
    return x if n == LANES else jnp.concatenate([x] * (n // LANES), axis=1)


def _lane_fold(p):
    acc = p[:, 0:LANES]
    for c in range(1, p.shape[1] // LANES):
        acc = acc + p[:, c * LANES:(c + 1) * LANES]
    return acc


def _attend(s, v, m_ref, l_ref, acc_ref, idx):
    m_prev = m_ref[idx]
    m_new = jnp.maximum(m_prev, jnp.max(s, axis=1, keepdims=True))
    alpha = jnp.exp2(m_prev - m_new)
    p = jnp.exp2(s - _lane_tile(m_new, s.shape[1]))
    if l_ref is not None:
        l_ref[idx] = alpha * l_ref[idx] + _lane_fold(p)
    acc_ref[idx] = _lane_tile(alpha, v.shape[1]) * acc_ref[idx] + _dot(p.astype(v.dtype), v)
    m_ref[idx] = m_new


def _flash_frame(qi_ref, kj_ref, fl_ref, m_ref, l_ref, acc_ref, scores, update, finish, first=None, *,
                 tq, tk, rs, rs_full, q_off, n_k, causal, kinds):
    n = pl.program_id(2)
    flags = fl_ref[n]
    kind = flags >> KIND_SHIFT

    @pl.when((flags & FLAG_FIRST) != 0)
    def _():
        m_ref[...] = jnp.full_like(m_ref, NEG_INF)
        if l_ref is not None:
            l_ref[...] = jnp.zeros_like(l_ref)
        acc_ref[...] = jnp.zeros_like(acc_ref)
        if first is not None:
            first()

    qpos0 = q_off + qi_ref[n] * tq
    kpos0 = kj_ref[n] * tk
    rs_masked = rs

    def run(k):
        rs = rs_full if k == KIND_FULL else rs_masked

        def masked_scores(r):
            n_keys = (r + 1) * rs if k == KIND_DIAG else tk
            rows = slice(r * rs, (r + 1) * rs)
            tiles = scores(rows, n_keys)
            if k == KIND_DIAG:
                lo = r * rs
                vis = _visible((rs, rs), qpos0 + lo, kpos0 + lo, None, causal)
                blks = [jnp.where(vis, s[:, lo:], NEG_INF) for s in tiles]
                tiles = blks if lo == 0 else [jnp.concatenate([s[:, :lo], b], axis=1) for s, b in zip(tiles, blks)]
            elif k == KIND_MASK:
                vis = _visible(tiles[0].shape, qpos0 + r * rs, kpos0, n_k, causal)
                tiles = [jnp.where(vis, s, NEG_INF) for s in tiles]
            return rows, n_keys, tiles

        order = list(range(tq // rs))
        if k == KIND_DIAG:
            order.reverse()
        pending = [masked_scores(r) for r in order[:SCORE_LOOKAHEAD]]
        for idx in range(len(order)):
            rows, n_keys, tiles = pending.pop(0)
            if idx + SCORE_LOOKAHEAD < len(order):
                pending.append(masked_scores(order[idx + SCORE_LOOKAHEAD]))
            update(rows, n_keys, tiles)

    for k in kinds:
        pl.when(kind == k)(functools.partial(run, k))

    @pl.when((flags & FLAG_LAST) != 0)
    def _():
        finish()


def _diff_kernel(qi_ref, kj_ref, fl_ref, q_ref, k_ref, v_ref, lam_ref, sub_ref, o_ref,
                 m_ref, l_ref, acc_ref, *, lam_init, **frame):
    hd = LANES

    def scores(rows, n_keys):
        return [_dot_nt(q_ref[rows, c * hd:(c + 1) * hd], k_ref[0:n_keys, c * hd:(c + 1) * hd]) for c in range(2)]

    def update(rows, n_keys, tiles):
        v = v_ref[0:n_keys, :]
        for c in range(2):
            _attend(tiles[c], v, m_ref, l_ref, acc_ref, (c, rows))

    def finish():
        lp = lam_ref[...]
        lam = (jnp.exp(jnp.sum(lp[0:1] * lp[1:2], axis=-1, keepdims=True))
               - jnp.exp(jnp.sum(lp[2:3] * lp[3:4], axis=-1, keepdims=True)) + lam_init)
        l0 = jnp.sum(l_ref[0], axis=1, keepdims=True)
        l1 = jnp.sum(l_ref[1], axis=1, keepdims=True)
        o = acc_ref[0] / l0 - lam * (acc_ref[1] / l1)
        o_ref[...] = (_rms(o, sub_ref[...]) * (1.0 - lam_init)).astype(o_ref.dtype)

    _flash_frame(qi_ref, kj_ref, fl_ref, m_ref, l_ref, acc_ref, scores, update, finish, **frame)


def _ones_finish(acc_ref, o_ref):
    o_ref[...] = (acc_ref[:, 0:LANES] / acc_ref[:, LANES:2 * LANES]).astype(o_ref.dtype)


def _fox_kernel(qi_ref, kj_ref, fl_ref, q_ref, aq_ref, k_ref, v_ref, o_ref,
                m_ref, acc_ref, qq_ref, **frame):
    def first():
        qq_ref[:, 0:LANES] = q_ref[...]
        qq_ref[:, LANES:2 * LANES] = aq_ref[...]

    def scores(rows, n_keys):
        return [_dot_nt(qq_ref[rows, :], k_ref[0:n_keys, :])]

    def update(rows, n_keys, tiles):
        _attend(tiles[0], v_ref[0:n_keys, :], m_ref, None, acc_ref, rows)

    _flash_frame(qi_ref, kj_ref, fl_ref, m_ref, None, acc_ref, scores, update,
                 functools.partial(_ones_finish, acc_ref, o_ref), first, **frame)


def _mla_kernel(qi_ref, kj_ref, fl_ref, q_ref, k_ref, v_ref, o_ref,
                m_ref, acc_ref, **frame):
    def scores(rows, n_keys):
        return [_dot_nt(q_ref[rows, :], k_ref[0:n_keys, :])]

    def update(rows, n_keys, tiles):
        _attend(tiles[0], v_ref[0:n_keys, :], m_ref, None, acc_ref, rows)

    _flash_frame(qi_ref, kj_ref, fl_ref, m_ref, None, acc_ref, scores, update,
                 functools.partial(_ones_finish, acc_ref, o_ref), **frame)


def _flash_tiles(n_q, n_k_pad, tile):
    tq = _pick_tile(n_q, tile, 16)
    tk = next((t for t in (tile, tile // 2, tile // 4) if n_k_pad % t == 0), None) or _pick_tile(n_k_pad, tile, LANES)
    rs = _pick_tile(tq, 256, 16)
    return tq, tk, rs


def _flash_call(kernel, n_heads, operands, n_q, n_k_pad, out_width, scratch, n_k, q_off, causal, name,
                tile=2048, **kw):
    b = operands[0][0].shape[0]
    tq, tk, rs = _flash_tiles(n_q, n_k_pad, tile)
    qi, kj, fl, kinds = _pair_table(n_q, n_k, tq, tk, q_off, causal)
    if KIND_DIAG in kinds:
        assert rs % CHUNK == 0 and rs % LANES == 0

    in_specs, args = [], []
    for op in operands:
        a = op[0]
        if len(op) == 1:
            in_specs.append(pl.BlockSpec(a.shape, lambda bi, h, n, qi, kj, fl: (0, 0)))
        elif op[2]:
            in_specs.append(pl.BlockSpec((None, tq, op[1]), lambda bi, h, n, qi, kj, fl: (bi, qi[n], h)))
        else:
            in_specs.append(pl.BlockSpec((None, tk, op[1]), lambda bi, h, n, qi, kj, fl: (bi, kj[n], h)))
        args.append(a)
    vmem = 2 * tq * tk * 4 + 8 * max(tq, tk) * 2 * LANES * 2 * len(operands) + 10 * tq * 2 * LANES * 4 + (8 << 20)
    grid_spec = pltpu.PrefetchScalarGridSpec(
        num_scalar_prefetch=3,
        grid=(b, n_heads, int(qi.shape[0])),
        in_specs=in_specs,
        out_specs=pl.BlockSpec((None, tq, out_width), lambda bi, h, n, qi, kj, fl: (bi, qi[n], h)),
        scratch_shapes=scratch(tq),
    )
    return pl.pallas_call(
        functools.partial(kernel, tq=tq, tk=tk, rs=rs, rs_full=_pick_tile(tq, 2 * rs, 16), q_off=q_off, n_k=n_k, causal=causal, kinds=kinds, **kw),
        out_shape=jax.ShapeDtypeStruct((b, n_q, n_heads * out_width), BF16),
        grid_spec=grid_spec,
        compiler_params=_params(("parallel", "parallel", "arbitrary"), vmem),
        name=name,
    )(qi, kj, fl, *args)


def _flash_diff(q, k, v, lam_p, subln, lam_init, n_heads, n_k, q_off):
    w = 2 * LANES
    scratch = lambda tq: [pltpu.VMEM((2, tq, LANES), F32), pltpu.VMEM((2, tq, LANES), F32), pltpu.VMEM((2, tq, w), F32)]
    ops = [(q, w, True), (k, w, False), (v, w, False), (lam_p,), (subln.reshape(1, w),)]
    return _flash_call(_diff_kernel, n_heads, ops, q.shape[1], k.shape[1], w, scratch, n_k, q_off, False,
                       "diff_attn", tile=2048, lam_init=lam_init)


def _flash_fox(q, aq, kk, vv, n_heads, n_k, q_off):
    w = LANES
    scratch = lambda tq: [pltpu.VMEM((tq, LANES), F32), pltpu.VMEM((tq, 2 * w), F32), pltpu.VMEM((tq, 2 * w), BF16)]
    ops = [(q, w, True), (aq, w, True), (kk, 2 * w, False), (vv, 2 * w, False)]
    return _flash_call(_fox_kernel, n_heads, ops, q.shape[1], kk.shape[1], w, scratch, n_k, q_off, True, "fox_attn")


def _flash_mla(qq, kk, vv, n_heads, n_k, q_off):
    w = LANES
    scratch = lambda tq: [pltpu.VMEM((tq, LANES), F32), pltpu.VMEM((tq, 2 * w), F32)]
    ops = [(qq, 2 * w, True), (kk, 2 * w, False), (vv, 2 * w, False)]
    return _flash_call(_mla_kernel, n_heads, ops, qq.shape[1], kk.shape[1], w, scratch, n_k, q_off, False, "mla_attn")


NEW_ROWS = LANES


def _cached_frame(m_ref, l_ref, acc_ref, chains, finish, *, nkb, tk, ta, q_off, n_k, causal):
    n = pl.program_id(1)

    @pl.when(n == 0)
    def _():
        m_ref[...] = jnp.full_like(m_ref, NEG_INF)
        l_ref[...] = jnp.zeros_like(l_ref)
        acc_ref[...] = jnp.zeros_like(acc_ref)

    @pl.when(n < nkb)
    def _():
        chains(True, tk, None)

    @pl.when(n == nkb)
    def _():
        mask_fn = lambda s: jnp.where(_visible(s.shape, q_off, q_off, n_k, causal, ta), s, NEG_INF)
        chains(False, NEW_ROWS, mask_fn)
        finish()


def _dec_diff_kernel(q_ref, kc_ref, vc_ref, kn_ref, vn_ref, lam_ref, sub_ref, o_ref,
                     m_ref, l_ref, acc_ref, *, n_heads, lam_init, **frame):
    hd, g, tk = LANES, 2 * n_heads, frame['tk']

    def chains(cached, n_keys, mask_fn):
        for h in range(n_heads):
            if cached:
                v = jnp.concatenate([vc_ref[pl.ds(h, tk, stride=g), :],
                                     vc_ref[pl.ds(n_heads + h, tk, stride=g), :]],
                                    axis=1).astype(BF16)
            else:
                v = vn_ref[:, 2 * h * hd:(2 * h + 2) * hd]
            for c in range(2):
                j = 2 * h + c
                k = kc_ref[pl.ds(j, tk, stride=g), :].astype(BF16) if cached else kn_ref[:, j * hd:(j + 1) * hd]
                s = _dot_nt(q_ref[:, j * hd:(j + 1) * hd], k)
                if mask_fn is not None:
                    s = mask_fn(s)
                _attend(s, v, m_ref, l_ref, acc_ref, j)

    def finish():
        lp = lam_ref[...]
        lam = (jnp.exp(jnp.sum(lp[0:1] * lp[1:2], axis=-1, keepdims=True))
               - jnp.exp(jnp.sum(lp[2:3] * lp[3:4], axis=-1, keepdims=True)) + lam_init)
        for h in range(n_heads):
            l0 = jnp.sum(l_ref[2 * h], axis=1, keepdims=True)
            l1 = jnp.sum(l_ref[2 * h + 1], axis=1, keepdims=True)
            o = acc_ref[2 * h] / l0 - lam * (acc_ref[2 * h + 1] / l1)
            o_ref[:, 2 * h * hd:(2 * h + 2) * hd] = (_rms(o, sub_ref[...]) * (1.0 - lam_init)).astype(o_ref.dtype)

    _cached_frame(m_ref, l_ref, acc_ref, chains, finish, **frame)


def _dec_fox_kernel(q_ref, kc_ref, vc_ref, kn_ref, vn_ref, fq_ref, fk_ref, o_ref,
                    m_ref, l_ref, acc_ref, *, n_heads, **frame):
    hd, g, tk = LANES, n_heads, frame['tk']

    def chains(cached, n_keys, mask_fn):
        for h in range(n_heads):
            if cached:
                k = kc_ref[pl.ds(h, tk, stride=g), :].astype(BF16)
                v = vc_ref[pl.ds(h, tk, stride=g), :].astype(BF16)
            else:
                k = kn_ref[:, h * hd:(h + 1) * hd]
                v = vn_ref[:, h * hd:(h + 1) * hd]
            bias = (fq_ref[:, h:h + 1] - fk_ref[h:h + 1, 0:n_keys]) * LOG2E
            s = _dot_nt(q_ref[:, h * hd:(h + 1) * hd], k) + bias
            if mask_fn is not None:
                s = mask_fn(s)
            _attend(s, v, m_ref, l_ref, acc_ref, h)

    def finish():
        for h in range(n_heads):
            o = acc_ref[h] / jnp.sum(l_ref[h], axis=1, keepdims=True)
            o_ref[:, h * hd:(h + 1) * hd] = o.astype(o_ref.dtype)

    _cached_frame(m_ref, l_ref, acc_ref, chains, finish, **frame)


def _cached_attn(kernel, q, cache_k, cache_v, new_k, new_v, extra, extra_specs, groups, v_width, n_k, causal,
                 name, **kw):
    b, ta, _ = q.shape
    past = cache_k.shape[1]
    tk = _pick_tile(past, 1024, LANES)
    nkb = past // tk
    kc = cache_k.reshape(b, past * groups, LANES)
    if cache_v.shape[-1] == 2 * LANES:
        vc = cache_v.reshape(b, past, groups // 2, 2, LANES).swapaxes(2, 3).reshape(b, past * groups, LANES)
    else:
        vc = cache_v.reshape(b, past * groups, LANES)
    kn, vn = _pad_rows(new_k, NEW_ROWS), _pad_rows(new_v, NEW_ROWS)
    whole = lambda a: pl.BlockSpec((None,) + a.shape[1:], lambda bi, n: (bi,) + (0,) * (a.ndim - 1))
    cache = pl.BlockSpec((None, tk * groups, LANES), lambda bi, n: (bi, jnp.minimum(n, nkb - 1), 0))
    out_w = q.shape[2]
    return pl.pallas_call(
        functools.partial(kernel, nkb=nkb, tk=tk, ta=ta, q_off=past, n_k=n_k, causal=causal, **kw),
        out_shape=jax.ShapeDtypeStruct((b, ta, out_w), BF16),
        grid=(b, nkb + 1),
        in_specs=[whole(q), cache, cache, whole(kn), whole(vn)] + extra_specs(tk),
        out_specs=pl.BlockSpec((None, ta, out_w), lambda bi, n: (bi, 0, 0)),
        scratch_shapes=[pltpu.VMEM((groups, ta, LANES), F32), pltpu.VMEM((groups, ta, LANES), F32),
                        pltpu.VMEM((groups, ta, v_width), F32)],
        compiler_params=_params(("parallel", "arbitrary"), 4 * tk * groups * LANES * 4 + (16 << 20)),
        name=name,
    )(q, kc, vc, kn, vn, *extra)


def _dec_mla_kernel(q_ref, ckv_ref, kp_ref, ckvn_ref, kpn_ref, wk_ref, wv_ref, gk_ref, o_ref,
                    m_ref, l_ref, acc_ref, s_ref, qp_ref, *, n_heads, **frame):
    hd, ta = LANES, frame['ta']

    @pl.when(pl.program_id(1) == 0)
    def _():
        for h in range(n_heads):
            qp_ref[h * ta:(h + 1) * ta, :] = q_ref[:, (2 * h + 1) * hd:(2 * h + 2) * hd]

    def chains(cached, n_keys, mask_fn):
        ckv = ckv_ref[...].astype(BF16) if cached else ckvn_ref[...]
        kp = kp_ref[...] if cached else kpn_ref[...]
        kvn = _dot(ckv, wk_ref[...])
        for h in range(n_heads):
            kn = _rms(kvn[:, h * hd:(h + 1) * hd], gk_ref[...]).astype(BF16)
            s_ref[h * ta:(h + 1) * ta, 0:n_keys] = _dot_nt(q_ref[:, 2 * h * hd:(2 * h + 1) * hd], kn)
        s = s_ref[:, 0:n_keys] + _dot_nt(qp_ref[...], kp)
        if mask_fn is not None:
            s = mask_fn(s)
        _attend(s, ckv, m_ref, l_ref, acc_ref, slice(None))

    def finish():
        lat = acc_ref[...] / jnp.sum(l_ref[...], axis=1, keepdims=True)
        for h in range(n_heads):
            o = _dot(lat[h * ta:(h + 1) * ta, :].astype(BF16), wv_ref[:, h * hd:(h + 1) * hd])
            o_ref[:, h * hd:(h + 1) * hd] = o.astype(o_ref.dtype)

    _cached_frame(m_ref, l_ref, acc_ref, chains, finish, **frame)


def _dec_mla(qq, cache_ckv, kp_cache, new_ckv, new_kp, w_kn, w_v, g_nope_k, n_heads, n_k):
    b, ta, _ = qq.shape
    past, c = cache_ckv.shape[1:]
    tk = _pick_tile(past, 1024, LANES)
    nkb = past // tk
    rows = n_heads * ta
    ckvn, kpn = _pad_rows(new_ckv, NEW_ROWS), _pad_rows(new_kp, NEW_ROWS)
    whole = lambda a: pl.BlockSpec((None,) + a.shape[1:], lambda bi, n: (bi,) + (0,) * (a.ndim - 1))
    const = lambda a: pl.BlockSpec(a.shape, lambda bi, n: (0,) * a.ndim)
    blk = lambda w: pl.BlockSpec((None, tk, w), lambda bi, n: (bi, jnp.minimum(n, nkb - 1), 0))

    kernel = functools.partial(_dec_mla_kernel, n_heads=n_heads, nkb=nkb, tk=tk, ta=ta, q_off=past, n_k=n_k,
                               causal=False)
    gk = g_nope_k.reshape(1, LANES)
    return pl.pallas_call(
        kernel,
        out_shape=jax.ShapeDtypeStruct((b, ta, n_heads * LANES), BF16),
        grid=(b, nkb + 1),
        in_specs=[whole(qq), blk(c), blk(LANES), whole(ckvn), whole(kpn), const(w_kn), const(w_v), const(gk)],
        out_specs=pl.BlockSpec((None, ta, n_heads * LANES), lambda bi, n: (bi, 0, 0)),
        scratch_shapes=[pltpu.VMEM((rows, LANES), F32), pltpu.VMEM((rows, LANES), F32), pltpu.VMEM((rows, c), F32),
                        pltpu.VMEM((rows, tk), F32), pltpu.VMEM((rows, LANES), BF16)],
        compiler_params=_params(("parallel", "arbitrary"), 40 << 20),
        name="mla_attn_cached",
    )(qq, cache_ckv, kp_cache, ckvn, kpn, w_kn, w_v, gk)


def _cumsum_rows_kernel(x_ref, o_ref, carry_ref):
    @pl.when(pl.program_id(1) == 0)
    def _():
        carry_ref[...] = jnp.zeros_like(carry_ref)

    x = x_ref[...]
    tb = x.shape[1]
    r = lax.broadcasted_iota(jnp.int32, (tb, tb), 0)
    c = lax.broadcasted_iota(jnp.int32, (tb, tb), 1)
    tri = jnp.where(r <= c, 1.0, 0.0).astype(BF16)
    hi, mid, lo = _split3(x)
    cum = _dot(hi, tri) + _dot(mid, tri) + _dot(lo, tri) + carry_ref[:, 0:1]
    o_ref[...] = cum
    carry_ref[...] = jnp.broadcast_to(cum[:, tb - 1:tb], carry_ref.shape)


def _cumsum_rows(x, tb):
    b, g, t = x.shape
    return pl.pallas_call(
        _cumsum_rows_kernel,
        out_shape=jax.ShapeDtypeStruct((b, g, t), F32),
        grid=(b, t // tb),
        in_specs=[pl.BlockSpec((None, g, tb), lambda bi, i: (bi, 0, i))],
        out_specs=pl.BlockSpec((None, g, tb), lambda bi, i: (bi, 0, i)),
        scratch_shapes=[pltpu.VMEM((g, LANES), F32)],
        compiler_params=_params(("parallel", "arbitrary"), 32 << 20),
        name="forget_cumsum",
    )(x)


def _dec_diff(q, cache_k, cache_v, new_k, new_v, lam_p, subln, lam_init, n_heads, n_k):
    b, past = cache_k.shape[:2]
    specs = lambda tk: [pl.BlockSpec(lam_p.shape, lambda bi, n: (0, 0)),
                        pl.BlockSpec((1, 2 * LANES), lambda bi, n: (0, 0))]
    return _cached_attn(_dec_diff_kernel, q, cache_k, cache_v, new_k, new_v, [lam_p, subln.reshape(1, 2 * LANES)],
                        specs, 2 * n_heads, 2 * LANES, n_k, False, "diff_attn_cached",
                        n_heads=n_heads, lam_init=lam_init)


def _dec_fox(q, cache_k, cache_v, new_k, new_v, past_logf, new_logf, n_heads, n_k):
    b, past = cache_k.shape[:2]
    ta = q.shape[1]
    tk = _pick_tile(past, 1024, LANES)
    lf = jnp.concatenate([past_logf.astype(F32), new_logf[:, :, :n_heads]], axis=1)
    lf_rows = _pad_lanes(jnp.swapaxes(lf, 1, 2), past + tk)
    f_rows = _cumsum_rows(lf_rows, tk)
    f_q = _pad_lanes(jnp.swapaxes(f_rows[:, :, past:past + ta], 1, 2), LANES)
    specs = lambda tk: [pl.BlockSpec((None, ta, LANES), lambda bi, n: (bi, 0, 0)),
                        pl.BlockSpec((None, n_heads, tk), lambda bi, n: (bi, 0, n))]
    return _cached_attn(_dec_fox_kernel, q, cache_k, cache_v, new_k, new_v, [f_q, f_rows], specs,
                        n_heads, LANES, n_k, True, "fox_attn_cached", n_heads=n_heads)


def _rope_half(y, cos, sin, rope_dim):
    half = rope_dim // 2
    lane = lax.broadcasted_iota(jnp.int32, y.shape, 1)
    rot = jnp.where(lane < half, pltpu.roll(y, LANES - half, 1), pltpu.roll(y, half, 1))
    return y * cos + rot * sin


def _rms_low(x, gain, n):
    return x * lax.rsqrt(jnp.sum(x * x, axis=-1, keepdims=True) * (1.0 / n) + EPS) * gain


def _odd_post_kernel(z_ref, cos_ref, sin_ref, gq_ref, gkv_ref, gr_ref,
                     cq_ref, ckvf_ref, ckvb_ref, kpf_ref, kpb_ref, *, q_lora, kv_lora, rope_dim):
    cq_ref[...] = _rms(z_ref[:, 0:q_lora], gq_ref[...]).astype(BF16)
    ckv = _rms(z_ref[:, q_lora:q_lora + kv_lora], gkv_ref[...])
    ckvf_ref[...] = ckv
    ckvb_ref[...] = ckv.astype(BF16)
    kp = _rms_low(z_ref[:, q_lora + kv_lora:q_lora + kv_lora + LANES], gr_ref[...], rope_dim)
    kp = _rope_half(kp, cos_ref[...], sin_ref[...], rope_dim)
    kpf_ref[...] = kp
    kpb_ref[...] = kp.astype(BF16)


def _odd_post(z, cos, sin, g_cq, g_ckv, g_rope_k_pad, q_lora, kv_lora, rope_dim):
    b, t, n = z.shape
    tm = _pick_tile(t, 512, 16)
    row = lambda w: pl.BlockSpec((None, tm, w), lambda bi, i: (bi, i, 0))
    tab = pl.BlockSpec((tm, LANES), lambda bi, i: (i, 0))
    vec = lambda w: pl.BlockSpec((1, w), lambda bi, i: (0, 0))
    outs = [(q_lora, BF16), (kv_lora, F32), (kv_lora, BF16), (LANES, F32), (LANES, BF16)]
    return pl.pallas_call(
        functools.partial(_odd_post_kernel, q_lora=q_lora, kv_lora=kv_lora, rope_dim=rope_dim),
        out_shape=[jax.ShapeDtypeStruct((b, t, w), dt) for w, dt in outs],
        grid=(b, t // tm),
        in_specs=[row(n), tab, tab, vec(q_lora), vec(kv_lora), vec(LANES)],
        out_specs=[row(w) for w, _ in outs],
        compiler_params=_params(("parallel", "parallel"), 32 << 20),
        name="odd_post",
    )(z, cos, sin, g_cq.reshape(1, -1), g_ckv.reshape(1, -1), g_rope_k_pad)


def _qup_kernel(cq_ref, w_ref, cos_ref, sin_ref, gn_ref, gr_ref, qq_ref, *, heads, rope_dim, q_scale):
    cq = cq_ref[...]
    cos, sin = cos_ref[...], sin_ref[...]
    for h in range(heads):
        q = _dot(cq, w_ref[:, 2 * h * LANES:(2 * h + 2) * LANES])
        qn = _rms(q[:, 0:LANES], gn_ref[...])
        qq_ref[:, 2 * h * LANES:(2 * h + 1) * LANES] = (qn * q_scale).astype(BF16)
        qp = _rms_low(q[:, LANES:2 * LANES], gr_ref[...], rope_dim)
        qp = _rope_half(qp, cos, sin, rope_dim)
        qq_ref[:, (2 * h + 1) * LANES:(2 * h + 2) * LANES] = (qp * q_scale).astype(BF16)


def _qup(cq, w_pad, cos, sin, g_nope_q, g_rope_q_pad, n_heads, rope_dim, qk_dim):
    b, t, kq = cq.shape
    tm = _pick_tile(t, 512, 16)
    hg = 4 if n_heads % 4 == 0 else 1
    tn = hg * 2 * LANES
    return pl.pallas_call(
        functools.partial(_qup_kernel, heads=hg, rope_dim=rope_dim, q_scale=qk_dim ** -0.5 * LOG2E),
        out_shape=jax.ShapeDtypeStruct((b, t, n_heads * 2 * LANES), BF16),
        grid=(b, t // tm, n_heads // hg),
        in_specs=[
            pl.BlockSpec((None, tm, kq), lambda bi, i, j: (bi, i, 0)),
            pl.BlockSpec((kq, tn), lambda bi, i, j: (0, j)),
            pl.BlockSpec((tm, LANES), lambda bi, i, j: (i, 0)),
            pl.BlockSpec((tm, LANES), lambda bi, i, j: (i, 0)),
            pl.BlockSpec((1, LANES), lambda bi, i, j: (0, 0)),
            pl.BlockSpec((1, LANES), lambda bi, i, j: (0, 0)),
        ],
        out_specs=pl.BlockSpec((None, tm, tn), lambda bi, i, j: (bi, i, j)),
        compiler_params=_params(("parallel", "parallel", "arbitrary"), 32 << 20),
        name="mla_q_up",
    )(cq, w_pad, cos, sin, g_nope_q.reshape(1, LANES), g_rope_q_pad)


def _kvup_kernel(ckv_ref, kp_ref, w_ref, gn_ref, kk_ref, v_ref, *, heads):
    ckv = ckv_ref[...]
    kp = kp_ref[...]
    for h in range(heads):
        kv = _dot(ckv, w_ref[:, 2 * h * LANES:(2 * h + 2) * LANES])
        kn = _rms(kv[:, 0:LANES], gn_ref[...])
        kk_ref[:, 2 * h * LANES:(2 * h + 1) * LANES] = kn.astype(BF16)
        kk_ref[:, (2 * h + 1) * LANES:(2 * h + 2) * LANES] = kp
        v_ref[:, 2 * h * LANES:(2 * h + 1) * LANES] = kv[:, LANES:2 * LANES].astype(BF16)
        v_ref[:, (2 * h + 1) * LANES:(2 * h + 2) * LANES] = jnp.ones((kv.shape[0], LANES), BF16)


def _kvup(ckv, kp, w, g_nope_k, n_heads):
    b, t, kk = ckv.shape
    tm = _pick_tile(t, 512, LANES)
    hg = 4 if n_heads % 4 == 0 else 1
    tn = hg * 2 * LANES
    return pl.pallas_call(
        functools.partial(_kvup_kernel, heads=hg),
        out_shape=[jax.ShapeDtypeStruct((b, t, n_heads * 2 * LANES), BF16)] * 2,
        grid=(b, t // tm, n_heads // hg),
        in_specs=[
            pl.BlockSpec((None, tm, kk), lambda bi, i, j: (bi, i, 0)),
            pl.BlockSpec((None, tm, LANES), lambda bi, i, j: (bi, i, 0)),
            pl.BlockSpec((kk, tn), lambda bi, i, j: (0, j)),
            pl.BlockSpec((1, LANES), lambda bi, i, j: (0, 0)),
        ],
        out_specs=[pl.BlockSpec((None, tm, tn), lambda bi, i, j: (bi, i, j))] * 2,
        compiler_params=_params(("parallel", "parallel", "arbitrary"), 32 << 20),
        name="mla_kv_up",
    )(ckv, kp, w, g_nope_k.reshape(1, LANES))


def _rope_tables(pos, dim):
    half = dim // 2
    inv = ROPE_THETA ** (-jnp.arange(half, dtype=F32) * 2.0 / dim)
    ang = pos.astype(F32)[:, None] * inv[None, :]
    cos, sin = jnp.cos(ang), jnp.sin(ang)
    pad = ((0, 0), (0, LANES - dim))
    return (jnp.pad(jnp.concatenate([cos, cos], axis=-1), pad),
            jnp.pad(jnp.concatenate([-sin, sin], axis=-1), pad))


def _pad_lanes(a, width):
    return jnp.pad(a, [(0, 0)] * (a.ndim - 1) + [(0, width - a.shape[-1])])


def _pad_rows(a, rows):
    return jnp.pad(a, [(0, 0), (0, rows - a.shape[1])] + [(0, 0)] * (a.ndim - 2))


def _layer_stack(x, c_mod, tok_pos, seq_shape, caches, p):
    bx, tx, d = x.shape
    ba, ta = seq_shape
    depth = p['w_ffn_in'].shape[0]
    n_diff = p['n_diff']
    n_fox = p['n_fox']
    n_mla = p['n_mla']
    past_len = 0 if caches is None else caches[0].shape[2]
    n_k = past_len + ta
    cos128, sin128 = _rope_tables(tok_pos, LANES)
    rope_dim = p['rope_dim']
    cos_r, sin_r = _rope_tables(tok_pos, rope_dim)
    new = [[] for _ in range(7)]

    def mods(l, s):
        m = c_mod[l]
        sh, sc, gt = m[:, 3 * s], m[:, 3 * s + 1], m[:, 3 * s + 2]
        if bx == m.shape[0]:
            return tuple(a[:, None, :] for a in (sh, sc, gt))
        rep = lambda a: jnp.repeat(a, ta, axis=0).reshape(bx, tx, d)
        return rep(sh), rep(sc), rep(gt)

    seq = lambda a: a.reshape(ba, ta, a.shape[-1])
    for l in range(depth):
        i = l // 2
        g = p['norm_gains'][l]
        sh, sc, gt = mods(l, 0)
        x = _ffn(x, sh, sc, gt, g[0], p['w_ffn_in'], p['w_ffn_out'], l, 0, 0.5)
        sh, sc, gt = mods(l, 1)
        if l % 2 == 0:
            (qa, kaf, kab, vaf, vab, qb, kbf, kbb, vbf, vbb, lf) = _even_proj(
                x, sh, sc, g[1], p['w_in_even'][i], p['w_in_forget'][i], cos128, sin128,
                p['qk_norm_even'][i], p['b_forget_pad'][i], 2 * n_diff, n_fox)
            new[0].append(kaf.reshape(ba, ta, n_diff, 2, LANES))
            new[1].append(vaf.reshape(ba, ta, 2, n_diff, LANES).swapaxes(2, 3).reshape(ba, ta, n_diff, 2 * LANES))
            new[2].append(kbf.reshape(ba, ta, n_fox, LANES))
            new[3].append(vbf.reshape(ba, ta, n_fox, LANES))
            new[4].append(seq(lf)[:, :, :n_fox])
            lam_init = 0.8 - 0.6 * math.exp(-0.3 * l)
            if caches is None:
                aq, kk, vv = _fox_prep(lf, kbb, vbb, n_fox)
                oa = _flash_diff(qa, kab, vab, p['diff_lambda'][i], p['diff_subln'][i], lam_init, n_diff, n_k, 0)
                ob = _flash_fox(qb, aq, kk, vv, n_fox, n_k, 0)
            else:
                past = tuple(a[i] for a in caches[:5])
                oa = _dec_diff(seq(qa), past[0], past[1], seq(kab), seq(vab), p['diff_lambda'][i],
                               p['diff_subln'][i], lam_init, n_diff, n_k)
                ob = _dec_fox(seq(qb), past[2], past[3], seq(kbb), seq(vbb), past[4], seq(lf), n_fox, n_k)
            x = _outproj(oa.reshape(bx, tx, -1), ob.reshape(bx, tx, -1), 0, 0, p['w_out_even'][i], x, gt)
        else:
            q_lora, kv_lora = p['q_lora'], p['kv_lora']
            z = _modproj(x, sh, sc, g[1], p['w_in_odd'][i])
            cq, ckvf, ckvb, kpf, kpb = _odd_post(z, cos_r, sin_r, p['mla_cq_norm'][i], p['mla_ckv_norm'][i],
                                                 p['g_rope_pad'][i, 1:2], q_lora, kv_lora, rope_dim)
            new[5].append(seq(ckvf))
            new[6].append(seq(kpf)[:, :, :rope_dim])
            qq = _qup(cq, p['w_uq_pad'][i], cos_r, sin_r, p['mla_qk_norm_nope'][i, 0],
                      p['g_rope_pad'][i, 0:1], n_mla, rope_dim, p['mla_qk_dim'])
            if caches is None:
                kk, v = _kvup(ckvb, kpb, p['w_ukv'][i], p['mla_qk_norm_nope'][i, 1], n_mla)
                o = _flash_mla(qq, kk, v, n_mla, n_k, 0)
            else:
                kp_cache = _pad_lanes(caches[6][i], LANES).astype(BF16)
                o = _dec_mla(seq(qq), caches[5][i], kp_cache, seq(ckvb), seq(kpb), p['w_kn'][i], p['w_v'][i],
                             p['mla_qk_norm_nope'][i, 1], n_mla, n_k)
            o = o.reshape(bx, tx, -1)
            x = _outproj(o, o, 0, 1, p['w_out_odd'][i], x, gt)
        sh, sc, gt = mods(l, 2)
        x = _ffn(x, sh, sc, gt, g[2], p['w_ffn_in'], p['w_ffn_out'], l, 1, 0.5, final_gain=g[3])
    return x, tuple(jnp.stack(lst) for lst in new)


def kernel(x_prompt, x_sample, c_prompt, c_sample, cache_diff_k, cache_diff_v, cache_fox_k, cache_fox_v, cache_fox_logf, cache_mla_ckv, cache_mla_kpe, w_ada, b_ada, norm_gains, w_ffn_in, w_ffn_out, w_in_even, b_forget, qk_norm_even, diff_lambda, diff_subln, w_out_even, w_in_odd, mla_cq_norm, mla_ckv_norm, w_uq, w_ukv, mla_qk_norm_nope, mla_qk_norm_rope, w_out_odd):
    d = x_prompt.shape[-1]
    n_diff, n_fox = cache_diff_k.shape[3], cache_fox_k.shape[3]
    assert cache_diff_k.shape[-1] == LANES and cache_fox_k.shape[-1] == LANES
    q_lora, kv_lora = mla_cq_norm.shape[-1], mla_ckv_norm.shape[-1]
    rope_dim, nope = cache_mla_kpe.shape[-1], mla_qk_norm_nope.shape[-1]
    n_mla = w_uq.shape[-1] // (nope + rope_dim)
    assert nope == LANES and rope_dim <= LANES and w_ukv.shape[-1] == n_mla * 2 * LANES
    n_odd = w_uq.shape[0]
    n_main = w_in_even.shape[-1] - n_fox
    assert n_main == 6 * 2 * n_diff * LANES and n_fox <= LANES

    w_uq_pad = _pad_lanes(w_uq.reshape(n_odd, q_lora, n_mla, nope + rope_dim), 2 * LANES)
    p = {
        'n_diff': n_diff, 'n_fox': n_fox, 'n_mla': n_mla, 'rope_dim': rope_dim,
        'q_lora': q_lora, 'kv_lora': kv_lora, 'mla_qk_dim': nope + rope_dim,
        'norm_gains': norm_gains,
        'w_ffn_in': _interleave_gate_up(w_ffn_in), 'w_ffn_out': w_ffn_out.astype(BF16),
        'w_in_even': w_in_even[:, :, :n_main].astype(BF16),
        'w_in_forget': _pad_lanes(w_in_even[:, :, n_main:], LANES).astype(BF16),
        'b_forget_pad': _pad_lanes(b_forget, LANES)[:, None, :],
        'qk_norm_even': qk_norm_even, 'diff_lambda': diff_lambda, 'diff_subln': diff_subln,
        'w_out_even': w_out_even.astype(BF16),
        'w_in_odd': _pad_lanes(w_in_odd, q_lora + kv_lora + LANES).astype(BF16),
        'mla_cq_norm': mla_cq_norm, 'mla_ckv_norm': mla_ckv_norm,
        'w_uq_pad': w_uq_pad.reshape(n_odd, q_lora, n_mla * 2 * LANES).astype(BF16),
        'w_ukv': w_ukv.astype(BF16),
        'w_kn': w_ukv.reshape(n_odd, kv_lora, n_mla, 2 * LANES)[..., :LANES].reshape(n_odd, kv_lora, -1).astype(BF16),
        'w_v': w_ukv.reshape(n_odd, kv_lora, n_mla, 2 * LANES)[..., LANES:].reshape(n_odd, kv_lora, -1).astype(BF16),
        'mla_qk_norm_nope': mla_qk_norm_nope,
        'g_rope_pad': _pad_lanes(mla_qk_norm_rope, LANES),
        'w_out_odd': w_out_odd.astype(BF16),
    }

    bp, tp = x_prompt.shape[:2]
    bs, ts = x_sample.shape[:2]
    past_len = cache_diff_k.shape[2]
    mod = _ada(jnp.concatenate([c_prompt, c_sample], axis=0), w_ada, b_ada)
    mod = mod.reshape(mod.shape[0], bp + bs, N_MOD, d)

    pos_p = jnp.arange(tp, dtype=jnp.int32)
    y_prompt, st_p = _layer_stack(x_prompt, mod[:, :bp], pos_p, (bp, tp), None, p)

    pos_s = jnp.tile(past_len + jnp.arange(ts, dtype=jnp.int32), bs)
    caches = (cache_diff_k, cache_diff_v, cache_fox_k, cache_fox_v, cache_fox_logf, cache_mla_ckv, cache_mla_kpe)
    y_sample, st_s = _layer_stack(x_sample.reshape(1, bs * ts, d), mod[:, bp:], pos_s, (bs, ts), caches, p)
    return (y_prompt, y_sample.reshape(bs, ts, d)) + st_p + st_s
```

```python
import functools
import math

import numpy as np
import jax
import jax.numpy as jnp
from jax import lax
from jax.experimental import pallas as pl
from jax.experimental.pallas import tpu as pltpu

F32 = jnp.float32
BF16 = jnp.bfloat16

CHUNK = 64
ROPE_THETA = 10000.0
EPS = 1e-6
NEG_INF = -1e30
N_MOD = 9

LANES = 128
SUBLANES = 8
VMEM_CAP_BYTES = 56 * 1024 * 1024

LOG2E = math.log2(math.e)
CHUNK_SHIFT = CHUNK.bit_length() - 1
assert (1 << CHUNK_SHIFT) == CHUNK


def _round_up(n, m):
    return (n + m - 1) // m * m


def _pick_tile(n, target, quantum):
    if n <= target:
        return n
    best = None
    t = quantum
    while t <= target:
        if n % t == 0:
            best = t
        t += quantum
    assert best is not None, (n, target, quantum)
    return best


def _params(semantics, vmem_bytes):
    limit = int(min(max(vmem_bytes, 16 * 1024 * 1024), VMEM_CAP_BYTES))
    return pltpu.CompilerParams(dimension_semantics=semantics, vmem_limit_bytes=limit)


def _rms(x, gain):
    return x * lax.rsqrt(jnp.mean(x * x, axis=-1, keepdims=True) + EPS) * gain


def _silu(g):
    return g / (1.0 + jnp.exp(-g))


def _dot(a, b):
    return jnp.dot(a, b, preferred_element_type=F32)


def _dot_nt(a, b):
    return lax.dot_general(a, b, (((1,), (1,)), ((), ())), preferred_element_type=F32)


def _ada_kernel(c_ref, w_ref, b_ref, o_ref):
    a = _silu(c_ref[...]).astype(BF16)
    o_ref[...] = _dot(a, w_ref[...].astype(BF16)) + b_ref[...]


def _ada(c_all, w_ada, b_ada):
    depth, d, n = w_ada.shape
    r = c_all.shape[0]
    tn = _pick_tile(n, 1024, LANES)
    return pl.pallas_call(
        _ada_kernel,
        out_shape=jax.ShapeDtypeStruct((depth, r, n), F32),
        grid=(depth, n // tn),
        in_specs=[
            pl.BlockSpec((r, d), lambda l, j: (0, 0)),
            pl.BlockSpec((None, d, tn), lambda l, j: (l, 0, j)),
            pl.BlockSpec((None, 1, tn), lambda l, j: (l, 0, j)),
        ],
        out_specs=pl.BlockSpec((None, r, tn), lambda l, j: (l, 0, j)),
        compiler_params=_params(("arbitrary", "arbitrary"), 2 * d * tn * 4 + 3 * d * tn * 2 + (4 << 20)),
        name="ada_mod",
    )(c_all, w_ada, b_ada.reshape(depth, 1, n))


def _mod_spec(mod, tm):
    d = mod.shape[-1]
    if mod.shape[1] == 1:
        return pl.BlockSpec((None, 1, d), lambda b, i, j: (b, 0, 0))
    return pl.BlockSpec((None, tm, d), lambda b, i, j: (b, i, 0))


def _ffn_kernel(x_ref, sh_ref, sc_ref, gt_ref, g_ref, wgu_ref, wo_ref, *rest,
                gate_mul, final_norm):
    if final_norm:
        gf_ref, o_ref, h_ref, acc_ref = rest
    else:
        o_ref, h_ref, acc_ref = rest
    f = pl.program_id(2)

    @pl.when(f == 0)
    def _():
        h = _rms(x_ref[...], g_ref[...]) * (1.0 + sc_ref[...]) + sh_ref[...]
        h_ref[...] = h.astype(BF16)
        acc_ref[...] = jnp.zeros_like(acc_ref)

    tf = wo_ref.shape[0]
    gu = _dot(h_ref[...], wgu_ref[...])
    a = (_silu(gu[:, 0:tf]) * gu[:, tf:2 * tf]).astype(BF16)
    acc_ref[...] += _dot(a, wo_ref[...])

    @pl.when(f == pl.num_programs(2) - 1)
    def _():
        xn = x_ref[...] + (gate_mul * gt_ref[...]) * acc_ref[...]
        if final_norm:
            xn = _rms(xn, gf_ref[...])
        o_ref[...] = xn


def _ffn_tile(ff):
    return _pick_tile(ff, 512, LANES)


def _gate_up_kernel(g_ref, u_ref, o_ref):
    tf = g_ref.shape[1]
    o_ref[:, 0:tf] = g_ref[...].astype(BF16)
    o_ref[:, tf:2 * tf] = u_ref[...].astype(BF16)


def _interleave_gate_up(w_in):
    nl, ns, d, ff2 = w_in.shape
    tf = _ffn_tile(ff2 // 2)
    nf = ff2 // 2 // tf
    return pl.pallas_call(
        _gate_up_kernel,
        out_shape=jax.ShapeDtypeStruct(w_in.shape, BF16),
        grid=(nl, ns, nf),
        in_specs=[pl.BlockSpec((None, None, d, tf), lambda l, s, f: (l, s, 0, f)),
                  pl.BlockSpec((None, None, d, tf), lambda l, s, f: (l, s, 0, nf + f))],
        out_specs=pl.BlockSpec((None, None, d, 2 * tf), lambda l, s, f: (l, s, 0, f)),
        compiler_params=_params(("parallel", "parallel", "parallel"), 4 * d * tf * 4 + 2 * d * 2 * tf * 2 + (8 << 20)),
        name="ffn_weight_prep",
    )(w_in, w_in)


def _ffn(x, sh, sc, gt, gain, w_in, w_out, layer, sub, gate_mul, final_gain=None):
    b, t, d = x.shape
    ff = w_out.shape[2]
    tm = _pick_tile(t, 512, 16)
    tf = _ffn_tile(ff)
    nf = ff // tf
    in_specs = [
        pl.BlockSpec((None, tm, d), lambda bi, i, f: (bi, i, 0)),
        _mod_spec(sh, tm), _mod_spec(sc, tm), _mod_spec(gt, tm),
        pl.BlockSpec((1, d), lambda bi, i, f: (0, 0)),
        pl.BlockSpec((None, None, d, 2 * tf), lambda bi, i, f: (layer, sub, 0, f)),
        pl.BlockSpec((None, None, tf, d), lambda bi, i, f: (layer, sub, f, 0)),
    ]
    args = [x, sh, sc, gt, gain.reshape(1, d), w_in, w_out]
    if final_gain is not None:
        in_specs.append(pl.BlockSpec((1, d), lambda bi, i, f: (0, 0)))
        args.append(final_gain.reshape(1, d))
    vmem = (4 * tm * d * 4 + tm * d * 2 + tm * d * 4 + 6 * d * tf * 2 + 4 * tm * tf * 4
            + 6 * tm * d * 4 * (sh.shape[1] != 1) + (4 << 20))
    return pl.pallas_call(
        functools.partial(_ffn_kernel, gate_mul=gate_mul, final_norm=final_gain is not None),
        out_shape=jax.ShapeDtypeStruct((b, t, d), F32),
        grid=(b, t // tm, nf),
        in_specs=in_specs,
        out_specs=pl.BlockSpec((None, tm, d), lambda bi, i, f: (bi, i, 0)),
        scratch_shapes=[pltpu.VMEM((tm, d), BF16), pltpu.VMEM((tm, d), F32)],
        compiler_params=_params(("parallel", "parallel", "arbitrary"), vmem),
        name="ffn",
    )(*args)


def _modproj_kernel(x_ref, sh_ref, sc_ref, g_ref, w_ref, o_ref, h_ref):
    @pl.when(pl.program_id(2) == 0)
    def _():
        h = _rms(x_ref[...], g_ref[...]) * (1.0 + sc_ref[...]) + sh_ref[...]
        h_ref[...] = h.astype(BF16)

    o_ref[...] = _dot(h_ref[...], w_ref[...])


def _modproj(x, sh, sc, gain, w):
    b, t, d = x.shape
    n = w.shape[1]
    tm = _pick_tile(t, 1024, 16)
    tn = _pick_tile(n, 1280, LANES)
    vmem = (2 * tm * d * 4 + tm * d * 2 + 2 * d * tn * 2 + 3 * tm * tn * 4
            + 4 * tm * d * 4 * (sh.shape[1] != 1) + 3 * tm * d * 4 + (4 << 20))
    return pl.pallas_call(
        _modproj_kernel,
        out_shape=jax.ShapeDtypeStruct((b, t, n), F32),
        grid=(b, t // tm, n // tn),
        in_specs=[
            pl.BlockSpec((None, tm, d), lambda bi, i, j: (bi, i, 0)),
            _mod_spec(sh, tm), _mod_spec(sc, tm),
            pl.BlockSpec((1, d), lambda bi, i, j: (0, 0)),
            pl.BlockSpec((d, tn), lambda bi, i, j: (0, j)),
        ],
        out_specs=pl.BlockSpec((None, tm, tn), lambda bi, i, j: (bi, i, j)),
        scratch_shapes=[pltpu.VMEM((tm, d), BF16)],
        compiler_params=_params(("parallel", "parallel", "arbitrary"), vmem),
        name="mod_proj",
    )(x, sh, sc, gain.reshape(1, d), w)


def _outproj_kernel(a1_ref, a2_ref, w1_ref, w2_ref, x_ref, gt_ref, o_ref):
    y = _dot(a1_ref[...], w1_ref[...]) + _dot(a2_ref[...], w2_ref[...])
    o_ref[...] = x_ref[...] + gt_ref[...] * y


def _outproj(a1, a2, blk1, blk2, w, x, gt):
    b, t, d = x.shape
    kh = w.shape[0] // 2
    tm = _pick_tile(t, 512, 16)
    vmem = 4 * tm * kh * 2 + 4 * kh * d * 2 + 5 * tm * d * 4 + 2 * tm * d * 4 * (gt.shape[1] != 1) + (4 << 20)
    return pl.pallas_call(
        _outproj_kernel,
        out_shape=jax.ShapeDtypeStruct((b, t, d), F32),
        grid=(b, t // tm, 1),
        in_specs=[
            pl.BlockSpec((None, tm, kh), lambda bi, i, j: (bi, i, blk1)),
            pl.BlockSpec((None, tm, kh), lambda bi, i, j: (bi, i, blk2)),
            pl.BlockSpec((kh, d), lambda bi, i, j: (0, 0)),
            pl.BlockSpec((kh, d), lambda bi, i, j: (1, 0)),
            pl.BlockSpec((None, tm, d), lambda bi, i, j: (bi, i, 0)),
            _mod_spec(gt, tm),
        ],
        out_specs=pl.BlockSpec((None, tm, d), lambda bi, i, j: (bi, i, 0)),
        compiler_params=_params(("parallel", "parallel", "arbitrary"), vmem),
        name="out_proj",
    )(a1, a2, w, w, x, gt)


def _rope128(y, cos, sin):
    return y * cos + pltpu.roll(y, LANES // 2, 1) * sin


def _log_sigmoid(x):
    return jnp.minimum(x, 0.0) - jnp.log(1.0 + jnp.exp(-jnp.abs(x)))


def _even_proj_kernel(x_ref, sh_ref, sc_ref, g_ref, w_ref, wf_ref, cos_ref, sin_ref, gn_ref, bf_ref,
                      qa_ref, kaf_ref, kab_ref, vaf_ref, vab_ref,
                      qb_ref, kbf_ref, kbb_ref, vbf_ref, vbb_ref, lf_ref, h_ref,
                      *, n_grp, n_forget, q_scale):
    j = pl.program_id(2)
    tm, hd = x_ref.shape[0], LANES

    @pl.when(j == 0)
    def _():
        h = (_rms(x_ref[...], g_ref[...]) * (1.0 + sc_ref[...]) + sh_ref[...]).astype(BF16)
        h_ref[...] = h
        fg = _dot(h, wf_ref[...]) + bf_ref[...]
        lane = lax.broadcasted_iota(jnp.int32, fg.shape, 1)
        lf_ref[...] = jnp.where(lane < n_forget, _log_sigmoid(fg), 0.0)

    def heads():
        h = h_ref[...]
        for c in range(n_grp // 2):
            z = _dot(h, w_ref[:, 2 * c * hd:(2 * c + 2) * hd])
            yield z[:, 0:hd]
            yield z[:, hd:2 * hd]

    @pl.when(j == 0)
    def _():
        for i, z in enumerate(heads()):
            q = _rope128(_rms(z, gn_ref[0:1, :]), cos_ref[...], sin_ref[...])
            qa_ref[:, i * hd:(i + 1) * hd] = (q * q_scale).astype(BF16)

    @pl.when(j == 1)
    def _():
        for i, z in enumerate(heads()):
            k = _rope128(_rms(z, gn_ref[1:2, :]), cos_ref[...], sin_ref[...])
            kaf_ref[pl.ds(i, tm, stride=n_grp), :] = k
            kab_ref[:, i * hd:(i + 1) * hd] = k.astype(BF16)

    @pl.when(j == 2)
    def _():
        for i, z in enumerate(heads()):
            vaf_ref[pl.ds((i % 2) * (n_grp // 2) + i // 2, tm, stride=n_grp), :] = z
            vab_ref[:, i * hd:(i + 1) * hd] = z.astype(BF16)

    @pl.when(j == 3)
    def _():
        for i, z in enumerate(heads()):
            qb_ref[:, i * hd:(i + 1) * hd] = (_rms(z, gn_ref[2:3, :]) * q_scale).astype(BF16)

    @pl.when(j == 4)
    def _():
        for i, z in enumerate(heads()):
            k = _rms(z, gn_ref[3:4, :])
            kbf_ref[:, i * hd:(i + 1) * hd] = k
            kbb_ref[:, i * hd:(i + 1) * hd] = k.astype(BF16)

    @pl.when(j == 5)
    def _():
        for i, z in enumerate(heads()):
            vbf_ref[:, i * hd:(i + 1) * hd] = z
            vbb_ref[:, i * hd:(i + 1) * hd] = z.astype(BF16)


def _even_proj(x, sh, sc, gain, w_main, w_forget, cos, sin, qk_gain, b_forget_pad, n_grp, n_fox):
    b, t, d = x.shape
    wg = n_grp * LANES
    assert w_main.shape[1] == 6 * wg and n_fox == n_grp
    tm = _pick_tile(t, 512, 16)
    row = lambda w: pl.BlockSpec((None, tm, w), lambda bi, i, j: (bi, i, 0))
    tab = pl.BlockSpec((tm, LANES), lambda bi, i, j: (i, 0))
    const = lambda shape: pl.BlockSpec(shape, lambda bi, i, j: (0,) * len(shape))
    outs = [(wg, BF16), (wg, F32), (wg, BF16), (wg, F32), (wg, BF16),
            (wg, BF16), (wg, F32), (wg, BF16), (wg, F32), (wg, BF16), (LANES, F32)]
    out_shape = [jax.ShapeDtypeStruct((b, t, w), dt) for w, dt in outs]
    out_specs = [row(w) for w, _ in outs]
    for idx in (1, 3):
        out_shape[idx] = jax.ShapeDtypeStruct((b, t * n_grp, LANES), F32)
        out_specs[idx] = pl.BlockSpec((None, tm * n_grp, LANES), lambda bi, i, j: (bi, i, 0))
    vmem = (2 * tm * d * 4 + tm * d * 2 + 4 * d * wg * 2 + 2 * sum(tm * w * jnp.dtype(dt).itemsize for w, dt in outs)
            + 4 * tm * wg * 4 + 4 * tm * d * 4 * (sh.shape[1] != 1) + (4 << 20))
    return pl.pallas_call(
        functools.partial(_even_proj_kernel, n_grp=n_grp, n_forget=n_fox, q_scale=LANES ** -0.5 * LOG2E),
        out_shape=out_shape,
        grid=(b, t // tm, 6),
        in_specs=[pl.BlockSpec((None, tm, d), lambda bi, i, j: (bi, i, 0)),
                  _mod_spec(sh, tm), _mod_spec(sc, tm), const((1, d)),
                  pl.BlockSpec((d, wg), lambda bi, i, j: (0, j)), const((d, LANES)),
                  tab, tab, const((4, LANES)), const((1, LANES))],
        out_specs=out_specs,
        scratch_shapes=[pltpu.VMEM((tm, d), BF16)],
        compiler_params=_params(("parallel", "parallel", "arbitrary"), vmem),
        name="even_proj",
    )(x, sh, sc, gain.reshape(1, d), w_main, w_forget, cos, sin, qk_gain, b_forget_pad)


def _split3(x):
    hi = x.astype(BF16)
    r = x - hi.astype(F32)
    mid = r.astype(BF16)
    lo = (r - mid.astype(F32)).astype(BF16)
    return hi, mid, lo


def _fox_prep_kernel(lf_ref, k_ref, v_ref, aq_ref, kk_ref, vv_ref, carry_ref, *, n_heads):
    @pl.when(pl.program_id(1) == 0)
    def _():
        carry_ref[...] = jnp.zeros_like(carry_ref)

    x = lf_ref[...]
    tb = x.shape[0]
    r = lax.broadcasted_iota(jnp.int32, (tb, tb), 0)
    c = lax.broadcasted_iota(jnp.int32, (tb, tb), 1)
    tri = jnp.where(r >= c, 1.0, 0.0).astype(BF16)
    hi, mid, lo = _split3(x)
    cum = _dot(tri, hi) + _dot(tri, mid) + _dot(tri, lo) + carry_ref[...]
    carry_ref[...] = cum[tb - 1:tb, :]
    fh, fm, fl = (p.astype(F32) for p in _split3(cum * LOG2E))
    lane = lax.broadcasted_iota(jnp.int32, (tb, LANES), 1)
    ones_q = jnp.where((lane >= 3) & (lane < 6), 1.0, 0.0)
    ones_k = jnp.where(lane < 3, 1.0, 0.0)
    for h in range(n_heads):
        a, m, l = fh[:, h:h + 1], fm[:, h:h + 1], fl[:, h:h + 1]
        aq = jnp.where(lane == 0, a, jnp.where(lane == 1, m, jnp.where(lane == 2, l, ones_q)))
        ak = jnp.where(lane == 3, -a, jnp.where(lane == 4, -m, jnp.where(lane == 5, -l, ones_k)))
        aq_ref[:, h * LANES:(h + 1) * LANES] = aq.astype(BF16)
        kk_ref[:, 2 * h * LANES:(2 * h + 1) * LANES] = k_ref[:, h * LANES:(h + 1) * LANES]
        kk_ref[:, (2 * h + 1) * LANES:(2 * h + 2) * LANES] = ak.astype(BF16)
        vv_ref[:, 2 * h * LANES:(2 * h + 1) * LANES] = v_ref[:, h * LANES:(h + 1) * LANES]
        vv_ref[:, (2 * h + 1) * LANES:(2 * h + 2) * LANES] = jnp.ones((tb, LANES), BF16)


def _fox_prep(logf_pad, k_all, v_all, n_heads):
    b, t, _ = logf_pad.shape
    tb = _pick_tile(t, 512, LANES)
    w = n_heads * LANES
    narrow = pl.BlockSpec((None, tb, w), lambda bi, i: (bi, i, 0))
    wide = pl.BlockSpec((None, tb, 2 * w), lambda bi, i: (bi, i, 0))
    return pl.pallas_call(
        functools.partial(_fox_prep_kernel, n_heads=n_heads),
        out_shape=[jax.ShapeDtypeStruct((b, t, w), BF16), jax.ShapeDtypeStruct((b, t, 2 * w), BF16),
                   jax.ShapeDtypeStruct((b, t, 2 * w), BF16)],
        grid=(b, t // tb),
        in_specs=[pl.BlockSpec((None, tb, LANES), lambda bi, i: (bi, i, 0)), narrow, narrow],
        out_specs=[narrow, wide, wide],
        scratch_shapes=[pltpu.VMEM((1, LANES), F32)],
        compiler_params=_params(("parallel", "arbitrary"), 32 << 20),
        name="fox_prep",
    )(logf_pad, k_all, v_all)


FLAG_FIRST, FLAG_LAST = 1, 2
KIND_SHIFT = 2
KIND_FULL, KIND_MASK, KIND_DIAG = 0, 1, 2
SCORE_LOOKAHEAD = 1


def _pair_table(n_q, n_k, tq, tk, q_off, causal):
    tk_pad = _round_up(n_k, tk)
    aligned = tq == tk and q_off % tq == 0
    qi, kj, fl = [], [], []
    for i in range(n_q // tq):
        qmin, qmax = q_off + i * tq, q_off + (i + 1) * tq - 1
        row = []
        for j in range(tk_pad // tk):
            kmin, kmax = j * tk, min((j + 1) * tk, n_k) - 1
            if kmin >= n_k:
                continue
            if causal:
                any_vis, all_vis = kmin <= qmax, kmax <= qmin
            else:
                any_vis, all_vis = kmin // CHUNK <= qmax // CHUNK, kmax // CHUNK <= qmin // CHUNK
            all_vis = all_vis and (j + 1) * tk <= n_k
            if any_vis:
                diag = aligned and kmin == qmin and (j + 1) * tk <= n_k
                row.append((j, KIND_FULL if all_vis else KIND_DIAG if diag else KIND_MASK))
        assert row and row[0][0] == 0
        for idx, (j, kind) in enumerate(row):
            qi.append(i)
            kj.append(j)
            fl.append((kind << KIND_SHIFT) | (FLAG_FIRST if idx == 0 else 0) | (FLAG_LAST if idx == len(row) - 1 else 0))
    kinds = sorted({f >> KIND_SHIFT for f in fl})
    as_arr = lambda v: jnp.asarray(np.array(v, np.int32))
    return as_arr(qi), as_arr(kj), as_arr(fl), kinds


def _visible(shape, qpos0, kpos0, n_k, causal, row_period=None):
    if row_period is None:
        rows = qpos0 + lax.broadcasted_iota(jnp.int32, shape, 0)
    else:
        one = lax.broadcasted_iota(jnp.int32, (row_period, shape[1]), 0)
        rows = qpos0 + jnp.concatenate([one] * (shape[0] // row_period), axis=0)
    cols = kpos0 + lax.broadcasted_iota(jnp.int32, shape, 1)
    if causal:
        ok = cols <= rows
    else:
        ok = (cols >> CHUNK_SHIFT) <= (rows >> CHUNK_SHIFT)
    return ok if n_k is None else ok & (cols < n_k)


def _lane_tile(x, n):
    return x if n == LANES else jnp.concatenate([x] * (n // LANES), axis=1)


def _lane_fold(p):
    acc = p[:, 0:LANES]
    for c in range(1, p.shape[1] // LANES):
        acc = acc + p[:, c * LANES:(c + 1) * LANES]
    return acc


def _attend(s, v, m_ref, l_ref, acc_ref, idx):
    m_prev = m_ref[idx]
    m_new = jnp.maximum(m_prev, jnp.max(s, axis=1, keepdims=True))
    alpha = jnp.exp2(m_prev - m_new)
    p = jnp.exp2(s - _lane_tile(m_new, s.shape[1]))
    if l_ref is not None:
        l_ref[idx] = alpha * l_ref[idx] + _lane_fold(p)
    acc_ref[idx] = _lane_tile(alpha, v.shape[1]) * acc_ref[idx] + _dot(p.astype(v.dtype), v)
    m_ref[idx] = m_new


def _flash_frame(qi_ref, kj_ref, fl_ref, m_ref, l_ref, acc_ref, scores, update, finish, first=None, *,
                 tq, tk, rs, rs_full, q_off, n_k, causal, kinds):
    n = pl.program_id(2)
    flags = fl_ref[n]
    kind = flags >> KIND_SHIFT

    @pl.when((flags & FLAG_FIRST) != 0)
    def _():
        m_ref[...] = jnp.full_like(m_ref, NEG_INF)
        if l_ref is not None:
            l_ref[...] = jnp.zeros_like(l_ref)
        acc_ref[...] = jnp.zeros_like(acc_ref)
        if first is not None:
            first()

    qpos0 = q_off + qi_ref[n] * tq
    kpos0 = kj_ref[n] * tk
    rs_masked = rs

    def run(k):
        rs = rs_full if k == KIND_FULL else rs_masked

        def masked_scores(r):
            n_keys = (r + 1) * rs if k == KIND_DIAG else tk
            rows = slice(r * rs, (r + 1) * rs)
            tiles = scores(rows, n_keys)
            if k == KIND_DIAG:
                lo = r * rs
                vis = _visible((rs, rs), qpos0 + lo, kpos0 + lo, None, causal)
                blks = [jnp.where(vis, s[:, lo:], NEG_INF) for s in tiles]
                tiles = blks if lo == 0 else [jnp.concatenate([s[:, :lo], b], axis=1) for s, b in zip(tiles, blks)]
            elif k == KIND_MASK:
                vis = _visible(tiles[0].shape, qpos0 + r * rs, kpos0, n_k, causal)
                tiles = [jnp.where(vis, s, NEG_INF) for s in tiles]
            return rows, n_keys, tiles

        order = list(range(tq // rs))
        if k == KIND_DIAG:
            order.reverse()
        pending = [masked_scores(r) for r in order[:SCORE_LOOKAHEAD]]
        for idx in range(len(order)):
            rows, n_keys, tiles = pending.pop(0)
            if idx + SCORE_LOOKAHEAD < len(order):
                pending.append(masked_scores(order[idx + SCORE_LOOKAHEAD]))
            update(rows, n_keys, tiles)

    for k in kinds:
        pl.when(kind == k)(functools.partial(run, k))

    @pl.when((flags & FLAG_LAST) != 0)
    def _():
        finish()


def _diff_kernel(qi_ref, kj_ref, fl_ref, q_ref, k_ref, v_ref, lam_ref, sub_ref, o_ref,
                 m_ref, l_ref, acc_ref, *, lam_init, **frame):
    hd = LANES

    def scores(rows, n_keys):
        return [_dot_nt(q_ref[rows, c * hd:(c + 1) * hd], k_ref[0:n_keys, c * hd:(c + 1) * hd]) for c in range(2)]

    def update(rows, n_keys, tiles):
        v = v_ref[0:n_keys, :]
        for c in range(2):
            _attend(tiles[c], v, m_ref, l_ref, acc_ref, (c, rows))

    def finish():
        lp = lam_ref[...]
        lam = (jnp.exp(jnp.sum(lp[0:1] * lp[1:2], axis=-1, keepdims=True))
               - jnp.exp(jnp.sum(lp[2:3] * lp[3:4], axis=-1, keepdims=True)) + lam_init)
        l0 = jnp.sum(l_ref[0], axis=1, keepdims=True)
        l1 = jnp.sum(l_ref[1], axis=1, keepdims=True)
        o = acc_ref[0] / l0 - lam * (acc_ref[1] / l1)
        o_ref[...] = (_rms(o, sub_ref[...]) * (1.0 - lam_init)).astype(o_ref.dtype)

    _flash_frame(qi_ref, kj_ref, fl_ref, m_ref, l_ref, acc_ref, scores, update, finish, **frame)


def _ones_finish(acc_ref, o_ref):
    o_ref[...] = (acc_ref[:, 0:LANES] / acc_ref[:, LANES:2 * LANES]).astype(o_ref.dtype)


def _fox_kernel(qi_ref, kj_ref, fl_ref, q_ref, aq_ref, k_ref, v_ref, o_ref,
                m_ref, acc_ref, qq_ref, **frame):
    def first():
        qq_ref[:, 0:LANES] = q_ref[...]
        qq_ref[:, LANES:2 * LANES] = aq_ref[...]

    def scores(rows, n_keys):
        return [_dot_nt(qq_ref[rows, :], k_ref[0:n_keys, :])]

    def update(rows, n_keys, tiles):
        _attend(tiles[0], v_ref[0:n_keys, :], m_ref, None, acc_ref, rows)

    _flash_frame(qi_ref, kj_ref, fl_ref, m_ref, None, acc_ref, scores, update,
                 functools.partial(_ones_finish, acc_ref, o_ref), first, **frame)


def _mla_kernel(qi_ref, kj_ref, fl_ref, q_ref, k_ref, v_ref, o_ref,
                m_ref, acc_ref, **frame):
    def scores(rows, n_keys):
        return [_dot_nt(q_ref[rows, :], k_ref[0:n_keys, :])]

    def update(rows, n_keys, tiles):
        _attend(tiles[0], v_ref[0:n_keys, :], m_ref, None, acc_ref, rows)

    _flash_frame(qi_ref, kj_ref, fl_ref, m_ref, None, acc_ref, scores, update,
                 functools.partial(_ones_finish, acc_ref, o_ref), **frame)


def _flash_tiles(n_q, n_k_pad, tile):
    tq = _pick_tile(n_q, tile, 16)
    tk = next((t for t in (tile, tile // 2, tile // 4) if n_k_pad % t == 0), None) or _pick_tile(n_k_pad, tile, LANES)
    rs = _pick_tile(tq, 256, 16)
    return tq, tk, rs


def _flash_call(kernel, n_heads, operands, n_q, n_k_pad, out_width, scratch, n_k, q_off, causal, name,
                tile=2048, **kw):
    b = operands[0][0].shape[0]
    tq, tk, rs = _flash_tiles(n_q, n_k_pad, tile)
    qi, kj, fl, kinds = _pair_table(n_q, n_k, tq, tk, q_off, causal)
    if KIND_DIAG in kinds:
        assert rs % CHUNK == 0 and rs % LANES == 0

    in_specs, args = [], []
    for op in operands:
        a = op[0]
        if len(op) == 1:
            in_specs.append(pl.BlockSpec(a.shape, lambda bi, h, n, qi, kj, fl: (0, 0)))
        elif op[2]:
            in_specs.append(pl.BlockSpec((None, tq, op[1]), lambda bi, h, n, qi, kj, fl: (bi, qi[n], h)))
        else:
            in_specs.append(pl.BlockSpec((None, tk, op[1]), lambda bi, h, n, qi, kj, fl: (bi, kj[n], h)))
        args.append(a)
    vmem = 2 * tq * tk * 4 + 8 * max(tq, tk) * 2 * LANES * 2 * len(operands) + 10 * tq * 2 * LANES * 4 + (8 << 20)
    grid_spec = pltpu.PrefetchScalarGridSpec(
        num_scalar_prefetch=3,
        grid=(b, n_heads, int(qi.shape[0])),
        in_specs=in_specs,
        out_specs=pl.BlockSpec((None, tq, out_width), lambda bi, h, n, qi, kj, fl: (bi, qi[n], h)),
        scratch_shapes=scratch(tq),
    )
    return pl.pallas_call(
        functools.partial(kernel, tq=tq, tk=tk, rs=rs, rs_full=_pick_tile(tq, 4 * rs, 16), q_off=q_off, n_k=n_k, causal=causal, kinds=kinds, **kw),
        out_shape=jax.ShapeDtypeStruct((b, n_q, n_heads * out_width), BF16),
        grid_spec=grid_spec,
        compiler_params=_params(("parallel", "parallel", "arbitrary"), vmem),
        name=name,
    )(qi, kj, fl, *args)


def _flash_diff(q, k, v, lam_p, subln, lam_init, n_heads, n_k, q_off):
    w = 2 * LANES
    scratch = lambda tq: [pltpu.VMEM((2, tq, LANES), F32), pltpu.VMEM((2, tq, LANES), F32), pltpu.VMEM((2, tq, w), F32)]
    ops = [(q, w, True), (k, w, False), (v, w, False), (lam_p,), (subln.reshape(1, w),)]
    return _flash_call(_diff_kernel, n_heads, ops, q.shape[1], k.shape[1], w, scratch, n_k, q_off, False,
                       "diff_attn", tile=2048, lam_init=lam_init)


def _flash_fox(q, aq, kk, vv, n_heads, n_k, q_off):
    w = LANES
    scratch = lambda tq: [pltpu.VMEM((tq, LANES), F32), pltpu.VMEM((tq, 2 * w), F32), pltpu.VMEM((tq, 2 * w), BF16)]
    ops = [(q, w, True), (aq, w, True), (kk, 2 * w, False), (vv, 2 * w, False)]
    return _flash_call(_fox_kernel, n_heads, ops, q.shape[1], kk.shape[1], w, scratch, n_k, q_off, True, "fox_attn")


def _flash_mla(qq, kk, vv, n_heads, n_k, q_off):
    w = LANES
    scratch = lambda tq: [pltpu.VMEM((tq, LANES), F32), pltpu.VMEM((tq, 2 * w), F32)]
    ops = [(qq, 2 * w, True), (kk, 2 * w, False), (vv, 2 * w, False)]
    return _flash_call(_mla_kernel, n_heads, ops, qq.shape[1], kk.shape[1], w, scratch, n_k, q_off, False, "mla_attn")


NEW_ROWS = LANES


def _cached_frame(m_ref, l_ref, acc_ref, chains, finish, *, nkb, tk, ta, q_off, n_k, causal):
    n = pl.program_id(1)

    @pl.when(n == 0)
    def _():
        m_ref[...] = jnp.full_like(m_ref, NEG_INF)
        l_ref[...] = jnp.zeros_like(l_ref)
        acc_ref[...] = jnp.zeros_like(acc_ref)

    @pl.when(n < nkb)
    def _():
        chains(True, tk, None)

    @pl.when(n == nkb)
    def _():
        mask_fn = lambda s: jnp.where(_visible(s.shape, q_off, q_off, n_k, causal, ta), s, NEG_INF)
        chains(False, NEW_ROWS, mask_fn)
        finish()


def _dec_diff_kernel(q_ref, kc_ref, vc_ref, kn_ref, vn_ref, lam_ref, sub_ref, o_ref,
                     m_ref, l_ref, acc_ref, *, n_heads, lam_init, **frame):
    hd, g, tk = LANES, 2 * n_heads, frame['tk']

    def chains(cached, n_keys, mask_fn):
        for h in range(n_heads):
            if cached:
                v = jnp.concatenate([vc_ref[pl.ds(h, tk, stride=g), :],
                                     vc_ref[pl.ds(n_heads + h, tk, stride=g), :]],
                                    axis=1).astype(BF16)
            else:
                v = vn_ref[:, 2 * h * hd:(2 * h + 2) * hd]
            for c in range(2):
                j = 2 * h + c
                k = kc_ref[pl.ds(j, tk, stride=g), :].astype(BF16) if cached else kn_ref[:, j * hd:(j + 1) * hd]
                s = _dot_nt(q_ref[:, j * hd:(j + 1) * hd], k)
                if mask_fn is not None:
                    s = mask_fn(s)
                _attend(s, v, m_ref, l_ref, acc_ref, j)

    def finish():
        lp = lam_ref[...]
        lam = (jnp.exp(jnp.sum(lp[0:1] * lp[1:2], axis=-1, keepdims=True))
               - jnp.exp(jnp.sum(lp[2:3] * lp[3:4], axis=-1, keepdims=True)) + lam_init)
        for h in range(n_heads):
            l0 = jnp.sum(l_ref[2 * h], axis=1, keepdims=True)
            l1 = jnp.sum(l_ref[2 * h + 1], axis=1, keepdims=True)
            o = acc_ref[2 * h] / l0 - lam * (acc_ref[2 * h + 1] / l1)
            o_ref[:, 2 * h * hd:(2 * h + 2) * hd] = (_rms(o, sub_ref[...]) * (1.0 - lam_init)).astype(o_ref.dtype)

    _cached_frame(m_ref, l_ref, acc_ref, chains, finish, **frame)


def _dec_fox_kernel(q_ref, kc_ref, vc_ref, kn_ref, vn_ref, fq_ref, fk_ref, o_ref,
                    m_ref, l_ref, acc_ref, *, n_heads, **frame):
    hd, g, tk = LANES, n_heads, frame['tk']

    def chains(cached, n_keys, mask_fn):
        for h in range(n_heads):
            if cached:
                k = kc_ref[pl.ds(h, tk, stride=g), :].astype(BF16)
                v = vc_ref[pl.ds(h, tk, stride=g), :].astype(BF16)
            else:
                k = kn_ref[:, h * hd:(h + 1) * hd]
                v = vn_ref[:, h * hd:(h + 1) * hd]
            bias = (fq_ref[:, h:h + 1] - fk_ref[h:h + 1, 0:n_keys]) * LOG2E
            s = _dot_nt(q_ref[:, h * hd:(h + 1) * hd], k) + bias
            if mask_fn is not None:
                s = mask_fn(s)
            _attend(s, v, m_ref, l_ref, acc_ref, h)

    def finish():
        for h in range(n_heads):
            o = acc_ref[h] / jnp.sum(l_ref[h], axis=1, keepdims=True)
            o_ref[:, h * hd:(h + 1) * hd] = o.astype(o_ref.dtype)

    _cached_frame(m_ref, l_ref, acc_ref, chains, finish, **frame)


def _cached_attn(kernel, q, cache_k, cache_v, new_k, new_v, extra, extra_specs, groups, v_width, n_k, causal,
                 name, **kw):
    b, ta, _ = q.shape
    past = cache_k.shape[1]
    tk = _pick_tile(past, 1024, LANES)
    nkb = past // tk
    kc = cache_k.reshape(b, past * groups, LANES)
    if cache_v.shape[-1] == 2 * LANES:
        vc = cache_v.reshape(b, past, groups // 2, 2, LANES).swapaxes(2, 3).reshape(b, past * groups, LANES)
    else:
        vc = cache_v.reshape(b, past * groups, LANES)
    kn, vn = _pad_rows(new_k, NEW_ROWS), _pad_rows(new_v, NEW_ROWS)
    whole = lambda a: pl.BlockSpec((None,) + a.shape[1:], lambda bi, n: (bi,) + (0,) * (a.ndim - 1))
    cache = pl.BlockSpec((None, tk * groups, LANES), lambda bi, n: (bi, jnp.minimum(n, nkb - 1), 0))
    out_w = q.shape[2]
    return pl.pallas_call(
        functools.partial(kernel, nkb=nkb, tk=tk, ta=ta, q_off=past, n_k=n_k, causal=causal, **kw),
        out_shape=jax.ShapeDtypeStruct((b, ta, out_w), BF16),
        grid=(b, nkb + 1),
        in_specs=[whole(q), cache, cache, whole(kn), whole(vn)] + extra_specs(tk),
        out_specs=pl.BlockSpec((None, ta, out_w), lambda bi, n: (bi, 0, 0)),
        scratch_shapes=[pltpu.VMEM((groups, ta, LANES), F32), pltpu.VMEM((groups, ta, LANES), F32),
                        pltpu.VMEM((groups, ta, v_width), F32)],
        compiler_params=_params(("parallel", "arbitrary"), 4 * tk * groups * LANES * 4 + (16 << 20)),
        name=name,
    )(q, kc, vc, kn, vn, *extra)


def _dec_mla_kernel(q_ref, ckv_ref, kp_ref, ckvn_ref, kpn_ref, wk_ref, wv_ref, gk_ref, o_ref,
                    m_ref, l_ref, acc_ref, s_ref, qp_ref, *, n_heads, **frame):
    hd, ta = LANES, frame['ta']

    @pl.when(pl.program_id(1) == 0)
    def _():
        for h in range(n_heads):
            qp_ref[h * ta:(h + 1) * ta, :] = q_ref[:, (2 * h + 1) * hd:(2 * h + 2) * hd]

    def chains(cached, n_keys, mask_fn):
        ckv = ckv_ref[...].astype(BF16) if cached else ckvn_ref[...]
        kp = kp_ref[...] if cached else kpn_ref[...]
        kvn = _dot(ckv, wk_ref[...])
        for h in range(n_heads):
            kn = _rms(kvn[:, h * hd:(h + 1) * hd], gk_ref[...]).astype(BF16)
            s_ref[h * ta:(h + 1) * ta, 0:n_keys] = _dot_nt(q_ref[:, 2 * h * hd:(2 * h + 1) * hd], kn)
        s = s_ref[:, 0:n_keys] + _dot_nt(qp_ref[...], kp)
        if mask_fn is not None:
            s = mask_fn(s)
        _attend(s, ckv, m_ref, l_ref, acc_ref, slice(None))

    def finish():
        lat = acc_ref[...] / jnp.sum(l_ref[...], axis=1, keepdims=True)
        for h in range(n_heads):
            o = _dot(lat[h * ta:(h + 1) * ta, :].astype(BF16), wv_ref[:, h * hd:(h + 1) * hd])
            o_ref[:, h * hd:(h + 1) * hd] = o.astype(o_ref.dtype)

    _cached_frame(m_ref, l_ref, acc_ref, chains, finish, **frame)


def _dec_mla(qq, cache_ckv, kp_cache, new_ckv, new_kp, w_kn, w_v, g_nope_k, n_heads, n_k):
    b, ta, _ = qq.shape
    past, c = cache_ckv.shape[1:]
    tk = _pick_tile(past, 1024, LANES)
    nkb = past // tk
    rows = n_heads * ta
    ckvn, kpn = _pad_rows(new_ckv, NEW_ROWS), _pad_rows(new_kp, NEW_ROWS)
    whole = lambda a: pl.BlockSpec((None,) + a.shape[1:], lambda bi, n: (bi,) + (0,) * (a.ndim - 1))
    const = lambda a: pl.BlockSpec(a.shape, lambda bi, n: (0,) * a.ndim)
    blk = lambda w: pl.BlockSpec((None, tk, w), lambda bi, n: (bi, jnp.minimum(n, nkb - 1), 0))

    kernel = functools.partial(_dec_mla_kernel, n_heads=n_heads, nkb=nkb, tk=tk, ta=ta, q_off=past, n_k=n_k,
                               causal=False)
    gk = g_nope_k.reshape(1, LANES)
    return pl.pallas_call(
        kernel,
        out_shape=jax.ShapeDtypeStruct((b, ta, n_heads * LANES), BF16),
        grid=(b, nkb + 1),
        in_specs=[whole(qq), blk(c), blk(LANES), whole(ckvn), whole(kpn), const(w_kn), const(w_v), const(gk)],
        out_specs=pl.BlockSpec((None, ta, n_heads * LANES), lambda bi, n: (bi, 0, 0)),
        scratch_shapes=[pltpu.VMEM((rows, LANES), F32), pltpu.VMEM((rows, LANES), F32), pltpu.VMEM((rows, c), F32),
                        pltpu.VMEM((rows, tk), F32), pltpu.VMEM((rows, LANES), BF16)],
        compiler_params=_params(("parallel", "arbitrary"), 40 << 20),
        name="mla_attn_cached",
    )(qq, cache_ckv, kp_cache, ckvn, kpn, w_kn, w_v, gk)


def _cumsum_rows_kernel(x_ref, o_ref, carry_ref):
    @pl.when(pl.program_id(1) == 0)
    def _():
        carry_ref[...] = jnp.zeros_like(carry_ref)

    x = x_ref[...]
    tb = x.shape[1]
    r = lax.broadcasted_iota(jnp.int32, (tb, tb), 0)
    c = lax.broadcasted_iota(jnp.int32, (tb, tb), 1)
    tri = jnp.where(r <= c, 1.0, 0.0).astype(BF16)
    hi, mid, lo = _split3(x)
    cum = _dot(hi, tri) + _dot(mid, tri) + _dot(lo, tri) + carry_ref[:, 0:1]
    o_ref[...] = cum
    carry_ref[...] = jnp.broadcast_to(cum[:, tb - 1:tb], carry_ref.shape)


def _cumsum_rows(x, tb):
    b, g, t = x.shape
    return pl.pallas_call(
        _cumsum_rows_kernel,
        out_shape=jax.ShapeDtypeStruct((b, g, t), F32),
        grid=(b, t // tb),
        in_specs=[pl.BlockSpec((None, g, tb), lambda bi, i: (bi, 0, i))],
        out_specs=pl.BlockSpec((None, g, tb), lambda bi, i: (bi, 0, i)),
        scratch_shapes=[pltpu.VMEM((g, LANES), F32)],
        compiler_params=_params(("parallel", "arbitrary"), 32 << 20),
        name="forget_cumsum",
    )(x)


def _dec_diff(q, cache_k, cache_v, new_k, new_v, lam_p, subln, lam_init, n_heads, n_k):
    b, past = cache_k.shape[:2]
    specs = lambda tk: [pl.BlockSpec(lam_p.shape, lambda bi, n: (0, 0)),
                        pl.BlockSpec((1, 2 * LANES), lambda bi, n: (0, 0))]
    return _cached_attn(_dec_diff_kernel, q, cache_k, cache_v, new_k, new_v, [lam_p, subln.reshape(1, 2 * LANES)],
                        specs, 2 * n_heads, 2 * LANES, n_k, False, "diff_attn_cached",
                        n_heads=n_heads, lam_init=lam_init)


def _dec_fox(q, cache_k, cache_v, new_k, new_v, past_logf, new_logf, n_heads, n_k):
    b, past = cache_k.shape[:2]
    ta = q.shape[1]
    tk = _pick_tile(past, 1024, LANES)
    lf = jnp.concatenate([past_logf.astype(F32), new_logf[:, :, :n_heads]], axis=1)
    lf_rows = _pad_lanes(jnp.swapaxes(lf, 1, 2), past + tk)
    f_rows = _cumsum_rows(lf_rows, tk)
    f_q = _pad_lanes(jnp.swapaxes(f_rows[:, :, past:past + ta], 1, 2), LANES)
    specs = lambda tk: [pl.BlockSpec((None, ta, LANES), lambda bi, n: (bi, 0, 0)),
                        pl.BlockSpec((None, n_heads, tk), lambda bi, n: (bi, 0, n))]
    return _cached_attn(_dec_fox_kernel, q, cache_k, cache_v, new_k, new_v, [f_q, f_rows], specs,
                        n_heads, LANES, n_k, True, "fox_attn_cached", n_heads=n_heads)


def _rope_half(y, cos, sin, rope_dim):
    half = rope_dim // 2
    lane = lax.broadcasted_iota(jnp.int32, y.shape, 1)
    rot = jnp.where(lane < half, pltpu.roll(y, LANES - half, 1), pltpu.roll(y, half, 1))
    return y * cos + rot * sin


def _rms_low(x, gain, n):
    return x * lax.rsqrt(jnp.sum(x * x, axis=-1, keepdims=True) * (1.0 / n) + EPS) * gain


def _odd_post_kernel(z_ref, cos_ref, sin_ref, gq_ref, gkv_ref, gr_ref,
                     cq_ref, ckvf_ref, ckvb_ref, kpf_ref, kpb_ref, *, q_lora, kv_lora, rope_dim):
    cq_ref[...] = _rms(z_ref[:, 0:q_lora], gq_ref[...]).astype(BF16)
    ckv = _rms(z_ref[:, q_lora:q_lora + kv_lora], gkv_ref[...])
    ckvf_ref[...] = ckv
    ckvb_ref[...] = ckv.astype(BF16)
    kp = _rms_low(z_ref[:, q_lora + kv_lora:q_lora + kv_lora + LANES], gr_ref[...], rope_dim)
    kp = _rope_half(kp, cos_ref[...], sin_ref[...], rope_dim)
    kpf_ref[...] = kp
    kpb_ref[...] = kp.astype(BF16)


def _odd_post(z, cos, sin, g_cq, g_ckv, g_rope_k_pad, q_lora, kv_lora, rope_dim):
    b, t, n = z.shape
    tm = _pick_tile(t, 512, 16)
    row = lambda w: pl.BlockSpec((None, tm, w), lambda bi, i: (bi, i, 0))
    tab = pl.BlockSpec((tm, LANES), lambda bi, i: (i, 0))
    vec = lambda w: pl.BlockSpec((1, w), lambda bi, i: (0, 0))
    outs = [(q_lora, BF16), (kv_lora, F32), (kv_lora, BF16), (LANES, F32), (LANES, BF16)]
    return pl.pallas_call(
        functools.partial(_odd_post_kernel, q_lora=q_lora, kv_lora=kv_lora, rope_dim=rope_dim),
        out_shape=[jax.ShapeDtypeStruct((b, t, w), dt) for w, dt in outs],
        grid=(b, t // tm),
        in_specs=[row(n), tab, tab, vec(q_lora), vec(kv_lora), vec(LANES)],
        out_specs=[row(w) for w, _ in outs],
        compiler_params=_params(("parallel", "parallel"), 32 << 20),
        name="odd_post",
    )(z, cos, sin, g_cq.reshape(1, -1), g_ckv.reshape(1, -1), g_rope_k_pad)


def _qup_kernel(cq_ref, w_ref, cos_ref, sin_ref, gn_ref, gr_ref, qq_ref, *, heads, rope_dim, q_scale):
    cq = cq_ref[...]
    cos, sin = cos_ref[...], sin_ref[...]
    for h in range(heads):
        q = _dot(cq, w_ref[:, 2 * h * LANES:(2 * h + 2) * LANES])
        qn = _rms(q[:, 0:LANES], gn_ref[...])
        qq_ref[:, 2 * h * LANES:(2 * h + 1) * LANES] = (qn * q_scale).astype(BF16)
        qp = _rms_low(q[:, LANES:2 * LANES], gr_ref[...], rope_dim)
        qp = _rope_half(qp, cos, sin, rope_dim)
        qq_ref[:, (2 * h + 1) * LANES:(2 * h + 2) * LANES] = (qp * q_scale).astype(BF16)


def _qup(cq, w_pad, cos, sin, g_nope_q, g_rope_q_pad, n_heads, rope_dim, qk_dim):
    b, t, kq = cq.shape
    tm = _pick_tile(t, 512, 16)
    hg = 4 if n_heads % 4 == 0 else 1
    tn = hg * 2 * LANES
    return pl.pallas_call(
        functools.partial(_qup_kernel, heads=hg, rope_dim=rope_dim, q_scale=qk_dim ** -0.5 * LOG2E),
        out_shape=jax.ShapeDtypeStruct((b, t, n_heads * 2 * LANES), BF16),
        grid=(b, t // tm, n_heads // hg),
        in_specs=[
            pl.BlockSpec((None, tm, kq), lambda bi, i, j: (bi, i, 0)),
            pl.BlockSpec((kq, tn), lambda bi, i, j: (0, j)),
            pl.BlockSpec((tm, LANES), lambda bi, i, j: (i, 0)),
            pl.BlockSpec((tm, LANES), lambda bi, i, j: (i, 0)),
            pl.BlockSpec((1, LANES), lambda bi, i, j: (0, 0)),
            pl.BlockSpec((1, LANES), lambda bi, i, j: (0, 0)),
        ],
        out_specs=pl.BlockSpec((None, tm, tn), lambda bi, i, j: (bi, i, j)),
        compiler_params=_params(("parallel", "parallel", "arbitrary"), 32 << 20),
        name="mla_q_up",
    )(cq, w_pad, cos, sin, g_nope_q.reshape(1, LANES), g_rope_q_pad)


def _kvup_kernel(ckv_ref, kp_ref, w_ref, gn_ref, kk_ref, v_ref, *, heads):
    ckv = ckv_ref[...]
    kp = kp_ref[...]
    for h in range(heads):
        kv = _dot(ckv, w_ref[:, 2 * h * LANES:(2 * h + 2) * LANES])
        kn = _rms(kv[:, 0:LANES], gn_ref[...])
        kk_ref[:, 2 * h * LANES:(2 * h + 1) * LANES] = kn.astype(BF16)
        kk_ref[:, (2 * h + 1) * LANES:(2 * h + 2) * LANES] = kp
        v_ref[:, 2 * h * LANES:(2 * h + 1) * LANES] = kv[:, LANES:2 * LANES].astype(BF16)
        v_ref[:, (2 * h + 1) * LANES:(2 * h + 2) * LANES] = jnp.ones((kv.shape[0], LANES), BF16)


def _kvup(ckv, kp, w, g_nope_k, n_heads):
    b, t, kk = ckv.shape
    tm = _pick_tile(t, 512, LANES)
    hg = 4 if n_heads % 4 == 0 else 1
    tn = hg * 2 * LANES
    return pl.pallas_call(
        functools.partial(_kvup_kernel, heads=hg),
        out_shape=[jax.ShapeDtypeStruct((b, t, n_heads * 2 * LANES), BF16)] * 2,
        grid=(b, t // tm, n_heads // hg),
        in_specs=[
            pl.BlockSpec((None, tm, kk), lambda bi, i, j: (bi, i, 0)),
            pl.BlockSpec((None, tm, LANES), lambda bi, i, j: (bi, i, 0)),
            pl.BlockSpec((kk, tn), lambda bi, i, j: (0, j)),
            pl.BlockSpec((1, LANES), lambda bi, i, j: (0, 0)),
        ],
        out_specs=[pl.BlockSpec((None, tm, tn), lambda bi, i, j: (bi, i, j))] * 2,
        compiler_params=_params(("parallel", "parallel", "arbitrary"), 32 << 20),
        name="mla_kv_up",
    )(ckv, kp, w, g_nope_k.reshape(1, LANES))


def _rope_tables(pos, dim):
    half = dim // 2
    inv = ROPE_THETA ** (-jnp.arange(half, dtype=F32) * 2.0 / dim)
    ang = pos.astype(F32)[:, None] * inv[None, :]
    cos, sin = jnp.cos(ang), jnp.sin(ang)
    pad = ((0, 0), (0, LANES - dim))
    return (jnp.pad(jnp.concatenate([cos, cos], axis=-1), pad),
            jnp.pad(jnp.concatenate([-sin, sin], axis=-1), pad))


def _pad_lanes(a, width):
    return jnp.pad(a, [(0, 0)] * (a.ndim - 1) + [(0, width - a.shape[-1])])


def _pad_rows(a, rows):
    return jnp.pad(a, [(0, 0), (0, rows - a.shape[1])] + [(0, 0)] * (a.ndim - 2))


def _layer_stack(x, c_mod, tok_pos, seq_shape, caches, p):
    bx, tx, d = x.shape
    ba, ta = seq_shape
    depth = p['w_ffn_in'].shape[0]
    n_diff = p['n_diff']
    n_fox = p['n_fox']
    n_mla = p['n_mla']
    past_len = 0 if caches is None else caches[0].shape[2]
    n_k = past_len + ta
    cos128, sin128 = _rope_tables(tok_pos, LANES)
    rope_dim = p['rope_dim']
    cos_r, sin_r = _rope_tables(tok_pos, rope_dim)
    new = [[] for _ in range(7)]

    def mods(l, s):
        m = c_mod[l]
        sh, sc, gt = m[:, 3 * s], m[:, 3 * s + 1], m[:, 3 * s + 2]
        if bx == m.shape[0]:
            return tuple(a[:, None, :] for a in (sh, sc, gt))
        rep = lambda a: jnp.repeat(a, ta, axis=0).reshape(bx, tx, d)
        return rep(sh), rep(sc), rep(gt)

    seq = lambda a: a.reshape(ba, ta, a.shape[-1])
    for l in range(depth):
        i = l // 2
        g = p['norm_gains'][l]
        sh, sc, gt = mods(l, 0)
        x = _ffn(x, sh, sc, gt, g[0], p['w_ffn_in'], p['w_ffn_out'], l, 0, 0.5)
        sh, sc, gt = mods(l, 1)
        if l % 2 == 0:
            (qa, kaf, kab, vaf, vab, qb, kbf, kbb, vbf, vbb, lf) = _even_proj(
                x, sh, sc, g[1], p['w_in_even'][i], p['w_in_forget'][i], cos128, sin128,
                p['qk_norm_even'][i], p['b_forget_pad'][i], 2 * n_diff, n_fox)
            new[0].append(kaf.reshape(ba, ta, n_diff, 2, LANES))
            new[1].append(vaf.reshape(ba, ta, 2, n_diff, LANES).swapaxes(2, 3).reshape(ba, ta, n_diff, 2 * LANES))
            new[2].append(kbf.reshape(ba, ta, n_fox, LANES))
            new[3].append(vbf.reshape(ba, ta, n_fox, LANES))
            new[4].append(seq(lf)[:, :, :n_fox])
            lam_init = 0.8 - 0.6 * math.exp(-0.3 * l)
            if caches is None:
                aq, kk, vv = _fox_prep(lf, kbb, vbb, n_fox)
                oa = _flash_diff(qa, kab, vab, p['diff_lambda'][i], p['diff_subln'][i], lam_init, n_diff, n_k, 0)
                ob = _flash_fox(qb, aq, kk, vv, n_fox, n_k, 0)
            else:
                past = tuple(a[i] for a in caches[:5])
                oa = _dec_diff(seq(qa), past[0], past[1], seq(kab), seq(vab), p['diff_lambda'][i],
                               p['diff_subln'][i], lam_init, n_diff, n_k)
                ob = _dec_fox(seq(qb), past[2], past[3], seq(kbb), seq(vbb), past[4], seq(lf), n_fox, n_k)
            x = _outproj(oa.reshape(bx, tx, -1), ob.reshape(bx, tx, -1), 0, 0, p['w_out_even'][i], x, gt)
        else:
            q_lora, kv_lora = p['q_lora'], p['kv_lora']
            z = _modproj(x, sh, sc, g[1], p['w_in_odd'][i])
            cq, ckvf, ckvb, kpf, kpb = _odd_post(z, cos_r, sin_r, p['mla_cq_norm'][i], p['mla_ckv_norm'][i],
                                                 p['g_rope_pad'][i, 1:2], q_lora, kv_lora, rope_dim)
            new[5].append(seq(ckvf))
            new[6].append(seq(kpf)[:, :, :rope_dim])
            qq = _qup(cq, p['w_uq_pad'][i], cos_r, sin_r, p['mla_qk_norm_nope'][i, 0],
                      p['g_rope_pad'][i, 0:1], n_mla, rope_dim, p['mla_qk_dim'])
            if caches is None:
                kk, v = _kvup(ckvb, kpb, p['w_ukv'][i], p['mla_qk_norm_nope'][i, 1], n_mla)
                o = _flash_mla(qq, kk, v, n_mla, n_k, 0)
            else:
                kp_cache = _pad_lanes(caches[6][i], LANES).astype(BF16)
                o = _dec_mla(seq(qq), caches[5][i], kp_cache, seq(ckvb), seq(kpb), p['w_kn'][i], p['w_v'][i],
                             p['mla_qk_norm_nope'][i, 1], n_mla, n_k)
            o = o.reshape(bx, tx, -1)
            x = _outproj(o, o, 0, 1, p['w_out_odd'][i], x, gt)
        sh, sc, gt = mods(l, 2)
        x = _ffn(x, sh, sc, gt, g[2], p['w_ffn_in'], p['w_ffn_out'], l, 1, 0.5, final_gain=g[3])
    return x, tuple(jnp.stack(lst) for lst in new)


def kernel(x_prompt, x_sample, c_prompt, c_sample, cache_diff_k, cache_diff_v, cache_fox_k, cache_fox_v, cache_fox_logf, cache_mla_ckv, cache_mla_kpe, w_ada, b_ada, norm_gains, w_ffn_in, w_ffn_out, w_in_even, b_forget, qk_norm_even, diff_lambda, diff_subln, w_out_even, w_in_odd, mla_cq_norm, mla_ckv_norm, w_uq, w_ukv, mla_qk_norm_nope, mla_qk_norm_rope, w_out_odd):
    d = x_prompt.shape[-1]
    n_diff, n_fox = cache_diff_k.shape[3], cache_fox_k.shape[3]
    assert cache_diff_k.shape[-1] == LANES and cache_fox_k.shape[-1] == LANES
    q_lora, kv_lora = mla_cq_norm.shape[-1], mla_ckv_norm.shape[-1]
    rope_dim, nope = cache_mla_kpe.shape[-1], mla_qk_norm_nope.shape[-1]
    n_mla = w_uq.shape[-1] // (nope + rope_dim)
    assert nope == LANES and rope_dim <= LANES and w_ukv.shape[-1] == n_mla * 2 * LANES
    n_odd = w_uq.shape[0]
    n_main = w_in_even.shape[-1] - n_fox
    assert n_main == 6 * 2 * n_diff * LANES and n_fox <= LANES

    w_uq_pad = _pad_lanes(w_uq.reshape(n_odd, q_lora, n_mla, nope + rope_dim), 2 * LANES)
    p = {
        'n_diff': n_diff, 'n_fox': n_fox, 'n_mla': n_mla, 'rope_dim': rope_dim,
        'q_lora': q_lora, 'kv_lora': kv_lora, 'mla_qk_dim': nope + rope_dim,
        'norm_gains': norm_gains,
        'w_ffn_in': _interleave_gate_up(w_ffn_in), 'w_ffn_out': w_ffn_out.astype(BF16),
        'w_in_even': w_in_even[:, :, :n_main].astype(BF16),
        'w_in_forget': _pad_lanes(w_in_even[:, :, n_main:], LANES).astype(BF16),
        'b_forget_pad': _pad_lanes(b_forget, LANES)[:, None, :],
        'qk_norm_even': qk_norm_even, 'diff_lambda': diff_lambda, 'diff_subln': diff_subln,
        'w_out_even': w_out_even.astype(BF16),
        'w_in_odd': _pad_lanes(w_in_odd, q_lora + kv_lora + LANES).astype(BF16),
        'mla_cq_norm': mla_cq_norm, 'mla_ckv_norm': mla_ckv_norm,
        'w_uq_pad': w_uq_pad.reshape(n_odd, q_lora, n_mla * 2 * LANES).astype(BF16),
        'w_ukv': w_ukv.astype(BF16),
        'w_kn': w_ukv.reshape(n_odd, kv_lora, n_mla, 2 * LANES)[..., :LANES].reshape(n_odd, kv_lora, -1).astype(BF16),
        'w_v': w_ukv.reshape(n_odd, kv_lora, n_mla, 2 * LANES)[..., LANES:].reshape(n_odd, kv_lora, -1).astype(BF16),
        'mla_qk_norm_nope': mla_qk_norm_nope,
        'g_rope_pad': _pad_lanes(mla_qk_norm_rope, LANES),
        'w_out_odd': w_out_odd.astype(BF16),
    }

    bp, tp = x_prompt.shape[:2]
    bs, ts = x_sample.shape[:2]
    past_len = cache_diff_k.shape[2]
    mod = _ada(jnp.concatenate([c_prompt, c_sample], axis=0), w_ada, b_ada)
    mod = mod.reshape(mod.shape[0], bp + bs, N_MOD, d)

    pos_p = jnp.arange(tp, dtype=jnp.int32)
    y_prompt, st_p = _layer_stack(x_prompt, mod[:, :bp], pos_p, (bp, tp), None, p)

    pos_s = jnp.tile(past_len + jnp.arange(ts, dtype=jnp.int32), bs)
    caches = (cache_diff_k, cache_diff_v, cache_fox_k, cache_fox_v, cache_fox_logf, cache_mla_ckv, cache_mla_kpe)
    y_sample, st_s = _layer_stack(x_sample.reshape(1, bs * ts, d), mod[:, bp:], pos_s, (bs, ts), caches, p)
    return (y_prompt, y_sample.reshape(bs, ts, d)) + st_p + st_s
```

```python
import functools
import math

import numpy as np
import jax
import jax.numpy as jnp
from jax import lax
from jax.experimental import pallas as pl
from jax.experimental.pallas import tpu as pltpu

F32 = jnp.float32
BF16 = jnp.bfloat16

CHUNK = 64
ROPE_THETA = 10000.0
EPS = 1e-6
NEG_INF = -1e30
N_MOD = 9

LANES = 128
SUBLANES = 8
VMEM_CAP_BYTES = 56 * 1024 * 1024

LOG2E = math.log2(math.e)
CHUNK_SHIFT = CHUNK.bit_length() - 1
assert (1 << CHUNK_SHIFT) == CHUNK


def _round_up(n, m):
    return (n + m - 1) // m * m


def _pick_tile(n, target, quantum):
    if n <= target:
        return n
    best = None
    t = quantum
    while t <= target:
        if n % t == 0:
            best = t
        t += quantum
    assert best is not None, (n, target, quantum)
    return best


def _params(semantics, vmem_bytes):
    limit = int(min(max(vmem_bytes, 16 * 1024 * 1024), VMEM_CAP_BYTES))
    return pltpu.CompilerParams(dimension_semantics=semantics, vmem_limit_bytes=limit)


def _rms(x, gain):
    return x * lax.rsqrt(jnp.mean(x * x, axis=-1, keepdims=True) + EPS) * gain


def _rms_mxu(x, gain, n=LANES):
    sq = x * x
    hi = sq.astype(BF16)
    lo = (sq - hi.astype(F32)).astype(BF16)
    ones = jnp.ones((LANES, LANES), BF16)
    ss = _dot(hi, ones) + _dot(lo, ones)
    return x * lax.rsqrt(ss * (1.0 / n) + EPS) * gain


def _silu(g):
    return g / (1.0 + jnp.exp(-g))


def _dot(a, b):
    return jnp.dot(a, b, preferred_element_type=F32)


def _dot_nt(a, b):
    return lax.dot_general(a, b, (((1,), (1,)), ((), ())), preferred_element_type=F32)


def _ada_kernel(c_ref, w_ref, b_ref, o_ref):
    a = _silu(c_ref[...]).astype(BF16)
    o_ref[...] = _dot(a, w_ref[...].astype(BF16)) + b_ref[...]


def _ada(c_all, w_ada, b_ada):
    depth, d, n = w_ada.shape
    r = c_all.shape[0]
    tn = _pick_tile(n, 1024, LANES)
    return pl.pallas_call(
        _ada_kernel,
        out_shape=jax.ShapeDtypeStruct((depth, r, n), F32),
        grid=(depth, n // tn),
        in_specs=[
            pl.BlockSpec((r, d), lambda l, j: (0, 0)),
            pl.BlockSpec((None, d, tn), lambda l, j: (l, 0, j)),
            pl.BlockSpec((None, 1, tn), lambda l, j: (l, 0, j)),
        ],
        out_specs=pl.BlockSpec((None, r, tn), lambda l, j: (l, 0, j)),
        compiler_params=_params(("arbitrary", "arbitrary"), 2 * d * tn * 4 + 3 * d * tn * 2 + (4 << 20)),
        name="ada_mod",
    )(c_all, w_ada, b_ada.reshape(depth, 1, n))


def _mod_spec(mod, tm):
    d = mod.shape[-1]
    if mod.shape[1] == 1:
        return pl.BlockSpec((None, 1, d), lambda b, i, j: (b, 0, 0))
    return pl.BlockSpec((None, tm, d), lambda b, i, j: (b, i, 0))


def _ffn_kernel(x_ref, sh_ref, sc_ref, gt_ref, g_ref, wgu_ref, wo_ref, *rest,
                gate_mul, final_norm):
    if final_norm:
        gf_ref, o_ref, h_ref, acc_ref = rest
    else:
        o_ref, h_ref, acc_ref = rest
    f = pl.program_id(2)

    @pl.when(f == 0)
    def _():
        h = _rms(x_ref[...], g_ref[...]) * (1.0 + sc_ref[...]) + sh_ref[...]
        h_ref[...] = h.astype(BF16)
        acc_ref[...] = jnp.zeros_like(acc_ref)

    tf = wo_ref.shape[0]
    gu = _dot(h_ref[...], wgu_ref[...])
    a = (_silu(gu[:, 0:tf]) * gu[:, tf:2 * tf]).astype(BF16)
    acc_ref[...] += _dot(a, wo_ref[...])

    @pl.when(f == pl.num_programs(2) - 1)
    def _():
        xn = x_ref[...] + (gate_mul * gt_ref[...]) * acc_ref[...]
        if final_norm:
            xn = _rms(xn, gf_ref[...])
        o_ref[...] = xn


def _ffn_tile(ff):
    return _pick_tile(ff, 512, LANES)


def _gate_up_kernel(g_ref, u_ref, o_ref):
    tf = g_ref.shape[1]
    o_ref[:, 0:tf] = g_ref[...].astype(BF16)
    o_ref[:, tf:2 * tf] = u_ref[...].astype(BF16)


def _interleave_gate_up(w_in):
    nl, ns, d, ff2 = w_in.shape
    tf = _ffn_tile(ff2 // 2)
    nf = ff2 // 2 // tf
    return pl.pallas_call(
        _gate_up_kernel,
        out_shape=jax.ShapeDtypeStruct(w_in.shape, BF16),
        grid=(nl, ns, nf),
        in_specs=[pl.BlockSpec((None, None, d, tf), lambda l, s, f: (l, s, 0, f)),
                  pl.BlockSpec((None, None, d, tf), lambda l, s, f: (l, s, 0, nf + f))],
        out_specs=pl.BlockSpec((None, None, d, 2 * tf), lambda l, s, f: (l, s, 0, f)),
        compiler_params=_params(("parallel", "parallel", "parallel"), 4 * d * tf * 4 + 2 * d * 2 * tf * 2 + (8 << 20)),
        name="ffn_weight_prep",
    )(w_in, w_in)


def _ffn(x, sh, sc, gt, gain, w_in, w_out, layer, sub, gate_mul, final_gain=None):
    b, t, d = x.shape
    ff = w_out.shape[2]
    tm = _pick_tile(t, 512, 16)
    tf = _ffn_tile(ff)
    nf = ff // tf
    in_specs = [
        pl.BlockSpec((None, tm, d), lambda bi, i, f: (bi, i, 0)),
        _mod_spec(sh, tm), _mod_spec(sc, tm), _mod_spec(gt, tm),
        pl.BlockSpec((1, d), lambda bi, i, f: (0, 0)),
        pl.BlockSpec((None, None, d, 2 * tf), lambda bi, i, f: (layer, sub, 0, f)),
        pl.BlockSpec((None, None, tf, d), lambda bi, i, f: (layer, sub, f, 0)),
    ]
    args = [x, sh, sc, gt, gain.reshape(1, d), w_in, w_out]
    if final_gain is not None:
        in_specs.append(pl.BlockSpec((1, d), lambda bi, i, f: (0, 0)))
        args.append(final_gain.reshape(1, d))
    vmem = (4 * tm * d * 4 + tm * d * 2 + tm * d * 4 + 6 * d * tf * 2 + 4 * tm * tf * 4
            + 6 * tm * d * 4 * (sh.shape[1] != 1) + (4 << 20))
    return pl.pallas_call(
        functools.partial(_ffn_kernel, gate_mul=gate_mul, final_norm=final_gain is not None),
        out_shape=jax.ShapeDtypeStruct((b, t, d), F32),
        grid=(b, t // tm, nf),
        in_specs=in_specs,
        out_specs=pl.BlockSpec((None, tm, d), lambda bi, i, f: (bi, i, 0)),
        scratch_shapes=[pltpu.VMEM((tm, d), BF16), pltpu.VMEM((tm, d), F32)],
        compiler_params=_params(("parallel", "parallel", "arbitrary"), vmem),
        name="ffn",
    )(*args)


def _modproj_kernel(x_ref, sh_ref, sc_ref, g_ref, w_ref, o_ref, h_ref):
    @pl.when(pl.program_id(2) == 0)
    def _():
        h = _rms(x_ref[...], g_ref[...]) * (1.0 + sc_ref[...]) + sh_ref[...]
        h_ref[...] = h.astype(BF16)

    o_ref[...] = _dot(h_ref[...], w_ref[...])


def _modproj(x, sh, sc, gain, w):
    b, t, d = x.shape
    n = w.shape[1]
    tm = _pick_tile(t, 1024, 16)
    tn = _pick_tile(n, 1280, LANES)
    vmem = (2 * tm * d * 4 + tm * d * 2 + 2 * d * tn * 2 + 3 * tm * tn * 4
            + 4 * tm * d * 4 * (sh.shape[1] != 1) + 3 * tm * d * 4 + (4 << 20))
    return pl.pallas_call(
        _modproj_kernel,
        out_shape=jax.ShapeDtypeStruct((b, t, n), F32),
        grid=(b, t // tm, n // tn),
        in_specs=[
            pl.BlockSpec((None, tm, d), lambda bi, i, j: (bi, i, 0)),
            _mod_spec(sh, tm), _mod_spec(sc, tm),
            pl.BlockSpec((1, d), lambda bi, i, j: (0, 0)),
            pl.BlockSpec((d, tn), lambda bi, i, j: (0, j)),
        ],
        out_specs=pl.BlockSpec((None, tm, tn), lambda bi, i, j: (bi, i, j)),
        scratch_shapes=[pltpu.VMEM((tm, d), BF16)],
        compiler_params=_params(("parallel", "parallel", "arbitrary"), vmem),
        name="mod_proj",
    )(x, sh, sc, gain.reshape(1, d), w)


def _outproj_kernel(a1_ref, a2_ref, w1_ref, w2_ref, x_ref, gt_ref, o_ref):
    y = _dot(a1_ref[...], w1_ref[...]) + _dot(a2_ref[...], w2_ref[...])
    o_ref[...] = x_ref[...] + gt_ref[...] * y


def _outproj(a1, a2, blk1, blk2, w, x, gt):
    b, t, d = x.shape
    kh = w.shape[0] // 2
    tm = _pick_tile(t, 512, 16)
    vmem = 4 * tm * kh * 2 + 4 * kh * d * 2 + 5 * tm * d * 4 + 2 * tm * d * 4 * (gt.shape[1] != 1) + (4 << 20)
    return pl.pallas_call(
        _outproj_kernel,
        out_shape=jax.ShapeDtypeStruct((b, t, d), F32),
        grid=(b, t // tm, 1),
        in_specs=[
            pl.BlockSpec((None, tm, kh), lambda bi, i, j: (bi, i, blk1)),
            pl.BlockSpec((None, tm, kh), lambda bi, i, j: (bi, i, blk2)),
            pl.BlockSpec((kh, d), lambda bi, i, j: (0, 0)),
            pl.BlockSpec((kh, d), lambda bi, i, j: (1, 0)),
            pl.BlockSpec((None, tm, d), lambda bi, i, j: (bi, i, 0)),
            _mod_spec(gt, tm),
        ],
        out_specs=pl.BlockSpec((None, tm, d), lambda bi, i, j: (bi, i, 0)),
        compiler_params=_params(("parallel", "parallel", "arbitrary"), vmem),
        name="out_proj",
    )(a1, a2, w, w, x, gt)


def _rope128(y, cos, sin):
    return y * cos + pltpu.roll(y, LANES // 2, 1) * sin


def _log_sigmoid(x):
    return jnp.minimum(x, 0.0) - jnp.log(1.0 + jnp.exp(-jnp.abs(x)))


def _even_proj_kernel(x_ref, sh_ref, sc_ref, g_ref, w_ref, wf_ref, cos_ref, sin_ref, gn_ref, bf_ref,
                      qa_ref, kaf_ref, kab_ref, vaf_ref, vab_ref,
                      qb_ref, kbf_ref, kbb_ref, vbf_ref, vbb_ref, lf_ref, h_ref,
                      *, n_grp, n_forget, q_scale):
    j = pl.program_id(2)
    tm, hd = x_ref.shape[0], LANES

    @pl.when(j == 0)
    def _():
        h = (_rms(x_ref[...], g_ref[...]) * (1.0 + sc_ref[...]) + sh_ref[...]).astype(BF16)
        h_ref[...] = h
        fg = _dot(h, wf_ref[...]) + bf_ref[...]
        lane = lax.broadcasted_iota(jnp.int32, fg.shape, 1)
        lf_ref[...] = jnp.where(lane < n_forget, _log_sigmoid(fg), 0.0)

    def heads():
        h = h_ref[...]
        for c in range(n_grp // 2):
            z = _dot(h, w_ref[:, 2 * c * hd:(2 * c + 2) * hd])
            yield z[:, 0:hd]
            yield z[:, hd:2 * hd]

    @pl.when(j == 0)
    def _():
        for i, z in enumerate(heads()):
            q = _rope128(_rms_mxu(z, gn_ref[0:1, :]), cos_ref[...], sin_ref[...])
            qa_ref[:, i * hd:(i + 1) * hd] = (q * q_scale).astype(BF16)

    @pl.when(j == 1)
    def _():
        for i, z in enumerate(heads()):
            k = _rope128(_rms_mxu(z, gn_ref[1:2, :]), cos_ref[...], sin_ref[...])
            kaf_ref[pl.ds(i, tm, stride=n_grp), :] = k
            kab_ref[:, i * hd:(i + 1) * hd] = k.astype(BF16)

    @pl.when(j == 2)
    def _():
        for i, z in enumerate(heads()):
            vaf_ref[pl.ds((i % 2) * (n_grp // 2) + i // 2, tm, stride=n_grp), :] = z
            vab_ref[:, i * hd:(i + 1) * hd] = z.astype(BF16)

    @pl.when(j == 3)
    def _():
        for i, z in enumerate(heads()):
            qb_ref[:, i * hd:(i + 1) * hd] = (_rms_mxu(z, gn_ref[2:3, :]) * q_scale).astype(BF16)

    @pl.when(j == 4)
    def _():
        for i, z in enumerate(heads()):
            k = _rms_mxu(z, gn_ref[3:4, :])
            kbf_ref[:, i * hd:(i + 1) * hd] = k
            kbb_ref[:, i * hd:(i + 1) * hd] = k.astype(BF16)

    @pl.when(j == 5)
    def _():
        for i, z in enumerate(heads()):
            vbf_ref[:, i * hd:(i + 1) * hd] = z
            vbb_ref[:, i * hd:(i + 1) * hd] = z.astype(BF16)


def _even_proj(x, sh, sc, gain, w_main, w_forget, cos, sin, qk_gain, b_forget_pad, n_grp, n_fox):
    b, t, d = x.shape
    wg = n_grp * LANES
    assert w_main.shape[1] == 6 * wg and n_fox == n_grp
    tm = _pick_tile(t, 512, 16)
    row = lambda w: pl.BlockSpec((None, tm, w), lambda bi, i, j: (bi, i, 0))
    tab = pl.BlockSpec((tm, LANES), lambda bi, i, j: (i, 0))
    const = lambda shape: pl.BlockSpec(shape, lambda bi, i, j: (0,) * len(shape))
    outs = [(wg, BF16), (wg, F32), (wg, BF16), (wg, F32), (wg, BF16),
            (wg, BF16), (wg, F32), (wg, BF16), (wg, F32), (wg, BF16), (LANES, F32)]
    out_shape = [jax.ShapeDtypeStruct((b, t, w), dt) for w, dt in outs]
    out_specs = [row(w) for w, _ in outs]
    for idx in (1, 3):
        out_shape[idx] = jax.ShapeDtypeStruct((b, t * n_grp, LANES), F32)
        out_specs[idx] = pl.BlockSpec((None, tm * n_grp, LANES), lambda bi, i, j: (bi, i, 0))
    vmem = (2 * tm * d * 4 + tm * d * 2 + 4 * d * wg * 2 + 2 * sum(tm * w * jnp.dtype(dt).itemsize for w, dt in outs)
            + 4 * tm * wg * 4 + 4 * tm * d * 4 * (sh.shape[1] != 1) + (4 << 20))
    return pl.pallas_call(
        functools.partial(_even_proj_kernel, n_grp=n_grp, n_forget=n_fox, q_scale=LANES ** -0.5 * LOG2E),
        out_shape=out_shape,
        grid=(b, t // tm, 6),
        in_specs=[pl.BlockSpec((None, tm, d), lambda bi, i, j: (bi, i, 0)),
                  _mod_spec(sh, tm), _mod_spec(sc, tm), const((1, d)),
                  pl.BlockSpec((d, wg), lambda bi, i, j: (0, j)), const((d, LANES)),
                  tab, tab, const((4, LANES)), const((1, LANES))],
        out_specs=out_specs,
        scratch_shapes=[pltpu.VMEM((tm, d), BF16)],
        compiler_params=_params(("parallel", "parallel", "arbitrary"), vmem),
        name="even_proj",
    )(x, sh, sc, gain.reshape(1, d), w_main, w_forget, cos, sin, qk_gain, b_forget_pad)


def _split3(x):
    hi = x.astype(BF16)
    r = x - hi.astype(F32)
    mid = r.astype(BF16)
    lo = (r - mid.astype(F32)).astype(BF16)
    return hi, mid, lo


def _fox_prep_kernel(lf_ref, k_ref, v_ref, aq_ref, kk_ref, vv_ref, carry_ref, *, n_heads):
    @pl.when(pl.program_id(1) == 0)
    def _():
        carry_ref[...] = jnp.zeros_like(carry_ref)

    x = lf_ref[...]
    tb = x.shape[0]
    r = lax.broadcasted_iota(jnp.int32, (tb, tb), 0)
    c = lax.broadcasted_iota(jnp.int32, (tb, tb), 1)
    tri = jnp.where(r >= c, 1.0, 0.0).astype(BF16)
    hi, mid, lo = _split3(x)
    cum = _dot(tri, hi) + _dot(tri, mid) + _dot(tri, lo) + carry_ref[...]
    carry_ref[...] = cum[tb - 1:tb, :]
    fh, fm, fl = (p.astype(F32) for p in _split3(cum * LOG2E))
    lane = lax.broadcasted_iota(jnp.int32, (tb, LANES), 1)
    ones_q = jnp.where((lane >= 3) & (lane < 6), 1.0, 0.0)
    ones_k = jnp.where(lane < 3, 1.0, 0.0)
    for h in range(n_heads):
        a, m, l = fh[:, h:h + 1], fm[:, h:h + 1], fl[:, h:h + 1]
        aq = jnp.where(lane == 0, a, jnp.where(lane == 1, m, jnp.where(lane == 2, l, ones_q)))
        ak = jnp.where(lane == 3, -a, jnp.where(lane == 4, -m, jnp.where(lane == 5, -l, ones_k)))
        aq_ref[:, h * LANES:(h + 1) * LANES] = aq.astype(BF16)
        kk_ref[:, 2 * h * LANES:(2 * h + 1) * LANES] = k_ref[:, h * LANES:(h + 1) * LANES]
        kk_ref[:, (2 * h + 1) * LANES:(2 * h + 2) * LANES] = ak.astype(BF16)
        vv_ref[:, 2 * h * LANES:(2 * h + 1) * LANES] = v_ref[:, h * LANES:(h + 1) * LANES]
        vv_ref[:, (2 * h + 1) * LANES:(2 * h + 2) * LANES] = jnp.ones((tb, LANES), BF16)


def _fox_prep(logf_pad, k_all, v_all, n_heads):
    b, t, _ = logf_pad.shape
    tb = _pick_tile(t, 512, LANES)
    w = n_heads * LANES
    narrow = pl.BlockSpec((None, tb, w), lambda bi, i: (bi, i, 0))
    wide = pl.BlockSpec((None, tb, 2 * w), lambda bi, i: (bi, i, 0))
    return pl.pallas_call(
        functools.partial(_fox_prep_kernel, n_heads=n_heads),
        out_shape=[jax.ShapeDtypeStruct((b, t, w), BF16), jax.ShapeDtypeStruct((b, t, 2 * w), BF16),
                   jax.ShapeDtypeStruct((b, t, 2 * w), BF16)],
        grid=(b, t // tb),
        in_specs=[pl.BlockSpec((None, tb, LANES), lambda bi, i: (bi, i, 0)), narrow, narrow],
        out_specs=[narrow, wide, wide],
        scratch_shapes=[pltpu.VMEM((1, LANES), F32)],
        compiler_params=_params(("parallel", "arbitrary"), 32 << 20),
        name="fox_prep",
    )(logf_pad, k_all, v_all)


FLAG_FIRST, FLAG_LAST = 1, 2
KIND_SHIFT = 2
KIND_FULL, KIND_MASK, KIND_DIAG = 0, 1, 2
SCORE_LOOKAHEAD = 1


def _pair_table(n_q, n_k, tq, tk, q_off, causal):
    tk_pad = _round_up(n_k, tk)
    aligned = tq == tk and q_off % tq == 0
    qi, kj, fl = [], [], []
    for i in range(n_q // tq):
        qmin, qmax = q_off + i * tq, q_off + (i + 1) * tq - 1
        row = []
        for j in range(tk_pad // tk):
            kmin, kmax = j * tk, min((j + 1) * tk, n_k) - 1
            if kmin >= n_k:
                continue
            if causal:
                any_vis, all_vis = kmin <= qmax, kmax <= qmin
            else:
                any_vis, all_vis = kmin // CHUNK <= qmax // CHUNK, kmax // CHUNK <= qmin // CHUNK
            all_vis = all_vis and (j + 1) * tk <= n_k
            if any_vis:
                diag = aligned and kmin == qmin and (j + 1) * tk <= n_k
                row.append((j, KIND_FULL if all_vis else KIND_DIAG if diag else KIND_MASK))
        assert row and row[0][0] == 0
        for idx, (j, kind) in enumerate(row):
            qi.append(i)
            kj.append(j)
            fl.append((kind << KIND_SHIFT) | (FLAG_FIRST if idx == 0 else 0) | (FLAG_LAST if idx == len(row) - 1 else 0))
    kinds = sorted({f >> KIND_SHIFT for f in fl})
    as_arr = lambda v: jnp.asarray(np.array(v, np.int32))
    return as_arr(qi), as_arr(kj), as_arr(fl), kinds


def _visible(shape, qpos0, kpos0, n_k, causal, row_period=None):
    if row_period is None:
        rows = qpos0 + lax.broadcasted_iota(jnp.int32, shape, 0)
    else:
        one = lax.broadcasted_iota(jnp.int32, (row_period, shape[1]), 0)
        rows = qpos0 + jnp.concatenate([one] * (shape[0] // row_period), axis=0)
    cols = kpos0 + lax.broadcasted_iota(jnp.int32, shape, 1)
    if causal:
        ok = cols <= rows
    else:
        ok = (cols >> CHUNK_SHIFT) <= (rows >> CHUNK_SHIFT)
    return ok if n_k is None else ok & (cols < n_k)


def _lane_tile(x, n):
    return x if n == LANES else jnp.concatenate([x] * (n // LANES), axis=1)


def _lane_fold(p):
    acc = p[:, 0:LANES]
    for c in range(1, p.shape[1] // LANES):
        acc = acc + p[:, c * LANES:(c + 1) * LANES]
    return acc


def _attend(s, v, m_ref, l_ref, acc_ref, idx):
    m_prev = m_ref[idx]
    m_new = jnp.maximum(m_prev, jnp.max(s, axis=1, keepdims=True))
    alpha = jnp.exp2(m_prev - m_new)
    p = jnp.exp2(s - _lane_tile(m_new, s.shape[1]))
    if l_ref is not None:
        l_ref[idx] = alpha * l_ref[idx] + _lane_fold(p)
    acc_ref[idx] = _lane_tile(alpha, v.shape[1]) * acc_ref[idx] + _dot(p.astype(v.dtype), v)
    m_ref[idx] = m_new


def _flash_frame(qi_ref, kj_ref, fl_ref, m_ref, l_ref, acc_ref, scores, update, finish, first=None, *,
                 tq, tk, rs, rs_full, q_off, n_k, causal, kinds):
    n = pl.program_id(2)
    flags = fl_ref[n]
    kind = flags >> KIND_SHIFT

    @pl.when((flags & FLAG_FIRST) != 0)
    def _():
        m_ref[...] = jnp.full_like(m_ref, NEG_INF)
        if l_ref is not None:
            l_ref[...] = jnp.zeros_like(l_ref)
        acc_ref[...] = jnp.zeros_like(acc_ref)
        if first is not None:
            first()

    qpos0 = q_off + qi_ref[n] * tq
    kpos0 = kj_ref[n] * tk
    rs_masked = rs

    def run(k):
        rs = rs_full if k == KIND_FULL else rs_masked

        def masked_scores(r):
            n_keys = (r + 1) * rs if k == KIND_DIAG else tk
            rows = slice(r * rs, (r + 1) * rs)
            tiles = scores(rows, n_keys)
            if k == KIND_DIAG:
                lo = r * rs
                vis = _visible((rs, rs), qpos0 + lo, kpos0 + lo, None, causal)
                blks = [jnp.where(vis, s[:, lo:], NEG_INF) for s in tiles]
                tiles = blks if lo == 0 else [jnp.concatenate([s[:, :lo], b], axis=1) for s, b in zip(tiles, blks)]
            elif k == KIND_MASK:
                vis = _visible(tiles[0].shape, qpos0 + r * rs, kpos0, n_k, causal)
                tiles = [jnp.where(vis, s, NEG_INF) for s in tiles]
            return rows, n_keys, tiles

        order = list(range(tq // rs))
        if k == KIND_DIAG:
            order.reverse()
        pending = [masked_scores(r) for r in order[:SCORE_LOOKAHEAD]]
        for idx in range(len(order)):
            rows, n_keys, tiles = pending.pop(0)
            if idx + SCORE_LOOKAHEAD < len(order):
                pending.append(masked_scores(order[idx + SCORE_LOOKAHEAD]))
            update(rows, n_keys, tiles)

    for k in kinds:
        pl.when(kind == k)(functools.partial(run, k))

    @pl.when((flags & FLAG_LAST) != 0)
    def _():
        finish()


def _diff_kernel(qi_ref, kj_ref, fl_ref, q_ref, k_ref, v_ref, lam_ref, sub_ref, o_ref,
                 m_ref, l_ref, acc_ref, *, lam_init, **frame):
    hd = LANES

    def scores(rows, n_keys):
        return [_dot_nt(q_ref[rows, c * hd:(c + 1) * hd], k_ref[0:n_keys, c * hd:(c + 1) * hd]) for c in range(2)]

    def update(rows, n_keys, tiles):
        v = v_ref[0:n_keys, :]
        for c in range(2):
            _attend(tiles[c], v, m_ref, l_ref, acc_ref, (c, rows))

    def finish():
        lp = lam_ref[...]
        lam = (jnp.exp(jnp.sum(lp[0:1] * lp[1:2], axis=-1, keepdims=True))
               - jnp.exp(jnp.sum(lp[2:3] * lp[3:4], axis=-1, keepdims=True)) + lam_init)
        l0 = jnp.sum(l_ref[0], axis=1, keepdims=True)
        l1 = jnp.sum(l_ref[1], axis=1, keepdims=True)
        o = acc_ref[0] / l0 - lam * (acc_ref[1] / l1)
        o_ref[...] = (_rms(o, sub_ref[...]) * (1.0 - lam_init)).astype(o_ref.dtype)

    _flash_frame(qi_ref, kj_ref, fl_ref, m_ref, l_ref, acc_ref, scores, update, finish, **frame)


def _ones_finish(acc_ref, o_ref):
    o_ref[...] = (acc_ref[:, 0:LANES] / acc_ref[:, LANES:2 * LANES]).astype(o_ref.dtype)


def _fox_kernel(qi_ref, kj_ref, fl_ref, q_ref, aq_ref, k_ref, v_ref, o_ref,
                m_ref, acc_ref, qq_ref, **frame):
    def first():
        qq_ref[:, 0:LANES] = q_ref[...]
        qq_ref[:, LANES:2 * LANES] = aq_ref[...]

    def scores(rows, n_keys):
        return [_dot_nt(qq_ref[rows, :], k_ref[0:n_keys, :])]

    def update(rows, n_keys, tiles):
        _attend(tiles[0], v_ref[0:n_keys, :], m_ref, None, acc_ref, rows)

    _flash_frame(qi_ref, kj_ref, fl_ref, m_ref, None, acc_ref, scores, update,
                 functools.partial(_ones_finish, acc_ref, o_ref), first, **frame)


def _mla_kernel(qi_ref, kj_ref, fl_ref, q_ref, k_ref, v_ref, o_ref,
                m_ref, acc_ref, **frame):
    def scores(rows, n_keys):
        return [_dot_nt(q_ref[rows, :], k_ref[0:n_keys, :])]

    def update(rows, n_keys, tiles):
        _attend(tiles[0], v_ref[0:n_keys, :], m_ref, None, acc_ref, rows)

    _flash_frame(qi_ref, kj_ref, fl_ref, m_ref, None, acc_ref, scores, update,
                 functools.partial(_ones_finish, acc_ref, o_ref), **frame)


def _flash_tiles(n_q, n_k_pad, tile):
    tq = _pick_tile(n_q, tile, 16)
    tk = next((t for t in (tile, tile // 2, tile // 4) if n_k_pad % t == 0), None) or _pick_tile(n_k_pad, tile, LANES)
    rs = _pick_tile(tq, 256, 16)
    return tq, tk, rs


def _flash_call(kernel, n_heads, operands, n_q, n_k_pad, out_width, scratch, n_k, q_off, causal, name,
                tile=2048, **kw):
    b = operands[0][0].shape[0]
    tq, tk, rs = _flash_tiles(n_q, n_k_pad, tile)
    qi, kj, fl, kinds = _pair_table(n_q, n_k, tq, tk, q_off, causal)
    if KIND_DIAG in kinds:
        assert rs % CHUNK == 0 and rs % LANES == 0

    in_specs, args = [], []
    for op in operands:
        a = op[0]
        if len(op) == 1:
            in_specs.append(pl.BlockSpec(a.shape, lambda bi, h, n, qi, kj, fl: (0, 0)))
        elif op[2]:
            in_specs.append(pl.BlockSpec((None, tq, op[1]), lambda bi, h, n, qi, kj, fl: (bi, qi[n], h)))
        else:
            in_specs.append(pl.BlockSpec((None, tk, op[1]), lambda bi, h, n, qi, kj, fl: (bi, kj[n], h)))
        args.append(a)
    vmem = 2 * tq * tk * 4 + 8 * max(tq, tk) * 2 * LANES * 2 * len(operands) + 10 * tq * 2 * LANES * 4 + (8 << 20)
    grid_spec = pltpu.PrefetchScalarGridSpec(
        num_scalar_prefetch=3,
        grid=(b, n_heads, int(qi.shape[0])),
        in_specs=in_specs,
        out_specs=pl.BlockSpec((None, tq, out_width), lambda bi, h, n, qi, kj, fl: (bi, qi[n], h)),
        scratch_shapes=scratch(tq),
    )
    return pl.pallas_call(
        functools.partial(kernel, tq=tq, tk=tk, rs=rs, rs_full=_pick_tile(tq, 4 * rs, 16), q_off=q_off, n_k=n_k, causal=causal, kinds=kinds, **kw),
        out_shape=jax.ShapeDtypeStruct((b, n_q, n_heads * out_width), BF16),
        grid_spec=grid_spec,
        compiler_params=_params(("parallel", "parallel", "arbitrary"), vmem),
        name=name,
    )(qi, kj, fl, *args)


def _flash_diff(q, k, v, lam_p, subln, lam_init, n_heads, n_k, q_off):
    w = 2 * LANES
    scratch = lambda tq: [pltpu.VMEM((2, tq, LANES), F32), pltpu.VMEM((2, tq, LANES), F32), pltpu.VMEM((2, tq, w), F32)]
    ops = [(q, w, True), (k, w, False), (v, w, False), (lam_p,), (subln.reshape(1, w),)]
    return _flash_call(_diff_kernel, n_heads, ops, q.shape[1], k.shape[1], w, scratch, n_k, q_off, False,
                       "diff_attn", tile=2048, lam_init=lam_init)


def _flash_fox(q, aq, kk, vv, n_heads, n_k, q_off):
    w = LANES
    scratch = lambda tq: [pltpu.VMEM((tq, LANES), F32), pltpu.VMEM((tq, 2 * w), F32), pltpu.VMEM((tq, 2 * w), BF16)]
    ops = [(q, w, True), (aq, w, True), (kk, 2 * w, False), (vv, 2 * w, False)]
    return _flash_call(_fox_kernel, n_heads, ops, q.shape[1], kk.shape[1], w, scratch, n_k, q_off, True, "fox_attn")


def _flash_mla(qq, kk, vv, n_heads, n_k, q_off):
    w = LANES
    scratch = lambda tq: [pltpu.VMEM((tq, LANES), F32), pltpu.VMEM((tq, 2 * w), F32)]
    ops = [(qq, 2 * w, True), (kk, 2 * w, False), (vv, 2 * w, False)]
    return _flash_call(_mla_kernel, n_heads, ops, qq.shape[1], kk.shape[1], w, scratch, n_k, q_off, False, "mla_attn")


NEW_ROWS = LANES


def _cached_frame(m_ref, l_ref, acc_ref, chains, finish, *, nkb, tk, ta, q_off, n_k, causal):
    n = pl.program_id(1)

    @pl.when(n == 0)
    def _():
        m_ref[...] = jnp.full_like(m_ref, NEG_INF)
        l_ref[...] = jnp.zeros_like(l_ref)
        acc_ref[...] = jnp.zeros_like(acc_ref)

    @pl.when(n < nkb)
    def _():
        chains(True, tk, None)

    @pl.when(n == nkb)
    def _():
        mask_fn = lambda s: jnp.where(_visible(s.shape, q_off, q_off, n_k, causal, ta), s, NEG_INF)
        chains(False, NEW_ROWS, mask_fn)
        finish()


def _dec_diff_kernel(q_ref, kc_ref, vc_ref, kn_ref, vn_ref, lam_ref, sub_ref, o_ref,
                     m_ref, l_ref, acc_ref, *, n_heads, lam_init, **frame):
    hd, g, tk = LANES, 2 * n_heads, frame['tk']

    def chains(cached, n_keys, mask_fn):
        for h in range(n_heads):
            if cached:
                v = jnp.concatenate([vc_ref[pl.ds(h, tk, stride=g), :],
                                     vc_ref[pl.ds(n_heads + h, tk, stride=g), :]],
                                    axis=1).astype(BF16)
            else:
                v = vn_ref[:, 2 * h * hd:(2 * h + 2) * hd]
            for c in range(2):
                j = 2 * h + c
                k = kc_ref[pl.ds(j, tk, stride=g), :].astype(BF16) if cached else kn_ref[:, j * hd:(j + 1) * hd]
                s = _dot_nt(q_ref[:, j * hd:(j + 1) * hd], k)
                if mask_fn is not None:
                    s = mask_fn(s)
                _attend(s, v, m_ref, l_ref, acc_ref, j)

    def finish():
        lp = lam_ref[...]
        lam = (jnp.exp(jnp.sum(lp[0:1] * lp[1:2], axis=-1, keepdims=True))
               - jnp.exp(jnp.sum(lp[2:3] * lp[3:4], axis=-1, keepdims=True)) + lam_init)
        for h in range(n_heads):
            l0 = jnp.sum(l_ref[2 * h], axis=1, keepdims=True)
            l1 = jnp.sum(l_ref[2 * h + 1], axis=1, keepdims=True)
            o = acc_ref[2 * h] / l0 - lam * (acc_ref[2 * h + 1] / l1)
            o_ref[:, 2 * h * hd:(2 * h + 2) * hd] = (_rms(o, sub_ref[...]) * (1.0 - lam_init)).astype(o_ref.dtype)

    _cached_frame(m_ref, l_ref, acc_ref, chains, finish, **frame)


def _dec_fox_kernel(q_ref, kc_ref, vc_ref, kn_ref, vn_ref, fq_ref, fk_ref, o_ref,
                    m_ref, l_ref, acc_ref, *, n_heads, **frame):
    hd, g, tk = LANES, n_heads, frame['tk']

    def chains(cached, n_keys, mask_fn):
        for h in range(n_heads):
            if cached:
                k = kc_ref[pl.ds(h, tk, stride=g), :].astype(BF16)
                v = vc_ref[pl.ds(h, tk, stride=g), :].astype(BF16)
            else:
                k = kn_ref[:, h * hd:(h + 1) * hd]
                v = vn_ref[:, h * hd:(h + 1) * hd]
            bias = (fq_ref[:, h:h + 1] - fk_ref[h:h + 1, 0:n_keys]) * LOG2E
            s = _dot_nt(q_ref[:, h * hd:(h + 1) * hd], k) + bias
            if mask_fn is not None:
                s = mask_fn(s)
            _attend(s, v, m_ref, l_ref, acc_ref, h)

    def finish():
        for h in range(n_heads):
            o = acc_ref[h] / jnp.sum(l_ref[h], axis=1, keepdims=True)
            o_ref[:, h * hd:(h + 1) * hd] = o.astype(o_ref.dtype)

    _cached_frame(m_ref, l_ref, acc_ref, chains, finish, **frame)


def _cached_attn(kernel, q, cache_k, cache_v, new_k, new_v, extra, extra_specs, groups, v_width, n_k, causal,
                 name, **kw):
    b, ta, _ = q.shape
    past = cache_k.shape[1]
    tk = _pick_tile(past, 1024, LANES)
    nkb = past // tk
    kc = cache_k.reshape(b, past * groups, LANES)
    if cache_v.shape[-1] == 2 * LANES:
        vc = cache_v.reshape(b, past, groups // 2, 2, LANES).swapaxes(2, 3).reshape(b, past * groups, LANES)
    else:
        vc = cache_v.reshape(b, past * groups, LANES)
    kn, vn = _pad_rows(new_k, NEW_ROWS), _pad_rows(new_v, NEW_ROWS)
    whole = lambda a: pl.BlockSpec((None,) + a.shape[1:], lambda bi, n: (bi,) + (0,) * (a.ndim - 1))
    cache = pl.BlockSpec((None, tk * groups, LANES), lambda bi, n: (bi, jnp.minimum(n, nkb - 1), 0))
    out_w = q.shape[2]
    return pl.pallas_call(
        functools.partial(kernel, nkb=nkb, tk=tk, ta=ta, q_off=past, n_k=n_k, causal=causal, **kw),
        out_shape=jax.ShapeDtypeStruct((b, ta, out_w), BF16),
        grid=(b, nkb + 1),
        in_specs=[whole(q), cache, cache, whole(kn), whole(vn)] + extra_specs(tk),
        out_specs=pl.BlockSpec((None, ta, out_w), lambda bi, n: (bi, 0, 0)),
        scratch_shapes=[pltpu.VMEM((groups, ta, LANES), F32), pltpu.VMEM((groups, ta, LANES), F32),
                        pltpu.VMEM((groups, ta, v_width), F32)],
        compiler_params=_params(("parallel", "arbitrary"), 4 * tk * groups * LANES * 4 + (16 << 20)),
        name=name,
    )(q, kc, vc, kn, vn, *extra)


def _dec_mla_kernel(q_ref, ckv_ref, kp_ref, ckvn_ref, kpn_ref, wk_ref, wv_ref, gk_ref, o_ref,
                    m_ref, l_ref, acc_ref, s_ref, qp_ref, *, n_heads, **frame):
    hd, ta = LANES, frame['ta']

    @pl.when(pl.program_id(1) == 0)
    def _():
        for h in range(n_heads):
            qp_ref[h * ta:(h + 1) * ta, :] = q_ref[:, (2 * h + 1) * hd:(2 * h + 2) * hd]

    def chains(cached, n_keys, mask_fn):
        ckv = ckv_ref[...].astype(BF16) if cached else ckvn_ref[...]
        kp = kp_ref[...] if cached else kpn_ref[...]
        kvn = _dot(ckv, wk_ref[...])
        for h in range(n_heads):
            kn = _rms(kvn[:, h * hd:(h + 1) * hd], gk_ref[...]).astype(BF16)
            s_ref[h * ta:(h + 1) * ta, 0:n_keys] = _dot_nt(q_ref[:, 2 * h * hd:(2 * h + 1) * hd], kn)
        s = s_ref[:, 0:n_keys] + _dot_nt(qp_ref[...], kp)
        if mask_fn is not None:
            s = mask_fn(s)
        _attend(s, ckv, m_ref, l_ref, acc_ref, slice(None))

    def finish():
        lat = acc_ref[...] / jnp.sum(l_ref[...], axis=1, keepdims=True)
        for h in range(n_heads):
            o = _dot(lat[h * ta:(h + 1) * ta, :].astype(BF16), wv_ref[:, h * hd:(h + 1) * hd])
            o_ref[:, h * hd:(h + 1) * hd] = o.astype(o_ref.dtype)

    _cached_frame(m_ref, l_ref, acc_ref, chains, finish, **frame)


def _dec_mla(qq, cache_ckv, kp_cache, new_ckv, new_kp, w_kn, w_v, g_nope_k, n_heads, n_k):
    b, ta, _ = qq.shape
    past, c = cache_ckv.shape[1:]
    tk = _pick_tile(past, 1024, LANES)
    nkb = past // tk
    rows = n_heads * ta
    ckvn, kpn = _pad_rows(new_ckv, NEW_ROWS), _pad_rows(new_kp, NEW_ROWS)
    whole = lambda a: pl.BlockSpec((None,) + a.shape[1:], lambda bi, n: (bi,) + (0,) * (a.ndim - 1))
    const = lambda a: pl.BlockSpec(a.shape, lambda bi, n: (0,) * a.ndim)
    blk = lambda w: pl.BlockSpec((None, tk, w), lambda bi, n: (bi, jnp.minimum(n, nkb - 1), 0))

    kernel = functools.partial(_dec_mla_kernel, n_heads=n_heads, nkb=nkb, tk=tk, ta=ta, q_off=past, n_k=n_k,
                               causal=False)
    gk = g_nope_k.reshape(1, LANES)
    return pl.pallas_call(
        kernel,
        out_shape=jax.ShapeDtypeStruct((b, ta, n_heads * LANES), BF16),
        grid=(b, nkb + 1),
        in_specs=[whole(qq), blk(c), blk(LANES), whole(ckvn), whole(kpn), const(w_kn), const(w_v), const(gk)],
        out_specs=pl.BlockSpec((None, ta, n_heads * LANES), lambda bi, n: (bi, 0, 0)),
        scratch_shapes=[pltpu.VMEM((rows, LANES), F32), pltpu.VMEM((rows, LANES), F32), pltpu.VMEM((rows, c), F32),
                        pltpu.VMEM((rows, tk), F32), pltpu.VMEM((rows, LANES), BF16)],
        compiler_params=_params(("parallel", "arbitrary"), 40 << 20),
        name="mla_attn_cached",
    )(qq, cache_ckv, kp_cache, ckvn, kpn, w_kn, w_v, gk)


def _cumsum_rows_kernel(x_ref, o_ref, carry_ref):
    @pl.when(pl.program_id(1) == 0)
    def _():
        carry_ref[...] = jnp.zeros_like(carry_ref)

    x = x_ref[...]
    tb = x.shape[1]
    r = lax.broadcasted_iota(jnp.int32, (tb, tb), 0)
    c = lax.broadcasted_iota(jnp.int32, (tb, tb), 1)
    tri = jnp.where(r <= c, 1.0, 0.0).astype(BF16)
    hi, mid, lo = _split3(x)
    cum = _dot(hi, tri) + _dot(mid, tri) + _dot(lo, tri) + carry_ref[:, 0:1]
    o_ref[...] = cum
    carry_ref[...] = jnp.broadcast_to(cum[:, tb - 1:tb], carry_ref.shape)


def _cumsum_rows(x, tb):
    b, g, t = x.shape
    return pl.pallas_call(
        _cumsum_rows_kernel,
        out_shape=jax.ShapeDtypeStruct((b, g, t), F32),
        grid=(b, t // tb),
        in_specs=[pl.BlockSpec((None, g, tb), lambda bi, i: (bi, 0, i))],
        out_specs=pl.BlockSpec((None, g, tb), lambda bi, i: (bi, 0, i)),
        scratch_shapes=[pltpu.VMEM((g, LANES), F32)],
        compiler_params=_params(("parallel", "arbitrary"), 32 << 20),
        name="forget_cumsum",
    )(x)


def _dec_diff(q, cache_k, cache_v, new_k, new_v, lam_p, subln, lam_init, n_heads, n_k):
    b, past = cache_k.shape[:2]
    specs = lambda tk: [pl.BlockSpec(lam_p.shape, lambda bi, n: (0, 0)),
                        pl.BlockSpec((1, 2 * LANES), lambda bi, n: (0, 0))]
    return _cached_attn(_dec_diff_kernel, q, cache_k, cache_v, new_k, new_v, [lam_p, subln.reshape(1, 2 * LANES)],
                        specs, 2 * n_heads, 2 * LANES, n_k, False, "diff_attn_cached",
                        n_heads=n_heads, lam_init=lam_init)


def _dec_fox(q, cache_k, cache_v, new_k, new_v, past_logf, new_logf, n_heads, n_k):
    b, past = cache_k.shape[:2]
    ta = q.shape[1]
    tk = _pick_tile(past, 1024, LANES)
    lf = jnp.concatenate([past_logf.astype(F32), new_logf[:, :, :n_heads]], axis=1)
    lf_rows = _pad_lanes(jnp.swapaxes(lf, 1, 2), past + tk)
    f_rows = _cumsum_rows(lf_rows, tk)
    f_q = _pad_lanes(jnp.swapaxes(f_rows[:, :, past:past + ta], 1, 2), LANES)
    specs = lambda tk: [pl.BlockSpec((None, ta, LANES), lambda bi, n: (bi, 0, 0)),
                        pl.BlockSpec((None, n_heads, tk), lambda bi, n: (bi, 0, n))]
    return _cached_attn(_dec_fox_kernel, q, cache_k, cache_v, new_k, new_v, [f_q, f_rows], specs,
                        n_heads, LANES, n_k, True, "fox_attn_cached", n_heads=n_heads)


def _rope_half(y, cos, sin, rope_dim):
    half = rope_dim // 2
    lane = lax.broadcasted_iota(jnp.int32, y.shape, 1)
    rot = jnp.where(lane < half, pltpu.roll(y, LANES - half, 1), pltpu.roll(y, half, 1))
    return y * cos + rot * sin


def _rms_low(x, gain, n):
    return x * lax.rsqrt(jnp.sum(x * x, axis=-1, keepdims=True) * (1.0 / n) + EPS) * gain


def _odd_post_kernel(z_ref, cos_ref, sin_ref, gq_ref, gkv_ref, gr_ref,
                     cq_ref, ckvf_ref, ckvb_ref, kpf_ref, kpb_ref, *, q_lora, kv_lora, rope_dim):
    cq_ref[...] = _rms(z_ref[:, 0:q_lora], gq_ref[...]).astype(BF16)
    ckv = _rms(z_ref[:, q_lora:q_lora + kv_lora], gkv_ref[...])
    ckvf_ref[...] = ckv
    ckvb_ref[...] = ckv.astype(BF16)
    kp = _rms_low(z_ref[:, q_lora + kv_lora:q_lora + kv_lora + LANES], gr_ref[...], rope_dim)
    kp = _rope_half(kp, cos_ref[...], sin_ref[...], rope_dim)
    kpf_ref[...] = kp
    kpb_ref[...] = kp.astype(BF16)


def _odd_post(z, cos, sin, g_cq, g_ckv, g_rope_k_pad, q_lora, kv_lora, rope_dim):
    b, t, n = z.shape
    tm = _pick_tile(t, 512, 16)
    row = lambda w: pl.BlockSpec((None, tm, w), lambda bi, i: (bi, i, 0))
    tab = pl.BlockSpec((tm, LANES), lambda bi, i: (i, 0))
    vec = lambda w: pl.BlockSpec((1, w), lambda bi, i: (0, 0))
    outs = [(q_lora, BF16), (kv_lora, F32), (kv_lora, BF16), (LANES, F32), (LANES, BF16)]
    return pl.pallas_call(
        functools.partial(_odd_post_kernel, q_lora=q_lora, kv_lora=kv_lora, rope_dim=rope_dim),
        out_shape=[jax.ShapeDtypeStruct((b, t, w), dt) for w, dt in outs],
        grid=(b, t // tm),
        in_specs=[row(n), tab, tab, vec(q_lora), vec(kv_lora), vec(LANES)],
        out_specs=[row(w) for w, _ in outs],
        compiler_params=_params(("parallel", "parallel"), 32 << 20),
        name="odd_post",
    )(z, cos, sin, g_cq.reshape(1, -1), g_ckv.reshape(1, -1), g_rope_k_pad)


def _qup_kernel(cq_ref, w_ref, cos_ref, sin_ref, gn_ref, gr_ref, qq_ref, *, heads, rope_dim, q_scale):
    cq = cq_ref[...]
    cos, sin = cos_ref[...], sin_ref[...]
    for h in range(heads):
        q = _dot(cq, w_ref[:, 2 * h * LANES:(2 * h + 2) * LANES])
        qn = _rms_mxu(q[:, 0:LANES], gn_ref[...])
        qq_ref[:, 2 * h * LANES:(2 * h + 1) * LANES] = (qn * q_scale).astype(BF16)
        qp = _rms_mxu(q[:, LANES:2 * LANES], gr_ref[...], rope_dim)
        qp = _rope_half(qp, cos, sin, rope_dim)
        qq_ref[:, (2 * h + 1) * LANES:(2 * h + 2) * LANES] = (qp * q_scale).astype(BF16)


def _qup(cq, w_pad, cos, sin, g_nope_q, g_rope_q_pad, n_heads, rope_dim, qk_dim):
    b, t, kq = cq.shape
    tm = _pick_tile(t, 512, 16)
    hg = 4 if n_heads % 4 == 0 else 1
    tn = hg * 2 * LANES
    return pl.pallas_call(
        functools.partial(_qup_kernel, heads=hg, rope_dim=rope_dim, q_scale=qk_dim ** -0.5 * LOG2E),
        out_shape=jax.ShapeDtypeStruct((b, t, n_heads * 2 * LANES), BF16),
        grid=(b, t // tm, n_heads // hg),
        in_specs=[
            pl.BlockSpec((None, tm, kq), lambda bi, i, j: (bi, i, 0)),
            pl.BlockSpec((kq, tn), lambda bi, i, j: (0, j)),
            pl.BlockSpec((tm, LANES), lambda bi, i, j: (i, 0)),
            pl.BlockSpec((tm, LANES), lambda bi, i, j: (i, 0)),
            pl.BlockSpec((1, LANES), lambda bi, i, j: (0, 0)),
            pl.BlockSpec((1, LANES), lambda bi, i, j: (0, 0)),
        ],
        out_specs=pl.BlockSpec((None, tm, tn), lambda bi, i, j: (bi, i, j)),
        compiler_params=_params(("parallel", "parallel", "arbitrary"), 32 << 20),
        name="mla_q_up",
    )(cq, w_pad, cos, sin, g_nope_q.reshape(1, LANES), g_rope_q_pad)


def _kvup_kernel(ckv_ref, kp_ref, w_ref, gn_ref, kk_ref, v_ref, *, heads):
    ckv = ckv_ref[...]
    kp = kp_ref[...]
    for h in range(heads):
        kv = _dot(ckv, w_ref[:, 2 * h * LANES:(2 * h + 2) * LANES])
        kn = _rms_mxu(kv[:, 0:LANES], gn_ref[...])
        kk_ref[:, 2 * h * LANES:(2 * h + 1) * LANES] = kn.astype(BF16)
        kk_ref[:, (2 * h + 1) * LANES:(2 * h + 2) * LANES] = kp
        v_ref[:, 2 * h * LANES:(2 * h + 1) * LANES] = kv[:, LANES:2 * LANES].astype(BF16)
        v_ref[:, (2 * h + 1) * LANES:(2 * h + 2) * LANES] = jnp.ones((kv.shape[0], LANES), BF16)


def _kvup(ckv, kp, w, g_nope_k, n_heads):
    b, t, kk = ckv.shape
    tm = _pick_tile(t, 512, LANES)
    hg = 4 if n_heads % 4 == 0 else 1
    tn = hg * 2 * LANES
    return pl.pallas_call(
        functools.partial(_kvup_kernel, heads=hg),
        out_shape=[jax.ShapeDtypeStruct((b, t, n_heads * 2 * LANES), BF16)] * 2,
        grid=(b, t // tm, n_heads // hg),
        in_specs=[
            pl.BlockSpec((None, tm, kk), lambda bi, i, j: (bi, i, 0)),
            pl.BlockSpec((None, tm, LANES), lambda bi, i, j: (bi, i, 0)),
            pl.BlockSpec((kk, tn), lambda bi, i, j: (0, j)),
            pl.BlockSpec((1, LANES), lambda bi, i, j: (0, 0)),
        ],
        out_specs=[pl.BlockSpec((None, tm, tn), lambda bi, i, j: (bi, i, j))] * 2,
        compiler_params=_params(("parallel", "parallel", "arbitrary"), 32 << 20),
        name="mla_kv_up",
    )(ckv, kp, w, g_nope_k.reshape(1, LANES))


def _rope_tables(pos, dim):
    half = dim // 2
    inv = ROPE_THETA ** (-jnp.arange(half, dtype=F32) * 2.0 / dim)
    ang = pos.astype(F32)[:, None] * inv[None, :]
    cos, sin = jnp.cos(ang), jnp.sin(ang)
    pad = ((0, 0), (0, LANES - dim))
    return (jnp.pad(jnp.concatenate([cos, cos], axis=-1), pad),
            jnp.pad(jnp.concatenate([-sin, sin], axis=-1), pad))


def _pad_lanes(a, width):
    return jnp.pad(a, [(0, 0)] * (a.ndim - 1) + [(0, width - a.shape[-1])])


def _pad_rows(a, rows):
    return jnp.pad(a, [(0, 0), (0, rows - a.shape[1])] + [(0, 0)] * (a.ndim - 2))


def _layer_stack(x, c_mod, tok_pos, seq_shape, caches, p):
    bx, tx, d = x.shape
    ba, ta = seq_shape
    depth = p['w_ffn_in'].shape[0]
    n_diff = p['n_diff']
    n_fox = p['n_fox']
    n_mla = p['n_mla']
    past_len = 0 if caches is None else caches[0].shape[2]
    n_k = past_len + ta
    cos128, sin128 = _rope_tables(tok_pos, LANES)
    rope_dim = p['rope_dim']
    cos_r, sin_r = _rope_tables(tok_pos, rope_dim)
    new = [[] for _ in range(7)]

    def mods(l, s):
        m = c_mod[l]
        sh, sc, gt = m[:, 3 * s], m[:, 3 * s + 1], m[:, 3 * s + 2]
        if bx == m.shape[0]:
            return tuple(a[:, None, :] for a in (sh, sc, gt))
        rep = lambda a: jnp.repeat(a, ta, axis=0).reshape(bx, tx, d)
        return rep(sh), rep(sc), rep(gt)

    seq = lambda a: a.reshape(ba, ta, a.shape[-1])
    for l in range(depth):
        i = l // 2
        g = p['norm_gains'][l]
        sh, sc, gt = mods(l, 0)
        x = _ffn(x, sh, sc, gt, g[0], p['w_ffn_in'], p['w_ffn_out'], l, 0, 0.5)
        sh, sc, gt = mods(l, 1)
        if l % 2 == 0:
            (qa, kaf, kab, vaf, vab, qb, kbf, kbb, vbf, vbb, lf) = _even_proj(
                x, sh, sc, g[1], p['w_in_even'][i], p['w_in_forget'][i], cos128, sin128,
                p['qk_norm_even'][i], p['b_forget_pad'][i], 2 * n_diff, n_fox)
            new[0].append(kaf.reshape(ba, ta, n_diff, 2, LANES))
            new[1].append(vaf.reshape(ba, ta, 2, n_diff, LANES).swapaxes(2, 3).reshape(ba, ta, n_diff, 2 * LANES))
            new[2].append(kbf.reshape(ba, ta, n_fox, LANES))
            new[3].append(vbf.reshape(ba, ta, n_fox, LANES))
            new[4].append(seq(lf)[:, :, :n_fox])
            lam_init = 0.8 - 0.6 * math.exp(-0.3 * l)
            if caches is None:
                aq, kk, vv = _fox_prep(lf, kbb, vbb, n_fox)
                oa = _flash_diff(qa, kab, vab, p['diff_lambda'][i], p['diff_subln'][i], lam_init, n_diff, n_k, 0)
                ob = _flash_fox(qb, aq, kk, vv, n_fox, n_k, 0)
            else:
                past = tuple(a[i] for a in caches[:5])
                oa = _dec_diff(seq(qa), past[0], past[1], seq(kab), seq(vab), p['diff_lambda'][i],
                               p['diff_subln'][i], lam_init, n_diff, n_k)
                ob = _dec_fox(seq(qb), past[2], past[3], seq(kbb), seq(vbb), past[4], seq(lf), n_fox, n_k)
            x = _outproj(oa.reshape(bx, tx, -1), ob.reshape(bx, tx, -1), 0, 0, p['w_out_even'][i], x, gt)
        else:
            q_lora, kv_lora = p['q_lora'], p['kv_lora']
            z = _modproj(x, sh, sc, g[1], p['w_in_odd'][i])
            cq, ckvf, ckvb, kpf, kpb = _odd_post(z, cos_r, sin_r, p['mla_cq_norm'][i], p['mla_ckv_norm'][i],
                                                 p['g_rope_pad'][i, 1:2], q_lora, kv_lora, rope_dim)
            new[5].append(seq(ckvf))
            new[6].append(seq(kpf)[:, :, :rope_dim])
            qq = _qup(cq, p['w_uq_pad'][i], cos_r, sin_r, p['mla_qk_norm_nope'][i, 0],
                      p['g_rope_pad'][i, 0:1], n_mla, rope_dim, p['mla_qk_dim'])
            if caches is None:
                kk, v = _kvup(ckvb, kpb, p['w_ukv'][i], p['mla_qk_norm_nope'][i, 1], n_mla)
                o = _flash_mla(qq, kk, v, n_mla, n_k, 0)
            else:
                kp_cache = _pad_lanes(caches[6][i], LANES).astype(BF16)
                o = _dec_mla(seq(qq), caches[5][i], kp_cache, seq(ckvb), seq(kpb), p['w_kn'][i], p['w_v'][i],
                             p['mla_qk_norm_nope'][i, 1], n_mla, n_k)
            o = o.reshape(bx, tx, -1)
            x = _outproj(o, o, 0, 1, p['w_out_odd'][i], x, gt)
        sh, sc, gt = mods(l, 2)
        x = _ffn(x, sh, sc, gt, g[2], p['w_ffn_in'], p['w_ffn_out'], l, 1, 0.5, final_gain=g[3])
    return x, tuple(jnp.stack(lst) for lst in new)


def kernel(x_prompt, x_sample, c_prompt, c_sample, cache_diff_k, cache_diff_v, cache_fox_k, cache_fox_v, cache_fox_logf, cache_mla_ckv, cache_mla_kpe, w_ada, b_ada, norm_gains, w_ffn_in, w_ffn_out, w_in_even, b_forget, qk_norm_even, diff_lambda, diff_subln, w_out_even, w_in_odd, mla_cq_norm, mla_ckv_norm, w_uq, w_ukv, mla_qk_norm_nope, mla_qk_norm_rope, w_out_odd):
    d = x_prompt.shape[-1]
    n_diff, n_fox = cache_diff_k.shape[3], cache_fox_k.shape[3]
    assert cache_diff_k.shape[-1] == LANES and cache_fox_k.shape[-1] == LANES
    q_lora, kv_lora = mla_cq_norm.shape[-1], mla_ckv_norm.shape[-1]
    rope_dim, nope = cache_mla_kpe.shape[-1], mla_qk_norm_nope.shape[-1]
    n_mla = w_uq.shape[-1] // (nope + rope_dim)
    assert nope == LANES and rope_dim <= LANES and w_ukv.shape[-1] == n_mla * 2 * LANES
    n_odd = w_uq.shape[0]
    n_main = w_in_even.shape[-1] - n_fox
    assert n_main == 6 * 2 * n_diff * LANES and n_fox <= LANES

    w_uq_pad = _pad_lanes(w_uq.reshape(n_odd, q_lora, n_mla, nope + rope_dim), 2 * LANES)
    p = {
        'n_diff': n_diff, 'n_fox': n_fox, 'n_mla': n_mla, 'rope_dim': rope_dim,
        'q_lora': q_lora, 'kv_lora': kv_lora, 'mla_qk_dim': nope + rope_dim,
        'norm_gains': norm_gains,
        'w_ffn_in': _interleave_gate_up(w_ffn_in), 'w_ffn_out': w_ffn_out.astype(BF16),
        'w_in_even': w_in_even[:, :, :n_main].astype(BF16),
        'w_in_forget': _pad_lanes(w_in_even[:, :, n_main:], LANES).astype(BF16),
        'b_forget_pad': _pad_lanes(b_forget, LANES)[:, None, :],
        'qk_norm_even': qk_norm_even, 'diff_lambda': diff_lambda, 'diff_subln': diff_subln,
        'w_out_even': w_out_even.astype(BF16),
        'w_in_odd': _pad_lanes(w_in_odd, q_lora + kv_lora + LANES).astype(BF16),
        'mla_cq_norm': mla_cq_norm, 'mla_ckv_norm': mla_ckv_norm,
        'w_uq_pad': w_uq_pad.reshape(n_odd, q_lora, n_mla * 2 * LANES).astype(BF16),
        'w_ukv': w_ukv.astype(BF16),
        'w_kn': w_ukv.reshape(n_odd, kv_lora, n_mla, 2 * LANES)[..., :LANES].reshape(n_odd, kv_lora, -1).astype(BF16),
        'w_v': w_ukv.reshape(n_odd, kv_lora, n_mla, 2 * LANES)[..., LANES:].reshape(n_odd, kv_lora, -1).astype(BF16),
        'mla_qk_norm_nope': mla_qk_norm_nope,
        'g_rope_pad': _pad_lanes(mla_qk_norm_rope, LANES),
        'w_out_odd': w_out_odd.astype(BF16),
    }

    bp, tp = x_prompt.shape[:2]
    bs, ts = x_sample.shape[:2]
    past_len = cache_diff_k.shape[2]
    mod = _ada(jnp.concatenate([c_prompt, c_sample], axis=0), w_ada, b_ada)
    mod = mod.reshape(mod.shape[0], bp + bs, N_MOD, d)

    pos_p = jnp.arange(tp, dtype=jnp.int32)
    y_prompt, st_p = _layer_stack(x_prompt, mod[:, :bp], pos_p, (bp, tp), None, p)

    pos_s = jnp.tile(past_len + jnp.arange(ts, dtype=jnp.int32), bs)
    caches = (cache_diff_k, cache_diff_v, cache_fox_k, cache_fox_v, cache_fox_logf, cache_mla_ckv, cache_mla_kpe)
    y_sample, st_s = _layer_stack(x_sample.reshape(1, bs * ts, d), mod[:, bp:], pos_s, (bs, ts), caches, p)
    return (y_prompt, y_sample.reshape(bs, ts, d)) + st_p + st_s
```
